```python
import jax, jax.numpy as jnp
from jax import lax
import numpy as np

D_MODEL = 1024
BATCH = 32
SEQ = 2048
DEPTH = 1

MEM_LEN = 256
EPS = 1e-6
HEAD_DIM = 64
N_Q_HEADS = 16
N_KV_HEADS = 2
WINDOW = 128
ATT_BLOCK = 128
ROPE_THETA = 10000.0
GMLP_WIDTH = 512
GMLP_GROUPS = 4
GMLP_GROUP_DIM = GMLP_WIDTH // GMLP_GROUPS
GMLP_CHUNK = 128
X_HEADS = 4
X_HEAD_DIM = 128
N_BRANCHES = 3
N_EXPERTS = 32
TOP_K = 4
D_EXPERT = D_MODEL
SWIGLU_LIMIT = 7.0
SWIGLU_ALPHA = 1.702
MOE_BLOCK = 512

A_Q = N_Q_HEADS * HEAD_DIM
A_KV = N_KV_HEADS * HEAD_DIM
C_Q = X_HEADS * X_HEAD_DIM
IN_WIDTHS = [A_Q, A_KV, A_KV, GMLP_WIDTH, GMLP_WIDTH, C_Q, D_MODEL, D_MODEL, D_MODEL]
D_IN = int(sum(IN_WIDTHS))
IN_SPLITS = [int(s) for s in np.cumsum(IN_WIDTHS)[:-1]]

kernel_name = "hybrid_swa_gmlp_memxattn_moe_block"

F32 = jnp.float32


def rms_norm(x, g):
    xf = x.astype(F32)
    y = xf * lax.rsqrt(jnp.mean(xf * xf, axis=-1, keepdims=True) + EPS)
    return (y * g.astype(F32)).astype(x.dtype)


def layer_norm(x, g, b):
    xf = x.astype(F32)
    mu = jnp.mean(xf, axis=-1, keepdims=True)
    var = jnp.mean(jnp.square(xf - mu), axis=-1, keepdims=True)
    y = (xf - mu) * lax.rsqrt(var + EPS)
    return (y * g.astype(F32) + b.astype(F32)).astype(x.dtype)


def rope_tables(positions, dim):
    inv_freq = ROPE_THETA ** (-jnp.arange(0, dim, 2, dtype=F32) / dim)
    ang = positions.astype(F32)[..., None] * inv_freq
    return jnp.cos(ang)[:, :, None, :], jnp.sin(ang)[:, :, None, :]


def apply_rope(x, cos, sin):
    xf = x.astype(F32)
    x1, x2 = jnp.split(xf, 2, axis=-1)
    return jnp.concatenate([x1 * cos - x2 * sin, x2 * cos + x1 * sin], axis=-1).astype(x.dtype)


def sliding_window_sink_attention(q, k, v, sinks):
    B, S = q.shape[0], q.shape[1]
    nb = S // ATT_BLOCK
    R = N_Q_HEADS // N_KV_HEADS
    qb = q.reshape(B, nb, ATT_BLOCK, N_KV_HEADS, R, HEAD_DIM).transpose(1, 0, 2, 3, 4, 5)

    def band(t):
        tb = t.reshape(B, nb, ATT_BLOCK, N_KV_HEADS, HEAD_DIM)
        prev = jnp.pad(tb[:, :-1], ((0, 0), (1, 0), (0, 0), (0, 0), (0, 0)))
        return jnp.concatenate([prev, tb], axis=2).transpose(1, 0, 2, 3, 4)

    kb, vb = band(k), band(v)
    qi_idx = jnp.arange(ATT_BLOCK)[:, None]
    kj_idx = jnp.arange(2 * ATT_BLOCK)[None, :]
    rel = ATT_BLOCK + qi_idx - kj_idx
    band_mask = (rel >= 0) & (rel < WINDOW)
    sink = sinks.astype(F32).reshape(N_KV_HEADS, R)[None, :, :, None]
    scale = HEAD_DIM ** -0.5

    def one_block(args):
        blk, qi, ki, vi = args
        s = jnp.einsum('bqhrd,bkhd->bhrqk', qi, ki).astype(F32) * scale
        valid = band_mask & ((blk * ATT_BLOCK - ATT_BLOCK + kj_idx) >= 0)
        s = jnp.where(valid, s, -jnp.inf)
        m = jnp.maximum(jnp.max(s, axis=-1), sink)
        p = jnp.exp(s - m[..., None])
        denom = jnp.sum(p, axis=-1) + jnp.exp(sink - m)
        o = jnp.einsum('bhrqk,bkhd->bqhrd', p, vi.astype(F32))
        o = o / denom.transpose(0, 3, 1, 2)[..., None]
        return o.astype(q.dtype)

    out = lax.map(one_block, (jnp.arange(nb), qb, kb, vb))
    return out.transpose(1, 0, 2, 3, 4, 5).reshape(B, S, A_Q)


def chunked_spatial_gating(u, v, w_s, b_s, ln_g, ln_b):
    B, S = v.shape[0], v.shape[1]
    nc = S // GMLP_CHUNK
    vn = layer_norm(v, ln_g, ln_b)
    causal = jnp.tril(jnp.ones((GMLP_CHUNK, GMLP_CHUNK), dtype=bool))
    w = jnp.where(causal[None], w_s, jnp.zeros_like(w_s)).astype(v.dtype)
    vc = vn.reshape(B, nc, GMLP_CHUNK, GMLP_GROUPS, GMLP_GROUP_DIM)
    mixed = jnp.einsum('gts,bcsgd->bctgd', w, vc) + b_s.T.astype(v.dtype)[None, None, :, :, None]
    return u * mixed.reshape(B, S, GMLP_WIDTH)


def memory_cross_attention(q, k, v):
    B, S = q.shape[0], q.shape[1]
    s = jnp.einsum('bshd,bmhd->bhsm', q, k).astype(F32) * (X_HEAD_DIM ** -0.5)
    p = jax.nn.softmax(s, axis=-1)
    o = jnp.einsum('bhsm,bmhd->bshd', p, v.astype(F32))
    return o.astype(q.dtype).reshape(B, S, C_Q)


def clamped_swiglu_expert(xb, w_gu, b_gu, w_d, b_d):
    gu = xb @ w_gu + b_gu
    gate, up = gu[:, :D_EXPERT], gu[:, D_EXPERT:]
    gate = jnp.minimum(gate, SWIGLU_LIMIT)
    up = jnp.clip(up, -SWIGLU_LIMIT, SWIGLU_LIMIT)
    glu = gate * jax.nn.sigmoid(gate * SWIGLU_ALPHA)
    return ((up + 1.0) * glu) @ w_d + b_d


def moe_ffn(h, router_w, router_b, w_gate_up, b_gate_up, w_down, b_down):
    T, D = h.shape
    logits = (h @ router_w).astype(F32) + router_b.astype(F32)
    top_vals, top_idx = lax.top_k(logits, TOP_K)
    gates = jax.nn.softmax(top_vals, axis=-1)
    A = T * TOP_K
    flat_e = top_idx.reshape(A).astype(jnp.int32)
    flat_tok = jnp.arange(A, dtype=jnp.int32) // TOP_K
    flat_gate = gates.reshape(A)
    order = jnp.argsort(flat_e, stable=True)
    sorted_e = flat_e[order]
    counts = jnp.bincount(flat_e, length=N_EXPERTS).astype(jnp.int32)
    padded = (counts + MOE_BLOCK - 1) // MOE_BLOCK * MOE_BLOCK
    padded_end = jnp.cumsum(padded)
    padded_start = padded_end - padded
    group_start = jnp.cumsum(counts) - counts
    rank = jnp.arange(A, dtype=jnp.int32) - group_start[sorted_e]
    dest = padded_start[sorted_e] + rank
    n_blocks = -(-A // MOE_BLOCK) + N_EXPERTS
    P = n_blocks * MOE_BLOCK
    slot_tok = jnp.full((P,), T, dtype=jnp.int32).at[dest].set(flat_tok[order])
    slot_gate = jnp.zeros((P,), F32).at[dest].set(flat_gate[order])
    block_expert = jnp.minimum(
        jnp.searchsorted(padded_end, jnp.arange(n_blocks, dtype=jnp.int32) * MOE_BLOCK, side='right'),
        N_EXPERTS - 1)
    h_pad = jnp.concatenate([h, jnp.zeros((1, D), h.dtype)], axis=0)

    def run_block(args):
        tok, e = args
        return clamped_swiglu_expert(h_pad[tok], w_gate_up[e], b_gate_up[e], w_down[e], b_down[e])

    out = lax.map(run_block, (slot_tok.reshape(n_blocks, MOE_BLOCK), block_expert))
    out = out.reshape(P, D) * slot_gate[:, None].astype(out.dtype)
    return jnp.zeros((T + 1, D), h.dtype).at[slot_tok].add(out)[:T]


def setup_inputs(seed: int = 0) -> dict:
    key = jax.random.key(seed)
    ks = iter(jax.random.split(key, 40))
    nrm = lambda shape, scale: jax.random.normal(next(ks), shape, F32) * scale
    gain = lambda shape: 1.0 + nrm(shape, 0.05)
    L = DEPTH
    return {
        "x": nrm((BATCH, SEQ, D_MODEL), 1.0),
        "mem": nrm((BATCH, MEM_LEN, D_MODEL), 1.0),
        "positions": jnp.broadcast_to(jnp.arange(SEQ, dtype=jnp.int32)[None, :], (BATCH, SEQ)),
        "attn_norm_g": gain((L, D_MODEL)),
        "mem_norm_g": gain((L, D_MODEL)),
        "w_in": nrm((L, D_MODEL, D_IN), D_MODEL ** -0.5),
        "b_gates": nrm((L, N_BRANCHES, D_MODEL), 0.1),
        "a_q_norm_g": gain((L, HEAD_DIM)),
        "a_k_norm_g": gain((L, HEAD_DIM)),
        "a_sinks": nrm((L, N_Q_HEADS), 0.5),
        "w_o_a": nrm((L, A_Q, D_MODEL), A_Q ** -0.5),
        "gmlp_ln_g": gain((L, GMLP_WIDTH)),
        "gmlp_ln_b": nrm((L, GMLP_WIDTH), 0.02),
        "gmlp_w_s": nrm((L, GMLP_GROUPS, GMLP_CHUNK, GMLP_CHUNK), 0.5 * GMLP_CHUNK ** -0.5),
        "gmlp_b_s": gain((L, GMLP_GROUPS, GMLP_CHUNK)),
        "w_o_b": nrm((L, GMLP_WIDTH, D_MODEL), GMLP_WIDTH ** -0.5),
        "w_mem_kv": nrm((L, D_MODEL, 2 * C_Q), D_MODEL ** -0.5),
        "c_q_norm_g": gain((L, X_HEAD_DIM)),
        "c_k_norm_g": gain((L, X_HEAD_DIM)),
        "w_o_c": nrm((L, C_Q, D_MODEL), C_Q ** -0.5),
        "w_out": nrm((L, D_MODEL, D_MODEL), D_MODEL ** -0.5),
        "ffn_norm_g": gain((L, D_MODEL)),
        "router_w": nrm((L, D_MODEL, N_EXPERTS), D_MODEL ** -0.5),
        "router_b": nrm((L, N_EXPERTS), 0.01),
        "w_gate_up": nrm((L, N_EXPERTS, D_MODEL, 2 * D_EXPERT), D_MODEL ** -0.5),
        "b_gate_up": nrm((L, N_EXPERTS, 2 * D_EXPERT), 0.01),
        "w_down": nrm((L, N_EXPERTS, D_EXPERT, D_MODEL), D_EXPERT ** -0.5),
        "b_down": nrm((L, N_EXPERTS, D_MODEL), 0.01),
    }


def reference(x, mem, positions, attn_norm_g, mem_norm_g, w_in, b_gates, a_q_norm_g, a_k_norm_g,
              a_sinks, w_o_a, gmlp_ln_g, gmlp_ln_b, gmlp_w_s, gmlp_b_s, w_o_b, w_mem_kv,
              c_q_norm_g, c_k_norm_g, w_o_c, w_out, ffn_norm_g, router_w, router_b,
              w_gate_up, b_gate_up, w_down, b_down):
    B, S, D = x.shape
    M = mem.shape[1]
    cos, sin = rope_tables(positions, HEAD_DIM)
    for l in range(DEPTH):
        h = rms_norm(x, attn_norm_g[l])
        proj = h @ w_in[l]
        qa, ka, va, ub, vb, qc, ga, gb, gc = jnp.split(proj, IN_SPLITS, axis=-1)

        qa = apply_rope(rms_norm(qa.reshape(B, S, N_Q_HEADS, HEAD_DIM), a_q_norm_g[l]), cos, sin)
        ka = apply_rope(rms_norm(ka.reshape(B, S, N_KV_HEADS, HEAD_DIM), a_k_norm_g[l]), cos, sin)
        va = va.reshape(B, S, N_KV_HEADS, HEAD_DIM)
        ya = sliding_window_sink_attention(qa, ka, va, a_sinks[l]) @ w_o_a[l]

        yb = chunked_spatial_gating(jax.nn.gelu(ub, approximate=False), jax.nn.gelu(vb, approximate=False),
                                    gmlp_w_s[l], gmlp_b_s[l], gmlp_ln_g[l], gmlp_ln_b[l]) @ w_o_b[l]

        kv = rms_norm(mem, mem_norm_g[l]) @ w_mem_kv[l]
        kc, vc = jnp.split(kv, 2, axis=-1)
        kc = rms_norm(kc.reshape(B, M, X_HEADS, X_HEAD_DIM), c_k_norm_g[l])
        vc = vc.reshape(B, M, X_HEADS, X_HEAD_DIM)
        qc = rms_norm(qc.reshape(B, S, X_HEADS, X_HEAD_DIM), c_q_norm_g[l])
        yc = memory_cross_attention(qc, kc, vc) @ w_o_c[l]

        merged = (jax.nn.sigmoid(ga + b_gates[l, 0]) * ya
                  + jax.nn.sigmoid(gb + b_gates[l, 1]) * yb
                  + jax.nn.sigmoid(gc + b_gates[l, 2]) * yc)
        x = x + merged @ w_out[l]

        h2 = rms_norm(x, ffn_norm_g[l]).reshape(B * S, D)
        x = x + moe_ffn(h2, router_w[l], router_b[l], w_gate_up[l], b_gate_up[l],
                        w_down[l], b_down[l]).reshape(B, S, D)
    return x
```

```python
import functools

import numpy as np
import jax
import jax.numpy as jnp
from jax import lax
from jax.experimental import pallas as pl
from jax.experimental.pallas import tpu as pltpu

F32 = jnp.float32
BF16 = jnp.bfloat16
I32 = jnp.int32

EPS = 1e-6
LANES = 128
HEAD_DIM = 64
N_Q_HEADS = 16
N_KV_HEADS = 2
ATT_BLOCK = 128
ROPE_THETA = 10000.0
GMLP_WIDTH = 512
GMLP_GROUPS = 4
GMLP_CHUNK = 128
X_HEADS = 4
X_HEAD_DIM = 128
N_EXPERTS = 32
TOP_K = 4
SWIGLU_LIMIT = 7.0
SWIGLU_ALPHA = 1.702
MOE_BLOCK = 512

A_Q = N_Q_HEADS * HEAD_DIM
A_KV = N_KV_HEADS * HEAD_DIM
C_Q = X_HEADS * X_HEAD_DIM

OFF_Q = 0
OFF_K4 = OFF_Q + A_Q
OFF_V4 = OFF_K4 + 4 * LANES
OFF_U = OFF_V4 + 4 * LANES
OFF_VN = OFF_U + GMLP_WIDTH
OFF_QC = OFF_VN + GMLP_WIDTH
OFF_GA = OFF_QC + C_Q
PROJ_W_BASE = OFF_GA

VMEM_LIMIT = 56 * 1024 * 1024


def _lane_iota(shape):
    return lax.broadcasted_iota(I32, shape, len(shape) - 1)


def _rms(x, g):
    return x * lax.rsqrt(jnp.mean(x * x, axis=-1, keepdims=True) + EPS) * g


def _gelu(x):
    return 0.5 * x * (1.0 + lax.erf(x * np.float32(np.sqrt(0.5))))


def _memkv_kernel(mem_ref, g_ref, w_ref, gk_ref, kc_ref, vc_ref):
    h = _rms(mem_ref[...], g_ref[...]).astype(BF16)
    kv = jnp.dot(h, w_ref[...], preferred_element_type=F32)
    for hh in range(X_HEADS):
        sl = slice(hh * X_HEAD_DIM, (hh + 1) * X_HEAD_DIM)
        kc_ref[:, sl] = _rms(kv[:, sl], gk_ref[...]).astype(BF16)
    vc_ref[...] = kv[:, C_Q:].astype(BF16)


def _mem_kv(mem2d, g, w_bf, gk, n_batch, m_len):
    d = mem2d.shape[1]
    return pl.pallas_call(
        _memkv_kernel,
        grid=(n_batch,),
        in_specs=[
            pl.BlockSpec((m_len, d), lambda b: (b, 0)),
            pl.BlockSpec((1, d), lambda b: (0, 0)),
            pl.BlockSpec((d, 2 * C_Q), lambda b: (0, 0)),
            pl.BlockSpec((1, X_HEAD_DIM), lambda b: (0, 0)),
        ],
        out_specs=[
            pl.BlockSpec((m_len, C_Q), lambda b: (b, 0)),
            pl.BlockSpec((m_len, C_Q), lambda b: (b, 0)),
        ],
        out_shape=[jax.ShapeDtypeStruct((n_batch * m_len, C_Q), BF16)] * 2,
        name="mem_kv",
        compiler_params=pltpu.CompilerParams(dimension_semantics=("arbitrary",)),
    )(mem2d, g, w_bf, gk)


def _proj_kernel(x_ref, pos_ref, g_ref, w_ref, invf_ref, sgn_ref, gq_ref, gk_ref, gcq_ref,
                 lng_ref, lnb_ref, bg_ref, out_ref, *, d_model):
    tm = x_ref.shape[0]
    h = _rms(x_ref[...], g_ref[...]).astype(BF16)

    ang = pos_ref[...].astype(F32) * invf_ref[...]
    cosv = jnp.cos(ang)
    sinv = jnp.sin(ang) * sgn_ref[...]
    lane = _lane_iota((tm, LANES))
    first_head = lane < HEAD_DIM
    lo_half = (lane % HEAD_DIM) < (HEAD_DIM // 2)

    def head_norm_rope(blk, g):
        y = blk * blk
        s_lo = jnp.sum(jnp.where(first_head, y, 0.0), axis=-1, keepdims=True)
        s_hi = jnp.sum(jnp.where(first_head, 0.0, y), axis=-1, keepdims=True)
        ss = jnp.where(first_head, s_lo, s_hi)
        n = blk * lax.rsqrt(ss * (1.0 / HEAD_DIM) + EPS) * g
        rot = jnp.where(lo_half, pltpu.roll(n, LANES - HEAD_DIM // 2, 1),
                        pltpu.roll(n, HEAD_DIM // 2, 1))
        return n * cosv + rot * sinv

    def proj(a, b):
        return jnp.dot(h, w_ref[:, a:b], preferred_element_type=F32)

    pq = proj(0, A_Q)
    for c in range(A_Q // LANES):
        sl = slice(c * LANES, (c + 1) * LANES)
        out_ref[:, OFF_Q + c * LANES:OFF_Q + (c + 1) * LANES] = (
            head_norm_rope(pq[:, sl], gq_ref[...]) * (HEAD_DIM ** -0.5)).astype(BF16)

    pkv = proj(A_Q, A_Q + 2 * A_KV)
    kn = head_norm_rope(pkv[:, :LANES], gk_ref[...])
    vv = pkv[:, LANES:]
    for off, t in ((OFF_K4, kn), (OFF_V4, vv)):
        tr = pltpu.roll(t, HEAD_DIM, 1)
        parts = (jnp.where(first_head, t, 0.0), jnp.where(first_head, 0.0, tr),
                 jnp.where(first_head, tr, 0.0), jnp.where(first_head, 0.0, t))
        for j, p in enumerate(parts):
            out_ref[:, off + j * LANES:off + (j + 1) * LANES] = p.astype(BF16)

    o0 = A_Q + 2 * A_KV
    out_ref[:, OFF_U:OFF_U + GMLP_WIDTH] = _gelu(proj(o0, o0 + GMLP_WIDTH)).astype(BF16)
    gv = _gelu(proj(o0 + GMLP_WIDTH, o0 + 2 * GMLP_WIDTH))
    mu = jnp.mean(gv, axis=-1, keepdims=True)
    var = jnp.mean(jnp.square(gv - mu), axis=-1, keepdims=True)
    out_ref[:, OFF_VN:OFF_VN + GMLP_WIDTH] = (
        (gv - mu) * lax.rsqrt(var + EPS) * lng_ref[...] + lnb_ref[...]).astype(BF16)

    o1 = o0 + 2 * GMLP_WIDTH
    pc = proj(o1, o1 + C_Q)
    for hh in range(X_HEADS):
        sl = slice(hh * X_HEAD_DIM, (hh + 1) * X_HEAD_DIM)
        out_ref[:, OFF_QC + hh * X_HEAD_DIM:OFF_QC + (hh + 1) * X_HEAD_DIM] = (
            _rms(pc[:, sl], gcq_ref[...]) * (X_HEAD_DIM ** -0.5)).astype(BF16)

    o2 = o1 + C_Q
    for j in range(3):
        sl = slice(j * d_model, (j + 1) * d_model)
        out_ref[:, OFF_GA + j * d_model:OFF_GA + (j + 1) * d_model] = jax.nn.sigmoid(
            proj(o2 + j * d_model, o2 + (j + 1) * d_model) + bg_ref[:, sl]).astype(BF16)


def _proj(x2d, pos2d, g, w_bf, invf, sgn, gq, gk, gcq, lng, lnb, bg, tm):
    t, d = x2d.shape
    d_in = w_bf.shape[1]
    pw = PROJ_W_BASE + 3 * d
    full = lambda shape: pl.BlockSpec(shape, lambda i: (0,) * len(shape))
    return pl.pallas_call(
        functools.partial(_proj_kernel, d_model=d),
        grid=(t // tm,),
        in_specs=[
            pl.BlockSpec((tm, d), lambda i: (i, 0)),
            pl.BlockSpec((tm, 1), lambda i: (i, 0)),
            full((1, d)), full((d, d_in)), full((1, LANES)), full((1, LANES)),
            full((1, LANES)), full((1, LANES)), full((1, X_HEAD_DIM)),
            full((1, GMLP_WIDTH)), full((1, GMLP_WIDTH)), full((1, 3 * d)),
        ],
        out_specs=pl.BlockSpec((tm, pw), lambda i: (i, 0)),
        out_shape=jax.ShapeDtypeStruct((t, pw), BF16),
        name="proj",
        compiler_params=pltpu.CompilerParams(
            dimension_semantics=("arbitrary",), vmem_limit_bytes=VMEM_LIMIT),
    )(x2d, pos2d, g, w_bf, invf, sgn, gq, gk, gcq, lng, lnb, bg)


def _mix_kernel(sinks_ref, proj_ref, kprev_ref, vprev_ref, x_ref, kc_ref, vc_ref, ws_ref, bst_ref,
                woa_ref, wob_ref, woc_ref, wout_ref, gffn_ref, rw_ref, rb_ref,
                x1_ref, h2_ref, idx_ref, rank_ref, gate_ref, cnt_ref,
                run_ref, oa_ref, ob_ref, oc_ref, *, d_model):
    ts = x_ref.shape[0]
    s_idx = pl.program_id(1)
    first_step = jnp.logical_and(pl.program_id(0) == 0, s_idx == 0)
    neg_inf = float("-inf")

    qi = lax.broadcasted_iota(I32, (ATT_BLOCK, 2 * ATT_BLOCK), 0)
    kj = lax.broadcasted_iota(I32, (ATT_BLOCK, 2 * ATT_BLOCK), 1)
    band = jnp.logical_and(kj <= ATT_BLOCK + qi, kj > qi)
    for qb in range(ts // ATT_BLOCK):
        r0 = qb * ATT_BLOCK
        rows = slice(r0, r0 + ATT_BLOCK)
        if qb == 0:
            kp, vp = kprev_ref[...], vprev_ref[...]
            mask = jnp.logical_and(band, jnp.logical_or(kj >= ATT_BLOCK, s_idx > 0))
        else:
            prow = slice(r0 - ATT_BLOCK, r0)
            kp = proj_ref[prow, OFF_K4:OFF_K4 + 4 * LANES]
            vp = proj_ref[prow, OFF_V4:OFF_V4 + 4 * LANES]
            mask = band
        k4 = jnp.concatenate([kp, proj_ref[rows, OFF_K4:OFF_K4 + 4 * LANES]], axis=0)
        v4 = jnp.concatenate([vp, proj_ref[rows, OFF_V4:OFF_V4 + 4 * LANES]], axis=0)
        for c in range(A_Q // LANES):
            kvh = (2 * c) // (N_Q_HEADS // N_KV_HEADS)
            qch = proj_ref[rows, OFF_Q + c * LANES:OFF_Q + (c + 1) * LANES]
            o = jnp.zeros((ATT_BLOCK, LANES), F32)
            for half in range(2):
                col = slice((2 * kvh + half) * LANES, (2 * kvh + half + 1) * LANES)
                s = lax.dot_general(qch, k4[:, col], (((1,), (1,)), ((), ())),
                                    preferred_element_type=F32)
                s = jnp.where(mask, s, neg_inf)
                sink = sinks_ref[2 * c + half]
                m = jnp.maximum(jnp.max(s, axis=-1, keepdims=True), sink)
                p = jnp.exp(s - m)
                den = jnp.sum(p, axis=-1, keepdims=True) + jnp.exp(sink - m)
                o = o + jnp.dot(p.astype(BF16), v4[:, col], preferred_element_type=F32) / den
            oa_ref[rows, c * LANES:(c + 1) * LANES] = o.astype(BF16)

    ti = lax.broadcasted_iota(I32, (GMLP_CHUNK, GMLP_CHUNK), 0)
    si = lax.broadcasted_iota(I32, (GMLP_CHUNK, GMLP_CHUNK), 1)
    for g in range(GMLP_GROUPS):
        wt = jnp.where(si <= ti, ws_ref[g], 0.0).astype(BF16)
        bcol = bst_ref[:, g:g + 1]
        for ch in range(ts // GMLP_CHUNK):
            rows = slice(ch * GMLP_CHUNK, (ch + 1) * GMLP_CHUNK)
            vn = proj_ref[rows, OFF_VN + g * LANES:OFF_VN + (g + 1) * LANES]
            u = proj_ref[rows, OFF_U + g * LANES:OFF_U + (g + 1) * LANES].astype(F32)
            mixed = jnp.dot(wt, vn, preferred_element_type=F32) + bcol
            ob_ref[rows, g * LANES:(g + 1) * LANES] = (u * mixed).astype(BF16)

    for hh in range(X_HEADS):
        sl = slice(hh * X_HEAD_DIM, (hh + 1) * X_HEAD_DIM)
        qc = proj_ref[:, OFF_QC + hh * X_HEAD_DIM:OFF_QC + (hh + 1) * X_HEAD_DIM]
        s = lax.dot_general(qc, kc_ref[:, sl], (((1,), (1,)), ((), ())),
                            preferred_element_type=F32)
        p = jnp.exp(s - jnp.max(s, axis=-1, keepdims=True))
        den = jnp.sum(p, axis=-1, keepdims=True)
        oc_ref[:, sl] = (jnp.dot(p.astype(BF16), vc_ref[:, sl],
                                 preferred_element_type=F32) / den).astype(BF16)

    def gate(j):
        return proj_ref[:, OFF_GA + j * d_model:OFF_GA + (j + 1) * d_model].astype(F32)

    merged = gate(0) * jnp.dot(oa_ref[...], woa_ref[...], preferred_element_type=F32)
    merged = merged + gate(1) * jnp.dot(ob_ref[...], wob_ref[...], preferred_element_type=F32)
    merged = merged + gate(2) * jnp.dot(oc_ref[...], woc_ref[...], preferred_element_type=F32)
    x1 = x_ref[...] + jnp.dot(merged.astype(BF16), wout_ref[...], preferred_element_type=F32)
    x1_ref[...] = x1

    h2 = _rms(x1, gffn_ref[...])
    for j in range(d_model // LANES):
        h2_ref[:, j, :] = h2[:, j * LANES:(j + 1) * LANES]
    lane = _lane_iota((ts, LANES))
    logits = jnp.dot(h2, rw_ref[...], precision=lax.Precision.HIGHEST,
                     preferred_element_type=F32) + rb_ref[...]
    logits = jnp.where(lane < N_EXPERTS, logits, neg_inf)
    vals, idxs = [], []
    for _ in range(TOP_K):
        m = jnp.max(logits, axis=-1, keepdims=True)
        i = jnp.min(jnp.where(logits == m, lane, LANES), axis=-1, keepdims=True)
        vals.append(m)
        idxs.append(i)
        logits = jnp.where(lane == i, neg_inf, logits)
    es = [jnp.exp(v - vals[0]) for v in vals]
    den = es[0] + es[1] + es[2] + es[3]

    @pl.when(first_step)
    def _():
        run_ref[...] = jnp.zeros_like(run_ref)

    hot = [lane == i for i in idxs]
    multihot = jnp.where(jnp.logical_or(jnp.logical_or(hot[0], hot[1]),
                                        jnp.logical_or(hot[2], hot[3])), 1.0, 0.0)
    tr = lax.broadcasted_iota(I32, (ts, ts), 0)
    tc = lax.broadcasted_iota(I32, (ts, ts), 1)
    ltri = jnp.where(tc < tr, 1.0, 0.0).astype(BF16)
    before = jnp.dot(ltri, multihot.astype(BF16), preferred_element_type=F32) + run_ref[...]
    idx_out = jnp.zeros((ts, LANES), I32)
    rank_out = jnp.zeros((ts, LANES), I32)
    gate_out = jnp.zeros((ts, LANES), F32)
    for k in range(TOP_K):
        rk = jnp.sum(jnp.where(hot[k], before, 0.0), axis=-1, keepdims=True)
        idx_out = jnp.where(lane == k, idxs[k], idx_out)
        rank_out = jnp.where(lane == k, rk.astype(I32), rank_out)
        gate_out = jnp.where(lane == k, es[k] / den, gate_out)
    idx_ref[...] = idx_out
    rank_ref[...] = rank_out
    gate_ref[...] = gate_out
    run_ref[...] = run_ref[...] + jnp.sum(multihot, axis=0, keepdims=True)
    cnt_ref[...] = run_ref[...]


def _mix(sinks, proj, x2d, kc, vc, w_s, bst, woa, wob, woc, wout, gffn, rw, rb,
         n_batch, seq, m_len, ts):
    t, d = x2d.shape
    pw = proj.shape[1]
    ns = seq // ts
    nblk = seq // ATT_BLOCK
    per = ts // ATT_BLOCK
    full = lambda shape: pl.BlockSpec(shape, lambda b, s: (0,) * len(shape))
    row = lambda width: pl.BlockSpec((ts, width), lambda b, s: (b * ns + s, 0))
    prev = lambda colblk: pl.BlockSpec(
        (ATT_BLOCK, 4 * LANES), lambda b, s: (b * nblk + jnp.maximum(s * per - 1, 0), colblk))
    return pl.pallas_call(
        functools.partial(_mix_kernel, d_model=d),
        grid=(n_batch, ns),
        in_specs=[
            pl.BlockSpec(memory_space=pltpu.SMEM),
            row(pw), prev(OFF_K4 // (4 * LANES)), prev(OFF_V4 // (4 * LANES)), row(d),
            pl.BlockSpec((m_len, C_Q), lambda b, s: (b, 0)),
            pl.BlockSpec((m_len, C_Q), lambda b, s: (b, 0)),
            full((GMLP_GROUPS, GMLP_CHUNK, GMLP_CHUNK)), full((GMLP_CHUNK, GMLP_GROUPS)),
            full((A_Q, d)), full((GMLP_WIDTH, d)), full((C_Q, d)), full((d, d)),
            full((1, d)), full((d, LANES)), full((1, LANES)),
        ],
        out_specs=[row(d), pl.BlockSpec((ts, d // LANES, LANES), lambda b, s: (b * ns + s, 0, 0)),
                   row(LANES), row(LANES), row(LANES), full((1, LANES))],
        out_shape=[
            jax.ShapeDtypeStruct((t, d), F32), jax.ShapeDtypeStruct((t, d // LANES, LANES), F32),
            jax.ShapeDtypeStruct((t, LANES), I32), jax.ShapeDtypeStruct((t, LANES), I32),
            jax.ShapeDtypeStruct((t, LANES), F32), jax.ShapeDtypeStruct((1, LANES), F32),
        ],
        scratch_shapes=[
            pltpu.VMEM((1, LANES), F32),
            pltpu.VMEM((ts, A_Q), BF16), pltpu.VMEM((ts, GMLP_WIDTH), BF16),
            pltpu.VMEM((ts, C_Q), BF16),
        ],
        name="mix",
        compiler_params=pltpu.CompilerParams(
            dimension_semantics=("arbitrary", "arbitrary"), vmem_limit_bytes=VMEM_LIMIT),
    )(sinks, proj, proj, proj, x2d, kc, vc, w_s, bst, woa, wob, woc, wout, gffn, rw, rb)


def _pos_kernel(pstart_ref, idx_ref, rank_ref, pos_ref):
    idx = idx_ref[...]
    pos = rank_ref[...]
    for e in range(N_EXPERTS):
        pos = pos + jnp.where(idx == e, pstart_ref[e], 0)
    pos_ref[...] = pos


def _pos(pstart, idx, rank, tp):
    t = idx.shape[0]
    return pl.pallas_call(
        _pos_kernel,
        grid_spec=pltpu.PrefetchScalarGridSpec(
            num_scalar_prefetch=1,
            grid=(t // tp,),
            in_specs=[pl.BlockSpec((tp, LANES), lambda i, ps: (i, 0)),
                      pl.BlockSpec((tp, LANES), lambda i, ps: (i, 0))],
            out_specs=pl.BlockSpec((tp, LANES), lambda i, ps: (i, 0)),
        ),
        out_shape=jax.ShapeDtypeStruct((t, LANES), I32),
        name="slot_pos",
        compiler_params=pltpu.CompilerParams(dimension_semantics=("arbitrary",)),
    )(pstart, idx, rank)


def _dispatch_kernel(zstart_ref, zflag_ref, pos_ref, h2_ref, xs_ref, zbuf_ref, zsem, sem, *, tsd):
    i = pl.program_id(0)

    @pl.when(i == 0)
    def _():
        zbuf_ref[...] = jnp.zeros_like(zbuf_ref)
        for e in range(N_EXPERTS):
            @pl.when(zflag_ref[e] > 0)
            def _():
                cp = pltpu.make_async_copy(
                    zbuf_ref, xs_ref.at[pl.ds(zstart_ref[e], MOE_BLOCK)], zsem)
                cp.start()
                cp.wait()

    def row_copy(r, k):
        return pltpu.make_async_copy(
            h2_ref.at[i * tsd + r], xs_ref.at[pos_ref[TOP_K * r + k]], sem)

    def start(r, c):
        for k in range(TOP_K):
            row_copy(r, k).start()
        return c

    def wait(r, c):
        for k in range(TOP_K):
            row_copy(r, k).wait()
        return c

    lax.fori_loop(0, tsd, start, 0)
    lax.fori_loop(0, tsd, wait, 0)


def _dispatch(zstart, zflag, pos1d, h2, n_rows, tsd):
    t, nch, _ = h2.shape
    return pl.pallas_call(
        functools.partial(_dispatch_kernel, tsd=tsd),
        grid_spec=pltpu.PrefetchScalarGridSpec(
            num_scalar_prefetch=2,
            grid=(t // tsd,),
            in_specs=[
                pl.BlockSpec((tsd * TOP_K,), lambda i, a, b: (i,), memory_space=pltpu.SMEM),
                pl.BlockSpec(memory_space=pl.ANY),
            ],
            out_specs=pl.BlockSpec(memory_space=pl.ANY),
            scratch_shapes=[pltpu.VMEM((MOE_BLOCK, nch, LANES), F32),
                            pltpu.SemaphoreType.DMA, pltpu.SemaphoreType.DMA],
        ),
        out_shape=jax.ShapeDtypeStruct((n_rows, nch, LANES), F32),
        name="dispatch",
        compiler_params=pltpu.CompilerParams(dimension_semantics=("arbitrary",)),
    )(zstart, zflag, pos1d, h2)


def _expert_kernel(bexp_ref, nvalid_ref, xs_ref, wgu_ref, bgu_ref, wd_ref, bd_ref, ys_ref, *, d_exp):
    @pl.when(pl.program_id(0) < nvalid_ref[0])
    def _():
        xb = jnp.concatenate([xs_ref[:, j, :] for j in range(xs_ref.shape[1])],
                             axis=1).astype(BF16)
        gu = jnp.dot(xb, wgu_ref[...], preferred_element_type=F32) + bgu_ref[...]
        gate = jnp.minimum(gu[:, :d_exp], SWIGLU_LIMIT)
        up = jnp.clip(gu[:, d_exp:], -SWIGLU_LIMIT, SWIGLU_LIMIT)
        glu = gate * jax.nn.sigmoid(gate * SWIGLU_ALPHA)
        act = ((up + 1.0) * glu).astype(BF16)
        y = jnp.dot(act, wd_ref[...], preferred_element_type=F32) + bd_ref[...]
        for j in range(ys_ref.shape[1]):
            ys_ref[:, j, :] = y[:, j * LANES:(j + 1) * LANES]


def _experts(bexp, nvalid, xs, wgu_bf, bgu, wd_bf, bd):
    n_rows, nch, _ = xs.shape
    d = nch * LANES
    d_exp = wd_bf.shape[1]
    nb = n_rows // MOE_BLOCK
    blk = lambda b, be, nv: jnp.minimum(b, nv[0] - 1)
    return pl.pallas_call(
        functools.partial(_expert_kernel, d_exp=d_exp),
        grid_spec=pltpu.PrefetchScalarGridSpec(
            num_scalar_prefetch=2,
            grid=(nb,),
            in_specs=[
                pl.BlockSpec((MOE_BLOCK, nch, LANES), lambda b, be, nv: (blk(b, be, nv), 0, 0)),
                pl.BlockSpec((None, d, 2 * d_exp), lambda b, be, nv: (be[blk(b, be, nv)], 0, 0)),
                pl.BlockSpec((None, 1, 2 * d_exp), lambda b, be, nv: (be[blk(b, be, nv)], 0, 0)),
                pl.BlockSpec((None, d_exp, d), lambda b, be, nv: (be[blk(b, be, nv)], 0, 0)),
                pl.BlockSpec((None, 1, d), lambda b, be, nv: (be[blk(b, be, nv)], 0, 0)),
            ],
            out_specs=pl.BlockSpec((MOE_BLOCK, nch, LANES),
                                   lambda b, be, nv: (blk(b, be, nv), 0, 0)),
        ),
        out_shape=jax.ShapeDtypeStruct((n_rows, nch, LANES), F32),
        name="experts",
        compiler_params=pltpu.CompilerParams(
            dimension_semantics=("arbitrary",), vmem_limit_bytes=VMEM_LIMIT),
    )(bexp, nvalid, xs, wgu_bf, bgu, wd_bf, bd)


def _combine_kernel(pos_ref, x1_ref, gate_ref, ys_ref, out_ref, buf_ref, sem, *, tcb):
    def row_copy(r, k):
        return pltpu.make_async_copy(
            ys_ref.at[pos_ref[TOP_K * r + k]], buf_ref.at[k, r], sem)

    def start(r, c):
        for k in range(TOP_K):
            row_copy(r, k).start()
        return c

    def wait(r, c):
        for k in range(TOP_K):
            row_copy(r, k).wait()
        return c

    lax.fori_loop(0, tcb, start, 0)
    lax.fori_loop(0, tcb, wait, 0)
    for j in range(buf_ref.shape[2]):
        sl = slice(j * LANES, (j + 1) * LANES)
        acc = x1_ref[:, sl]
        for k in range(TOP_K):
            acc = acc + gate_ref[:, k:k + 1] * buf_ref[k, :, j, :]
        out_ref[:, sl] = acc


def _combine(pos1d, x1, gates, ys, tcb):
    t, d = x1.shape
    return pl.pallas_call(
        functools.partial(_combine_kernel, tcb=tcb),
        grid=(t // tcb,),
        in_specs=[
            pl.BlockSpec((tcb * TOP_K,), lambda i: (i,), memory_space=pltpu.SMEM),
            pl.BlockSpec((tcb, d), lambda i: (i, 0)),
            pl.BlockSpec((tcb, LANES), lambda i: (i, 0)),
            pl.BlockSpec(memory_space=pl.ANY),
        ],
        out_specs=pl.BlockSpec((tcb, d), lambda i: (i, 0)),
        out_shape=jax.ShapeDtypeStruct((t, d), F32),
        scratch_shapes=[pltpu.VMEM((TOP_K, tcb, d // LANES, LANES), F32),
                        pltpu.SemaphoreType.DMA],
        name="combine",
        compiler_params=pltpu.CompilerParams(dimension_semantics=("arbitrary",)),
    )(pos1d, x1, gates, ys)


def _pick_tile(n, pref):
    t = min(pref, n)
    while n % t:
        t //= 2
    return t


def kernel(x, mem, positions, attn_norm_g, mem_norm_g, w_in, b_gates, a_q_norm_g, a_k_norm_g,
           a_sinks, w_o_a, gmlp_ln_g, gmlp_ln_b, gmlp_w_s, gmlp_b_s, w_o_b, w_mem_kv,
           c_q_norm_g, c_k_norm_g, w_o_c, w_out, ffn_norm_g, router_w, router_b,
           w_gate_up, b_gate_up, w_down, b_down):
    n_batch, seq, d = x.shape
    m_len = mem.shape[1]
    depth = w_in.shape[0]
    t = n_batch * seq
    n_asg = t * TOP_K
    nb = -(-n_asg // MOE_BLOCK) + N_EXPERTS
    n_rows = nb * MOE_BLOCK

    inv_freq = ROPE_THETA ** (-jnp.arange(0, HEAD_DIM, 2, dtype=F32) / HEAD_DIM)
    invf = jnp.tile(inv_freq, LANES // (HEAD_DIM // 2))[None, :]
    sgn = jnp.tile(jnp.concatenate([-jnp.ones((HEAD_DIM // 2,), F32),
                                    jnp.ones((HEAD_DIM // 2,), F32)]), LANES // HEAD_DIM)[None, :]
    pos2d = positions.reshape(t, 1).astype(I32)
    mem2d = mem.reshape(n_batch * m_len, d)
    x2d = x.reshape(t, d)

    tm = _pick_tile(t, 256)
    ts = _pick_tile(seq, 256)
    tp = _pick_tile(t, 2048)
    tsd = _pick_tile(t, 512)
    tcb = _pick_tile(t, 256)

    for l in range(depth):
        kc, vc = _mem_kv(mem2d, mem_norm_g[l][None, :], w_mem_kv[l].astype(BF16),
                         c_k_norm_g[l][None, :], n_batch, m_len)
        proj = _proj(
            x2d, pos2d, attn_norm_g[l][None, :], w_in[l].astype(BF16), invf, sgn,
            jnp.tile(a_q_norm_g[l], LANES // HEAD_DIM)[None, :],
            jnp.tile(a_k_norm_g[l], LANES // HEAD_DIM)[None, :],
            c_q_norm_g[l][None, :], gmlp_ln_g[l][None, :], gmlp_ln_b[l][None, :],
            b_gates[l].reshape(1, 3 * d), tm)
        rw = jnp.pad(router_w[l], ((0, 0), (0, LANES - N_EXPERTS)))
        rb = jnp.pad(router_b[l], (0, LANES - N_EXPERTS))[None, :]
        x1, h2, idx, rank, gates, counts = _mix(
            a_sinks[l], proj, x2d, kc, vc, gmlp_w_s[l], gmlp_b_s[l].T,
            w_o_a[l].astype(BF16), w_o_b[l].astype(BF16), w_o_c[l].astype(BF16),
            w_out[l].astype(BF16), ffn_norm_g[l][None, :], rw, rb, n_batch, seq, m_len, ts)

        cnt = counts[0, :N_EXPERTS].astype(I32)
        padded = (cnt + MOE_BLOCK - 1) // MOE_BLOCK * MOE_BLOCK
        pend = jnp.cumsum(padded)
        pstart = (pend - padded).astype(I32)
        nvalid = (pend[-1:] // MOE_BLOCK).astype(I32)
        bexp = jnp.minimum(
            jnp.searchsorted(pend, jnp.arange(nb, dtype=I32) * MOE_BLOCK, side="right"),
            N_EXPERTS - 1).astype(I32)
        zstart = jnp.maximum(pend - MOE_BLOCK, 0).astype(I32)
        zflag = (cnt > 0).astype(I32)

        pos = _pos(pstart, idx, rank, tp)
        pos1d = pos[:, :TOP_K].reshape(n_asg)
        xs = _dispatch(zstart, zflag, pos1d, h2, n_rows, tsd)
        ys = _experts(bexp, nvalid, xs, w_gate_up[l].astype(BF16), b_gate_up[l][:, None, :],
                      w_down[l].astype(BF16), b_down[l][:, None, :])
        x2d = _combine(pos1d, x1, gates, ys, tcb)
    return x2d.reshape(n_batch, seq, d)
```

```python
import functools

import numpy as np
import jax
import jax.numpy as jnp
from jax import lax
from jax.experimental import pallas as pl
from jax.experimental.pallas import tpu as pltpu

F32 = jnp.float32
BF16 = jnp.bfloat16
I32 = jnp.int32

EPS = 1e-6
LANES = 128
HEAD_DIM = 64
N_Q_HEADS = 16
N_KV_HEADS = 2
ATT_BLOCK = 128
ROPE_THETA = 10000.0
GMLP_WIDTH = 512
GMLP_GROUPS = 4
GMLP_CHUNK = 128
X_HEADS = 4
X_HEAD_DIM = 128
N_EXPERTS = 32
TOP_K = 4
SWIGLU_LIMIT = 7.0
SWIGLU_ALPHA = 1.702
MOE_BLOCK = 512

A_Q = N_Q_HEADS * HEAD_DIM
A_KV = N_KV_HEADS * HEAD_DIM
C_Q = X_HEADS * X_HEAD_DIM

OFF_Q = 0
OFF_K4 = OFF_Q + A_Q
OFF_V4 = OFF_K4 + 4 * LANES
OFF_U = OFF_V4 + 4 * LANES
OFF_VN = OFF_U + GMLP_WIDTH
OFF_QC = OFF_VN + GMLP_WIDTH
OFF_GA = OFF_QC + C_Q
PROJ_W_BASE = OFF_GA

VMEM_LIMIT = 56 * 1024 * 1024


def _lane_iota(shape):
    return lax.broadcasted_iota(I32, shape, len(shape) - 1)


def _rms(x, g):
    return x * lax.rsqrt(jnp.mean(x * x, axis=-1, keepdims=True) + EPS) * g


def _gelu(x):
    return 0.5 * x * (1.0 + lax.erf(x * np.float32(np.sqrt(0.5))))


def _memkv_kernel(mem_ref, g_ref, w_ref, gk_ref, kc_ref, vc_ref):
    h = _rms(mem_ref[...], g_ref[...]).astype(BF16)
    kv = jnp.dot(h, w_ref[...], preferred_element_type=F32)
    for hh in range(X_HEADS):
        sl = slice(hh * X_HEAD_DIM, (hh + 1) * X_HEAD_DIM)
        kc_ref[:, sl] = _rms(kv[:, sl], gk_ref[...]).astype(BF16)
    vc_ref[...] = kv[:, C_Q:].astype(BF16)


def _mem_kv(mem2d, g, w_bf, gk, n_batch, m_len):
    d = mem2d.shape[1]
    return pl.pallas_call(
        _memkv_kernel,
        grid=(n_batch,),
        in_specs=[
            pl.BlockSpec((m_len, d), lambda b: (b, 0)),
            pl.BlockSpec((1, d), lambda b: (0, 0)),
            pl.BlockSpec((d, 2 * C_Q), lambda b: (0, 0)),
            pl.BlockSpec((1, X_HEAD_DIM), lambda b: (0, 0)),
        ],
        out_specs=[
            pl.BlockSpec((m_len, C_Q), lambda b: (b, 0)),
            pl.BlockSpec((m_len, C_Q), lambda b: (b, 0)),
        ],
        out_shape=[jax.ShapeDtypeStruct((n_batch * m_len, C_Q), BF16)] * 2,
        name="mem_kv",
        compiler_params=pltpu.CompilerParams(dimension_semantics=("arbitrary",)),
    )(mem2d, g, w_bf, gk)


def _proj_kernel(x_ref, pos_ref, g_ref, w_ref, invf_ref, sgn_ref, gq_ref, gk_ref, gcq_ref,
                 lng_ref, lnb_ref, bg_ref, out_ref, *, d_model):
    tm = x_ref.shape[0]
    h = _rms(x_ref[...], g_ref[...]).astype(BF16)

    ang = pos_ref[...].astype(F32) * invf_ref[...]
    cosv = jnp.cos(ang)
    sinv = jnp.sin(ang) * sgn_ref[...]
    lane = _lane_iota((tm, LANES))
    first_head = lane < HEAD_DIM
    lo_half = (lane % HEAD_DIM) < (HEAD_DIM // 2)

    def head_norm_rope(blk, g):
        y = blk * blk
        s_lo = jnp.sum(jnp.where(first_head, y, 0.0), axis=-1, keepdims=True)
        s_hi = jnp.sum(jnp.where(first_head, 0.0, y), axis=-1, keepdims=True)
        ss = jnp.where(first_head, s_lo, s_hi)
        n = blk * lax.rsqrt(ss * (1.0 / HEAD_DIM) + EPS) * g
        rot = jnp.where(lo_half, pltpu.roll(n, LANES - HEAD_DIM // 2, 1),
                        pltpu.roll(n, HEAD_DIM // 2, 1))
        return n * cosv + rot * sinv

    def proj(a, b):
        return jnp.dot(h, w_ref[:, a:b], preferred_element_type=F32)

    pq = proj(0, A_Q)
    for c in range(A_Q // LANES):
        sl = slice(c * LANES, (c + 1) * LANES)
        out_ref[:, OFF_Q + c * LANES:OFF_Q + (c + 1) * LANES] = (
            head_norm_rope(pq[:, sl], gq_ref[...]) * (HEAD_DIM ** -0.5)).astype(BF16)

    pkv = proj(A_Q, A_Q + 2 * A_KV)
    kn = head_norm_rope(pkv[:, :LANES], gk_ref[...])
    vv = pkv[:, LANES:]
    for off, t in ((OFF_K4, kn), (OFF_V4, vv)):
        tr = pltpu.roll(t, HEAD_DIM, 1)
        parts = (jnp.where(first_head, t, 0.0), jnp.where(first_head, 0.0, tr),
                 jnp.where(first_head, tr, 0.0), jnp.where(first_head, 0.0, t))
        for j, p in enumerate(parts):
            out_ref[:, off + j * LANES:off + (j + 1) * LANES] = p.astype(BF16)

    o0 = A_Q + 2 * A_KV
    out_ref[:, OFF_U:OFF_U + GMLP_WIDTH] = _gelu(proj(o0, o0 + GMLP_WIDTH)).astype(BF16)
    gv = _gelu(proj(o0 + GMLP_WIDTH, o0 + 2 * GMLP_WIDTH))
    mu = jnp.mean(gv, axis=-1, keepdims=True)
    var = jnp.mean(jnp.square(gv - mu), axis=-1, keepdims=True)
    out_ref[:, OFF_VN:OFF_VN + GMLP_WIDTH] = (
        (gv - mu) * lax.rsqrt(var + EPS) * lng_ref[...] + lnb_ref[...]).astype(BF16)

    o1 = o0 + 2 * GMLP_WIDTH
    pc = proj(o1, o1 + C_Q)
    for hh in range(X_HEADS):
        sl = slice(hh * X_HEAD_DIM, (hh + 1) * X_HEAD_DIM)
        out_ref[:, OFF_QC + hh * X_HEAD_DIM:OFF_QC + (hh + 1) * X_HEAD_DIM] = (
            _rms(pc[:, sl], gcq_ref[...]) * (X_HEAD_DIM ** -0.5)).astype(BF16)

    o2 = o1 + C_Q
    for j in range(3):
        sl = slice(j * d_model, (j + 1) * d_model)
        out_ref[:, OFF_GA + j * d_model:OFF_GA + (j + 1) * d_model] = jax.nn.sigmoid(
            proj(o2 + j * d_model, o2 + (j + 1) * d_model) + bg_ref[:, sl]).astype(BF16)


def _proj(x2d, pos2d, g, w_bf, invf, sgn, gq, gk, gcq, lng, lnb, bg, tm):
    t, d = x2d.shape
    d_in = w_bf.shape[1]
    pw = PROJ_W_BASE + 3 * d
    full = lambda shape: pl.BlockSpec(shape, lambda i: (0,) * len(shape))
    return pl.pallas_call(
        functools.partial(_proj_kernel, d_model=d),
        grid=(t // tm,),
        in_specs=[
            pl.BlockSpec((tm, d), lambda i: (i, 0)),
            pl.BlockSpec((tm, 1), lambda i: (i, 0)),
            full((1, d)), full((d, d_in)), full((1, LANES)), full((1, LANES)),
            full((1, LANES)), full((1, LANES)), full((1, X_HEAD_DIM)),
            full((1, GMLP_WIDTH)), full((1, GMLP_WIDTH)), full((1, 3 * d)),
        ],
        out_specs=pl.BlockSpec((tm, pw), lambda i: (i, 0)),
        out_shape=jax.ShapeDtypeStruct((t, pw), BF16),
        name="proj",
        compiler_params=pltpu.CompilerParams(
            dimension_semantics=("arbitrary",), vmem_limit_bytes=VMEM_LIMIT),
    )(x2d, pos2d, g, w_bf, invf, sgn, gq, gk, gcq, lng, lnb, bg)


def _mix_kernel(sinks_ref, proj_ref, kprev_ref, vprev_ref, x_ref, kc_ref, vc_ref, ws_ref, bst_ref,
                woa_ref, wob_ref, woc_ref, wout_ref, gffn_ref, rw_ref, rb_ref,
                x1_ref, h2_ref, idx_ref, rank_ref, gate_ref, cnt_ref,
                run_ref, oa_ref, ob_ref, oc_ref, *, d_model):
    ts = x_ref.shape[0]
    s_idx = pl.program_id(1)
    first_step = jnp.logical_and(pl.program_id(0) == 0, s_idx == 0)
    neg_inf = float("-inf")

    qi = lax.broadcasted_iota(I32, (ATT_BLOCK, 2 * ATT_BLOCK), 0)
    kj = lax.broadcasted_iota(I32, (ATT_BLOCK, 2 * ATT_BLOCK), 1)
    band = jnp.logical_and(kj <= ATT_BLOCK + qi, kj > qi)
    for qb in range(ts // ATT_BLOCK):
        r0 = qb * ATT_BLOCK
        rows = slice(r0, r0 + ATT_BLOCK)
        if qb == 0:
            kp, vp = kprev_ref[...], vprev_ref[...]
            mask = jnp.logical_and(band, jnp.logical_or(kj >= ATT_BLOCK, s_idx > 0))
        else:
            prow = slice(r0 - ATT_BLOCK, r0)
            kp = proj_ref[prow, OFF_K4:OFF_K4 + 4 * LANES]
            vp = proj_ref[prow, OFF_V4:OFF_V4 + 4 * LANES]
            mask = band
        k4 = jnp.concatenate([kp, proj_ref[rows, OFF_K4:OFF_K4 + 4 * LANES]], axis=0)
        v4 = jnp.concatenate([vp, proj_ref[rows, OFF_V4:OFF_V4 + 4 * LANES]], axis=0)
        for c in range(A_Q // LANES):
            kvh = (2 * c) // (N_Q_HEADS // N_KV_HEADS)
            qch = proj_ref[rows, OFF_Q + c * LANES:OFF_Q + (c + 1) * LANES]
            o = jnp.zeros((ATT_BLOCK, LANES), F32)
            for half in range(2):
                col = slice((2 * kvh + half) * LANES, (2 * kvh + half + 1) * LANES)
                s = lax.dot_general(qch, k4[:, col], (((1,), (1,)), ((), ())),
                                    preferred_element_type=F32)
                s = jnp.where(mask, s, neg_inf)
                sink = sinks_ref[2 * c + half]
                m = jnp.maximum(jnp.max(s, axis=-1, keepdims=True), sink)
                p = jnp.exp(s - m)
                den = jnp.sum(p, axis=-1, keepdims=True) + jnp.exp(sink - m)
                o = o + jnp.dot(p.astype(BF16), v4[:, col], preferred_element_type=F32) / den
            oa_ref[rows, c * LANES:(c + 1) * LANES] = o.astype(BF16)

    ti = lax.broadcasted_iota(I32, (GMLP_CHUNK, GMLP_CHUNK), 0)
    si = lax.broadcasted_iota(I32, (GMLP_CHUNK, GMLP_CHUNK), 1)
    for g in range(GMLP_GROUPS):
        wt = jnp.where(si <= ti, ws_ref[g], 0.0).astype(BF16)
        bcol = bst_ref[:, g:g + 1]
        for ch in range(ts // GMLP_CHUNK):
            rows = slice(ch * GMLP_CHUNK, (ch + 1) * GMLP_CHUNK)
            vn = proj_ref[rows, OFF_VN + g * LANES:OFF_VN + (g + 1) * LANES]
            u = proj_ref[rows, OFF_U + g * LANES:OFF_U + (g + 1) * LANES].astype(F32)
            mixed = jnp.dot(wt, vn, preferred_element_type=F32) + bcol
            ob_ref[rows, g * LANES:(g + 1) * LANES] = (u * mixed).astype(BF16)

    for hh in range(X_HEADS):
        sl = slice(hh * X_HEAD_DIM, (hh + 1) * X_HEAD_DIM)
        qc = proj_ref[:, OFF_QC + hh * X_HEAD_DIM:OFF_QC + (hh + 1) * X_HEAD_DIM]
        s = lax.dot_general(qc, kc_ref[:, sl], (((1,), (1,)), ((), ())),
                            preferred_element_type=F32)
        p = jnp.exp(s - jnp.max(s, axis=-1, keepdims=True))
        den = jnp.sum(p, axis=-1, keepdims=True)
        oc_ref[:, sl] = (jnp.dot(p.astype(BF16), vc_ref[:, sl],
                                 preferred_element_type=F32) / den).astype(BF16)

    def gate(j):
        return proj_ref[:, OFF_GA + j * d_model:OFF_GA + (j + 1) * d_model].astype(F32)

    merged = gate(0) * jnp.dot(oa_ref[...], woa_ref[...], preferred_element_type=F32)
    merged = merged + gate(1) * jnp.dot(ob_ref[...], wob_ref[...], preferred_element_type=F32)
    merged = merged + gate(2) * jnp.dot(oc_ref[...], woc_ref[...], preferred_element_type=F32)
    x1 = x_ref[...] + jnp.dot(merged.astype(BF16), wout_ref[...], preferred_element_type=F32)
    x1_ref[...] = x1

    h2 = _rms(x1, gffn_ref[...])
    for j in range(d_model // LANES):
        h2_ref[:, j, :] = h2[:, j * LANES:(j + 1) * LANES]
    lane = _lane_iota((ts, LANES))
    logits = jnp.dot(h2, rw_ref[...], precision=lax.Precision.HIGHEST,
                     preferred_element_type=F32) + rb_ref[...]
    logits = jnp.where(lane < N_EXPERTS, logits, neg_inf)
    vals, idxs = [], []
    for _ in range(TOP_K):
        m = jnp.max(logits, axis=-1, keepdims=True)
        i = jnp.min(jnp.where(logits == m, lane, LANES), axis=-1, keepdims=True)
        vals.append(m)
        idxs.append(i)
        logits = jnp.where(lane == i, neg_inf, logits)
    es = [jnp.exp(v - vals[0]) for v in vals]
    den = es[0] + es[1] + es[2] + es[3]

    @pl.when(first_step)
    def _():
        run_ref[...] = jnp.zeros_like(run_ref)

    hot = [lane == i for i in idxs]
    multihot = jnp.where(jnp.logical_or(jnp.logical_or(hot[0], hot[1]),
                                        jnp.logical_or(hot[2], hot[3])), 1.0, 0.0)
    tr = lax.broadcasted_iota(I32, (ts, ts), 0)
    tc = lax.broadcasted_iota(I32, (ts, ts), 1)
    ltri = jnp.where(tc < tr, 1.0, 0.0).astype(BF16)
    before = jnp.dot(ltri, multihot.astype(BF16), preferred_element_type=F32) + run_ref[...]
    idx_out = jnp.zeros((ts, LANES), I32)
    rank_out = jnp.zeros((ts, LANES), I32)
    gate_out = jnp.zeros((ts, LANES), F32)
    for k in range(TOP_K):
        rk = jnp.sum(jnp.where(hot[k], before, 0.0), axis=-1, keepdims=True)
        idx_out = jnp.where(lane == k, idxs[k], idx_out)
        rank_out = jnp.where(lane == k, rk.astype(I32), rank_out)
        gate_out = jnp.where(lane == k, es[k] / den, gate_out)
    idx_ref[...] = idx_out
    rank_ref[...] = rank_out
    gate_ref[...] = gate_out
    run_ref[...] = run_ref[...] + jnp.sum(multihot, axis=0, keepdims=True)
    cnt_ref[...] = run_ref[...]


def _mix(sinks, proj, x2d, kc, vc, w_s, bst, woa, wob, woc, wout, gffn, rw, rb,
         n_batch, seq, m_len, ts):
    t, d = x2d.shape
    pw = proj.shape[1]
    ns = seq // ts
    nblk = seq // ATT_BLOCK
    per = ts // ATT_BLOCK
    full = lambda shape: pl.BlockSpec(shape, lambda b, s: (0,) * len(shape))
    row = lambda width: pl.BlockSpec((ts, width), lambda b, s: (b * ns + s, 0))
    prev = lambda colblk: pl.BlockSpec(
        (ATT_BLOCK, 4 * LANES), lambda b, s: (b * nblk + jnp.maximum(s * per - 1, 0), colblk))
    return pl.pallas_call(
        functools.partial(_mix_kernel, d_model=d),
        grid=(n_batch, ns),
        in_specs=[
            pl.BlockSpec(memory_space=pltpu.SMEM),
            row(pw), prev(OFF_K4 // (4 * LANES)), prev(OFF_V4 // (4 * LANES)), row(d),
            pl.BlockSpec((m_len, C_Q), lambda b, s: (b, 0)),
            pl.BlockSpec((m_len, C_Q), lambda b, s: (b, 0)),
            full((GMLP_GROUPS, GMLP_CHUNK, GMLP_CHUNK)), full((GMLP_CHUNK, GMLP_GROUPS)),
            full((A_Q, d)), full((GMLP_WIDTH, d)), full((C_Q, d)), full((d, d)),
            full((1, d)), full((d, LANES)), full((1, LANES)),
        ],
        out_specs=[row(d), pl.BlockSpec((ts, d // LANES, LANES), lambda b, s: (b * ns + s, 0, 0)),
                   row(LANES), row(LANES), row(LANES), full((1, LANES))],
        out_shape=[
            jax.ShapeDtypeStruct((t, d), F32), jax.ShapeDtypeStruct((t, d // LANES, LANES), F32),
            jax.ShapeDtypeStruct((t, LANES), I32), jax.ShapeDtypeStruct((t, LANES), I32),
            jax.ShapeDtypeStruct((t, LANES), F32), jax.ShapeDtypeStruct((1, LANES), F32),
        ],
        scratch_shapes=[
            pltpu.VMEM((1, LANES), F32),
            pltpu.VMEM((ts, A_Q), BF16), pltpu.VMEM((ts, GMLP_WIDTH), BF16),
            pltpu.VMEM((ts, C_Q), BF16),
        ],
        name="mix",
        compiler_params=pltpu.CompilerParams(
            dimension_semantics=("arbitrary", "arbitrary"), vmem_limit_bytes=VMEM_LIMIT),
    )(sinks, proj, proj, proj, x2d, kc, vc, w_s, bst, woa, wob, woc, wout, gffn, rw, rb)


def _pos_kernel(pstart_ref, idx_ref, rank_ref, pos_ref):
    idx = idx_ref[...]
    pos = rank_ref[...]
    for e in range(N_EXPERTS):
        pos = pos + jnp.where(idx == e, pstart_ref[e], 0)
    pos_ref[...] = pos


def _pos(pstart, idx, rank, tp):
    t = idx.shape[0]
    return pl.pallas_call(
        _pos_kernel,
        grid_spec=pltpu.PrefetchScalarGridSpec(
            num_scalar_prefetch=1,
            grid=(t // tp,),
            in_specs=[pl.BlockSpec((tp, LANES), lambda i, ps: (i, 0)),
                      pl.BlockSpec((tp, LANES), lambda i, ps: (i, 0))],
            out_specs=pl.BlockSpec((tp, LANES), lambda i, ps: (i, 0)),
        ),
        out_shape=jax.ShapeDtypeStruct((t, LANES), I32),
        name="slot_pos",
        compiler_params=pltpu.CompilerParams(dimension_semantics=("arbitrary",)),
    )(pstart, idx, rank)


def _dispatch_kernel(zstart_ref, zflag_ref, pos_ref, h2_ref, xs_ref, zbuf_ref, zsem, sem, *, tsd):
    i = pl.program_id(0)

    @pl.when(i == 0)
    def _():
        zbuf_ref[...] = jnp.zeros_like(zbuf_ref)
        for e in range(N_EXPERTS):
            @pl.when(zflag_ref[e] > 0)
            def _():
                cp = pltpu.make_async_copy(
                    zbuf_ref, xs_ref.at[pl.ds(zstart_ref[e], MOE_BLOCK)], zsem)
                cp.start()
                cp.wait()

    def row_copy(r, k):
        return pltpu.make_async_copy(h2_ref.at[r], xs_ref.at[pos_ref[TOP_K * r + k]], sem)

    def start(r, c):
        for k in range(TOP_K):
            row_copy(r, k).start()
        return c

    def wait(r, c):
        for k in range(TOP_K):
            row_copy(r, k).wait()
        return c

    lax.fori_loop(0, tsd, start, 0)
    lax.fori_loop(0, tsd, wait, 0)


def _dispatch(zstart, zflag, pos1d, h2, n_rows, tsd):
    t, nch, _ = h2.shape
    return pl.pallas_call(
        functools.partial(_dispatch_kernel, tsd=tsd),
        grid_spec=pltpu.PrefetchScalarGridSpec(
            num_scalar_prefetch=2,
            grid=(t // tsd,),
            in_specs=[
                pl.BlockSpec((tsd * TOP_K,), lambda i, a, b: (i,), memory_space=pltpu.SMEM),
                pl.BlockSpec((tsd, nch, LANES), lambda i, a, b: (i, 0, 0)),
            ],
            out_specs=pl.BlockSpec(memory_space=pl.ANY),
            scratch_shapes=[pltpu.VMEM((MOE_BLOCK, nch, LANES), F32),
                            pltpu.SemaphoreType.DMA, pltpu.SemaphoreType.DMA],
        ),
        out_shape=jax.ShapeDtypeStruct((n_rows, nch, LANES), F32),
        name="dispatch",
        compiler_params=pltpu.CompilerParams(dimension_semantics=("arbitrary",)),
    )(zstart, zflag, pos1d, h2)


def _expert_kernel(bexp_ref, nvalid_ref, xs_ref, wgu_ref, bgu_ref, wd_ref, bd_ref, ys_ref, *, d_exp):
    @pl.when(pl.program_id(0) < nvalid_ref[0])
    def _():
        xb = jnp.concatenate([xs_ref[:, j, :] for j in range(xs_ref.shape[1])],
                             axis=1).astype(BF16)
        gu = jnp.dot(xb, wgu_ref[...], preferred_element_type=F32) + bgu_ref[...]
        gate = jnp.minimum(gu[:, :d_exp], SWIGLU_LIMIT)
        up = jnp.clip(gu[:, d_exp:], -SWIGLU_LIMIT, SWIGLU_LIMIT)
        glu = gate * jax.nn.sigmoid(gate * SWIGLU_ALPHA)
        act = ((up + 1.0) * glu).astype(BF16)
        y = jnp.dot(act, wd_ref[...], preferred_element_type=F32) + bd_ref[...]
        for j in range(ys_ref.shape[1]):
            ys_ref[:, j, :] = y[:, j * LANES:(j + 1) * LANES]


def _experts(bexp, nvalid, xs, wgu_bf, bgu, wd_bf, bd):
    n_rows, nch, _ = xs.shape
    d = nch * LANES
    d_exp = wd_bf.shape[1]
    nb = n_rows // MOE_BLOCK
    blk = lambda b, be, nv: jnp.minimum(b, nv[0] - 1)
    return pl.pallas_call(
        functools.partial(_expert_kernel, d_exp=d_exp),
        grid_spec=pltpu.PrefetchScalarGridSpec(
            num_scalar_prefetch=2,
            grid=(nb,),
            in_specs=[
                pl.BlockSpec((MOE_BLOCK, nch, LANES), lambda b, be, nv: (blk(b, be, nv), 0, 0)),
                pl.BlockSpec((None, d, 2 * d_exp), lambda b, be, nv: (be[blk(b, be, nv)], 0, 0)),
                pl.BlockSpec((None, 1, 2 * d_exp), lambda b, be, nv: (be[blk(b, be, nv)], 0, 0)),
                pl.BlockSpec((None, d_exp, d), lambda b, be, nv: (be[blk(b, be, nv)], 0, 0)),
                pl.BlockSpec((None, 1, d), lambda b, be, nv: (be[blk(b, be, nv)], 0, 0)),
            ],
            out_specs=pl.BlockSpec((MOE_BLOCK, nch, LANES),
                                   lambda b, be, nv: (blk(b, be, nv), 0, 0)),
        ),
        out_shape=jax.ShapeDtypeStruct((n_rows, nch, LANES), F32),
        name="experts",
        compiler_params=pltpu.CompilerParams(
            dimension_semantics=("arbitrary",), vmem_limit_bytes=VMEM_LIMIT),
    )(bexp, nvalid, xs, wgu_bf, bgu, wd_bf, bd)


def _combine_kernel(pos_ref, x1_ref, gate_ref, ys_ref, out_ref, buf_ref, sem, *, tcb):
    def row_copy(r, k):
        return pltpu.make_async_copy(
            ys_ref.at[pos_ref[TOP_K * r + k]], buf_ref.at[k, r], sem)

    def start(r, c):
        for k in range(TOP_K):
            row_copy(r, k).start()
        return c

    def wait(r, c):
        for k in range(TOP_K):
            row_copy(r, k).wait()
        return c

    lax.fori_loop(0, tcb, start, 0)
    lax.fori_loop(0, tcb, wait, 0)
    for j in range(buf_ref.shape[2]):
        sl = slice(j * LANES, (j + 1) * LANES)
        acc = x1_ref[:, sl]
        for k in range(TOP_K):
            acc = acc + gate_ref[:, k:k + 1] * buf_ref[k, :, j, :]
        out_ref[:, sl] = acc


def _combine(pos1d, x1, gates, ys, tcb):
    t, d = x1.shape
    return pl.pallas_call(
        functools.partial(_combine_kernel, tcb=tcb),
        grid=(t // tcb,),
        in_specs=[
            pl.BlockSpec((tcb * TOP_K,), lambda i: (i,), memory_space=pltpu.SMEM),
            pl.BlockSpec((tcb, d), lambda i: (i, 0)),
            pl.BlockSpec((tcb, LANES), lambda i: (i, 0)),
            pl.BlockSpec(memory_space=pl.ANY),
        ],
        out_specs=pl.BlockSpec((tcb, d), lambda i: (i, 0)),
        out_shape=jax.ShapeDtypeStruct((t, d), F32),
        scratch_shapes=[pltpu.VMEM((TOP_K, tcb, d // LANES, LANES), F32),
                        pltpu.SemaphoreType.DMA],
        name="combine",
        compiler_params=pltpu.CompilerParams(dimension_semantics=("arbitrary",)),
    )(pos1d, x1, gates, ys)


def _pick_tile(n, pref):
    t = min(pref, n)
    while n % t:
        t //= 2
    return t


def kernel(x, mem, positions, attn_norm_g, mem_norm_g, w_in, b_gates, a_q_norm_g, a_k_norm_g,
           a_sinks, w_o_a, gmlp_ln_g, gmlp_ln_b, gmlp_w_s, gmlp_b_s, w_o_b, w_mem_kv,
           c_q_norm_g, c_k_norm_g, w_o_c, w_out, ffn_norm_g, router_w, router_b,
           w_gate_up, b_gate_up, w_down, b_down):
    n_batch, seq, d = x.shape
    m_len = mem.shape[1]
    depth = w_in.shape[0]
    t = n_batch * seq
    n_asg = t * TOP_K
    nb = -(-n_asg // MOE_BLOCK) + N_EXPERTS
    n_rows = nb * MOE_BLOCK

    inv_freq = ROPE_THETA ** (-jnp.arange(0, HEAD_DIM, 2, dtype=F32) / HEAD_DIM)
    invf = jnp.tile(inv_freq, LANES // (HEAD_DIM // 2))[None, :]
    sgn = jnp.tile(jnp.concatenate([-jnp.ones((HEAD_DIM // 2,), F32),
                                    jnp.ones((HEAD_DIM // 2,), F32)]), LANES // HEAD_DIM)[None, :]
    pos2d = positions.reshape(t, 1).astype(I32)
    mem2d = mem.reshape(n_batch * m_len, d)
    x2d = x.reshape(t, d)

    tm = _pick_tile(t, 256)
    ts = _pick_tile(seq, 256)
    tp = _pick_tile(t, 2048)
    tsd = _pick_tile(t, 512)
    tcb = _pick_tile(t, 256)

    for l in range(depth):
        kc, vc = _mem_kv(mem2d, mem_norm_g[l][None, :], w_mem_kv[l].astype(BF16),
                         c_k_norm_g[l][None, :], n_batch, m_len)
        proj = _proj(
            x2d, pos2d, attn_norm_g[l][None, :], w_in[l].astype(BF16), invf, sgn,
            jnp.tile(a_q_norm_g[l], LANES // HEAD_DIM)[None, :],
            jnp.tile(a_k_norm_g[l], LANES // HEAD_DIM)[None, :],
            c_q_norm_g[l][None, :], gmlp_ln_g[l][None, :], gmlp_ln_b[l][None, :],
            b_gates[l].reshape(1, 3 * d), tm)
        rw = jnp.pad(router_w[l], ((0, 0), (0, LANES - N_EXPERTS)))
        rb = jnp.pad(router_b[l], (0, LANES - N_EXPERTS))[None, :]
        x1, h2, idx, rank, gates, counts = _mix(
            a_sinks[l], proj, x2d, kc, vc, gmlp_w_s[l], gmlp_b_s[l].T,
            w_o_a[l].astype(BF16), w_o_b[l].astype(BF16), w_o_c[l].astype(BF16),
            w_out[l].astype(BF16), ffn_norm_g[l][None, :], rw, rb, n_batch, seq, m_len, ts)

        cnt = counts[0, :N_EXPERTS].astype(I32)
        padded = (cnt + MOE_BLOCK - 1) // MOE_BLOCK * MOE_BLOCK
        pend = jnp.cumsum(padded)
        pstart = (pend - padded).astype(I32)
        nvalid = (pend[-1:] // MOE_BLOCK).astype(I32)
        blk_row = jnp.arange(nb, dtype=I32) * MOE_BLOCK
        bexp = jnp.minimum(jnp.sum((pend[None, :] <= blk_row[:, None]).astype(I32), axis=1),
                           N_EXPERTS - 1).astype(I32)
        zstart = jnp.maximum(pend - MOE_BLOCK, 0).astype(I32)
        zflag = (cnt > 0).astype(I32)

        pos = _pos(pstart, idx, rank, tp)
        pos1d = pos[:, :TOP_K].reshape(n_asg)
        xs = _dispatch(zstart, zflag, pos1d, h2, n_rows, tsd)
        ys = _experts(bexp, nvalid, xs, w_gate_up[l].astype(BF16), b_gate_up[l][:, None, :],
                      w_down[l].astype(BF16), b_down[l][:, None, :])
        x2d = _combine(pos1d, x1, gates, ys, tcb)
    return x2d.reshape(n_batch, seq, d)
```

```python
import functools

import numpy as np
import jax
import jax.numpy as jnp
from jax import lax
from jax.experimental import pallas as pl
from jax.experimental.pallas import tpu as pltpu
from jax.experimental.pallas import tpu_sc as plsc

F32 = jnp.float32
BF16 = jnp.bfloat16
I32 = jnp.int32

EPS = 1e-6
LANES = 128
HEAD_DIM = 64
N_Q_HEADS = 16
N_KV_HEADS = 2
ATT_BLOCK = 128
ROPE_THETA = 10000.0
GMLP_WIDTH = 512
GMLP_GROUPS = 4
GMLP_CHUNK = 128
X_HEADS = 4
X_HEAD_DIM = 128
N_EXPERTS = 32
TOP_K = 4
SWIGLU_LIMIT = 7.0
SWIGLU_ALPHA = 1.702
MOE_BLOCK = 512
SC_ROWS = 128
SC_CHUNK = 256

A_Q = N_Q_HEADS * HEAD_DIM
A_KV = N_KV_HEADS * HEAD_DIM
C_Q = X_HEADS * X_HEAD_DIM

OFF_Q = 0
OFF_K4 = OFF_Q + A_Q
OFF_V4 = OFF_K4 + 4 * LANES
OFF_U = OFF_V4 + 4 * LANES
OFF_VN = OFF_U + GMLP_WIDTH
OFF_QC = OFF_VN + GMLP_WIDTH
OFF_GA = OFF_QC + C_Q
PROJ_W_BASE = OFF_GA

VMEM_LIMIT = 56 * 1024 * 1024


def _lane_iota(shape):
    return lax.broadcasted_iota(I32, shape, len(shape) - 1)


def _rms(x, g):
    return x * lax.rsqrt(jnp.mean(x * x, axis=-1, keepdims=True) + EPS) * g


def _gelu(x):
    return 0.5 * x * (1.0 + lax.erf(x * np.float32(np.sqrt(0.5))))


def _memkv_kernel(mem_ref, g_ref, w_ref, gk_ref, kc_ref, vc_ref):
    h = _rms(mem_ref[...], g_ref[...]).astype(BF16)
    kv = jnp.dot(h, w_ref[...], preferred_element_type=F32)
    for hh in range(X_HEADS):
        sl = slice(hh * X_HEAD_DIM, (hh + 1) * X_HEAD_DIM)
        kc_ref[:, sl] = _rms(kv[:, sl], gk_ref[...]).astype(BF16)
    vc_ref[...] = kv[:, C_Q:].astype(BF16)


def _mem_kv(mem2d, g, w_bf, gk, n_batch, m_len):
    d = mem2d.shape[1]
    return pl.pallas_call(
        _memkv_kernel,
        grid=(n_batch,),
        in_specs=[
            pl.BlockSpec((m_len, d), lambda b: (b, 0)),
            pl.BlockSpec((1, d), lambda b: (0, 0)),
            pl.BlockSpec((d, 2 * C_Q), lambda b: (0, 0)),
            pl.BlockSpec((1, X_HEAD_DIM), lambda b: (0, 0)),
        ],
        out_specs=[
            pl.BlockSpec((m_len, C_Q), lambda b: (b, 0)),
            pl.BlockSpec((m_len, C_Q), lambda b: (b, 0)),
        ],
        out_shape=[jax.ShapeDtypeStruct((n_batch * m_len, C_Q), BF16)] * 2,
        name="mem_kv",
        compiler_params=pltpu.CompilerParams(dimension_semantics=("arbitrary",)),
    )(mem2d, g, w_bf, gk)


def _proj_kernel(x_ref, pos_ref, g_ref, w_ref, invf_ref, sgn_ref, gq_ref, gk_ref, gcq_ref,
                 lng_ref, lnb_ref, bg_ref, out_ref, *, d_model):
    tm = x_ref.shape[0]
    h = _rms(x_ref[...], g_ref[...]).astype(BF16)

    ang = pos_ref[...].astype(F32) * invf_ref[...]
    cosv = jnp.cos(ang)
    sinv = jnp.sin(ang) * sgn_ref[...]
    lane = _lane_iota((tm, LANES))
    first_head = lane < HEAD_DIM
    lo_half = (lane % HEAD_DIM) < (HEAD_DIM // 2)

    def head_norm_rope(blk, g):
        y = blk * blk
        s_lo = jnp.sum(jnp.where(first_head, y, 0.0), axis=-1, keepdims=True)
        s_hi = jnp.sum(jnp.where(first_head, 0.0, y), axis=-1, keepdims=True)
        ss = jnp.where(first_head, s_lo, s_hi)
        n = blk * lax.rsqrt(ss * (1.0 / HEAD_DIM) + EPS) * g
        rot = jnp.where(lo_half, pltpu.roll(n, LANES - HEAD_DIM // 2, 1),
                        pltpu.roll(n, HEAD_DIM // 2, 1))
        return n * cosv + rot * sinv

    def proj(a, b):
        return jnp.dot(h, w_ref[:, a:b], preferred_element_type=F32)

    pq = proj(0, A_Q)
    for c in range(A_Q // LANES):
        sl = slice(c * LANES, (c + 1) * LANES)
        out_ref[:, OFF_Q + c * LANES:OFF_Q + (c + 1) * LANES] = (
            head_norm_rope(pq[:, sl], gq_ref[...]) * (HEAD_DIM ** -0.5)).astype(BF16)

    pkv = proj(A_Q, A_Q + 2 * A_KV)
    kn = head_norm_rope(pkv[:, :LANES], gk_ref[...])
    vv = pkv[:, LANES:]
    for off, t in ((OFF_K4, kn), (OFF_V4, vv)):
        tr = pltpu.roll(t, HEAD_DIM, 1)
        parts = (jnp.where(first_head, t, 0.0), jnp.where(first_head, 0.0, tr),
                 jnp.where(first_head, tr, 0.0), jnp.where(first_head, 0.0, t))
        for j, p in enumerate(parts):
            out_ref[:, off + j * LANES:off + (j + 1) * LANES] = p.astype(BF16)

    o0 = A_Q + 2 * A_KV
    out_ref[:, OFF_U:OFF_U + GMLP_WIDTH] = _gelu(proj(o0, o0 + GMLP_WIDTH)).astype(BF16)
    gv = _gelu(proj(o0 + GMLP_WIDTH, o0 + 2 * GMLP_WIDTH))
    mu = jnp.mean(gv, axis=-1, keepdims=True)
    var = jnp.mean(jnp.square(gv - mu), axis=-1, keepdims=True)
    out_ref[:, OFF_VN:OFF_VN + GMLP_WIDTH] = (
        (gv - mu) * lax.rsqrt(var + EPS) * lng_ref[...] + lnb_ref[...]).astype(BF16)

    o1 = o0 + 2 * GMLP_WIDTH
    pc = proj(o1, o1 + C_Q)
    for hh in range(X_HEADS):
        sl = slice(hh * X_HEAD_DIM, (hh + 1) * X_HEAD_DIM)
        out_ref[:, OFF_QC + hh * X_HEAD_DIM:OFF_QC + (hh + 1) * X_HEAD_DIM] = (
            _rms(pc[:, sl], gcq_ref[...]) * (X_HEAD_DIM ** -0.5)).astype(BF16)

    o2 = o1 + C_Q
    for j in range(3):
        sl = slice(j * d_model, (j + 1) * d_model)
        out_ref[:, OFF_GA + j * d_model:OFF_GA + (j + 1) * d_model] = jax.nn.sigmoid(
            proj(o2 + j * d_model, o2 + (j + 1) * d_model) + bg_ref[:, sl]).astype(BF16)


def _proj(x2d, pos2d, g, w_bf, invf, sgn, gq, gk, gcq, lng, lnb, bg, tm):
    t, d = x2d.shape
    d_in = w_bf.shape[1]
    pw = PROJ_W_BASE + 3 * d
    full = lambda shape: pl.BlockSpec(shape, lambda i: (0,) * len(shape))
    return pl.pallas_call(
        functools.partial(_proj_kernel, d_model=d),
        grid=(t // tm,),
        in_specs=[
            pl.BlockSpec((tm, d), lambda i: (i, 0)),
            pl.BlockSpec((tm, 1), lambda i: (i, 0)),
            full((1, d)), full((d, d_in)), full((1, LANES)), full((1, LANES)),
            full((1, LANES)), full((1, LANES)), full((1, X_HEAD_DIM)),
            full((1, GMLP_WIDTH)), full((1, GMLP_WIDTH)), full((1, 3 * d)),
        ],
        out_specs=pl.BlockSpec((tm, pw), lambda i: (i, 0)),
        out_shape=jax.ShapeDtypeStruct((t, pw), BF16),
        name="proj",
        compiler_params=pltpu.CompilerParams(
            dimension_semantics=("arbitrary",), vmem_limit_bytes=VMEM_LIMIT),
    )(x2d, pos2d, g, w_bf, invf, sgn, gq, gk, gcq, lng, lnb, bg)


def _mix_kernel(sinks_ref, proj_ref, kprev_ref, vprev_ref, x_ref, kc_ref, vc_ref, ws_ref, bst_ref,
                woa_ref, wob_ref, woc_ref, wout_ref, gffn_ref, rw_ref, rb_ref,
                x1_ref, h2_ref, idx_ref, rank_ref, gate_ref, cnt_ref,
                run_ref, oa_ref, ob_ref, oc_ref, *, d_model):
    ts = x_ref.shape[0]
    s_idx = pl.program_id(1)
    first_step = jnp.logical_and(pl.program_id(0) == 0, s_idx == 0)
    neg_inf = float("-inf")

    qi = lax.broadcasted_iota(I32, (ATT_BLOCK, 2 * ATT_BLOCK), 0)
    kj = lax.broadcasted_iota(I32, (ATT_BLOCK, 2 * ATT_BLOCK), 1)
    band = jnp.logical_and(kj <= ATT_BLOCK + qi, kj > qi)
    for qb in range(ts // ATT_BLOCK):
        r0 = qb * ATT_BLOCK
        rows = slice(r0, r0 + ATT_BLOCK)
        if qb == 0:
            kp, vp = kprev_ref[...], vprev_ref[...]
            mask = jnp.logical_and(band, jnp.logical_or(kj >= ATT_BLOCK, s_idx > 0))
        else:
            prow = slice(r0 - ATT_BLOCK, r0)
            kp = proj_ref[prow, OFF_K4:OFF_K4 + 4 * LANES]
            vp = proj_ref[prow, OFF_V4:OFF_V4 + 4 * LANES]
            mask = band
        k4 = jnp.concatenate([kp, proj_ref[rows, OFF_K4:OFF_K4 + 4 * LANES]], axis=0)
        v4 = jnp.concatenate([vp, proj_ref[rows, OFF_V4:OFF_V4 + 4 * LANES]], axis=0)
        for c in range(A_Q // LANES):
            kvh = (2 * c) // (N_Q_HEADS // N_KV_HEADS)
            qch = proj_ref[rows, OFF_Q + c * LANES:OFF_Q + (c + 1) * LANES]
            o = jnp.zeros((ATT_BLOCK, LANES), F32)
            for half in range(2):
                col = slice((2 * kvh + half) * LANES, (2 * kvh + half + 1) * LANES)
                s = lax.dot_general(qch, k4[:, col], (((1,), (1,)), ((), ())),
                                    preferred_element_type=F32)
                s = jnp.where(mask, s, neg_inf)
                sink = sinks_ref[2 * c + half]
                m = jnp.maximum(jnp.max(s, axis=-1, keepdims=True), sink)
                p = jnp.exp(s - m)
                den = jnp.sum(p, axis=-1, keepdims=True) + jnp.exp(sink - m)
                o = o + jnp.dot(p.astype(BF16), v4[:, col], preferred_element_type=F32) / den
            oa_ref[rows, c * LANES:(c + 1) * LANES] = o.astype(BF16)

    ti = lax.broadcasted_iota(I32, (GMLP_CHUNK, GMLP_CHUNK), 0)
    si = lax.broadcasted_iota(I32, (GMLP_CHUNK, GMLP_CHUNK), 1)
    for g in range(GMLP_GROUPS):
        wt = jnp.where(si <= ti, ws_ref[g], 0.0).astype(BF16)
        bcol = bst_ref[:, g:g + 1]
        for ch in range(ts // GMLP_CHUNK):
            rows = slice(ch * GMLP_CHUNK, (ch + 1) * GMLP_CHUNK)
            vn = proj_ref[rows, OFF_VN + g * LANES:OFF_VN + (g + 1) * LANES]
            u = proj_ref[rows, OFF_U + g * LANES:OFF_U + (g + 1) * LANES].astype(F32)
            mixed = jnp.dot(wt, vn, preferred_element_type=F32) + bcol
            ob_ref[rows, g * LANES:(g + 1) * LANES] = (u * mixed).astype(BF16)

    for hh in range(X_HEADS):
        sl = slice(hh * X_HEAD_DIM, (hh + 1) * X_HEAD_DIM)
        qc = proj_ref[:, OFF_QC + hh * X_HEAD_DIM:OFF_QC + (hh + 1) * X_HEAD_DIM]
        s = lax.dot_general(qc, kc_ref[:, sl], (((1,), (1,)), ((), ())),
                            preferred_element_type=F32)
        p = jnp.exp(s - jnp.max(s, axis=-1, keepdims=True))
        den = jnp.sum(p, axis=-1, keepdims=True)
        oc_ref[:, sl] = (jnp.dot(p.astype(BF16), vc_ref[:, sl],
                                 preferred_element_type=F32) / den).astype(BF16)

    def gate(j):
        return proj_ref[:, OFF_GA + j * d_model:OFF_GA + (j + 1) * d_model].astype(F32)

    merged = gate(0) * jnp.dot(oa_ref[...], woa_ref[...], preferred_element_type=F32)
    merged = merged + gate(1) * jnp.dot(ob_ref[...], wob_ref[...], preferred_element_type=F32)
    merged = merged + gate(2) * jnp.dot(oc_ref[...], woc_ref[...], preferred_element_type=F32)
    x1 = x_ref[...] + jnp.dot(merged.astype(BF16), wout_ref[...], preferred_element_type=F32)
    x1_ref[...] = x1

    h2 = _rms(x1, gffn_ref[...])
    for c in range(d_model // SC_CHUNK):
        h2_ref[c] = h2[:, c * SC_CHUNK:(c + 1) * SC_CHUNK]
    lane = _lane_iota((ts, LANES))
    logits = jnp.dot(h2, rw_ref[...], precision=lax.Precision.HIGHEST,
                     preferred_element_type=F32) + rb_ref[...]
    logits = jnp.where(lane < N_EXPERTS, logits, neg_inf)
    vals, idxs = [], []
    for _ in range(TOP_K):
        m = jnp.max(logits, axis=-1, keepdims=True)
        i = jnp.min(jnp.where(logits == m, lane, LANES), axis=-1, keepdims=True)
        vals.append(m)
        idxs.append(i)
        logits = jnp.where(lane == i, neg_inf, logits)
    es = [jnp.exp(v - vals[0]) for v in vals]
    den = es[0] + es[1] + es[2] + es[3]

    @pl.when(first_step)
    def _():
        run_ref[...] = jnp.zeros_like(run_ref)

    hot = [lane == i for i in idxs]
    multihot = jnp.where(jnp.logical_or(jnp.logical_or(hot[0], hot[1]),
                                        jnp.logical_or(hot[2], hot[3])), 1.0, 0.0)
    tr = lax.broadcasted_iota(I32, (ts, ts), 0)
    tc = lax.broadcasted_iota(I32, (ts, ts), 1)
    ltri = jnp.where(tc < tr, 1.0, 0.0).astype(BF16)
    before = jnp.dot(ltri, multihot.astype(BF16), preferred_element_type=F32) + run_ref[...]
    idx_out = jnp.zeros((ts, LANES), I32)
    rank_out = jnp.zeros((ts, LANES), I32)
    gate_out = jnp.zeros((ts, LANES), F32)
    for k in range(TOP_K):
        rk = jnp.sum(jnp.where(hot[k], before, 0.0), axis=-1, keepdims=True)
        idx_out = jnp.where(lane == k, idxs[k], idx_out)
        rank_out = jnp.where(lane == k, rk.astype(I32), rank_out)
        gate_out = jnp.where(lane == k, es[k] / den, gate_out)
    idx_ref[...] = idx_out
    rank_ref[...] = rank_out
    gate_ref[...] = gate_out
    run_ref[...] = run_ref[...] + jnp.sum(multihot, axis=0, keepdims=True)
    cnt_ref[...] = run_ref[...]


def _mix(sinks, proj, x2d, kc, vc, w_s, bst, woa, wob, woc, wout, gffn, rw, rb,
         n_batch, seq, m_len, ts):
    t, d = x2d.shape
    pw = proj.shape[1]
    ns = seq // ts
    nblk = seq // ATT_BLOCK
    per = ts // ATT_BLOCK
    full = lambda shape: pl.BlockSpec(shape, lambda b, s: (0,) * len(shape))
    row = lambda width: pl.BlockSpec((ts, width), lambda b, s: (b * ns + s, 0))
    prev = lambda colblk: pl.BlockSpec(
        (ATT_BLOCK, 4 * LANES), lambda b, s: (b * nblk + jnp.maximum(s * per - 1, 0), colblk))
    return pl.pallas_call(
        functools.partial(_mix_kernel, d_model=d),
        grid=(n_batch, ns),
        in_specs=[
            pl.BlockSpec(memory_space=pltpu.SMEM),
            row(pw), prev(OFF_K4 // (4 * LANES)), prev(OFF_V4 // (4 * LANES)), row(d),
            pl.BlockSpec((m_len, C_Q), lambda b, s: (b, 0)),
            pl.BlockSpec((m_len, C_Q), lambda b, s: (b, 0)),
            full((GMLP_GROUPS, GMLP_CHUNK, GMLP_CHUNK)), full((GMLP_CHUNK, GMLP_GROUPS)),
            full((A_Q, d)), full((GMLP_WIDTH, d)), full((C_Q, d)), full((d, d)),
            full((1, d)), full((d, LANES)), full((1, LANES)),
        ],
        out_specs=[row(d),
                   pl.BlockSpec((d // SC_CHUNK, ts, SC_CHUNK), lambda b, s: (0, b * ns + s, 0)),
                   row(LANES), row(LANES), row(LANES), full((1, LANES))],
        out_shape=[
            jax.ShapeDtypeStruct((t, d), F32),
            jax.ShapeDtypeStruct((d // SC_CHUNK, t, SC_CHUNK), F32),
            jax.ShapeDtypeStruct((t, LANES), I32), jax.ShapeDtypeStruct((t, LANES), I32),
            jax.ShapeDtypeStruct((t, LANES), F32), jax.ShapeDtypeStruct((1, LANES), F32),
        ],
        scratch_shapes=[
            pltpu.VMEM((1, LANES), F32),
            pltpu.VMEM((ts, A_Q), BF16), pltpu.VMEM((ts, GMLP_WIDTH), BF16),
            pltpu.VMEM((ts, C_Q), BF16),
        ],
        name="mix",
        compiler_params=pltpu.CompilerParams(
            dimension_semantics=("arbitrary", "arbitrary"), vmem_limit_bytes=VMEM_LIMIT),
    )(sinks, proj, proj, proj, x2d, kc, vc, w_s, bst, woa, wob, woc, wout, gffn, rw, rb)


def _pos_kernel(pstart_ref, idx_ref, rank_ref, pos_ref):
    idx = idx_ref[...]
    pos = rank_ref[...]
    for e in range(N_EXPERTS):
        pos = pos + jnp.where(idx == e, pstart_ref[e], 0)
    pos_ref[...] = pos


def _pos(pstart, idx, rank, tp):
    t = idx.shape[0]
    return pl.pallas_call(
        _pos_kernel,
        grid_spec=pltpu.PrefetchScalarGridSpec(
            num_scalar_prefetch=1,
            grid=(t // tp,),
            in_specs=[pl.BlockSpec((tp, LANES), lambda i, ps: (i, 0)),
                      pl.BlockSpec((tp, LANES), lambda i, ps: (i, 0))],
            out_specs=pl.BlockSpec((tp, LANES), lambda i, ps: (i, 0)),
        ),
        out_shape=jax.ShapeDtypeStruct((t, LANES), I32),
        name="slot_pos",
        compiler_params=pltpu.CompilerParams(dimension_semantics=("arbitrary",)),
    )(pstart, idx, rank)


def _sc_mesh():
    return plsc.VectorSubcoreMesh(core_axis_name="core", subcore_axis_name="subcore")


def _sc_scatter_rows(src, idx, n_rows):
    n_chunks, t, w = src.shape
    n_idx = idx.shape[1]
    src_blocks = t // SC_ROWS
    idx_blocks = n_idx // SC_ROWS

    @pl.kernel(out_type=jax.ShapeDtypeStruct((n_chunks * n_rows, w), src.dtype), mesh=_sc_mesh(),
               scratch_types=[], name="sc_dispatch")
    def scatter(src_hbm, idx_hbm, out_hbm):
        def body(src_vmem, idx_vmem):
            pltpu.sync_copy(src_vmem, out_hbm.at[idx_vmem.at[0]])

        pltpu.emit_pipeline(
            body,
            grid=(n_chunks, idx_blocks),
            in_specs=[pl.BlockSpec((SC_ROWS, w), lambda c, i: (c * src_blocks + i % src_blocks, 0)),
                      pl.BlockSpec((1, SC_ROWS), lambda c, i: (0, c * idx_blocks + i))],
            out_specs=[],
            core_axis_name=("core", "subcore"),
            dimension_semantics=(pltpu.PARALLEL, pltpu.PARALLEL),
        )(src_hbm, idx_hbm)

    return scatter(src.reshape(n_chunks * t, w), idx.reshape(1, n_chunks * n_idx)).reshape(
        n_chunks, n_rows, w)


def _sc_gather_rows(table, idx):
    n_chunks, p, w = table.shape
    n_idx = idx.shape[1]
    idx_blocks = n_idx // SC_ROWS

    @pl.kernel(out_type=jax.ShapeDtypeStruct((n_chunks * n_idx, w), table.dtype), mesh=_sc_mesh(),
               scratch_types=[], name="sc_gather")
    def gather(table_hbm, idx_hbm, out_hbm):
        def body(idx_vmem, out_vmem):
            pltpu.sync_copy(table_hbm.at[idx_vmem.at[0]], out_vmem)

        pltpu.emit_pipeline(
            body,
            grid=(n_chunks, idx_blocks),
            in_specs=[pl.BlockSpec((1, SC_ROWS), lambda c, i: (0, c * idx_blocks + i))],
            out_specs=[pl.BlockSpec((SC_ROWS, w), lambda c, i: (c * idx_blocks + i, 0))],
            core_axis_name=("core", "subcore"),
            dimension_semantics=(pltpu.PARALLEL, pltpu.PARALLEL),
        )(idx_hbm, out_hbm)

    return gather(table.reshape(n_chunks * p, w), idx.reshape(1, n_chunks * n_idx)).reshape(
        n_chunks, n_idx, w)


def _expert_kernel(bexp_ref, nvalid_ref, xs_ref, wgu_ref, bgu_ref, wd_ref, bd_ref, ys_ref, *, d_exp):
    @pl.when(pl.program_id(0) < nvalid_ref[0])
    def _():
        n_chunks = xs_ref.shape[0]
        xb = jnp.concatenate([xs_ref[c] for c in range(n_chunks)], axis=1).astype(BF16)
        gu = jnp.dot(xb, wgu_ref[...], preferred_element_type=F32) + bgu_ref[...]
        gate = jnp.minimum(gu[:, :d_exp], SWIGLU_LIMIT)
        up = jnp.clip(gu[:, d_exp:], -SWIGLU_LIMIT, SWIGLU_LIMIT)
        glu = gate * jax.nn.sigmoid(gate * SWIGLU_ALPHA)
        act = ((up + 1.0) * glu).astype(BF16)
        y = jnp.dot(act, wd_ref[...], preferred_element_type=F32) + bd_ref[...]
        for c in range(n_chunks):
            ys_ref[c] = y[:, c * SC_CHUNK:(c + 1) * SC_CHUNK]


def _experts(bexp, nvalid, xs, wgu_bf, bgu, wd_bf, bd):
    n_chunks, n_rows, _ = xs.shape
    d = n_chunks * SC_CHUNK
    d_exp = wd_bf.shape[1]
    nb = n_rows // MOE_BLOCK
    blk = lambda b, be, nv: jnp.minimum(b, nv[0] - 1)
    return pl.pallas_call(
        functools.partial(_expert_kernel, d_exp=d_exp),
        grid_spec=pltpu.PrefetchScalarGridSpec(
            num_scalar_prefetch=2,
            grid=(nb,),
            in_specs=[
                pl.BlockSpec((n_chunks, MOE_BLOCK, SC_CHUNK),
                             lambda b, be, nv: (0, blk(b, be, nv), 0)),
                pl.BlockSpec((None, d, 2 * d_exp), lambda b, be, nv: (be[blk(b, be, nv)], 0, 0)),
                pl.BlockSpec((None, 1, 2 * d_exp), lambda b, be, nv: (be[blk(b, be, nv)], 0, 0)),
                pl.BlockSpec((None, d_exp, d), lambda b, be, nv: (be[blk(b, be, nv)], 0, 0)),
                pl.BlockSpec((None, 1, d), lambda b, be, nv: (be[blk(b, be, nv)], 0, 0)),
            ],
            out_specs=pl.BlockSpec((n_chunks, MOE_BLOCK, SC_CHUNK),
                                   lambda b, be, nv: (0, blk(b, be, nv), 0)),
        ),
        out_shape=jax.ShapeDtypeStruct((n_chunks, n_rows, SC_CHUNK), F32),
        name="experts",
        compiler_params=pltpu.CompilerParams(
            dimension_semantics=("arbitrary",), vmem_limit_bytes=VMEM_LIMIT),
    )(bexp, nvalid, xs, wgu_bf, bgu, wd_bf, bd)


def _combine_kernel(x1_ref, gate_ref, yg_ref, out_ref):
    for c in range(yg_ref.shape[0]):
        sl = slice(c * SC_CHUNK, (c + 1) * SC_CHUNK)
        acc = x1_ref[:, sl]
        for k in range(TOP_K):
            acc = acc + gate_ref[:, k:k + 1] * yg_ref[c, k]
        out_ref[:, sl] = acc


def _combine(x1, gates, yg, tcb):
    t, d = x1.shape
    n_chunks = yg.shape[0]
    return pl.pallas_call(
        _combine_kernel,
        grid=(t // tcb,),
        in_specs=[
            pl.BlockSpec((tcb, d), lambda i: (i, 0)),
            pl.BlockSpec((tcb, LANES), lambda i: (i, 0)),
            pl.BlockSpec((n_chunks, TOP_K, tcb, SC_CHUNK), lambda i: (0, 0, i, 0)),
        ],
        out_specs=pl.BlockSpec((tcb, d), lambda i: (i, 0)),
        out_shape=jax.ShapeDtypeStruct((t, d), F32),
        name="combine",
        compiler_params=pltpu.CompilerParams(dimension_semantics=("arbitrary",)),
    )(x1, gates, yg)


def _pick_tile(n, pref):
    t = min(pref, n)
    while n % t:
        t //= 2
    return t


def kernel(x, mem, positions, attn_norm_g, mem_norm_g, w_in, b_gates, a_q_norm_g, a_k_norm_g,
           a_sinks, w_o_a, gmlp_ln_g, gmlp_ln_b, gmlp_w_s, gmlp_b_s, w_o_b, w_mem_kv,
           c_q_norm_g, c_k_norm_g, w_o_c, w_out, ffn_norm_g, router_w, router_b,
           w_gate_up, b_gate_up, w_down, b_down):
    n_batch, seq, d = x.shape
    m_len = mem.shape[1]
    depth = w_in.shape[0]
    t = n_batch * seq
    n_asg = t * TOP_K
    nb = -(-n_asg // MOE_BLOCK) + N_EXPERTS
    n_rows = nb * MOE_BLOCK

    inv_freq = ROPE_THETA ** (-jnp.arange(0, HEAD_DIM, 2, dtype=F32) / HEAD_DIM)
    invf = jnp.tile(inv_freq, LANES // (HEAD_DIM // 2))[None, :]
    sgn = jnp.tile(jnp.concatenate([-jnp.ones((HEAD_DIM // 2,), F32),
                                    jnp.ones((HEAD_DIM // 2,), F32)]), LANES // HEAD_DIM)[None, :]
    pos2d = positions.reshape(t, 1).astype(I32)
    mem2d = mem.reshape(n_batch * m_len, d)
    x2d = x.reshape(t, d)

    tm = _pick_tile(t, 256)
    ts = _pick_tile(seq, 256)
    tp = _pick_tile(t, 2048)
    tcb = _pick_tile(t, 256)

    for l in range(depth):
        kc, vc = _mem_kv(mem2d, mem_norm_g[l][None, :], w_mem_kv[l].astype(BF16),
                         c_k_norm_g[l][None, :], n_batch, m_len)
        proj = _proj(
            x2d, pos2d, attn_norm_g[l][None, :], w_in[l].astype(BF16), invf, sgn,
            jnp.tile(a_q_norm_g[l], LANES // HEAD_DIM)[None, :],
            jnp.tile(a_k_norm_g[l], LANES // HEAD_DIM)[None, :],
            c_q_norm_g[l][None, :], gmlp_ln_g[l][None, :], gmlp_ln_b[l][None, :],
            b_gates[l].reshape(1, 3 * d), tm)
        rw = jnp.pad(router_w[l], ((0, 0), (0, LANES - N_EXPERTS)))
        rb = jnp.pad(router_b[l], (0, LANES - N_EXPERTS))[None, :]
        x1, h2, idx, rank, gates, counts = _mix(
            a_sinks[l], proj, x2d, kc, vc, gmlp_w_s[l], gmlp_b_s[l].T,
            w_o_a[l].astype(BF16), w_o_b[l].astype(BF16), w_o_c[l].astype(BF16),
            w_out[l].astype(BF16), ffn_norm_g[l][None, :], rw, rb, n_batch, seq, m_len, ts)

        cnt = counts[0, :N_EXPERTS].astype(I32)
        padded = (cnt + MOE_BLOCK - 1) // MOE_BLOCK * MOE_BLOCK
        pend = jnp.cumsum(padded)
        pstart = (pend - padded).astype(I32)
        nvalid = (pend[-1:] // MOE_BLOCK).astype(I32)
        blk_row = jnp.arange(nb, dtype=I32) * MOE_BLOCK
        bexp = jnp.minimum(jnp.sum((pend[None, :] <= blk_row[:, None]).astype(I32), axis=1),
                           N_EXPERTS - 1).astype(I32)
        j = jnp.arange(MOE_BLOCK, dtype=I32)[None, :]
        fill = jnp.where(j < (padded - cnt)[:, None], pend[:, None] - 1 - j, n_rows - 1).astype(I32)

        pos = _pos(pstart, idx, rank, tp)
        pos_km = pos[:, :TOP_K].T.reshape(n_asg)
        chunk_off = (jnp.arange(d // SC_CHUNK, dtype=I32) * n_rows)[:, None]
        xs = _sc_scatter_rows(
            h2, jnp.concatenate([pos_km, fill.reshape(-1)])[None, :] + chunk_off, n_rows)
        ys = _experts(bexp, nvalid, xs, w_gate_up[l].astype(BF16), b_gate_up[l][:, None, :],
                      w_down[l].astype(BF16), b_down[l][:, None, :])
        yg = _sc_gather_rows(ys, pos_km[None, :] + chunk_off)
        x2d = _combine(x1, gates, yg.reshape(d // SC_CHUNK, TOP_K, t, SC_CHUNK), tcb)
    return x2d.reshape(n_batch, seq, d)
```

```python
import functools

import numpy as np
import jax
import jax.numpy as jnp
from jax import lax
from jax.experimental import pallas as pl
from jax.experimental.pallas import tpu as pltpu
from jax.experimental.pallas import tpu_sc as plsc

F32 = jnp.float32
BF16 = jnp.bfloat16
I32 = jnp.int32
U32 = jnp.uint32
HI16 = np.uint32(0xFFFF0000)

EPS = 1e-6
LANES = 128
HEAD_DIM = 64
N_Q_HEADS = 16
N_KV_HEADS = 2
ATT_BLOCK = 128
ROPE_THETA = 10000.0
GMLP_WIDTH = 512
GMLP_GROUPS = 4
GMLP_CHUNK = 128
X_HEADS = 4
X_HEAD_DIM = 128
N_EXPERTS = 32
TOP_K = 4
SWIGLU_LIMIT = 7.0
SWIGLU_ALPHA = 1.702
MOE_BLOCK = 512
SC_ROWS = 128
SC_CHUNK = 256

A_Q = N_Q_HEADS * HEAD_DIM
A_KV = N_KV_HEADS * HEAD_DIM
C_Q = X_HEADS * X_HEAD_DIM

OFF_Q = 0
OFF_K4 = OFF_Q + A_Q
OFF_V4 = OFF_K4 + 4 * LANES
OFF_U = OFF_V4 + 4 * LANES
OFF_VN = OFF_U + GMLP_WIDTH
OFF_QC = OFF_VN + GMLP_WIDTH
OFF_GA = OFF_QC + C_Q
PROJ_W_BASE = OFF_GA

VMEM_LIMIT = 56 * 1024 * 1024


def _lane_iota(shape):
    return lax.broadcasted_iota(I32, shape, len(shape) - 1)


def _rms(x, g):
    return x * lax.rsqrt(jnp.mean(x * x, axis=-1, keepdims=True) + EPS) * g


def _pack_bf16_pairs(x):
    n = x.shape[1] // 2
    bits = pltpu.bitcast(x.astype(BF16).astype(F32), U32)
    return (bits[:, :n] >> 16) | (bits[:, n:] & HI16)


def _unpack_bf16_pairs(w):
    return pltpu.bitcast(w << 16, F32), pltpu.bitcast(w & HI16, F32)


def _gelu(x):
    return 0.5 * x * (1.0 + lax.erf(x * np.float32(np.sqrt(0.5))))


def _memkv_kernel(mem_ref, g_ref, w_ref, gk_ref, kc_ref, vc_ref):
    h = _rms(mem_ref[...], g_ref[...]).astype(BF16)
    kv = jnp.dot(h, w_ref[...], preferred_element_type=F32)
    for hh in range(X_HEADS):
        sl = slice(hh * X_HEAD_DIM, (hh + 1) * X_HEAD_DIM)
        kc_ref[:, sl] = _rms(kv[:, sl], gk_ref[...]).astype(BF16)
    vc_ref[...] = kv[:, C_Q:].astype(BF16)


def _mem_kv(mem2d, g, w_bf, gk, n_batch, m_len):
    d = mem2d.shape[1]
    return pl.pallas_call(
        _memkv_kernel,
        grid=(n_batch,),
        in_specs=[
            pl.BlockSpec((m_len, d), lambda b: (b, 0)),
            pl.BlockSpec((1, d), lambda b: (0, 0)),
            pl.BlockSpec((d, 2 * C_Q), lambda b: (0, 0)),
            pl.BlockSpec((1, X_HEAD_DIM), lambda b: (0, 0)),
        ],
        out_specs=[
            pl.BlockSpec((m_len, C_Q), lambda b: (b, 0)),
            pl.BlockSpec((m_len, C_Q), lambda b: (b, 0)),
        ],
        out_shape=[jax.ShapeDtypeStruct((n_batch * m_len, C_Q), BF16)] * 2,
        name="mem_kv",
        compiler_params=pltpu.CompilerParams(dimension_semantics=("arbitrary",)),
    )(mem2d, g, w_bf, gk)


def _proj_kernel(x_ref, pos_ref, g_ref, w_ref, invf_ref, sgn_ref, gq_ref, gk_ref, gcq_ref,
                 lng_ref, lnb_ref, bg_ref, out_ref, *, d_model):
    tm = x_ref.shape[0]
    h = _rms(x_ref[...], g_ref[...]).astype(BF16)

    ang = pos_ref[...].astype(F32) * invf_ref[...]
    cosv = jnp.cos(ang)
    sinv = jnp.sin(ang) * sgn_ref[...]
    lane = _lane_iota((tm, LANES))
    first_head = lane < HEAD_DIM
    lo_half = (lane % HEAD_DIM) < (HEAD_DIM // 2)

    def head_norm_rope(blk, g):
        y = blk * blk
        s_lo = jnp.sum(jnp.where(first_head, y, 0.0), axis=-1, keepdims=True)
        s_hi = jnp.sum(jnp.where(first_head, 0.0, y), axis=-1, keepdims=True)
        ss = jnp.where(first_head, s_lo, s_hi)
        n = blk * lax.rsqrt(ss * (1.0 / HEAD_DIM) + EPS) * g
        rot = jnp.where(lo_half, pltpu.roll(n, LANES - HEAD_DIM // 2, 1),
                        pltpu.roll(n, HEAD_DIM // 2, 1))
        return n * cosv + rot * sinv

    def proj(a, b):
        return jnp.dot(h, w_ref[:, a:b], preferred_element_type=F32)

    pq = proj(0, A_Q)
    for c in range(A_Q // LANES):
        sl = slice(c * LANES, (c + 1) * LANES)
        out_ref[:, OFF_Q + c * LANES:OFF_Q + (c + 1) * LANES] = (
            head_norm_rope(pq[:, sl], gq_ref[...]) * (HEAD_DIM ** -0.5)).astype(BF16)

    pkv = proj(A_Q, A_Q + 2 * A_KV)
    kn = head_norm_rope(pkv[:, :LANES], gk_ref[...])
    vv = pkv[:, LANES:]
    for off, t in ((OFF_K4, kn), (OFF_V4, vv)):
        tr = pltpu.roll(t, HEAD_DIM, 1)
        parts = (jnp.where(first_head, t, 0.0), jnp.where(first_head, 0.0, tr),
                 jnp.where(first_head, tr, 0.0), jnp.where(first_head, 0.0, t))
        for j, p in enumerate(parts):
            out_ref[:, off + j * LANES:off + (j + 1) * LANES] = p.astype(BF16)

    o0 = A_Q + 2 * A_KV
    out_ref[:, OFF_U:OFF_U + GMLP_WIDTH] = _gelu(proj(o0, o0 + GMLP_WIDTH)).astype(BF16)
    gv = _gelu(proj(o0 + GMLP_WIDTH, o0 + 2 * GMLP_WIDTH))
    mu = jnp.mean(gv, axis=-1, keepdims=True)
    var = jnp.mean(jnp.square(gv - mu), axis=-1, keepdims=True)
    out_ref[:, OFF_VN:OFF_VN + GMLP_WIDTH] = (
        (gv - mu) * lax.rsqrt(var + EPS) * lng_ref[...] + lnb_ref[...]).astype(BF16)

    o1 = o0 + 2 * GMLP_WIDTH
    pc = proj(o1, o1 + C_Q)
    for hh in range(X_HEADS):
        sl = slice(hh * X_HEAD_DIM, (hh + 1) * X_HEAD_DIM)
        out_ref[:, OFF_QC + hh * X_HEAD_DIM:OFF_QC + (hh + 1) * X_HEAD_DIM] = (
            _rms(pc[:, sl], gcq_ref[...]) * (X_HEAD_DIM ** -0.5)).astype(BF16)

    o2 = o1 + C_Q
    for j in range(3):
        sl = slice(j * d_model, (j + 1) * d_model)
        out_ref[:, OFF_GA + j * d_model:OFF_GA + (j + 1) * d_model] = jax.nn.sigmoid(
            proj(o2 + j * d_model, o2 + (j + 1) * d_model) + bg_ref[:, sl]).astype(BF16)


def _proj(x2d, pos2d, g, w_bf, invf, sgn, gq, gk, gcq, lng, lnb, bg, tm):
    t, d = x2d.shape
    d_in = w_bf.shape[1]
    pw = PROJ_W_BASE + 3 * d
    full = lambda shape: pl.BlockSpec(shape, lambda i: (0,) * len(shape))
    return pl.pallas_call(
        functools.partial(_proj_kernel, d_model=d),
        grid=(t // tm,),
        in_specs=[
            pl.BlockSpec((tm, d), lambda i: (i, 0)),
            pl.BlockSpec((tm, 1), lambda i: (i, 0)),
            full((1, d)), full((d, d_in)), full((1, LANES)), full((1, LANES)),
            full((1, LANES)), full((1, LANES)), full((1, X_HEAD_DIM)),
            full((1, GMLP_WIDTH)), full((1, GMLP_WIDTH)), full((1, 3 * d)),
        ],
        out_specs=pl.BlockSpec((tm, pw), lambda i: (i, 0)),
        out_shape=jax.ShapeDtypeStruct((t, pw), BF16),
        name="proj",
        compiler_params=pltpu.CompilerParams(
            dimension_semantics=("arbitrary",), vmem_limit_bytes=VMEM_LIMIT),
    )(x2d, pos2d, g, w_bf, invf, sgn, gq, gk, gcq, lng, lnb, bg)


def _mix_kernel(sinks_ref, proj_ref, kprev_ref, vprev_ref, x_ref, kc_ref, vc_ref, ws_ref, bst_ref,
                woa_ref, wob_ref, woc_ref, wout_ref, gffn_ref, rw_ref, rb_ref,
                x1_ref, h2_ref, idx_ref, rank_ref, gate_ref, cnt_ref,
                run_ref, oa_ref, ob_ref, oc_ref, *, d_model):
    ts = x_ref.shape[0]
    s_idx = pl.program_id(1)
    first_step = jnp.logical_and(pl.program_id(0) == 0, s_idx == 0)
    neg_inf = float("-inf")

    qi = lax.broadcasted_iota(I32, (ATT_BLOCK, 2 * ATT_BLOCK), 0)
    kj = lax.broadcasted_iota(I32, (ATT_BLOCK, 2 * ATT_BLOCK), 1)
    band = jnp.logical_and(kj <= ATT_BLOCK + qi, kj > qi)
    for qb in range(ts // ATT_BLOCK):
        r0 = qb * ATT_BLOCK
        rows = slice(r0, r0 + ATT_BLOCK)
        if qb == 0:
            kp, vp = kprev_ref[...], vprev_ref[...]
            mask = jnp.logical_and(band, jnp.logical_or(kj >= ATT_BLOCK, s_idx > 0))
        else:
            prow = slice(r0 - ATT_BLOCK, r0)
            kp = proj_ref[prow, OFF_K4:OFF_K4 + 4 * LANES]
            vp = proj_ref[prow, OFF_V4:OFF_V4 + 4 * LANES]
            mask = band
        k4 = jnp.concatenate([kp, proj_ref[rows, OFF_K4:OFF_K4 + 4 * LANES]], axis=0)
        v4 = jnp.concatenate([vp, proj_ref[rows, OFF_V4:OFF_V4 + 4 * LANES]], axis=0)
        for c in range(A_Q // LANES):
            kvh = (2 * c) // (N_Q_HEADS // N_KV_HEADS)
            qch = proj_ref[rows, OFF_Q + c * LANES:OFF_Q + (c + 1) * LANES]
            o = jnp.zeros((ATT_BLOCK, LANES), F32)
            for half in range(2):
                col = slice((2 * kvh + half) * LANES, (2 * kvh + half + 1) * LANES)
                s = lax.dot_general(qch, k4[:, col], (((1,), (1,)), ((), ())),
                                    preferred_element_type=F32)
                s = jnp.where(mask, s, neg_inf)
                sink = sinks_ref[2 * c + half]
                m = jnp.maximum(jnp.max(s, axis=-1, keepdims=True), sink)
                p = jnp.exp(s - m)
                den = jnp.sum(p, axis=-1, keepdims=True) + jnp.exp(sink - m)
                o = o + jnp.dot(p.astype(BF16), v4[:, col], preferred_element_type=F32) / den
            oa_ref[rows, c * LANES:(c + 1) * LANES] = o.astype(BF16)

    ti = lax.broadcasted_iota(I32, (GMLP_CHUNK, GMLP_CHUNK), 0)
    si = lax.broadcasted_iota(I32, (GMLP_CHUNK, GMLP_CHUNK), 1)
    for g in range(GMLP_GROUPS):
        wt = jnp.where(si <= ti, ws_ref[g], 0.0).astype(BF16)
        bcol = bst_ref[:, g:g + 1]
        for ch in range(ts // GMLP_CHUNK):
            rows = slice(ch * GMLP_CHUNK, (ch + 1) * GMLP_CHUNK)
            vn = proj_ref[rows, OFF_VN + g * LANES:OFF_VN + (g + 1) * LANES]
            u = proj_ref[rows, OFF_U + g * LANES:OFF_U + (g + 1) * LANES].astype(F32)
            mixed = jnp.dot(wt, vn, preferred_element_type=F32) + bcol
            ob_ref[rows, g * LANES:(g + 1) * LANES] = (u * mixed).astype(BF16)

    for hh in range(X_HEADS):
        sl = slice(hh * X_HEAD_DIM, (hh + 1) * X_HEAD_DIM)
        qc = proj_ref[:, OFF_QC + hh * X_HEAD_DIM:OFF_QC + (hh + 1) * X_HEAD_DIM]
        s = lax.dot_general(qc, kc_ref[:, sl], (((1,), (1,)), ((), ())),
                            preferred_element_type=F32)
        p = jnp.exp(s - jnp.max(s, axis=-1, keepdims=True))
        den = jnp.sum(p, axis=-1, keepdims=True)
        oc_ref[:, sl] = (jnp.dot(p.astype(BF16), vc_ref[:, sl],
                                 preferred_element_type=F32) / den).astype(BF16)

    def gate(j):
        return proj_ref[:, OFF_GA + j * d_model:OFF_GA + (j + 1) * d_model].astype(F32)

    merged = gate(0) * jnp.dot(oa_ref[...], woa_ref[...], preferred_element_type=F32)
    merged = merged + gate(1) * jnp.dot(ob_ref[...], wob_ref[...], preferred_element_type=F32)
    merged = merged + gate(2) * jnp.dot(oc_ref[...], woc_ref[...], preferred_element_type=F32)
    x1 = x_ref[...] + jnp.dot(merged.astype(BF16), wout_ref[...], preferred_element_type=F32)
    x1_ref[...] = x1

    h2 = _rms(x1, gffn_ref[...])
    h2_words = _pack_bf16_pairs(h2)
    for c in range(h2_ref.shape[0]):
        h2_ref[c] = h2_words[:, c * SC_CHUNK:(c + 1) * SC_CHUNK]
    lane = _lane_iota((ts, LANES))
    h2_hi = h2.astype(BF16)
    h2_lo = (h2 - h2_hi.astype(F32)).astype(BF16)
    part = jnp.dot(h2_hi, rw_ref[...], preferred_element_type=F32)
    logits = (part[:, :LANES] + part[:, LANES:]
              + jnp.dot(h2_lo, rw_ref[:, :LANES], preferred_element_type=F32) + rb_ref[...])
    logits = jnp.where(lane < N_EXPERTS, logits, neg_inf)
    vals, idxs = [], []
    for _ in range(TOP_K):
        m = jnp.max(logits, axis=-1, keepdims=True)
        i = jnp.min(jnp.where(logits == m, lane, LANES), axis=-1, keepdims=True)
        vals.append(m)
        idxs.append(i)
        logits = jnp.where(lane == i, neg_inf, logits)
    es = [jnp.exp(v - vals[0]) for v in vals]
    den = es[0] + es[1] + es[2] + es[3]

    @pl.when(first_step)
    def _():
        run_ref[...] = jnp.zeros_like(run_ref)

    hot = [lane == i for i in idxs]
    multihot = jnp.where(jnp.logical_or(jnp.logical_or(hot[0], hot[1]),
                                        jnp.logical_or(hot[2], hot[3])), 1.0, 0.0)
    tr = lax.broadcasted_iota(I32, (ts, ts), 0)
    tc = lax.broadcasted_iota(I32, (ts, ts), 1)
    ltri = jnp.where(tc < tr, 1.0, 0.0).astype(BF16)
    before = jnp.dot(ltri, multihot.astype(BF16), preferred_element_type=F32) + run_ref[...]
    idx_out = jnp.zeros((ts, LANES), I32)
    rank_out = jnp.zeros((ts, LANES), I32)
    gate_out = jnp.zeros((ts, LANES), F32)
    for k in range(TOP_K):
        rk = jnp.sum(jnp.where(hot[k], before, 0.0), axis=-1, keepdims=True)
        idx_out = jnp.where(lane == k, idxs[k], idx_out)
        rank_out = jnp.where(lane == k, rk.astype(I32), rank_out)
        gate_out = jnp.where(lane == k, es[k] / den, gate_out)
    idx_ref[...] = idx_out
    rank_ref[...] = rank_out
    gate_ref[...] = gate_out
    run_ref[...] = run_ref[...] + jnp.sum(multihot, axis=0, keepdims=True)
    cnt_ref[...] = run_ref[...]


def _mix(sinks, proj, x2d, kc, vc, w_s, bst, woa, wob, woc, wout, gffn, rw, rb,
         n_batch, seq, m_len, ts):
    t, d = x2d.shape
    pw = proj.shape[1]
    ns = seq // ts
    nblk = seq // ATT_BLOCK
    per = ts // ATT_BLOCK
    n_chunks = d // (2 * SC_CHUNK)
    full = lambda shape: pl.BlockSpec(shape, lambda b, s: (0,) * len(shape))
    row = lambda width: pl.BlockSpec((ts, width), lambda b, s: (b * ns + s, 0))
    prev = lambda colblk: pl.BlockSpec(
        (ATT_BLOCK, 4 * LANES), lambda b, s: (b * nblk + jnp.maximum(s * per - 1, 0), colblk))
    return pl.pallas_call(
        functools.partial(_mix_kernel, d_model=d),
        grid=(n_batch, ns),
        in_specs=[
            pl.BlockSpec(memory_space=pltpu.SMEM),
            row(pw), prev(OFF_K4 // (4 * LANES)), prev(OFF_V4 // (4 * LANES)), row(d),
            pl.BlockSpec((m_len, C_Q), lambda b, s: (b, 0)),
            pl.BlockSpec((m_len, C_Q), lambda b, s: (b, 0)),
            full((GMLP_GROUPS, GMLP_CHUNK, GMLP_CHUNK)), full((GMLP_CHUNK, GMLP_GROUPS)),
            full((A_Q, d)), full((GMLP_WIDTH, d)), full((C_Q, d)), full((d, d)),
            full((1, d)), full((d, 2 * LANES)), full((1, LANES)),
        ],
        out_specs=[row(d),
                   pl.BlockSpec((n_chunks, ts, SC_CHUNK), lambda b, s: (0, b * ns + s, 0)),
                   row(LANES), row(LANES), row(LANES), full((1, LANES))],
        out_shape=[
            jax.ShapeDtypeStruct((t, d), F32),
            jax.ShapeDtypeStruct((n_chunks, t, SC_CHUNK), U32),
            jax.ShapeDtypeStruct((t, LANES), I32), jax.ShapeDtypeStruct((t, LANES), I32),
            jax.ShapeDtypeStruct((t, LANES), F32), jax.ShapeDtypeStruct((1, LANES), F32),
        ],
        scratch_shapes=[
            pltpu.VMEM((1, LANES), F32),
            pltpu.VMEM((ts, A_Q), BF16), pltpu.VMEM((ts, GMLP_WIDTH), BF16),
            pltpu.VMEM((ts, C_Q), BF16),
        ],
        name="mix",
        compiler_params=pltpu.CompilerParams(
            dimension_semantics=("arbitrary", "arbitrary"), vmem_limit_bytes=VMEM_LIMIT),
    )(sinks, proj, proj, proj, x2d, kc, vc, w_s, bst, woa, wob, woc, wout, gffn, rw, rb)


def _pos_kernel(pstart_ref, idx_ref, rank_ref, pos_ref):
    idx = idx_ref[...]
    pos = rank_ref[...]
    for e in range(N_EXPERTS):
        pos = pos + jnp.where(idx == e, pstart_ref[e], 0)
    pos_ref[...] = pos


def _pos(pstart, idx, rank, tp):
    t = idx.shape[0]
    return pl.pallas_call(
        _pos_kernel,
        grid_spec=pltpu.PrefetchScalarGridSpec(
            num_scalar_prefetch=1,
            grid=(t // tp,),
            in_specs=[pl.BlockSpec((tp, LANES), lambda i, ps: (i, 0)),
                      pl.BlockSpec((tp, LANES), lambda i, ps: (i, 0))],
            out_specs=pl.BlockSpec((tp, LANES), lambda i, ps: (i, 0)),
        ),
        out_shape=jax.ShapeDtypeStruct((t, LANES), I32),
        name="slot_pos",
        compiler_params=pltpu.CompilerParams(dimension_semantics=("arbitrary",)),
    )(pstart, idx, rank)


def _sc_mesh():
    return plsc.VectorSubcoreMesh(core_axis_name="core", subcore_axis_name="subcore")


def _sc_scatter_rows(src, idx, n_rows):
    n_chunks, t, w = src.shape
    n_idx = idx.shape[1]
    src_blocks = t // SC_ROWS
    idx_blocks = n_idx // SC_ROWS

    @pl.kernel(out_type=jax.ShapeDtypeStruct((n_chunks * n_rows, w), src.dtype), mesh=_sc_mesh(),
               scratch_types=[], name="sc_dispatch")
    def scatter(src_hbm, idx_hbm, out_hbm):
        def body(src_vmem, idx_vmem):
            pltpu.sync_copy(src_vmem, out_hbm.at[idx_vmem.at[0]])

        pltpu.emit_pipeline(
            body,
            grid=(n_chunks, idx_blocks),
            in_specs=[pl.BlockSpec((SC_ROWS, w), lambda c, i: (c * src_blocks + i % src_blocks, 0)),
                      pl.BlockSpec((1, SC_ROWS), lambda c, i: (0, c * idx_blocks + i))],
            out_specs=[],
            core_axis_name=("core", "subcore"),
            dimension_semantics=(pltpu.PARALLEL, pltpu.PARALLEL),
        )(src_hbm, idx_hbm)

    return scatter(src.reshape(n_chunks * t, w), idx.reshape(1, n_chunks * n_idx)).reshape(
        n_chunks, n_rows, w)


def _sc_gather_rows(table, idx):
    n_chunks, p, w = table.shape
    n_idx = idx.shape[1]
    idx_blocks = n_idx // SC_ROWS

    @pl.kernel(out_type=jax.ShapeDtypeStruct((n_chunks * n_idx, w), table.dtype), mesh=_sc_mesh(),
               scratch_types=[], name="sc_gather")
    def gather(table_hbm, idx_hbm, out_hbm):
        def body(idx_vmem, out_vmem):
            pltpu.sync_copy(table_hbm.at[idx_vmem.at[0]], out_vmem)

        pltpu.emit_pipeline(
            body,
            grid=(n_chunks, idx_blocks),
            in_specs=[pl.BlockSpec((1, SC_ROWS), lambda c, i: (0, c * idx_blocks + i))],
            out_specs=[pl.BlockSpec((SC_ROWS, w), lambda c, i: (c * idx_blocks + i, 0))],
            core_axis_name=("core", "subcore"),
            dimension_semantics=(pltpu.PARALLEL, pltpu.PARALLEL),
        )(idx_hbm, out_hbm)

    return gather(table.reshape(n_chunks * p, w), idx.reshape(1, n_chunks * n_idx)).reshape(
        n_chunks, n_idx, w)


def _expert_kernel(bexp_ref, nvalid_ref, xs_ref, wgu_ref, bgu_ref, wd_ref, bd_ref, ys_ref,
                   wgu_bf_ref, wd_bf_ref, *, d_exp):
    b = pl.program_id(0)

    @pl.when(b < nvalid_ref[0])
    def _():
        @pl.when(jnp.logical_or(b == 0, bexp_ref[b] != bexp_ref[jnp.maximum(b - 1, 0)]))
        def _():
            wgu_bf_ref[...] = wgu_ref[...].astype(BF16)
            wd_bf_ref[...] = wd_ref[...].astype(BF16)

        n_chunks = xs_ref.shape[0]
        lo, hi = _unpack_bf16_pairs(
            jnp.concatenate([xs_ref[c] for c in range(n_chunks)], axis=1))
        xb = jnp.concatenate([lo, hi], axis=1).astype(BF16)
        gu = jnp.dot(xb, wgu_bf_ref[...], preferred_element_type=F32) + bgu_ref[...]
        gate = jnp.minimum(gu[:, :d_exp], SWIGLU_LIMIT)
        up = jnp.clip(gu[:, d_exp:], -SWIGLU_LIMIT, SWIGLU_LIMIT)
        glu = gate * jax.nn.sigmoid(gate * SWIGLU_ALPHA)
        act = ((up + 1.0) * glu).astype(BF16)
        y = jnp.dot(act, wd_bf_ref[...], preferred_element_type=F32) + bd_ref[...]
        y_words = _pack_bf16_pairs(y)
        for c in range(n_chunks):
            ys_ref[c] = y_words[:, c * SC_CHUNK:(c + 1) * SC_CHUNK]


def _experts(bexp, nvalid, xs, wgu, bgu, wd, bd):
    n_chunks, n_rows, _ = xs.shape
    _, d, d_exp2 = wgu.shape
    d_exp = d_exp2 // 2
    nb = n_rows // MOE_BLOCK
    blk = lambda b, be, nv: jnp.minimum(b, nv[0] - 1)
    return pl.pallas_call(
        functools.partial(_expert_kernel, d_exp=d_exp),
        grid_spec=pltpu.PrefetchScalarGridSpec(
            num_scalar_prefetch=2,
            grid=(nb,),
            in_specs=[
                pl.BlockSpec((n_chunks, MOE_BLOCK, SC_CHUNK),
                             lambda b, be, nv: (0, blk(b, be, nv), 0)),
                pl.BlockSpec((None, d, 2 * d_exp), lambda b, be, nv: (be[blk(b, be, nv)], 0, 0)),
                pl.BlockSpec((None, 1, 2 * d_exp), lambda b, be, nv: (be[blk(b, be, nv)], 0, 0)),
                pl.BlockSpec((None, d_exp, d), lambda b, be, nv: (be[blk(b, be, nv)], 0, 0)),
                pl.BlockSpec((None, 1, d), lambda b, be, nv: (be[blk(b, be, nv)], 0, 0)),
            ],
            out_specs=pl.BlockSpec((n_chunks, MOE_BLOCK, SC_CHUNK),
                                   lambda b, be, nv: (0, blk(b, be, nv), 0)),
            scratch_shapes=[pltpu.VMEM((d, 2 * d_exp), BF16), pltpu.VMEM((d_exp, d), BF16)],
        ),
        out_shape=jax.ShapeDtypeStruct((n_chunks, n_rows, SC_CHUNK), U32),
        name="experts",
        compiler_params=pltpu.CompilerParams(
            dimension_semantics=("arbitrary",), vmem_limit_bytes=VMEM_LIMIT),
    )(bexp, nvalid, xs, wgu, bgu, wd, bd)


def _combine_kernel(x1_ref, gate_ref, yg_ref, out_ref):
    n_chunks = yg_ref.shape[0]
    half = n_chunks * SC_CHUNK
    for c in range(n_chunks):
        sl_lo = slice(c * SC_CHUNK, (c + 1) * SC_CHUNK)
        sl_hi = slice(half + c * SC_CHUNK, half + (c + 1) * SC_CHUNK)
        acc_lo = x1_ref[:, sl_lo]
        acc_hi = x1_ref[:, sl_hi]
        for k in range(TOP_K):
            lo, hi = _unpack_bf16_pairs(yg_ref[c, k])
            g = gate_ref[:, k:k + 1]
            acc_lo = acc_lo + g * lo
            acc_hi = acc_hi + g * hi
        out_ref[:, sl_lo] = acc_lo
        out_ref[:, sl_hi] = acc_hi


def _combine(x1, gates, yg, tcb):
    t, d = x1.shape
    n_chunks = yg.shape[0]
    return pl.pallas_call(
        _combine_kernel,
        grid=(t // tcb,),
        in_specs=[
            pl.BlockSpec((tcb, d), lambda i: (i, 0)),
            pl.BlockSpec((tcb, LANES), lambda i: (i, 0)),
            pl.BlockSpec((n_chunks, TOP_K, tcb, SC_CHUNK), lambda i: (0, 0, i, 0)),
        ],
        out_specs=pl.BlockSpec((tcb, d), lambda i: (i, 0)),
        out_shape=jax.ShapeDtypeStruct((t, d), F32),
        name="combine",
        compiler_params=pltpu.CompilerParams(dimension_semantics=("arbitrary",)),
    )(x1, gates, yg)


def _pick_tile(n, pref):
    t = min(pref, n)
    while n % t:
        t //= 2
    return t


def kernel(x, mem, positions, attn_norm_g, mem_norm_g, w_in, b_gates, a_q_norm_g, a_k_norm_g,
           a_sinks, w_o_a, gmlp_ln_g, gmlp_ln_b, gmlp_w_s, gmlp_b_s, w_o_b, w_mem_kv,
           c_q_norm_g, c_k_norm_g, w_o_c, w_out, ffn_norm_g, router_w, router_b,
           w_gate_up, b_gate_up, w_down, b_down):
    n_batch, seq, d = x.shape
    m_len = mem.shape[1]
    depth = w_in.shape[0]
    t = n_batch * seq
    n_asg = t * TOP_K
    nb = -(-n_asg // MOE_BLOCK) + N_EXPERTS
    n_rows = nb * MOE_BLOCK

    inv_freq = ROPE_THETA ** (-jnp.arange(0, HEAD_DIM, 2, dtype=F32) / HEAD_DIM)
    invf = jnp.tile(inv_freq, LANES // (HEAD_DIM // 2))[None, :]
    sgn = jnp.tile(jnp.concatenate([-jnp.ones((HEAD_DIM // 2,), F32),
                                    jnp.ones((HEAD_DIM // 2,), F32)]), LANES // HEAD_DIM)[None, :]
    pos2d = positions.reshape(t, 1).astype(I32)
    mem2d = mem.reshape(n_batch * m_len, d)
    x2d = x.reshape(t, d)

    tm = _pick_tile(t, 512)
    ts = _pick_tile(seq, 256)
    tp = _pick_tile(t, 2048)
    tcb = _pick_tile(t, 256)

    for l in range(depth):
        kc, vc = _mem_kv(mem2d, mem_norm_g[l][None, :], w_mem_kv[l].astype(BF16),
                         c_k_norm_g[l][None, :], n_batch, m_len)
        proj = _proj(
            x2d, pos2d, attn_norm_g[l][None, :], w_in[l].astype(BF16), invf, sgn,
            jnp.tile(a_q_norm_g[l], LANES // HEAD_DIM)[None, :],
            jnp.tile(a_k_norm_g[l], LANES // HEAD_DIM)[None, :],
            c_q_norm_g[l][None, :], gmlp_ln_g[l][None, :], gmlp_ln_b[l][None, :],
            b_gates[l].reshape(1, 3 * d), tm)
        rw32 = jnp.pad(router_w[l], ((0, 0), (0, LANES - N_EXPERTS)))
        rw_hi = rw32.astype(BF16)
        rw = jnp.concatenate([rw_hi, (rw32 - rw_hi.astype(F32)).astype(BF16)], axis=1)
        rb = jnp.pad(router_b[l], (0, LANES - N_EXPERTS))[None, :]
        x1, h2, idx, rank, gates, counts = _mix(
            a_sinks[l], proj, x2d, kc, vc, gmlp_w_s[l], gmlp_b_s[l].T,
            w_o_a[l].astype(BF16), w_o_b[l].astype(BF16), w_o_c[l].astype(BF16),
            w_out[l].astype(BF16), ffn_norm_g[l][None, :], rw, rb, n_batch, seq, m_len, ts)

        cnt = counts[0, :N_EXPERTS].astype(I32)
        padded = (cnt + MOE_BLOCK - 1) // MOE_BLOCK * MOE_BLOCK
        pend = jnp.cumsum(padded)
        pstart = (pend - padded).astype(I32)
        nvalid = (pend[-1:] // MOE_BLOCK).astype(I32)
        blk_row = jnp.arange(nb, dtype=I32) * MOE_BLOCK
        bexp = jnp.minimum(jnp.sum((pend[None, :] <= blk_row[:, None]).astype(I32), axis=1),
                           N_EXPERTS - 1).astype(I32)
        j = jnp.arange(MOE_BLOCK, dtype=I32)[None, :]
        fill = jnp.where(j < (padded - cnt)[:, None], pend[:, None] - 1 - j, n_rows - 1).astype(I32)

        pos = _pos(pstart, idx, rank, tp)
        pos_km = pos[:, :TOP_K].T.reshape(n_asg)
        n_chunks = h2.shape[0]
        chunk_off = (jnp.arange(n_chunks, dtype=I32) * n_rows)[:, None]
        xs = _sc_scatter_rows(
            h2, jnp.concatenate([pos_km, fill.reshape(-1)])[None, :] + chunk_off, n_rows)
        ys = _experts(bexp, nvalid, xs, w_gate_up[l], b_gate_up[l][:, None, :],
                      w_down[l], b_down[l][:, None, :])
        yg = _sc_gather_rows(ys, pos_km[None, :] + chunk_off)
        x2d = _combine(x1, gates, yg.reshape(n_chunks, TOP_K, t, SC_CHUNK), tcb)
    return x2d.reshape(n_batch, seq, d)
```

```python
import functools

import numpy as np
import jax
import jax.numpy as jnp
from jax import lax
from jax.experimental import pallas as pl
from jax.experimental.pallas import tpu as pltpu
from jax.experimental.pallas import tpu_sc as plsc

F32 = jnp.float32
BF16 = jnp.bfloat16
I32 = jnp.int32
U32 = jnp.uint32
HI16 = np.uint32(0xFFFF0000)

EPS = 1e-6
LANES = 128
HEAD_DIM = 64
N_Q_HEADS = 16
N_KV_HEADS = 2
ATT_BLOCK = 128
ROPE_THETA = 10000.0
GMLP_WIDTH = 512
GMLP_GROUPS = 4
GMLP_CHUNK = 128
X_HEADS = 4
X_HEAD_DIM = 128
N_EXPERTS = 32
TOP_K = 4
SWIGLU_LIMIT = 7.0
SWIGLU_ALPHA = 1.702
MOE_BLOCK = 512
SC_ROWS = 128
SC_CHUNK = 256
N_TOKEN_GROUPS = 2

A_Q = N_Q_HEADS * HEAD_DIM
A_KV = N_KV_HEADS * HEAD_DIM
C_Q = X_HEADS * X_HEAD_DIM

OFF_Q = 0
OFF_K4 = OFF_Q + A_Q
OFF_V4 = OFF_K4 + 4 * LANES
OFF_U = OFF_V4 + 4 * LANES
OFF_VN = OFF_U + GMLP_WIDTH
OFF_QC = OFF_VN + GMLP_WIDTH
OFF_GA = OFF_QC + C_Q
PROJ_W_BASE = OFF_GA

VMEM_LIMIT = 56 * 1024 * 1024


def _lane_iota(shape):
    return lax.broadcasted_iota(I32, shape, len(shape) - 1)


def _rms(x, g):
    return x * lax.rsqrt(jnp.mean(x * x, axis=-1, keepdims=True) + EPS) * g


def _pack_bf16_pairs(x):
    n = x.shape[1] // 2
    bits = pltpu.bitcast(x.astype(BF16).astype(F32), U32)
    return (bits[:, :n] >> 16) | (bits[:, n:] & HI16)


def _unpack_bf16_pairs(w):
    return pltpu.bitcast(w << 16, F32), pltpu.bitcast(w & HI16, F32)


def _gelu(x):
    return 0.5 * x * (1.0 + lax.erf(x * np.float32(np.sqrt(0.5))))


def _memkv_kernel(mem_ref, g_ref, w_ref, gk_ref, kc_ref, vc_ref):
    h = _rms(mem_ref[...], g_ref[...]).astype(BF16)
    kv = jnp.dot(h, w_ref[...], preferred_element_type=F32)
    for hh in range(X_HEADS):
        sl = slice(hh * X_HEAD_DIM, (hh + 1) * X_HEAD_DIM)
        kc_ref[:, sl] = _rms(kv[:, sl], gk_ref[...]).astype(BF16)
    vc_ref[...] = kv[:, C_Q:].astype(BF16)


def _mem_kv(mem2d, g, w_bf, gk, n_batch, m_len):
    d = mem2d.shape[1]
    return pl.pallas_call(
        _memkv_kernel,
        grid=(n_batch,),
        in_specs=[
            pl.BlockSpec((m_len, d), lambda b: (b, 0)),
            pl.BlockSpec((1, d), lambda b: (0, 0)),
            pl.BlockSpec((d, 2 * C_Q), lambda b: (0, 0)),
            pl.BlockSpec((1, X_HEAD_DIM), lambda b: (0, 0)),
        ],
        out_specs=[
            pl.BlockSpec((m_len, C_Q), lambda b: (b, 0)),
            pl.BlockSpec((m_len, C_Q), lambda b: (b, 0)),
        ],
        out_shape=[jax.ShapeDtypeStruct((n_batch * m_len, C_Q), BF16)] * 2,
        name="mem_kv",
        compiler_params=pltpu.CompilerParams(dimension_semantics=("arbitrary",)),
    )(mem2d, g, w_bf, gk)


def _proj_kernel(x_ref, pos_ref, g_ref, w_ref, invf_ref, sgn_ref, gq_ref, gk_ref, gcq_ref,
                 lng_ref, lnb_ref, bg_ref, out_ref, *, d_model):
    tm = x_ref.shape[0]
    h = _rms(x_ref[...], g_ref[...]).astype(BF16)

    ang = pos_ref[...].astype(F32) * invf_ref[...]
    cosv = jnp.cos(ang)
    sinv = jnp.sin(ang) * sgn_ref[...]
    lane = _lane_iota((tm, LANES))
    first_head = lane < HEAD_DIM
    lo_half = (lane % HEAD_DIM) < (HEAD_DIM // 2)

    def head_norm_rope(blk, g):
        y = blk * blk
        s_lo = jnp.sum(jnp.where(first_head, y, 0.0), axis=-1, keepdims=True)
        s_hi = jnp.sum(jnp.where(first_head, 0.0, y), axis=-1, keepdims=True)
        ss = jnp.where(first_head, s_lo, s_hi)
        n = blk * lax.rsqrt(ss * (1.0 / HEAD_DIM) + EPS) * g
        rot = jnp.where(lo_half, pltpu.roll(n, LANES - HEAD_DIM // 2, 1),
                        pltpu.roll(n, HEAD_DIM // 2, 1))
        return n * cosv + rot * sinv

    def proj(a, b):
        return jnp.dot(h, w_ref[:, a:b], preferred_element_type=F32)

    pq = proj(0, A_Q)
    for c in range(A_Q // LANES):
        sl = slice(c * LANES, (c + 1) * LANES)
        out_ref[:, OFF_Q + c * LANES:OFF_Q + (c + 1) * LANES] = (
            head_norm_rope(pq[:, sl], gq_ref[...]) * (HEAD_DIM ** -0.5)).astype(BF16)

    pkv = proj(A_Q, A_Q + 2 * A_KV)
    kn = head_norm_rope(pkv[:, :LANES], gk_ref[...])
    vv = pkv[:, LANES:]
    for off, t in ((OFF_K4, kn), (OFF_V4, vv)):
        tr = pltpu.roll(t, HEAD_DIM, 1)
        parts = (jnp.where(first_head, t, 0.0), jnp.where(first_head, 0.0, tr),
                 jnp.where(first_head, tr, 0.0), jnp.where(first_head, 0.0, t))
        for j, p in enumerate(parts):
            out_ref[:, off + j * LANES:off + (j + 1) * LANES] = p.astype(BF16)

    o0 = A_Q + 2 * A_KV
    out_ref[:, OFF_U:OFF_U + GMLP_WIDTH] = _gelu(proj(o0, o0 + GMLP_WIDTH)).astype(BF16)
    gv = _gelu(proj(o0 + GMLP_WIDTH, o0 + 2 * GMLP_WIDTH))
    mu = jnp.mean(gv, axis=-1, keepdims=True)
    var = jnp.mean(jnp.square(gv - mu), axis=-1, keepdims=True)
    out_ref[:, OFF_VN:OFF_VN + GMLP_WIDTH] = (
        (gv - mu) * lax.rsqrt(var + EPS) * lng_ref[...] + lnb_ref[...]).astype(BF16)

    o1 = o0 + 2 * GMLP_WIDTH
    pc = proj(o1, o1 + C_Q)
    for hh in range(X_HEADS):
        sl = slice(hh * X_HEAD_DIM, (hh + 1) * X_HEAD_DIM)
        out_ref[:, OFF_QC + hh * X_HEAD_DIM:OFF_QC + (hh + 1) * X_HEAD_DIM] = (
            _rms(pc[:, sl], gcq_ref[...]) * (X_HEAD_DIM ** -0.5)).astype(BF16)

    o2 = o1 + C_Q
    for j in range(3):
        sl = slice(j * d_model, (j + 1) * d_model)
        out_ref[:, OFF_GA + j * d_model:OFF_GA + (j + 1) * d_model] = jax.nn.sigmoid(
            proj(o2 + j * d_model, o2 + (j + 1) * d_model) + bg_ref[:, sl]).astype(BF16)


def _proj(x2d, pos2d, g, w_bf, invf, sgn, gq, gk, gcq, lng, lnb, bg, tm, row0, t):
    d = x2d.shape[1]
    d_in = w_bf.shape[1]
    pw = PROJ_W_BASE + 3 * d
    blk0 = row0 // tm
    full = lambda shape: pl.BlockSpec(shape, lambda i: (0,) * len(shape))
    return pl.pallas_call(
        functools.partial(_proj_kernel, d_model=d),
        grid=(t // tm,),
        in_specs=[
            pl.BlockSpec((tm, d), lambda i: (blk0 + i, 0)),
            pl.BlockSpec((tm, 1), lambda i: (blk0 + i, 0)),
            full((1, d)), full((d, d_in)), full((1, LANES)), full((1, LANES)),
            full((1, LANES)), full((1, LANES)), full((1, X_HEAD_DIM)),
            full((1, GMLP_WIDTH)), full((1, GMLP_WIDTH)), full((1, 3 * d)),
        ],
        out_specs=pl.BlockSpec((tm, pw), lambda i: (i, 0)),
        out_shape=jax.ShapeDtypeStruct((t, pw), BF16),
        name="proj",
        compiler_params=pltpu.CompilerParams(
            dimension_semantics=("arbitrary",), vmem_limit_bytes=VMEM_LIMIT),
    )(x2d, pos2d, g, w_bf, invf, sgn, gq, gk, gcq, lng, lnb, bg)


def _mix_kernel(sinks_ref, proj_ref, kprev_ref, vprev_ref, x_ref, kc_ref, vc_ref, ws_ref, bst_ref,
                woa_ref, wob_ref, woc_ref, wout_ref, gffn_ref, rw_ref, rb_ref,
                x1_ref, h2_ref, idx_ref, rank_ref, gate_ref, cnt_ref,
                run_ref, oa_ref, ob_ref, oc_ref, *, d_model):
    ts = x_ref.shape[0]
    s_idx = pl.program_id(1)
    first_step = jnp.logical_and(pl.program_id(0) == 0, s_idx == 0)
    neg_inf = float("-inf")

    qi = lax.broadcasted_iota(I32, (ATT_BLOCK, 2 * ATT_BLOCK), 0)
    kj = lax.broadcasted_iota(I32, (ATT_BLOCK, 2 * ATT_BLOCK), 1)
    band = jnp.logical_and(kj <= ATT_BLOCK + qi, kj > qi)
    for qb in range(ts // ATT_BLOCK):
        r0 = qb * ATT_BLOCK
        rows = slice(r0, r0 + ATT_BLOCK)
        if qb == 0:
            kp, vp = kprev_ref[...], vprev_ref[...]
            mask = jnp.logical_and(band, jnp.logical_or(kj >= ATT_BLOCK, s_idx > 0))
        else:
            prow = slice(r0 - ATT_BLOCK, r0)
            kp = proj_ref[prow, OFF_K4:OFF_K4 + 4 * LANES]
            vp = proj_ref[prow, OFF_V4:OFF_V4 + 4 * LANES]
            mask = band
        k4 = jnp.concatenate([kp, proj_ref[rows, OFF_K4:OFF_K4 + 4 * LANES]], axis=0)
        v4 = jnp.concatenate([vp, proj_ref[rows, OFF_V4:OFF_V4 + 4 * LANES]], axis=0)
        for c in range(A_Q // LANES):
            kvh = (2 * c) // (N_Q_HEADS // N_KV_HEADS)
            qch = proj_ref[rows, OFF_Q + c * LANES:OFF_Q + (c + 1) * LANES]
            o = jnp.zeros((ATT_BLOCK, LANES), F32)
            for half in range(2):
                col = slice((2 * kvh + half) * LANES, (2 * kvh + half + 1) * LANES)
                s = lax.dot_general(qch, k4[:, col], (((1,), (1,)), ((), ())),
                                    preferred_element_type=F32)
                s = jnp.where(mask, s, neg_inf)
                sink = sinks_ref[2 * c + half]
                m = jnp.maximum(jnp.max(s, axis=-1, keepdims=True), sink)
                p = jnp.exp(s - m)
                den = jnp.sum(p, axis=-1, keepdims=True) + jnp.exp(sink - m)
                o = o + jnp.dot(p.astype(BF16), v4[:, col], preferred_element_type=F32) / den
            oa_ref[rows, c * LANES:(c + 1) * LANES] = o.astype(BF16)

    ti = lax.broadcasted_iota(I32, (GMLP_CHUNK, GMLP_CHUNK), 0)
    si = lax.broadcasted_iota(I32, (GMLP_CHUNK, GMLP_CHUNK), 1)
    for g in range(GMLP_GROUPS):
        wt = jnp.where(si <= ti, ws_ref[g], 0.0).astype(BF16)
        bcol = bst_ref[:, g:g + 1]
        for ch in range(ts // GMLP_CHUNK):
            rows = slice(ch * GMLP_CHUNK, (ch + 1) * GMLP_CHUNK)
            vn = proj_ref[rows, OFF_VN + g * LANES:OFF_VN + (g + 1) * LANES]
            u = proj_ref[rows, OFF_U + g * LANES:OFF_U + (g + 1) * LANES].astype(F32)
            mixed = jnp.dot(wt, vn, preferred_element_type=F32) + bcol
            ob_ref[rows, g * LANES:(g + 1) * LANES] = (u * mixed).astype(BF16)

    for hh in range(X_HEADS):
        sl = slice(hh * X_HEAD_DIM, (hh + 1) * X_HEAD_DIM)
        qc = proj_ref[:, OFF_QC + hh * X_HEAD_DIM:OFF_QC + (hh + 1) * X_HEAD_DIM]
        s = lax.dot_general(qc, kc_ref[:, sl], (((1,), (1,)), ((), ())),
                            preferred_element_type=F32)
        p = jnp.exp(s - jnp.max(s, axis=-1, keepdims=True))
        den = jnp.sum(p, axis=-1, keepdims=True)
        oc_ref[:, sl] = (jnp.dot(p.astype(BF16), vc_ref[:, sl],
                                 preferred_element_type=F32) / den).astype(BF16)

    def gate(j):
        return proj_ref[:, OFF_GA + j * d_model:OFF_GA + (j + 1) * d_model].astype(F32)

    merged = gate(0) * jnp.dot(oa_ref[...], woa_ref[...], preferred_element_type=F32)
    merged = merged + gate(1) * jnp.dot(ob_ref[...], wob_ref[...], preferred_element_type=F32)
    merged = merged + gate(2) * jnp.dot(oc_ref[...], woc_ref[...], preferred_element_type=F32)
    x1 = x_ref[...] + jnp.dot(merged.astype(BF16), wout_ref[...], preferred_element_type=F32)
    x1_ref[...] = x1

    h2 = _rms(x1, gffn_ref[...])
    h2_words = _pack_bf16_pairs(h2)
    for c in range(h2_ref.shape[0]):
        h2_ref[c] = h2_words[:, c * SC_CHUNK:(c + 1) * SC_CHUNK]
    lane = _lane_iota((ts, LANES))
    h2_hi = h2.astype(BF16)
    h2_lo = (h2 - h2_hi.astype(F32)).astype(BF16)
    part = jnp.dot(h2_hi, rw_ref[...], preferred_element_type=F32)
    logits = (part[:, :LANES] + part[:, LANES:]
              + jnp.dot(h2_lo, rw_ref[:, :LANES], preferred_element_type=F32) + rb_ref[...])
    logits = jnp.where(lane < N_EXPERTS, logits, neg_inf)
    vals, idxs = [], []
    for _ in range(TOP_K):
        m = jnp.max(logits, axis=-1, keepdims=True)
        i = jnp.min(jnp.where(logits == m, lane, LANES), axis=-1, keepdims=True)
        vals.append(m)
        idxs.append(i)
        logits = jnp.where(lane == i, neg_inf, logits)
    es = [jnp.exp(v - vals[0]) for v in vals]
    den = es[0] + es[1] + es[2] + es[3]

    @pl.when(first_step)
    def _():
        run_ref[...] = jnp.zeros_like(run_ref)

    hot = [lane == i for i in idxs]
    multihot = jnp.where(jnp.logical_or(jnp.logical_or(hot[0], hot[1]),
                                        jnp.logical_or(hot[2], hot[3])), 1.0, 0.0)
    tr = lax.broadcasted_iota(I32, (ts, ts), 0)
    tc = lax.broadcasted_iota(I32, (ts, ts), 1)
    ltri = jnp.where(tc < tr, 1.0, 0.0).astype(BF16)
    before = jnp.dot(ltri, multihot.astype(BF16), preferred_element_type=F32) + run_ref[...]
    idx_out = jnp.zeros((ts, LANES), I32)
    rank_out = jnp.zeros((ts, LANES), I32)
    gate_out = jnp.zeros((ts, LANES), F32)
    for k in range(TOP_K):
        rk = jnp.sum(jnp.where(hot[k], before, 0.0), axis=-1, keepdims=True)
        idx_out = jnp.where(lane == k, idxs[k], idx_out)
        rank_out = jnp.where(lane == k, rk.astype(I32), rank_out)
        gate_out = jnp.where(lane == k, es[k] / den, gate_out)
    idx_ref[...] = idx_out
    rank_ref[...] = rank_out
    gate_ref[...] = gate_out
    run_ref[...] = run_ref[...] + jnp.sum(multihot, axis=0, keepdims=True)
    cnt_ref[...] = run_ref[...]


def _mix(sinks, proj, x2d, kc, vc, w_s, bst, woa, wob, woc, wout, gffn, rw, rb,
         n_batch, seq, m_len, ts, batch0):
    d = x2d.shape[1]
    t = n_batch * seq
    pw = proj.shape[1]
    ns = seq // ts
    nblk = seq // ATT_BLOCK
    per = ts // ATT_BLOCK
    n_chunks = d // (2 * SC_CHUNK)
    full = lambda shape: pl.BlockSpec(shape, lambda b, s: (0,) * len(shape))
    row = lambda width: pl.BlockSpec((ts, width), lambda b, s: (b * ns + s, 0))
    prev = lambda colblk: pl.BlockSpec(
        (ATT_BLOCK, 4 * LANES), lambda b, s: (b * nblk + jnp.maximum(s * per - 1, 0), colblk))
    return pl.pallas_call(
        functools.partial(_mix_kernel, d_model=d),
        grid=(n_batch, ns),
        in_specs=[
            pl.BlockSpec(memory_space=pltpu.SMEM),
            row(pw), prev(OFF_K4 // (4 * LANES)), prev(OFF_V4 // (4 * LANES)),
            pl.BlockSpec((ts, d), lambda b, s: ((batch0 + b) * ns + s, 0)),
            pl.BlockSpec((m_len, C_Q), lambda b, s: (batch0 + b, 0)),
            pl.BlockSpec((m_len, C_Q), lambda b, s: (batch0 + b, 0)),
            full((GMLP_GROUPS, GMLP_CHUNK, GMLP_CHUNK)), full((GMLP_CHUNK, GMLP_GROUPS)),
            full((A_Q, d)), full((GMLP_WIDTH, d)), full((C_Q, d)), full((d, d)),
            full((1, d)), full((d, 2 * LANES)), full((1, LANES)),
        ],
        out_specs=[row(d),
                   pl.BlockSpec((n_chunks, ts, SC_CHUNK), lambda b, s: (0, b * ns + s, 0)),
                   row(LANES), row(LANES), row(LANES), full((1, LANES))],
        out_shape=[
            jax.ShapeDtypeStruct((t, d), F32),
            jax.ShapeDtypeStruct((n_chunks, t, SC_CHUNK), U32),
            jax.ShapeDtypeStruct((t, LANES), I32), jax.ShapeDtypeStruct((t, LANES), I32),
            jax.ShapeDtypeStruct((t, LANES), F32), jax.ShapeDtypeStruct((1, LANES), F32),
        ],
        scratch_shapes=[
            pltpu.VMEM((1, LANES), F32),
            pltpu.VMEM((ts, A_Q), BF16), pltpu.VMEM((ts, GMLP_WIDTH), BF16),
            pltpu.VMEM((ts, C_Q), BF16),
        ],
        name="mix",
        compiler_params=pltpu.CompilerParams(
            dimension_semantics=("arbitrary", "arbitrary"), vmem_limit_bytes=VMEM_LIMIT),
    )(sinks, proj, proj, proj, x2d, kc, vc, w_s, bst, woa, wob, woc, wout, gffn, rw, rb)


def _pos_kernel(pstart_ref, idx_ref, rank_ref, pos_ref):
    idx = idx_ref[...]
    pos = rank_ref[...]
    for e in range(N_EXPERTS):
        pos = pos + jnp.where(idx == e, pstart_ref[e], 0)
    pos_ref[...] = pos


def _pos(pstart, idx, rank, tp):
    t = idx.shape[0]
    return pl.pallas_call(
        _pos_kernel,
        grid_spec=pltpu.PrefetchScalarGridSpec(
            num_scalar_prefetch=1,
            grid=(t // tp,),
            in_specs=[pl.BlockSpec((tp, LANES), lambda i, ps: (i, 0)),
                      pl.BlockSpec((tp, LANES), lambda i, ps: (i, 0))],
            out_specs=pl.BlockSpec((tp, LANES), lambda i, ps: (i, 0)),
        ),
        out_shape=jax.ShapeDtypeStruct((t, LANES), I32),
        name="slot_pos",
        compiler_params=pltpu.CompilerParams(dimension_semantics=("arbitrary",)),
    )(pstart, idx, rank)


def _sc_mesh():
    return plsc.VectorSubcoreMesh(core_axis_name="core", subcore_axis_name="subcore")


def _sc_scatter_rows(src, idx, n_rows):
    n_chunks, t, w = src.shape
    n_idx = idx.shape[1]
    src_blocks = t // SC_ROWS
    idx_blocks = n_idx // SC_ROWS

    @pl.kernel(out_type=jax.ShapeDtypeStruct((n_chunks * n_rows, w), src.dtype), mesh=_sc_mesh(),
               scratch_types=[], name="sc_dispatch")
    def scatter(src_hbm, idx_hbm, out_hbm):
        def body(src_vmem, idx_vmem):
            pltpu.sync_copy(src_vmem, out_hbm.at[idx_vmem.at[0]])

        pltpu.emit_pipeline(
            body,
            grid=(n_chunks, idx_blocks),
            in_specs=[pl.BlockSpec((SC_ROWS, w), lambda c, i: (c * src_blocks + i % src_blocks, 0)),
                      pl.BlockSpec((1, SC_ROWS), lambda c, i: (0, c * idx_blocks + i))],
            out_specs=[],
            core_axis_name=("core", "subcore"),
            dimension_semantics=(pltpu.PARALLEL, pltpu.PARALLEL),
        )(src_hbm, idx_hbm)

    return scatter(src.reshape(n_chunks * t, w), idx.reshape(1, n_chunks * n_idx)).reshape(
        n_chunks, n_rows, w)


def _sc_gather_rows(table, idx):
    n_chunks, p, w = table.shape
    n_idx = idx.shape[1]
    idx_blocks = n_idx // SC_ROWS

    @pl.kernel(out_type=jax.ShapeDtypeStruct((n_chunks * n_idx, w), table.dtype), mesh=_sc_mesh(),
               scratch_types=[], name="sc_gather")
    def gather(table_hbm, idx_hbm, out_hbm):
        def body(idx_vmem, out_vmem):
            pltpu.sync_copy(table_hbm.at[idx_vmem.at[0]], out_vmem)

        pltpu.emit_pipeline(
            body,
            grid=(n_chunks, idx_blocks),
            in_specs=[pl.BlockSpec((1, SC_ROWS), lambda c, i: (0, c * idx_blocks + i))],
            out_specs=[pl.BlockSpec((SC_ROWS, w), lambda c, i: (c * idx_blocks + i, 0))],
            core_axis_name=("core", "subcore"),
            dimension_semantics=(pltpu.PARALLEL, pltpu.PARALLEL),
        )(idx_hbm, out_hbm)

    return gather(table.reshape(n_chunks * p, w), idx.reshape(1, n_chunks * n_idx)).reshape(
        n_chunks, n_idx, w)


def _expert_kernel(bexp_ref, nvalid_ref, xs_ref, wgu_ref, bgu_ref, wd_ref, bd_ref, ys_ref,
                   wgu_bf_ref, wd_bf_ref, *, d_exp):
    b = pl.program_id(0)

    @pl.when(b < nvalid_ref[0])
    def _():
        @pl.when(jnp.logical_or(b == 0, bexp_ref[b] != bexp_ref[jnp.maximum(b - 1, 0)]))
        def _():
            wgu_bf_ref[...] = wgu_ref[...].astype(BF16)
            wd_bf_ref[...] = wd_ref[...].astype(BF16)

        n_chunks = xs_ref.shape[0]
        lo, hi = _unpack_bf16_pairs(
            jnp.concatenate([xs_ref[c] for c in range(n_chunks)], axis=1))
        xb = jnp.concatenate([lo, hi], axis=1).astype(BF16)
        gu = jnp.dot(xb, wgu_bf_ref[...], preferred_element_type=F32) + bgu_ref[...]
        gate = jnp.minimum(gu[:, :d_exp], SWIGLU_LIMIT)
        up = jnp.clip(gu[:, d_exp:], -SWIGLU_LIMIT, SWIGLU_LIMIT)
        glu = gate * jax.nn.sigmoid(gate * SWIGLU_ALPHA)
        act = ((up + 1.0) * glu).astype(BF16)
        y = jnp.dot(act, wd_bf_ref[...], preferred_element_type=F32) + bd_ref[...]
        y_words = _pack_bf16_pairs(y)
        for c in range(n_chunks):
            ys_ref[c] = y_words[:, c * SC_CHUNK:(c + 1) * SC_CHUNK]


def _experts(bexp, nvalid, xs, wgu, bgu, wd, bd):
    n_chunks, n_rows, _ = xs.shape
    _, d, d_exp2 = wgu.shape
    d_exp = d_exp2 // 2
    nb = n_rows // MOE_BLOCK
    blk = lambda b, be, nv: jnp.minimum(b, nv[0] - 1)
    return pl.pallas_call(
        functools.partial(_expert_kernel, d_exp=d_exp),
        grid_spec=pltpu.PrefetchScalarGridSpec(
            num_scalar_prefetch=2,
            grid=(nb,),
            in_specs=[
                pl.BlockSpec((n_chunks, MOE_BLOCK, SC_CHUNK),
                             lambda b, be, nv: (0, blk(b, be, nv), 0)),
                pl.BlockSpec((None, d, 2 * d_exp), lambda b, be, nv: (be[blk(b, be, nv)], 0, 0)),
                pl.BlockSpec((None, 1, 2 * d_exp), lambda b, be, nv: (be[blk(b, be, nv)], 0, 0)),
                pl.BlockSpec((None, d_exp, d), lambda b, be, nv: (be[blk(b, be, nv)], 0, 0)),
                pl.BlockSpec((None, 1, d), lambda b, be, nv: (be[blk(b, be, nv)], 0, 0)),
            ],
            out_specs=pl.BlockSpec((n_chunks, MOE_BLOCK, SC_CHUNK),
                                   lambda b, be, nv: (0, blk(b, be, nv), 0)),
            scratch_shapes=[pltpu.VMEM((d, 2 * d_exp), BF16), pltpu.VMEM((d_exp, d), BF16)],
        ),
        out_shape=jax.ShapeDtypeStruct((n_chunks, n_rows, SC_CHUNK), U32),
        name="experts",
        compiler_params=pltpu.CompilerParams(
            dimension_semantics=("arbitrary",), vmem_limit_bytes=VMEM_LIMIT),
    )(bexp, nvalid, xs, wgu, bgu, wd, bd)


def _combine_kernel(prev_ref, x1_ref, gate_ref, yg_ref, out_ref):
    del prev_ref
    n_chunks = yg_ref.shape[0]
    half = n_chunks * SC_CHUNK
    for c in range(n_chunks):
        sl_lo = slice(c * SC_CHUNK, (c + 1) * SC_CHUNK)
        sl_hi = slice(half + c * SC_CHUNK, half + (c + 1) * SC_CHUNK)
        acc_lo = x1_ref[:, sl_lo]
        acc_hi = x1_ref[:, sl_hi]
        for k in range(TOP_K):
            lo, hi = _unpack_bf16_pairs(yg_ref[c, k])
            g = gate_ref[:, k:k + 1]
            acc_lo = acc_lo + g * lo
            acc_hi = acc_hi + g * hi
        out_ref[:, sl_lo] = acc_lo
        out_ref[:, sl_hi] = acc_hi


def _combine(x1, gates, yg, out_prev, tcb, row0, t_total):
    t, d = x1.shape
    n_chunks = yg.shape[0]
    blk0 = row0 // tcb
    in_specs = [
        pl.BlockSpec((tcb, d), lambda i: (i, 0)),
        pl.BlockSpec((tcb, LANES), lambda i: (i, 0)),
        pl.BlockSpec((n_chunks, TOP_K, tcb, SC_CHUNK), lambda i: (0, 0, i, 0)),
    ]
    args = [x1, gates, yg]
    aliases = {}
    body = functools.partial(_combine_kernel, None)
    if out_prev is not None:
        in_specs.append(pl.BlockSpec(memory_space=pl.ANY))
        args.append(out_prev)
        aliases = {3: 0}
        body = lambda a, b, c, prev, o: _combine_kernel(prev, a, b, c, o)
    return pl.pallas_call(
        body,
        grid=(t // tcb,),
        in_specs=in_specs,
        out_specs=pl.BlockSpec((tcb, d), lambda i: (blk0 + i, 0)),
        out_shape=jax.ShapeDtypeStruct((t_total, d), F32),
        input_output_aliases=aliases,
        name="combine",
        compiler_params=pltpu.CompilerParams(dimension_semantics=("arbitrary",)),
    )(*args)


def _pick_tile(n, pref):
    t = min(pref, n)
    while n % t:
        t //= 2
    return t


def kernel(x, mem, positions, attn_norm_g, mem_norm_g, w_in, b_gates, a_q_norm_g, a_k_norm_g,
           a_sinks, w_o_a, gmlp_ln_g, gmlp_ln_b, gmlp_w_s, gmlp_b_s, w_o_b, w_mem_kv,
           c_q_norm_g, c_k_norm_g, w_o_c, w_out, ffn_norm_g, router_w, router_b,
           w_gate_up, b_gate_up, w_down, b_down):
    n_batch, seq, d = x.shape
    m_len = mem.shape[1]
    depth = w_in.shape[0]
    t = n_batch * seq
    n_groups = N_TOKEN_GROUPS if n_batch % N_TOKEN_GROUPS == 0 else 1
    gb = n_batch // n_groups
    tg = gb * seq
    n_asg = tg * TOP_K
    nb = -(-n_asg // MOE_BLOCK) + N_EXPERTS
    n_rows = nb * MOE_BLOCK

    inv_freq = ROPE_THETA ** (-jnp.arange(0, HEAD_DIM, 2, dtype=F32) / HEAD_DIM)
    invf = jnp.tile(inv_freq, LANES // (HEAD_DIM // 2))[None, :]
    sgn = jnp.tile(jnp.concatenate([-jnp.ones((HEAD_DIM // 2,), F32),
                                    jnp.ones((HEAD_DIM // 2,), F32)]), LANES // HEAD_DIM)[None, :]
    pos2d = positions.reshape(t, 1).astype(I32)
    mem2d = mem.reshape(n_batch * m_len, d)
    x2d = x.reshape(t, d)

    tm = _pick_tile(tg, 512)
    ts = _pick_tile(seq, 256)
    tp = _pick_tile(tg, 2048)
    tcb = _pick_tile(tg, 256)

    for l in range(depth):
        kc, vc = _mem_kv(mem2d, mem_norm_g[l][None, :], w_mem_kv[l].astype(BF16),
                         c_k_norm_g[l][None, :], n_batch, m_len)
        w_in_bf = w_in[l].astype(BF16)
        mix_w = (w_o_a[l].astype(BF16), w_o_b[l].astype(BF16), w_o_c[l].astype(BF16),
                 w_out[l].astype(BF16))
        gq = jnp.tile(a_q_norm_g[l], LANES // HEAD_DIM)[None, :]
        gk = jnp.tile(a_k_norm_g[l], LANES // HEAD_DIM)[None, :]
        rw32 = jnp.pad(router_w[l], ((0, 0), (0, LANES - N_EXPERTS)))
        rw_hi = rw32.astype(BF16)
        rw = jnp.concatenate([rw_hi, (rw32 - rw_hi.astype(F32)).astype(BF16)], axis=1)
        rb = jnp.pad(router_b[l], (0, LANES - N_EXPERTS))[None, :]

        routed = []
        for g in range(n_groups):
            proj = _proj(
                x2d, pos2d, attn_norm_g[l][None, :], w_in_bf, invf, sgn, gq, gk,
                c_q_norm_g[l][None, :], gmlp_ln_g[l][None, :], gmlp_ln_b[l][None, :],
                b_gates[l].reshape(1, 3 * d), tm, g * tg, tg)
            x1, h2, idx, rank, gates, counts = _mix(
                a_sinks[l], proj, x2d, kc, vc, gmlp_w_s[l], gmlp_b_s[l].T, *mix_w,
                ffn_norm_g[l][None, :], rw, rb, gb, seq, m_len, ts, g * gb)

            cnt = counts[0, :N_EXPERTS].astype(I32)
            padded = (cnt + MOE_BLOCK - 1) // MOE_BLOCK * MOE_BLOCK
            pend = jnp.cumsum(padded)
            pstart = (pend - padded).astype(I32)
            nvalid = (pend[-1:] // MOE_BLOCK).astype(I32)
            blk_row = jnp.arange(nb, dtype=I32) * MOE_BLOCK
            bexp = jnp.minimum(jnp.sum((pend[None, :] <= blk_row[:, None]).astype(I32), axis=1),
                               N_EXPERTS - 1).astype(I32)
            j = jnp.arange(MOE_BLOCK, dtype=I32)[None, :]
            fill = jnp.where(j < (padded - cnt)[:, None], pend[:, None] - 1 - j,
                             n_rows - 1).astype(I32)

            pos = _pos(pstart, idx, rank, tp)
            pos_km = pos[:, :TOP_K].T.reshape(n_asg)
            n_chunks = h2.shape[0]
            chunk_off = (jnp.arange(n_chunks, dtype=I32) * n_rows)[:, None]
            xs = _sc_scatter_rows(
                h2, jnp.concatenate([pos_km, fill.reshape(-1)])[None, :] + chunk_off, n_rows)
            routed.append((x1, gates, xs, bexp, nvalid, pos_km, chunk_off))

        gathered = []
        for x1, gates, xs, bexp, nvalid, pos_km, chunk_off in routed:
            ys = _experts(bexp, nvalid, xs, w_gate_up[l], b_gate_up[l][:, None, :],
                          w_down[l], b_down[l][:, None, :])
            yg = _sc_gather_rows(ys, pos_km[None, :] + chunk_off)
            gathered.append((x1, gates, yg.reshape(yg.shape[0], TOP_K, tg, SC_CHUNK)))

        out = None
        for g, (x1, gates, yg) in enumerate(gathered):
            out = _combine(x1, gates, yg, out, tcb, g * tg, t)
        x2d = out
    return x2d.reshape(n_batch, seq, d)
```

```python
import functools

import numpy as np
import jax
import jax.numpy as jnp
from jax import lax
from jax.experimental import pallas as pl
from jax.experimental.pallas import tpu as pltpu
from jax.experimental.pallas import tpu_sc as plsc

F32 = jnp.float32
BF16 = jnp.bfloat16
I32 = jnp.int32
U32 = jnp.uint32
HI16 = np.uint32(0xFFFF0000)

EPS = 1e-6
LANES = 128
HEAD_DIM = 64
N_Q_HEADS = 16
N_KV_HEADS = 2
ATT_BLOCK = 128
ROPE_THETA = 10000.0
GMLP_WIDTH = 512
GMLP_GROUPS = 4
GMLP_CHUNK = 128
X_HEADS = 4
X_HEAD_DIM = 128
N_EXPERTS = 32
TOP_K = 4
SWIGLU_LIMIT = 7.0
SWIGLU_ALPHA = 1.702
MOE_BLOCK = 512
SC_ROWS = 128
SC_CHUNK = 256
N_TOKEN_GROUPS = 2

A_Q = N_Q_HEADS * HEAD_DIM
A_KV = N_KV_HEADS * HEAD_DIM
C_Q = X_HEADS * X_HEAD_DIM

OFF_Q = 0
OFF_K4 = OFF_Q + A_Q
OFF_V4 = OFF_K4 + 4 * LANES
OFF_U = OFF_V4 + 4 * LANES
OFF_VN = OFF_U + GMLP_WIDTH
OFF_QC = OFF_VN + GMLP_WIDTH
OFF_GA = OFF_QC + C_Q
PROJ_W_BASE = OFF_GA

VMEM_LIMIT = 56 * 1024 * 1024


def _lane_iota(shape):
    return lax.broadcasted_iota(I32, shape, len(shape) - 1)


def _rms(x, g):
    return x * lax.rsqrt(jnp.mean(x * x, axis=-1, keepdims=True) + EPS) * g


def _pack_bf16_pairs(x):
    n = x.shape[1] // 2
    bits = pltpu.bitcast(x.astype(BF16).astype(F32), U32)
    return (bits[:, :n] >> 16) | (bits[:, n:] & HI16)


def _unpack_bf16_pairs(w):
    return pltpu.bitcast(w << 16, F32), pltpu.bitcast(w & HI16, F32)


def _gelu(x):
    return 0.5 * x * (1.0 + lax.erf(x * np.float32(np.sqrt(0.5))))


def _memkv_kernel(mem_ref, g_ref, w_ref, gk_ref, kc_ref, vc_ref):
    h = _rms(mem_ref[...], g_ref[...]).astype(BF16)
    kv = jnp.dot(h, w_ref[...], preferred_element_type=F32)
    for hh in range(X_HEADS):
        sl = slice(hh * X_HEAD_DIM, (hh + 1) * X_HEAD_DIM)
        kc_ref[:, sl] = _rms(kv[:, sl], gk_ref[...]).astype(BF16)
    vc_ref[...] = kv[:, C_Q:].astype(BF16)


def _mem_kv(mem2d, g, w_bf, gk, n_batch, m_len):
    d = mem2d.shape[1]
    return pl.pallas_call(
        _memkv_kernel,
        grid=(n_batch,),
        in_specs=[
            pl.BlockSpec((m_len, d), lambda b: (b, 0)),
            pl.BlockSpec((1, d), lambda b: (0, 0)),
            pl.BlockSpec((d, 2 * C_Q), lambda b: (0, 0)),
            pl.BlockSpec((1, X_HEAD_DIM), lambda b: (0, 0)),
        ],
        out_specs=[
            pl.BlockSpec((m_len, C_Q), lambda b: (b, 0)),
            pl.BlockSpec((m_len, C_Q), lambda b: (b, 0)),
        ],
        out_shape=[jax.ShapeDtypeStruct((n_batch * m_len, C_Q), BF16)] * 2,
        name="mem_kv",
        compiler_params=pltpu.CompilerParams(dimension_semantics=("arbitrary",)),
    )(mem2d, g, w_bf, gk)


def _proj_kernel(x_ref, pos_ref, g_ref, w_ref, invf_ref, sgn_ref, gq_ref, gk_ref, gcq_ref,
                 lng_ref, lnb_ref, bg_ref, out_ref, *, d_model):
    tm = x_ref.shape[0]
    h = _rms(x_ref[...], g_ref[...]).astype(BF16)

    ang = pos_ref[...].astype(F32) * invf_ref[...]
    cosv = jnp.cos(ang)
    sinv = jnp.sin(ang) * sgn_ref[...]
    lane = _lane_iota((tm, LANES))
    first_head = lane < HEAD_DIM
    lo_half = (lane % HEAD_DIM) < (HEAD_DIM // 2)

    def head_norm_rope(blk, g):
        y = blk * blk
        s_lo = jnp.sum(jnp.where(first_head, y, 0.0), axis=-1, keepdims=True)
        s_hi = jnp.sum(jnp.where(first_head, 0.0, y), axis=-1, keepdims=True)
        ss = jnp.where(first_head, s_lo, s_hi)
        n = blk * lax.rsqrt(ss * (1.0 / HEAD_DIM) + EPS) * g
        rot = jnp.where(lo_half, pltpu.roll(n, LANES - HEAD_DIM // 2, 1),
                        pltpu.roll(n, HEAD_DIM // 2, 1))
        return n * cosv + rot * sinv

    def proj(a, b):
        return jnp.dot(h, w_ref[:, a:b], preferred_element_type=F32)

    pq = proj(0, A_Q)
    for c in range(A_Q // LANES):
        sl = slice(c * LANES, (c + 1) * LANES)
        out_ref[:, OFF_Q + c * LANES:OFF_Q + (c + 1) * LANES] = (
            head_norm_rope(pq[:, sl], gq_ref[...]) * (HEAD_DIM ** -0.5)).astype(BF16)

    pkv = proj(A_Q, A_Q + 2 * A_KV)
    kn = head_norm_rope(pkv[:, :LANES], gk_ref[...])
    vv = pkv[:, LANES:]
    for off, t in ((OFF_K4, kn), (OFF_V4, vv)):
        tr = pltpu.roll(t, HEAD_DIM, 1)
        parts = (jnp.where(first_head, t, 0.0), jnp.where(first_head, 0.0, tr),
                 jnp.where(first_head, tr, 0.0), jnp.where(first_head, 0.0, t))
        for j, p in enumerate(parts):
            out_ref[:, off + j * LANES:off + (j + 1) * LANES] = p.astype(BF16)

    o0 = A_Q + 2 * A_KV
    out_ref[:, OFF_U:OFF_U + GMLP_WIDTH] = _gelu(proj(o0, o0 + GMLP_WIDTH)).astype(BF16)
    gv = _gelu(proj(o0 + GMLP_WIDTH, o0 + 2 * GMLP_WIDTH))
    mu = jnp.mean(gv, axis=-1, keepdims=True)
    var = jnp.mean(jnp.square(gv - mu), axis=-1, keepdims=True)
    out_ref[:, OFF_VN:OFF_VN + GMLP_WIDTH] = (
        (gv - mu) * lax.rsqrt(var + EPS) * lng_ref[...] + lnb_ref[...]).astype(BF16)

    o1 = o0 + 2 * GMLP_WIDTH
    pc = proj(o1, o1 + C_Q)
    for hh in range(X_HEADS):
        sl = slice(hh * X_HEAD_DIM, (hh + 1) * X_HEAD_DIM)
        out_ref[:, OFF_QC + hh * X_HEAD_DIM:OFF_QC + (hh + 1) * X_HEAD_DIM] = (
            _rms(pc[:, sl], gcq_ref[...]) * (X_HEAD_DIM ** -0.5)).astype(BF16)

    o2 = o1 + C_Q
    for j in range(3):
        sl = slice(j * d_model, (j + 1) * d_model)
        out_ref[:, OFF_GA + j * d_model:OFF_GA + (j + 1) * d_model] = jax.nn.sigmoid(
            proj(o2 + j * d_model, o2 + (j + 1) * d_model) + bg_ref[:, sl]).astype(BF16)


def _proj(x2d, pos2d, g, w_bf, invf, sgn, gq, gk, gcq, lng, lnb, bg, tm, row0, t):
    d = x2d.shape[1]
    d_in = w_bf.shape[1]
    pw = PROJ_W_BASE + 3 * d
    blk0 = row0 // tm
    full = lambda shape: pl.BlockSpec(shape, lambda i: (0,) * len(shape))
    return pl.pallas_call(
        functools.partial(_proj_kernel, d_model=d),
        grid=(t // tm,),
        in_specs=[
            pl.BlockSpec((tm, d), lambda i: (blk0 + i, 0)),
            pl.BlockSpec((tm, 1), lambda i: (blk0 + i, 0)),
            full((1, d)), full((d, d_in)), full((1, LANES)), full((1, LANES)),
            full((1, LANES)), full((1, LANES)), full((1, X_HEAD_DIM)),
            full((1, GMLP_WIDTH)), full((1, GMLP_WIDTH)), full((1, 3 * d)),
        ],
        out_specs=pl.BlockSpec((tm, pw), lambda i: (i, 0)),
        out_shape=jax.ShapeDtypeStruct((t, pw), BF16),
        name="proj",
        compiler_params=pltpu.CompilerParams(
            dimension_semantics=("arbitrary",), vmem_limit_bytes=VMEM_LIMIT),
    )(x2d, pos2d, g, w_bf, invf, sgn, gq, gk, gcq, lng, lnb, bg)


def _mix_kernel(sinks_ref, proj_ref, kprev_ref, vprev_ref, x_ref, kc_ref, vc_ref, ws_ref, bst_ref,
                woa_ref, wob_ref, woc_ref, wout_ref, gffn_ref, rw_ref, rb_ref,
                x1_ref, h2_ref, idx_ref, rank_ref, gate_ref, cnt_ref,
                run_ref, oa_ref, ob_ref, oc_ref, *, d_model):
    ts = x_ref.shape[0]
    s_idx = pl.program_id(1)
    first_step = jnp.logical_and(pl.program_id(0) == 0, s_idx == 0)
    neg_inf = float("-inf")

    cpk = A_Q // LANES // N_KV_HEADS
    stack = cpk * ATT_BLOCK
    srow = lax.broadcasted_iota(I32, (stack, 2 * ATT_BLOCK), 0)
    qi = srow % ATT_BLOCK
    kj = lax.broadcasted_iota(I32, (stack, 2 * ATT_BLOCK), 1)
    band = jnp.logical_and(kj <= ATT_BLOCK + qi, kj > qi)
    chunk_of_row = lax.broadcasted_iota(I32, (stack, 1), 0) // ATT_BLOCK
    for qb in range(ts // ATT_BLOCK):
        r0 = qb * ATT_BLOCK
        rows = slice(r0, r0 + ATT_BLOCK)
        if qb == 0:
            kp, vp = kprev_ref[...], vprev_ref[...]
            mask = jnp.logical_and(band, jnp.logical_or(kj >= ATT_BLOCK, s_idx > 0))
        else:
            prow = slice(r0 - ATT_BLOCK, r0)
            kp = proj_ref[prow, OFF_K4:OFF_K4 + 4 * LANES]
            vp = proj_ref[prow, OFF_V4:OFF_V4 + 4 * LANES]
            mask = band
        k4 = jnp.concatenate([kp, proj_ref[rows, OFF_K4:OFF_K4 + 4 * LANES]], axis=0)
        v4 = jnp.concatenate([vp, proj_ref[rows, OFF_V4:OFF_V4 + 4 * LANES]], axis=0)
        for kvh in range(N_KV_HEADS):
            c0 = kvh * cpk
            q4 = jnp.concatenate(
                [proj_ref[rows, OFF_Q + (c0 + c) * LANES:OFF_Q + (c0 + c + 1) * LANES]
                 for c in range(cpk)], axis=0)
            o = jnp.zeros((stack, LANES), F32)
            for half in range(2):
                col = slice((2 * kvh + half) * LANES, (2 * kvh + half + 1) * LANES)
                s = lax.dot_general(q4, k4[:, col], (((1,), (1,)), ((), ())),
                                    preferred_element_type=F32)
                s = jnp.where(mask, s, neg_inf)
                sink = jnp.zeros((stack, 1), F32)
                for c in range(cpk):
                    sink = jnp.where(chunk_of_row == c, sinks_ref[2 * (c0 + c) + half], sink)
                m = jnp.maximum(jnp.max(s, axis=-1, keepdims=True), sink)
                p = jnp.exp(s - m)
                den = jnp.sum(p, axis=-1, keepdims=True) + jnp.exp(sink - m)
                o = o + jnp.dot(p.astype(BF16), v4[:, col], preferred_element_type=F32) / den
            for c in range(cpk):
                oa_ref[rows, (c0 + c) * LANES:(c0 + c + 1) * LANES] = (
                    o[c * ATT_BLOCK:(c + 1) * ATT_BLOCK].astype(BF16))

    ti = lax.broadcasted_iota(I32, (GMLP_CHUNK, GMLP_CHUNK), 0)
    si = lax.broadcasted_iota(I32, (GMLP_CHUNK, GMLP_CHUNK), 1)
    for g in range(GMLP_GROUPS):
        wt = jnp.where(si <= ti, ws_ref[g], 0.0).astype(BF16)
        bcol = bst_ref[:, g:g + 1]
        for ch in range(ts // GMLP_CHUNK):
            rows = slice(ch * GMLP_CHUNK, (ch + 1) * GMLP_CHUNK)
            vn = proj_ref[rows, OFF_VN + g * LANES:OFF_VN + (g + 1) * LANES]
            u = proj_ref[rows, OFF_U + g * LANES:OFF_U + (g + 1) * LANES].astype(F32)
            mixed = jnp.dot(wt, vn, preferred_element_type=F32) + bcol
            ob_ref[rows, g * LANES:(g + 1) * LANES] = (u * mixed).astype(BF16)

    for hh in range(X_HEADS):
        sl = slice(hh * X_HEAD_DIM, (hh + 1) * X_HEAD_DIM)
        qc = proj_ref[:, OFF_QC + hh * X_HEAD_DIM:OFF_QC + (hh + 1) * X_HEAD_DIM]
        s = lax.dot_general(qc, kc_ref[:, sl], (((1,), (1,)), ((), ())),
                            preferred_element_type=F32)
        p = jnp.exp(s - jnp.max(s, axis=-1, keepdims=True))
        den = jnp.sum(p, axis=-1, keepdims=True)
        oc_ref[:, sl] = (jnp.dot(p.astype(BF16), vc_ref[:, sl],
                                 preferred_element_type=F32) / den).astype(BF16)

    def gate(j):
        return proj_ref[:, OFF_GA + j * d_model:OFF_GA + (j + 1) * d_model].astype(F32)

    merged = gate(0) * jnp.dot(oa_ref[...], woa_ref[...], preferred_element_type=F32)
    merged = merged + gate(1) * jnp.dot(ob_ref[...], wob_ref[...], preferred_element_type=F32)
    merged = merged + gate(2) * jnp.dot(oc_ref[...], woc_ref[...], preferred_element_type=F32)
    x1 = x_ref[...] + jnp.dot(merged.astype(BF16), wout_ref[...], preferred_element_type=F32)
    x1_ref[...] = x1

    h2 = _rms(x1, gffn_ref[...])
    h2_words = _pack_bf16_pairs(h2)
    for c in range(h2_ref.shape[0]):
        h2_ref[c] = h2_words[:, c * SC_CHUNK:(c + 1) * SC_CHUNK]
    lane = _lane_iota((ts, LANES))
    h2_hi = h2.astype(BF16)
    h2_lo = (h2 - h2_hi.astype(F32)).astype(BF16)
    part = jnp.dot(h2_hi, rw_ref[...], preferred_element_type=F32)
    logits = (part[:, :LANES] + part[:, LANES:]
              + jnp.dot(h2_lo, rw_ref[:, :LANES], preferred_element_type=F32) + rb_ref[...])
    logits = jnp.where(lane < N_EXPERTS, logits, neg_inf)
    vals, idxs = [], []
    for _ in range(TOP_K):
        m = jnp.max(logits, axis=-1, keepdims=True)
        i = jnp.min(jnp.where(logits == m, lane, LANES), axis=-1, keepdims=True)
        vals.append(m)
        idxs.append(i)
        logits = jnp.where(lane == i, neg_inf, logits)
    es = [jnp.exp(v - vals[0]) for v in vals]
    den = es[0] + es[1] + es[2] + es[3]

    @pl.when(first_step)
    def _():
        run_ref[...] = jnp.zeros_like(run_ref)

    hot = [lane == i for i in idxs]
    multihot = jnp.where(jnp.logical_or(jnp.logical_or(hot[0], hot[1]),
                                        jnp.logical_or(hot[2], hot[3])), 1.0, 0.0)
    tr = lax.broadcasted_iota(I32, (ts, ts), 0)
    tc = lax.broadcasted_iota(I32, (ts, ts), 1)
    ltri = jnp.where(tc < tr, 1.0, 0.0).astype(BF16)
    before = jnp.dot(ltri, multihot.astype(BF16), preferred_element_type=F32) + run_ref[...]
    idx_out = jnp.zeros((ts, LANES), I32)
    rank_out = jnp.zeros((ts, LANES), I32)
    gate_out = jnp.zeros((ts, LANES), F32)
    for k in range(TOP_K):
        rk = jnp.sum(jnp.where(hot[k], before, 0.0), axis=-1, keepdims=True)
        idx_out = jnp.where(lane == k, idxs[k], idx_out)
        rank_out = jnp.where(lane == k, rk.astype(I32), rank_out)
        gate_out = jnp.where(lane == k, es[k] / den, gate_out)
    idx_ref[...] = idx_out
    rank_ref[...] = rank_out
    gate_ref[...] = gate_out
    run_ref[...] = run_ref[...] + jnp.sum(multihot, axis=0, keepdims=True)
    cnt_ref[...] = run_ref[...]


def _mix(sinks, proj, x2d, kc, vc, w_s, bst, woa, wob, woc, wout, gffn, rw, rb,
         n_batch, seq, m_len, ts, batch0):
    d = x2d.shape[1]
    t = n_batch * seq
    pw = proj.shape[1]
    ns = seq // ts
    nblk = seq // ATT_BLOCK
    per = ts // ATT_BLOCK
    n_chunks = d // (2 * SC_CHUNK)
    full = lambda shape: pl.BlockSpec(shape, lambda b, s: (0,) * len(shape))
    row = lambda width: pl.BlockSpec((ts, width), lambda b, s: (b * ns + s, 0))
    prev = lambda colblk: pl.BlockSpec(
        (ATT_BLOCK, 4 * LANES), lambda b, s: (b * nblk + jnp.maximum(s * per - 1, 0), colblk))
    return pl.pallas_call(
        functools.partial(_mix_kernel, d_model=d),
        grid=(n_batch, ns),
        in_specs=[
            pl.BlockSpec(memory_space=pltpu.SMEM),
            row(pw), prev(OFF_K4 // (4 * LANES)), prev(OFF_V4 // (4 * LANES)),
            pl.BlockSpec((ts, d), lambda b, s: ((batch0 + b) * ns + s, 0)),
            pl.BlockSpec((m_len, C_Q), lambda b, s: (batch0 + b, 0)),
            pl.BlockSpec((m_len, C_Q), lambda b, s: (batch0 + b, 0)),
            full((GMLP_GROUPS, GMLP_CHUNK, GMLP_CHUNK)), full((GMLP_CHUNK, GMLP_GROUPS)),
            full((A_Q, d)), full((GMLP_WIDTH, d)), full((C_Q, d)), full((d, d)),
            full((1, d)), full((d, 2 * LANES)), full((1, LANES)),
        ],
        out_specs=[row(d),
                   pl.BlockSpec((n_chunks, ts, SC_CHUNK), lambda b, s: (0, b * ns + s, 0)),
                   row(LANES), row(LANES), row(LANES), full((1, LANES))],
        out_shape=[
            jax.ShapeDtypeStruct((t, d), F32),
            jax.ShapeDtypeStruct((n_chunks, t, SC_CHUNK), U32),
            jax.ShapeDtypeStruct((t, LANES), I32), jax.ShapeDtypeStruct((t, LANES), I32),
            jax.ShapeDtypeStruct((t, LANES), F32), jax.ShapeDtypeStruct((1, LANES), F32),
        ],
        scratch_shapes=[
            pltpu.VMEM((1, LANES), F32),
            pltpu.VMEM((ts, A_Q), BF16), pltpu.VMEM((ts, GMLP_WIDTH), BF16),
            pltpu.VMEM((ts, C_Q), BF16),
        ],
        name="mix",
        compiler_params=pltpu.CompilerParams(
            dimension_semantics=("arbitrary", "arbitrary"), vmem_limit_bytes=VMEM_LIMIT),
    )(sinks, proj, proj, proj, x2d, kc, vc, w_s, bst, woa, wob, woc, wout, gffn, rw, rb)


def _pos_kernel(pstart_ref, idx_ref, rank_ref, pos_ref):
    idx = idx_ref[...]
    pos = rank_ref[...]
    for e in range(N_EXPERTS):
        pos = pos + jnp.where(idx == e, pstart_ref[e], 0)
    pos_ref[...] = pos


def _pos(pstart, idx, rank, tp):
    t = idx.shape[0]
    return pl.pallas_call(
        _pos_kernel,
        grid_spec=pltpu.PrefetchScalarGridSpec(
            num_scalar_prefetch=1,
            grid=(t // tp,),
            in_specs=[pl.BlockSpec((tp, LANES), lambda i, ps: (i, 0)),
                      pl.BlockSpec((tp, LANES), lambda i, ps: (i, 0))],
            out_specs=pl.BlockSpec((tp, LANES), lambda i, ps: (i, 0)),
        ),
        out_shape=jax.ShapeDtypeStruct((t, LANES), I32),
        name="slot_pos",
        compiler_params=pltpu.CompilerParams(dimension_semantics=("arbitrary",)),
    )(pstart, idx, rank)


def _sc_mesh():
    return plsc.VectorSubcoreMesh(core_axis_name="core", subcore_axis_name="subcore")


def _sc_scatter_rows(src, idx, n_rows):
    n_chunks, t, w = src.shape
    n_idx = idx.shape[1]
    src_blocks = t // SC_ROWS
    idx_blocks = n_idx // SC_ROWS

    @pl.kernel(out_type=jax.ShapeDtypeStruct((n_chunks * n_rows, w), src.dtype), mesh=_sc_mesh(),
               scratch_types=[], name="sc_dispatch")
    def scatter(src_hbm, idx_hbm, out_hbm):
        def body(src_vmem, idx_vmem):
            pltpu.sync_copy(src_vmem, out_hbm.at[idx_vmem.at[0]])

        pltpu.emit_pipeline(
            body,
            grid=(n_chunks, idx_blocks),
            in_specs=[pl.BlockSpec((SC_ROWS, w), lambda c, i: (c * src_blocks + i % src_blocks, 0)),
                      pl.BlockSpec((1, SC_ROWS), lambda c, i: (0, c * idx_blocks + i))],
            out_specs=[],
            core_axis_name=("core", "subcore"),
            dimension_semantics=(pltpu.PARALLEL, pltpu.PARALLEL),
        )(src_hbm, idx_hbm)

    return scatter(src.reshape(n_chunks * t, w), idx.reshape(1, n_chunks * n_idx)).reshape(
        n_chunks, n_rows, w)


def _sc_gather_rows(table, idx):
    n_chunks, p, w = table.shape
    n_idx = idx.shape[1]
    idx_blocks = n_idx // SC_ROWS

    @pl.kernel(out_type=jax.ShapeDtypeStruct((n_chunks * n_idx, w), table.dtype), mesh=_sc_mesh(),
               scratch_types=[], name="sc_gather")
    def gather(table_hbm, idx_hbm, out_hbm):
        def body(idx_vmem, out_vmem):
            pltpu.sync_copy(table_hbm.at[idx_vmem.at[0]], out_vmem)

        pltpu.emit_pipeline(
            body,
            grid=(n_chunks, idx_blocks),
            in_specs=[pl.BlockSpec((1, SC_ROWS), lambda c, i: (0, c * idx_blocks + i))],
            out_specs=[pl.BlockSpec((SC_ROWS, w), lambda c, i: (c * idx_blocks + i, 0))],
            core_axis_name=("core", "subcore"),
            dimension_semantics=(pltpu.PARALLEL, pltpu.PARALLEL),
        )(idx_hbm, out_hbm)

    return gather(table.reshape(n_chunks * p, w), idx.reshape(1, n_chunks * n_idx)).reshape(
        n_chunks, n_idx, w)


def _expert_kernel(bexp_ref, nvalid_ref, xs_ref, wgu_ref, bgu_ref, wd_ref, bd_ref, ys_ref,
                   wgu_bf_ref, wd_bf_ref, *, d_exp):
    b = pl.program_id(0)

    @pl.when(b < nvalid_ref[0])
    def _():
        @pl.when(jnp.logical_or(b == 0, bexp_ref[b] != bexp_ref[jnp.maximum(b - 1, 0)]))
        def _():
            wgu_bf_ref[...] = wgu_ref[...].astype(BF16)
            wd_bf_ref[...] = wd_ref[...].astype(BF16)

        n_chunks = xs_ref.shape[0]
        lo, hi = _unpack_bf16_pairs(
            jnp.concatenate([xs_ref[c] for c in range(n_chunks)], axis=1))
        xb = jnp.concatenate([lo, hi], axis=1).astype(BF16)
        gu = jnp.dot(xb, wgu_bf_ref[...], preferred_element_type=F32) + bgu_ref[...]
        gate = jnp.minimum(gu[:, :d_exp], SWIGLU_LIMIT)
        up = jnp.clip(gu[:, d_exp:], -SWIGLU_LIMIT, SWIGLU_LIMIT)
        glu = gate * jax.nn.sigmoid(gate * SWIGLU_ALPHA)
        act = ((up + 1.0) * glu).astype(BF16)
        y = jnp.dot(act, wd_bf_ref[...], preferred_element_type=F32) + bd_ref[...]
        y_words = _pack_bf16_pairs(y)
        for c in range(n_chunks):
            ys_ref[c] = y_words[:, c * SC_CHUNK:(c + 1) * SC_CHUNK]


def _experts(bexp, nvalid, xs, wgu, bgu, wd, bd):
    n_chunks, n_rows, _ = xs.shape
    _, d, d_exp2 = wgu.shape
    d_exp = d_exp2 // 2
    nb = n_rows // MOE_BLOCK
    blk = lambda b, be, nv: jnp.minimum(b, nv[0] - 1)
    return pl.pallas_call(
        functools.partial(_expert_kernel, d_exp=d_exp),
        grid_spec=pltpu.PrefetchScalarGridSpec(
            num_scalar_prefetch=2,
            grid=(nb,),
            in_specs=[
                pl.BlockSpec((n_chunks, MOE_BLOCK, SC_CHUNK),
                             lambda b, be, nv: (0, blk(b, be, nv), 0)),
                pl.BlockSpec((None, d, 2 * d_exp), lambda b, be, nv: (be[blk(b, be, nv)], 0, 0)),
                pl.BlockSpec((None, 1, 2 * d_exp), lambda b, be, nv: (be[blk(b, be, nv)], 0, 0)),
                pl.BlockSpec((None, d_exp, d), lambda b, be, nv: (be[blk(b, be, nv)], 0, 0)),
                pl.BlockSpec((None, 1, d), lambda b, be, nv: (be[blk(b, be, nv)], 0, 0)),
            ],
            out_specs=pl.BlockSpec((n_chunks, MOE_BLOCK, SC_CHUNK),
                                   lambda b, be, nv: (0, blk(b, be, nv), 0)),
            scratch_shapes=[pltpu.VMEM((d, 2 * d_exp), BF16), pltpu.VMEM((d_exp, d), BF16)],
        ),
        out_shape=jax.ShapeDtypeStruct((n_chunks, n_rows, SC_CHUNK), U32),
        name="experts",
        compiler_params=pltpu.CompilerParams(
            dimension_semantics=("arbitrary",), vmem_limit_bytes=VMEM_LIMIT),
    )(bexp, nvalid, xs, wgu, bgu, wd, bd)


def _combine_kernel(prev_ref, x1_ref, gate_ref, yg_ref, out_ref):
    del prev_ref
    n_chunks = yg_ref.shape[0]
    half = n_chunks * SC_CHUNK
    for c in range(n_chunks):
        sl_lo = slice(c * SC_CHUNK, (c + 1) * SC_CHUNK)
        sl_hi = slice(half + c * SC_CHUNK, half + (c + 1) * SC_CHUNK)
        acc_lo = x1_ref[:, sl_lo]
        acc_hi = x1_ref[:, sl_hi]
        for k in range(TOP_K):
            lo, hi = _unpack_bf16_pairs(yg_ref[c, k])
            g = gate_ref[:, k:k + 1]
            acc_lo = acc_lo + g * lo
            acc_hi = acc_hi + g * hi
        out_ref[:, sl_lo] = acc_lo
        out_ref[:, sl_hi] = acc_hi


def _combine(x1, gates, yg, out_prev, tcb, row0, t_total):
    t, d = x1.shape
    n_chunks = yg.shape[0]
    blk0 = row0 // tcb
    in_specs = [
        pl.BlockSpec((tcb, d), lambda i: (i, 0)),
        pl.BlockSpec((tcb, LANES), lambda i: (i, 0)),
        pl.BlockSpec((n_chunks, TOP_K, tcb, SC_CHUNK), lambda i: (0, 0, i, 0)),
    ]
    args = [x1, gates, yg]
    aliases = {}
    body = functools.partial(_combine_kernel, None)
    if out_prev is not None:
        in_specs.append(pl.BlockSpec(memory_space=pl.ANY))
        args.append(out_prev)
        aliases = {3: 0}
        body = lambda a, b, c, prev, o: _combine_kernel(prev, a, b, c, o)
    return pl.pallas_call(
        body,
        grid=(t // tcb,),
        in_specs=in_specs,
        out_specs=pl.BlockSpec((tcb, d), lambda i: (blk0 + i, 0)),
        out_shape=jax.ShapeDtypeStruct((t_total, d), F32),
        input_output_aliases=aliases,
        name="combine",
        compiler_params=pltpu.CompilerParams(dimension_semantics=("arbitrary",)),
    )(*args)


def _pick_tile(n, pref):
    t = min(pref, n)
    while n % t:
        t //= 2
    return t


def kernel(x, mem, positions, attn_norm_g, mem_norm_g, w_in, b_gates, a_q_norm_g, a_k_norm_g,
           a_sinks, w_o_a, gmlp_ln_g, gmlp_ln_b, gmlp_w_s, gmlp_b_s, w_o_b, w_mem_kv,
           c_q_norm_g, c_k_norm_g, w_o_c, w_out, ffn_norm_g, router_w, router_b,
           w_gate_up, b_gate_up, w_down, b_down):
    n_batch, seq, d = x.shape
    m_len = mem.shape[1]
    depth = w_in.shape[0]
    t = n_batch * seq
    n_groups = N_TOKEN_GROUPS if n_batch % N_TOKEN_GROUPS == 0 else 1
    gb = n_batch // n_groups
    tg = gb * seq
    n_asg = tg * TOP_K
    nb = -(-n_asg // MOE_BLOCK) + N_EXPERTS
    n_rows = nb * MOE_BLOCK

    inv_freq = ROPE_THETA ** (-jnp.arange(0, HEAD_DIM, 2, dtype=F32) / HEAD_DIM)
    invf = jnp.tile(inv_freq, LANES // (HEAD_DIM // 2))[None, :]
    sgn = jnp.tile(jnp.concatenate([-jnp.ones((HEAD_DIM // 2,), F32),
                                    jnp.ones((HEAD_DIM // 2,), F32)]), LANES // HEAD_DIM)[None, :]
    pos2d = positions.reshape(t, 1).astype(I32)
    mem2d = mem.reshape(n_batch * m_len, d)
    x2d = x.reshape(t, d)

    tm = _pick_tile(tg, 512)
    ts = _pick_tile(seq, 512)
    tp = _pick_tile(tg, 2048)
    tcb = _pick_tile(tg, 256)

    for l in range(depth):
        kc, vc = _mem_kv(mem2d, mem_norm_g[l][None, :], w_mem_kv[l].astype(BF16),
                         c_k_norm_g[l][None, :], n_batch, m_len)
        w_in_bf = w_in[l].astype(BF16)
        mix_w = (w_o_a[l].astype(BF16), w_o_b[l].astype(BF16), w_o_c[l].astype(BF16),
                 w_out[l].astype(BF16))
        gq = jnp.tile(a_q_norm_g[l], LANES // HEAD_DIM)[None, :]
        gk = jnp.tile(a_k_norm_g[l], LANES // HEAD_DIM)[None, :]
        rw32 = jnp.pad(router_w[l], ((0, 0), (0, LANES - N_EXPERTS)))
        rw_hi = rw32.astype(BF16)
        rw = jnp.concatenate([rw_hi, (rw32 - rw_hi.astype(F32)).astype(BF16)], axis=1)
        rb = jnp.pad(router_b[l], (0, LANES - N_EXPERTS))[None, :]

        routed = []
        for g in range(n_groups):
            proj = _proj(
                x2d, pos2d, attn_norm_g[l][None, :], w_in_bf, invf, sgn, gq, gk,
                c_q_norm_g[l][None, :], gmlp_ln_g[l][None, :], gmlp_ln_b[l][None, :],
                b_gates[l].reshape(1, 3 * d), tm, g * tg, tg)
            x1, h2, idx, rank, gates, counts = _mix(
                a_sinks[l], proj, x2d, kc, vc, gmlp_w_s[l], gmlp_b_s[l].T, *mix_w,
                ffn_norm_g[l][None, :], rw, rb, gb, seq, m_len, ts, g * gb)

            cnt = counts[0, :N_EXPERTS].astype(I32)
            padded = (cnt + MOE_BLOCK - 1) // MOE_BLOCK * MOE_BLOCK
            pend = jnp.cumsum(padded)
            pstart = (pend - padded).astype(I32)
            nvalid = (pend[-1:] // MOE_BLOCK).astype(I32)
            blk_row = jnp.arange(nb, dtype=I32) * MOE_BLOCK
            bexp = jnp.minimum(jnp.sum((pend[None, :] <= blk_row[:, None]).astype(I32), axis=1),
                               N_EXPERTS - 1).astype(I32)
            j = jnp.arange(MOE_BLOCK, dtype=I32)[None, :]
            fill = jnp.where(j < (padded - cnt)[:, None], pend[:, None] - 1 - j,
                             n_rows - 1 - j).astype(I32)

            pos = _pos(pstart, idx, rank, tp)
            pos_km = pos[:, :TOP_K].T.reshape(n_asg)
            n_chunks = h2.shape[0]
            chunk_off = (jnp.arange(n_chunks, dtype=I32) * n_rows)[:, None]
            xs = _sc_scatter_rows(
                h2, jnp.concatenate([pos_km, fill.reshape(-1)])[None, :] + chunk_off, n_rows)
            routed.append((x1, gates, xs, bexp, nvalid, pos_km, chunk_off))

        gathered = []
        for x1, gates, xs, bexp, nvalid, pos_km, chunk_off in routed:
            ys = _experts(bexp, nvalid, xs, w_gate_up[l], b_gate_up[l][:, None, :],
                          w_down[l], b_down[l][:, None, :])
            yg = _sc_gather_rows(ys, pos_km[None, :] + chunk_off)
            gathered.append((x1, gates, yg.reshape(yg.shape[0], TOP_K, tg, SC_CHUNK)))

        out = None
        for g, (x1, gates, yg) in enumerate(gathered):
            out = _combine(x1, gates, yg, out, tcb, g * tg, t)
        x2d = out
    return x2d.reshape(n_batch, seq, d)
```

```python
import functools

import numpy as np
import jax
import jax.numpy as jnp
from jax import lax
from jax.experimental import pallas as pl
from jax.experimental.pallas import tpu as pltpu
from jax.experimental.pallas import tpu_sc as plsc

F32 = jnp.float32
BF16 = jnp.bfloat16
I32 = jnp.int32
U32 = jnp.uint32
HI16 = np.uint32(0xFFFF0000)

EPS = 1e-6
LANES = 128
HEAD_DIM = 64
N_Q_HEADS = 16
N_KV_HEADS = 2
ATT_BLOCK = 128
ROPE_THETA = 10000.0
GMLP_WIDTH = 512
GMLP_GROUPS = 4
GMLP_CHUNK = 128
X_HEADS = 4
X_HEAD_DIM = 128
N_EXPERTS = 32
TOP_K = 4
SWIGLU_LIMIT = 7.0
SWIGLU_ALPHA = 1.702
MOE_BLOCK = 512
SC_ROWS = 128
SC_CHUNK = 256
N_TOKEN_GROUPS = 2

A_Q = N_Q_HEADS * HEAD_DIM
A_KV = N_KV_HEADS * HEAD_DIM
C_Q = X_HEADS * X_HEAD_DIM

OFF_Q = 0
OFF_K4 = OFF_Q + A_Q
OFF_V4 = OFF_K4 + 4 * LANES
OFF_U = OFF_V4 + 4 * LANES
OFF_VN = OFF_U + GMLP_WIDTH
OFF_QC = OFF_VN + GMLP_WIDTH
OFF_GA = OFF_QC + C_Q
PROJ_W_BASE = OFF_GA

VMEM_LIMIT = 56 * 1024 * 1024


def _lane_iota(shape):
    return lax.broadcasted_iota(I32, shape, len(shape) - 1)


def _rms(x, g):
    return x * lax.rsqrt(jnp.mean(x * x, axis=-1, keepdims=True) + EPS) * g


def _pack_bf16_pairs(x):
    n = x.shape[1] // 2
    bits = pltpu.bitcast(x.astype(BF16).astype(F32), U32)
    return (bits[:, :n] >> 16) | (bits[:, n:] & HI16)


def _unpack_bf16_pairs(w):
    return pltpu.bitcast(w << 16, F32), pltpu.bitcast(w & HI16, F32)


def _gelu(x):
    return 0.5 * x * (1.0 + lax.erf(x * np.float32(np.sqrt(0.5))))


def _memkv_kernel(mem_ref, g_ref, w_ref, gk_ref, kc_ref, vc_ref):
    h = _rms(mem_ref[...], g_ref[...]).astype(BF16)
    kv = jnp.dot(h, w_ref[...], preferred_element_type=F32)
    for hh in range(X_HEADS):
        sl = slice(hh * X_HEAD_DIM, (hh + 1) * X_HEAD_DIM)
        kc_ref[:, sl] = _rms(kv[:, sl], gk_ref[...]).astype(BF16)
    vc_ref[...] = kv[:, C_Q:].astype(BF16)


def _mem_kv(mem2d, g, w_bf, gk, n_batch, m_len):
    d = mem2d.shape[1]
    return pl.pallas_call(
        _memkv_kernel,
        grid=(n_batch,),
        in_specs=[
            pl.BlockSpec((m_len, d), lambda b: (b, 0)),
            pl.BlockSpec((1, d), lambda b: (0, 0)),
            pl.BlockSpec((d, 2 * C_Q), lambda b: (0, 0)),
            pl.BlockSpec((1, X_HEAD_DIM), lambda b: (0, 0)),
        ],
        out_specs=[
            pl.BlockSpec((m_len, C_Q), lambda b: (b, 0)),
            pl.BlockSpec((m_len, C_Q), lambda b: (b, 0)),
        ],
        out_shape=[jax.ShapeDtypeStruct((n_batch * m_len, C_Q), BF16)] * 2,
        name="mem_kv",
        compiler_params=pltpu.CompilerParams(dimension_semantics=("arbitrary",)),
    )(mem2d, g, w_bf, gk)


def _proj_kernel(x_ref, pos_ref, g_ref, w_ref, invf_ref, sgn_ref, gq_ref, gk_ref, gcq_ref,
                 lng_ref, lnb_ref, bg_ref, out_ref, *, d_model):
    tm = x_ref.shape[0]
    h = _rms(x_ref[...], g_ref[...]).astype(BF16)

    ang = pos_ref[...].astype(F32) * invf_ref[...]
    cosv = jnp.cos(ang)
    sinv = jnp.sin(ang) * sgn_ref[...]
    lane = _lane_iota((tm, LANES))
    first_head = lane < HEAD_DIM
    lo_half = (lane % HEAD_DIM) < (HEAD_DIM // 2)

    def head_norm_rope(blk, g):
        y = blk * blk
        s_lo = jnp.sum(jnp.where(first_head, y, 0.0), axis=-1, keepdims=True)
        s_hi = jnp.sum(jnp.where(first_head, 0.0, y), axis=-1, keepdims=True)
        ss = jnp.where(first_head, s_lo, s_hi)
        n = blk * lax.rsqrt(ss * (1.0 / HEAD_DIM) + EPS) * g
        rot = jnp.where(lo_half, pltpu.roll(n, LANES - HEAD_DIM // 2, 1),
                        pltpu.roll(n, HEAD_DIM // 2, 1))
        return n * cosv + rot * sinv

    def proj(a, b):
        return jnp.dot(h, w_ref[:, a:b], preferred_element_type=F32)

    pq = proj(0, A_Q)
    for c in range(A_Q // LANES):
        sl = slice(c * LANES, (c + 1) * LANES)
        out_ref[:, OFF_Q + c * LANES:OFF_Q + (c + 1) * LANES] = (
            head_norm_rope(pq[:, sl], gq_ref[...]) * (HEAD_DIM ** -0.5)).astype(BF16)

    pkv = proj(A_Q, A_Q + 2 * A_KV)
    kn = head_norm_rope(pkv[:, :LANES], gk_ref[...])
    vv = pkv[:, LANES:]
    for off, t in ((OFF_K4, kn), (OFF_V4, vv)):
        tr = pltpu.roll(t, HEAD_DIM, 1)
        parts = (jnp.where(first_head, t, 0.0), jnp.where(first_head, 0.0, tr),
                 jnp.where(first_head, tr, 0.0), jnp.where(first_head, 0.0, t))
        for j, p in enumerate(parts):
            out_ref[:, off + j * LANES:off + (j + 1) * LANES] = p.astype(BF16)

    o0 = A_Q + 2 * A_KV
    out_ref[:, OFF_U:OFF_U + GMLP_WIDTH] = _gelu(proj(o0, o0 + GMLP_WIDTH)).astype(BF16)
    gv = _gelu(proj(o0 + GMLP_WIDTH, o0 + 2 * GMLP_WIDTH))
    mu = jnp.mean(gv, axis=-1, keepdims=True)
    var = jnp.mean(jnp.square(gv - mu), axis=-1, keepdims=True)
    out_ref[:, OFF_VN:OFF_VN + GMLP_WIDTH] = (
        (gv - mu) * lax.rsqrt(var + EPS) * lng_ref[...] + lnb_ref[...]).astype(BF16)

    o1 = o0 + 2 * GMLP_WIDTH
    pc = proj(o1, o1 + C_Q)
    for hh in range(X_HEADS):
        sl = slice(hh * X_HEAD_DIM, (hh + 1) * X_HEAD_DIM)
        out_ref[:, OFF_QC + hh * X_HEAD_DIM:OFF_QC + (hh + 1) * X_HEAD_DIM] = (
            _rms(pc[:, sl], gcq_ref[...]) * (X_HEAD_DIM ** -0.5)).astype(BF16)

    o2 = o1 + C_Q
    for j in range(3):
        sl = slice(j * d_model, (j + 1) * d_model)
        z = proj(o2 + j * d_model, o2 + (j + 1) * d_model) + bg_ref[:, sl]
        out_ref[:, OFF_GA + j * d_model:OFF_GA + (j + 1) * d_model] = (
            0.5 * jnp.tanh(0.5 * z) + 0.5).astype(BF16)


def _proj(x2d, pos2d, g, w_bf, invf, sgn, gq, gk, gcq, lng, lnb, bg, tm, row0, t):
    d = x2d.shape[1]
    d_in = w_bf.shape[1]
    pw = PROJ_W_BASE + 3 * d
    blk0 = row0 // tm
    full = lambda shape: pl.BlockSpec(shape, lambda i: (0,) * len(shape))
    return pl.pallas_call(
        functools.partial(_proj_kernel, d_model=d),
        grid=(t // tm,),
        in_specs=[
            pl.BlockSpec((tm, d), lambda i: (blk0 + i, 0)),
            pl.BlockSpec((tm, 1), lambda i: (blk0 + i, 0)),
            full((1, d)), full((d, d_in)), full((1, LANES)), full((1, LANES)),
            full((1, LANES)), full((1, LANES)), full((1, X_HEAD_DIM)),
            full((1, GMLP_WIDTH)), full((1, GMLP_WIDTH)), full((1, 3 * d)),
        ],
        out_specs=pl.BlockSpec((tm, pw), lambda i: (i, 0)),
        out_shape=jax.ShapeDtypeStruct((t, pw), BF16),
        name="proj",
        cost_estimate=pl.CostEstimate(
            flops=2 * t * d * d_in, transcendentals=t * (3 * d + 2 * GMLP_WIDTH + 2 * LANES),
            bytes_accessed=4 * t * d + 2 * t * pw + 2 * d * d_in),
        compiler_params=pltpu.CompilerParams(
            dimension_semantics=("arbitrary",), vmem_limit_bytes=VMEM_LIMIT),
    )(x2d, pos2d, g, w_bf, invf, sgn, gq, gk, gcq, lng, lnb, bg)


def _mix_kernel(sinks_ref, proj_ref, kprev_ref, vprev_ref, x_ref, kc_ref, vc_ref, ws_ref, bst_ref,
                woa_ref, wob_ref, woc_ref, wout_ref, gffn_ref, rw_ref, rb_ref,
                x1_ref, h2_ref, idx_ref, rank_ref, gate_ref, cnt_ref,
                run_ref, oa_ref, ob_ref, oc_ref, *, d_model):
    ts = x_ref.shape[0]
    s_idx = pl.program_id(1)
    first_step = jnp.logical_and(pl.program_id(0) == 0, s_idx == 0)
    neg_inf = float("-inf")

    cpk = A_Q // LANES // N_KV_HEADS
    stack = cpk * ATT_BLOCK
    srow = lax.broadcasted_iota(I32, (stack, 2 * ATT_BLOCK), 0)
    qi = srow % ATT_BLOCK
    kj = lax.broadcasted_iota(I32, (stack, 2 * ATT_BLOCK), 1)
    band = jnp.logical_and(kj <= ATT_BLOCK + qi, kj > qi)
    chunk_of_row = lax.broadcasted_iota(I32, (stack, 1), 0) // ATT_BLOCK
    for qb in range(ts // ATT_BLOCK):
        r0 = qb * ATT_BLOCK
        rows = slice(r0, r0 + ATT_BLOCK)
        if qb == 0:
            kp, vp = kprev_ref[...], vprev_ref[...]
            mask = jnp.logical_and(band, jnp.logical_or(kj >= ATT_BLOCK, s_idx > 0))
        else:
            prow = slice(r0 - ATT_BLOCK, r0)
            kp = proj_ref[prow, OFF_K4:OFF_K4 + 4 * LANES]
            vp = proj_ref[prow, OFF_V4:OFF_V4 + 4 * LANES]
            mask = band
        k4 = jnp.concatenate([kp, proj_ref[rows, OFF_K4:OFF_K4 + 4 * LANES]], axis=0)
        v4 = jnp.concatenate([vp, proj_ref[rows, OFF_V4:OFF_V4 + 4 * LANES]], axis=0)
        for kvh in range(N_KV_HEADS):
            c0 = kvh * cpk
            q4 = jnp.concatenate(
                [proj_ref[rows, OFF_Q + (c0 + c) * LANES:OFF_Q + (c0 + c + 1) * LANES]
                 for c in range(cpk)], axis=0)
            o = jnp.zeros((stack, LANES), F32)
            for half in range(2):
                col = slice((2 * kvh + half) * LANES, (2 * kvh + half + 1) * LANES)
                s = lax.dot_general(q4, k4[:, col], (((1,), (1,)), ((), ())),
                                    preferred_element_type=F32)
                s = jnp.where(mask, s, neg_inf)
                sink = jnp.zeros((stack, 1), F32)
                for c in range(cpk):
                    sink = jnp.where(chunk_of_row == c, sinks_ref[2 * (c0 + c) + half], sink)
                m = jnp.maximum(jnp.max(s, axis=-1, keepdims=True), sink)
                p = jnp.exp(s - m)
                den = jnp.sum(p, axis=-1, keepdims=True) + jnp.exp(sink - m)
                o = o + jnp.dot(p.astype(BF16), v4[:, col], preferred_element_type=F32) / den
            for c in range(cpk):
                oa_ref[rows, (c0 + c) * LANES:(c0 + c + 1) * LANES] = (
                    o[c * ATT_BLOCK:(c + 1) * ATT_BLOCK].astype(BF16))

    ti = lax.broadcasted_iota(I32, (GMLP_CHUNK, GMLP_CHUNK), 0)
    si = lax.broadcasted_iota(I32, (GMLP_CHUNK, GMLP_CHUNK), 1)
    for g in range(GMLP_GROUPS):
        wt = jnp.where(si <= ti, ws_ref[g], 0.0).astype(BF16)
        bcol = bst_ref[:, g:g + 1]
        for ch in range(ts // GMLP_CHUNK):
            rows = slice(ch * GMLP_CHUNK, (ch + 1) * GMLP_CHUNK)
            vn = proj_ref[rows, OFF_VN + g * LANES:OFF_VN + (g + 1) * LANES]
            u = proj_ref[rows, OFF_U + g * LANES:OFF_U + (g + 1) * LANES].astype(F32)
            mixed = jnp.dot(wt, vn, preferred_element_type=F32) + bcol
            ob_ref[rows, g * LANES:(g + 1) * LANES] = (u * mixed).astype(BF16)

    for hh in range(X_HEADS):
        sl = slice(hh * X_HEAD_DIM, (hh + 1) * X_HEAD_DIM)
        qc = proj_ref[:, OFF_QC + hh * X_HEAD_DIM:OFF_QC + (hh + 1) * X_HEAD_DIM]
        s = lax.dot_general(qc, kc_ref[:, sl], (((1,), (1,)), ((), ())),
                            preferred_element_type=F32)
        p = jnp.exp(s - jnp.max(s, axis=-1, keepdims=True))
        den = jnp.sum(p, axis=-1, keepdims=True)
        oc_ref[:, sl] = (jnp.dot(p.astype(BF16), vc_ref[:, sl],
                                 preferred_element_type=F32) / den).astype(BF16)

    def gate(j):
        return proj_ref[:, OFF_GA + j * d_model:OFF_GA + (j + 1) * d_model].astype(F32)

    merged = gate(0) * jnp.dot(oa_ref[...], woa_ref[...], preferred_element_type=F32)
    merged = merged + gate(1) * jnp.dot(ob_ref[...], wob_ref[...], preferred_element_type=F32)
    merged = merged + gate(2) * jnp.dot(oc_ref[...], woc_ref[...], preferred_element_type=F32)
    x1 = x_ref[...] + jnp.dot(merged.astype(BF16), wout_ref[...], preferred_element_type=F32)
    x1_ref[...] = x1

    h2 = _rms(x1, gffn_ref[...])
    h2_words = _pack_bf16_pairs(h2)
    for c in range(h2_ref.shape[0]):
        h2_ref[c] = h2_words[:, c * SC_CHUNK:(c + 1) * SC_CHUNK]
    lane = _lane_iota((ts, LANES))
    h2_hi = h2.astype(BF16)
    h2_lo = (h2 - h2_hi.astype(F32)).astype(BF16)
    part = jnp.dot(h2_hi, rw_ref[...], preferred_element_type=F32)
    logits = (part[:, :LANES] + part[:, LANES:]
              + jnp.dot(h2_lo, rw_ref[:, :LANES], preferred_element_type=F32) + rb_ref[...])
    logits = jnp.where(lane < N_EXPERTS, logits, neg_inf)
    vals, idxs = [], []
    for _ in range(TOP_K):
        m = jnp.max(logits, axis=-1, keepdims=True)
        i = jnp.min(jnp.where(logits == m, lane, LANES), axis=-1, keepdims=True)
        vals.append(m)
        idxs.append(i)
        logits = jnp.where(lane == i, neg_inf, logits)
    es = [jnp.exp(v - vals[0]) for v in vals]
    den = es[0] + es[1] + es[2] + es[3]

    @pl.when(first_step)
    def _():
        run_ref[...] = jnp.zeros_like(run_ref)

    hot = [lane == i for i in idxs]
    multihot = jnp.where(jnp.logical_or(jnp.logical_or(hot[0], hot[1]),
                                        jnp.logical_or(hot[2], hot[3])), 1.0, 0.0)
    tr = lax.broadcasted_iota(I32, (ts, ts), 0)
    tc = lax.broadcasted_iota(I32, (ts, ts), 1)
    ltri = jnp.where(tc < tr, 1.0, 0.0).astype(BF16)
    before = jnp.dot(ltri, multihot.astype(BF16), preferred_element_type=F32) + run_ref[...]
    idx_out = jnp.zeros((ts, LANES), I32)
    rank_out = jnp.zeros((ts, LANES), I32)
    gate_out = jnp.zeros((ts, LANES), F32)
    for k in range(TOP_K):
        rk = jnp.sum(jnp.where(hot[k], before, 0.0), axis=-1, keepdims=True)
        idx_out = jnp.where(lane == k, idxs[k], idx_out)
        rank_out = jnp.where(lane == k, rk.astype(I32), rank_out)
        gate_out = jnp.where(lane == k, es[k] / den, gate_out)
    idx_ref[...] = idx_out
    rank_ref[...] = rank_out
    gate_ref[...] = gate_out
    run_ref[...] = run_ref[...] + jnp.sum(multihot, axis=0, keepdims=True)
    cnt_ref[...] = run_ref[...]


def _mix(sinks, proj, x2d, kc, vc, w_s, bst, woa, wob, woc, wout, gffn, rw, rb,
         n_batch, seq, m_len, ts, batch0):
    d = x2d.shape[1]
    t = n_batch * seq
    pw = proj.shape[1]
    ns = seq // ts
    nblk = seq // ATT_BLOCK
    per = ts // ATT_BLOCK
    n_chunks = d // (2 * SC_CHUNK)
    full = lambda shape: pl.BlockSpec(shape, lambda b, s: (0,) * len(shape))
    row = lambda width: pl.BlockSpec((ts, width), lambda b, s: (b * ns + s, 0))
    prev = lambda colblk: pl.BlockSpec(
        (ATT_BLOCK, 4 * LANES), lambda b, s: (b * nblk + jnp.maximum(s * per - 1, 0), colblk))
    return pl.pallas_call(
        functools.partial(_mix_kernel, d_model=d),
        grid=(n_batch, ns),
        in_specs=[
            pl.BlockSpec(memory_space=pltpu.SMEM),
            row(pw), prev(OFF_K4 // (4 * LANES)), prev(OFF_V4 // (4 * LANES)),
            pl.BlockSpec((ts, d), lambda b, s: ((batch0 + b) * ns + s, 0)),
            pl.BlockSpec((m_len, C_Q), lambda b, s: (batch0 + b, 0)),
            pl.BlockSpec((m_len, C_Q), lambda b, s: (batch0 + b, 0)),
            full((GMLP_GROUPS, GMLP_CHUNK, GMLP_CHUNK)), full((GMLP_CHUNK, GMLP_GROUPS)),
            full((A_Q, d)), full((GMLP_WIDTH, d)), full((C_Q, d)), full((d, d)),
            full((1, d)), full((d, 2 * LANES)), full((1, LANES)),
        ],
        out_specs=[row(d),
                   pl.BlockSpec((n_chunks, ts, SC_CHUNK), lambda b, s: (0, b * ns + s, 0)),
                   row(LANES), row(LANES), row(LANES), full((1, LANES))],
        out_shape=[
            jax.ShapeDtypeStruct((t, d), F32),
            jax.ShapeDtypeStruct((n_chunks, t, SC_CHUNK), U32),
            jax.ShapeDtypeStruct((t, LANES), I32), jax.ShapeDtypeStruct((t, LANES), I32),
            jax.ShapeDtypeStruct((t, LANES), F32), jax.ShapeDtypeStruct((1, LANES), F32),
        ],
        scratch_shapes=[
            pltpu.VMEM((1, LANES), F32),
            pltpu.VMEM((ts, A_Q), BF16), pltpu.VMEM((ts, GMLP_WIDTH), BF16),
            pltpu.VMEM((ts, C_Q), BF16),
        ],
        name="mix",
        cost_estimate=pl.CostEstimate(
            flops=2 * t * (d * (A_Q + GMLP_WIDTH + C_Q + d) + 4 * ATT_BLOCK * A_Q
                           + GMLP_CHUNK * GMLP_WIDTH + 2 * m_len * C_Q + 3 * d * LANES),
            transcendentals=t * (2 * ATT_BLOCK * N_Q_HEADS + m_len * X_HEADS),
            bytes_accessed=t * (2 * pw + 4 * d + 4 * d + 2 * d + 12 * LANES)),
        compiler_params=pltpu.CompilerParams(
            dimension_semantics=("arbitrary", "arbitrary"), vmem_limit_bytes=VMEM_LIMIT),
    )(sinks, proj, proj, proj, x2d, kc, vc, w_s, bst, woa, wob, woc, wout, gffn, rw, rb)


def _pos_kernel(pstart_ref, idx_ref, rank_ref, pos_ref):
    idx = idx_ref[...]
    pos = rank_ref[...]
    for e in range(N_EXPERTS):
        pos = pos + jnp.where(idx == e, pstart_ref[e], 0)
    pos_ref[...] = pos


def _pos(pstart, idx, rank, tp):
    t = idx.shape[0]
    return pl.pallas_call(
        _pos_kernel,
        grid_spec=pltpu.PrefetchScalarGridSpec(
            num_scalar_prefetch=1,
            grid=(t // tp,),
            in_specs=[pl.BlockSpec((tp, LANES), lambda i, ps: (i, 0)),
                      pl.BlockSpec((tp, LANES), lambda i, ps: (i, 0))],
            out_specs=pl.BlockSpec((tp, LANES), lambda i, ps: (i, 0)),
        ),
        out_shape=jax.ShapeDtypeStruct((t, LANES), I32),
        name="slot_pos",
        compiler_params=pltpu.CompilerParams(dimension_semantics=("arbitrary",)),
    )(pstart, idx, rank)


def _sc_mesh():
    return plsc.VectorSubcoreMesh(core_axis_name="core", subcore_axis_name="subcore")


def _sc_scatter_rows(src, idx, n_rows):
    n_chunks, t, w = src.shape
    n_idx = idx.shape[1]
    src_blocks = t // SC_ROWS
    idx_blocks = n_idx // SC_ROWS

    @pl.kernel(out_type=jax.ShapeDtypeStruct((n_chunks * n_rows, w), src.dtype), mesh=_sc_mesh(),
               scratch_types=[], name="sc_dispatch",
               cost_estimate=pl.CostEstimate(
                   flops=0, transcendentals=0,
                   bytes_accessed=n_chunks * n_idx * (8 * w + 4)))
    def scatter(src_hbm, idx_hbm, out_hbm):
        def body(src_vmem, idx_vmem):
            pltpu.sync_copy(src_vmem, out_hbm.at[idx_vmem.at[0]])

        pltpu.emit_pipeline(
            body,
            grid=(n_chunks, idx_blocks),
            in_specs=[pl.BlockSpec((SC_ROWS, w), lambda c, i: (c * src_blocks + i % src_blocks, 0)),
                      pl.BlockSpec((1, SC_ROWS), lambda c, i: (0, c * idx_blocks + i))],
            out_specs=[],
            core_axis_name=("core", "subcore"),
            dimension_semantics=(pltpu.PARALLEL, pltpu.PARALLEL),
        )(src_hbm, idx_hbm)

    return scatter(src.reshape(n_chunks * t, w), idx.reshape(1, n_chunks * n_idx)).reshape(
        n_chunks, n_rows, w)


def _sc_gather_rows(table, idx):
    n_chunks, p, w = table.shape
    n_idx = idx.shape[1]
    idx_blocks = n_idx // SC_ROWS

    @pl.kernel(out_type=jax.ShapeDtypeStruct((n_chunks * n_idx, w), table.dtype), mesh=_sc_mesh(),
               scratch_types=[], name="sc_gather",
               cost_estimate=pl.CostEstimate(
                   flops=0, transcendentals=0,
                   bytes_accessed=n_chunks * n_idx * (8 * w + 4)))
    def gather(table_hbm, idx_hbm, out_hbm):
        def body(idx_vmem, out_vmem):
            pltpu.sync_copy(table_hbm.at[idx_vmem.at[0]], out_vmem)

        pltpu.emit_pipeline(
            body,
            grid=(n_chunks, idx_blocks),
            in_specs=[pl.BlockSpec((1, SC_ROWS), lambda c, i: (0, c * idx_blocks + i))],
            out_specs=[pl.BlockSpec((SC_ROWS, w), lambda c, i: (c * idx_blocks + i, 0))],
            core_axis_name=("core", "subcore"),
            dimension_semantics=(pltpu.PARALLEL, pltpu.PARALLEL),
        )(idx_hbm, out_hbm)

    return gather(table.reshape(n_chunks * p, w), idx.reshape(1, n_chunks * n_idx)).reshape(
        n_chunks, n_idx, w)


def _expert_kernel(bexp_ref, nvalid_ref, xs_ref, wgu_ref, bgu_ref, wd_ref, bd_ref, ys_ref,
                   wgu_bf_ref, wd_bf_ref, *, d_exp):
    b = pl.program_id(0)

    @pl.when(b < nvalid_ref[0])
    def _():
        @pl.when(jnp.logical_or(b == 0, bexp_ref[b] != bexp_ref[jnp.maximum(b - 1, 0)]))
        def _():
            wgu_bf_ref[...] = wgu_ref[...].astype(BF16)
            wd_bf_ref[...] = wd_ref[...].astype(BF16)

        n_chunks = xs_ref.shape[0]
        lo, hi = _unpack_bf16_pairs(
            jnp.concatenate([xs_ref[c] for c in range(n_chunks)], axis=1))
        xb = jnp.concatenate([lo, hi], axis=1).astype(BF16)
        gu = jnp.dot(xb, wgu_bf_ref[...], preferred_element_type=F32) + bgu_ref[...]
        gate = jnp.minimum(gu[:, :d_exp], SWIGLU_LIMIT)
        up = jnp.clip(gu[:, d_exp:], -SWIGLU_LIMIT, SWIGLU_LIMIT)
        glu = gate * jax.nn.sigmoid(gate * SWIGLU_ALPHA)
        act = ((up + 1.0) * glu).astype(BF16)
        y = jnp.dot(act, wd_bf_ref[...], preferred_element_type=F32) + bd_ref[...]
        y_words = _pack_bf16_pairs(y)
        for c in range(n_chunks):
            ys_ref[c] = y_words[:, c * SC_CHUNK:(c + 1) * SC_CHUNK]


def _experts(bexp, nvalid, xs, wgu, bgu, wd, bd):
    n_chunks, n_rows, _ = xs.shape
    _, d, d_exp2 = wgu.shape
    d_exp = d_exp2 // 2
    nb = n_rows // MOE_BLOCK
    blk = lambda b, be, nv: jnp.minimum(b, nv[0] - 1)
    return pl.pallas_call(
        functools.partial(_expert_kernel, d_exp=d_exp),
        grid_spec=pltpu.PrefetchScalarGridSpec(
            num_scalar_prefetch=2,
            grid=(nb,),
            in_specs=[
                pl.BlockSpec((n_chunks, MOE_BLOCK, SC_CHUNK),
                             lambda b, be, nv: (0, blk(b, be, nv), 0)),
                pl.BlockSpec((None, d, 2 * d_exp), lambda b, be, nv: (be[blk(b, be, nv)], 0, 0)),
                pl.BlockSpec((None, 1, 2 * d_exp), lambda b, be, nv: (be[blk(b, be, nv)], 0, 0)),
                pl.BlockSpec((None, d_exp, d), lambda b, be, nv: (be[blk(b, be, nv)], 0, 0)),
                pl.BlockSpec((None, 1, d), lambda b, be, nv: (be[blk(b, be, nv)], 0, 0)),
            ],
            out_specs=pl.BlockSpec((n_chunks, MOE_BLOCK, SC_CHUNK),
                                   lambda b, be, nv: (0, blk(b, be, nv), 0)),
            scratch_shapes=[pltpu.VMEM((d, 2 * d_exp), BF16), pltpu.VMEM((d_exp, d), BF16)],
        ),
        out_shape=jax.ShapeDtypeStruct((n_chunks, n_rows, SC_CHUNK), U32),
        name="experts",
        cost_estimate=pl.CostEstimate(
            flops=6 * n_rows * d * d_exp, transcendentals=n_rows * d_exp,
            bytes_accessed=4 * n_rows * d + 4 * N_EXPERTS * 3 * d * d_exp),
        compiler_params=pltpu.CompilerParams(
            dimension_semantics=("arbitrary",), vmem_limit_bytes=VMEM_LIMIT),
    )(bexp, nvalid, xs, wgu, bgu, wd, bd)


def _combine_kernel(prev_ref, x1_ref, gate_ref, yg_ref, out_ref):
    del prev_ref
    n_chunks = yg_ref.shape[0]
    half = n_chunks * SC_CHUNK
    for c in range(n_chunks):
        sl_lo = slice(c * SC_CHUNK, (c + 1) * SC_CHUNK)
        sl_hi = slice(half + c * SC_CHUNK, half + (c + 1) * SC_CHUNK)
        acc_lo = x1_ref[:, sl_lo]
        acc_hi = x1_ref[:, sl_hi]
        for k in range(TOP_K):
            lo, hi = _unpack_bf16_pairs(yg_ref[c, k])
            g = gate_ref[:, k:k + 1]
            acc_lo = acc_lo + g * lo
            acc_hi = acc_hi + g * hi
        out_ref[:, sl_lo] = acc_lo
        out_ref[:, sl_hi] = acc_hi


def _combine(x1, gates, yg, out_prev, tcb, row0, t_total):
    t, d = x1.shape
    n_chunks = yg.shape[0]
    blk0 = row0 // tcb
    in_specs = [
        pl.BlockSpec((tcb, d), lambda i: (i, 0)),
        pl.BlockSpec((tcb, LANES), lambda i: (i, 0)),
        pl.BlockSpec((n_chunks, TOP_K, tcb, SC_CHUNK), lambda i: (0, 0, i, 0)),
    ]
    args = [x1, gates, yg]
    aliases = {}
    body = functools.partial(_combine_kernel, None)
    if out_prev is not None:
        in_specs.append(pl.BlockSpec(memory_space=pl.ANY))
        args.append(out_prev)
        aliases = {3: 0}
        body = lambda a, b, c, prev, o: _combine_kernel(prev, a, b, c, o)
    return pl.pallas_call(
        body,
        grid=(t // tcb,),
        in_specs=in_specs,
        out_specs=pl.BlockSpec((tcb, d), lambda i: (blk0 + i, 0)),
        out_shape=jax.ShapeDtypeStruct((t_total, d), F32),
        input_output_aliases=aliases,
        name="combine",
        cost_estimate=pl.CostEstimate(
            flops=2 * TOP_K * t * d, transcendentals=0,
            bytes_accessed=t * (4 * d + 4 * d + 2 * TOP_K * d + 4 * LANES)),
        compiler_params=pltpu.CompilerParams(dimension_semantics=("arbitrary",)),
    )(*args)


def _pick_tile(n, pref):
    t = min(pref, n)
    while n % t:
        t //= 2
    return t


def kernel(x, mem, positions, attn_norm_g, mem_norm_g, w_in, b_gates, a_q_norm_g, a_k_norm_g,
           a_sinks, w_o_a, gmlp_ln_g, gmlp_ln_b, gmlp_w_s, gmlp_b_s, w_o_b, w_mem_kv,
           c_q_norm_g, c_k_norm_g, w_o_c, w_out, ffn_norm_g, router_w, router_b,
           w_gate_up, b_gate_up, w_down, b_down):
    n_batch, seq, d = x.shape
    m_len = mem.shape[1]
    depth = w_in.shape[0]
    t = n_batch * seq
    n_groups = N_TOKEN_GROUPS if n_batch % N_TOKEN_GROUPS == 0 else 1
    gb = n_batch // n_groups
    tg = gb * seq
    n_asg = tg * TOP_K
    nb = -(-n_asg // MOE_BLOCK) + N_EXPERTS
    n_rows = nb * MOE_BLOCK

    inv_freq = ROPE_THETA ** (-jnp.arange(0, HEAD_DIM, 2, dtype=F32) / HEAD_DIM)
    invf = jnp.tile(inv_freq, LANES // (HEAD_DIM // 2))[None, :]
    sgn = jnp.tile(jnp.concatenate([-jnp.ones((HEAD_DIM // 2,), F32),
                                    jnp.ones((HEAD_DIM // 2,), F32)]), LANES // HEAD_DIM)[None, :]
    pos2d = positions.reshape(t, 1).astype(I32)
    mem2d = mem.reshape(n_batch * m_len, d)
    x2d = x.reshape(t, d)

    tm = _pick_tile(tg, 512)
    ts = _pick_tile(seq, 512)
    tp = _pick_tile(tg, 2048)
    tcb = _pick_tile(tg, 512)

    for l in range(depth):
        kc, vc = _mem_kv(mem2d, mem_norm_g[l][None, :], w_mem_kv[l].astype(BF16),
                         c_k_norm_g[l][None, :], n_batch, m_len)
        w_in_bf = w_in[l].astype(BF16)
        mix_w = (w_o_a[l].astype(BF16), w_o_b[l].astype(BF16), w_o_c[l].astype(BF16),
                 w_out[l].astype(BF16))
        gq = jnp.tile(a_q_norm_g[l], LANES // HEAD_DIM)[None, :]
        gk = jnp.tile(a_k_norm_g[l], LANES // HEAD_DIM)[None, :]
        rw32 = jnp.pad(router_w[l], ((0, 0), (0, LANES - N_EXPERTS)))
        rw_hi = rw32.astype(BF16)
        rw = jnp.concatenate([rw_hi, (rw32 - rw_hi.astype(F32)).astype(BF16)], axis=1)
        rb = jnp.pad(router_b[l], (0, LANES - N_EXPERTS))[None, :]

        routed = []
        for g in range(n_groups):
            proj = _proj(
                x2d, pos2d, attn_norm_g[l][None, :], w_in_bf, invf, sgn, gq, gk,
                c_q_norm_g[l][None, :], gmlp_ln_g[l][None, :], gmlp_ln_b[l][None, :],
                b_gates[l].reshape(1, 3 * d), tm, g * tg, tg)
            x1, h2, idx, rank, gates, counts = _mix(
                a_sinks[l], proj, x2d, kc, vc, gmlp_w_s[l], gmlp_b_s[l].T, *mix_w,
                ffn_norm_g[l][None, :], rw, rb, gb, seq, m_len, ts, g * gb)

            cnt = counts[0, :N_EXPERTS].astype(I32)
            padded = (cnt + MOE_BLOCK - 1) // MOE_BLOCK * MOE_BLOCK
            pend = jnp.cumsum(padded)
            pstart = (pend - padded).astype(I32)
            nvalid = (pend[-1:] // MOE_BLOCK).astype(I32)
            blk_row = jnp.arange(nb, dtype=I32) * MOE_BLOCK
            bexp = jnp.minimum(jnp.sum((pend[None, :] <= blk_row[:, None]).astype(I32), axis=1),
                               N_EXPERTS - 1).astype(I32)
            j = jnp.arange(MOE_BLOCK, dtype=I32)[None, :]
            fill = jnp.where(j < (padded - cnt)[:, None], pend[:, None] - 1 - j,
                             n_rows - 1 - j).astype(I32)

            pos = _pos(pstart, idx, rank, tp)
            pos_km = pos[:, :TOP_K].T.reshape(n_asg)
            n_chunks = h2.shape[0]
            chunk_off = (jnp.arange(n_chunks, dtype=I32) * n_rows)[:, None]
            xs = _sc_scatter_rows(
                h2, jnp.concatenate([pos_km, fill.reshape(-1)])[None, :] + chunk_off, n_rows)
            routed.append((x1, gates, xs, bexp, nvalid, pos_km, chunk_off))

        gathered = []
        for x1, gates, xs, bexp, nvalid, pos_km, chunk_off in routed:
            ys = _experts(bexp, nvalid, xs, w_gate_up[l], b_gate_up[l][:, None, :],
                          w_down[l], b_down[l][:, None, :])
            yg = _sc_gather_rows(ys, pos_km[None, :] + chunk_off)
            gathered.append((x1, gates, yg.reshape(yg.shape[0], TOP_K, tg, SC_CHUNK)))

        out = None
        for g, (x1, gates, yg) in enumerate(gathered):
            out = _combine(x1, gates, yg, out, tcb, g * tg, t)
        x2d = out
    return x2d.reshape(n_batch, seq, d)
```

```python
import functools

import numpy as np
import jax
import jax.numpy as jnp
from jax import lax
from jax.experimental import pallas as pl
from jax.experimental.pallas import tpu as pltpu
from jax.experimental.pallas import tpu_sc as plsc

F32 = jnp.float32
BF16 = jnp.bfloat16
I32 = jnp.int32
U32 = jnp.uint32
HI16 = np.uint32(0xFFFF0000)

EPS = 1e-6
LANES = 128
HEAD_DIM = 64
N_Q_HEADS = 16
N_KV_HEADS = 2
ATT_BLOCK = 128
ROPE_THETA = 10000.0
GMLP_WIDTH = 512
GMLP_GROUPS = 4
GMLP_CHUNK = 128
X_HEADS = 4
X_HEAD_DIM = 128
N_EXPERTS = 32
TOP_K = 4
SWIGLU_LIMIT = 7.0
SWIGLU_ALPHA = 1.702
MOE_BLOCK = 512
SC_ROWS = 128
SC_CHUNK = 256
N_TOKEN_GROUPS = 2
IDX_ROWS = 8

A_Q = N_Q_HEADS * HEAD_DIM
A_KV = N_KV_HEADS * HEAD_DIM
C_Q = X_HEADS * X_HEAD_DIM

OFF_Q = 0
OFF_K4 = OFF_Q + A_Q
OFF_V4 = OFF_K4 + 4 * LANES
OFF_U = OFF_V4 + 4 * LANES
OFF_VN = OFF_U + GMLP_WIDTH
OFF_QC = OFF_VN + GMLP_WIDTH
OFF_GA = OFF_QC + C_Q
PROJ_W_BASE = OFF_GA

VMEM_LIMIT = 56 * 1024 * 1024


def _lane_iota(shape):
    return lax.broadcasted_iota(I32, shape, len(shape) - 1)


def _rms(x, g):
    return x * lax.rsqrt(jnp.mean(x * x, axis=-1, keepdims=True) + EPS) * g


def _pack_bf16_pairs(x):
    n = x.shape[1] // 2
    bits = pltpu.bitcast(x.astype(BF16).astype(F32), U32)
    return (bits[:, :n] >> 16) | (bits[:, n:] & HI16)


def _unpack_bf16_pairs(w):
    return pltpu.bitcast(w << 16, F32), pltpu.bitcast(w & HI16, F32)


def _gelu(x):
    return 0.5 * x * (1.0 + lax.erf(x * np.float32(np.sqrt(0.5))))


def _memkv_kernel(mem_ref, g_ref, w_ref, gk_ref, kc_ref, vc_ref):
    h = _rms(mem_ref[...], g_ref[...]).astype(BF16)
    kv = jnp.dot(h, w_ref[...], preferred_element_type=F32)
    for hh in range(X_HEADS):
        sl = slice(hh * X_HEAD_DIM, (hh + 1) * X_HEAD_DIM)
        kc_ref[:, sl] = _rms(kv[:, sl], gk_ref[...]).astype(BF16)
    vc_ref[...] = kv[:, C_Q:].astype(BF16)


def _mem_kv(mem2d, g, w_bf, gk, n_batch, m_len):
    d = mem2d.shape[1]
    return pl.pallas_call(
        _memkv_kernel,
        grid=(n_batch,),
        in_specs=[
            pl.BlockSpec((m_len, d), lambda b: (b, 0)),
            pl.BlockSpec((1, d), lambda b: (0, 0)),
            pl.BlockSpec((d, 2 * C_Q), lambda b: (0, 0)),
            pl.BlockSpec((1, X_HEAD_DIM), lambda b: (0, 0)),
        ],
        out_specs=[
            pl.BlockSpec((m_len, C_Q), lambda b: (b, 0)),
            pl.BlockSpec((m_len, C_Q), lambda b: (b, 0)),
        ],
        out_shape=[jax.ShapeDtypeStruct((n_batch * m_len, C_Q), BF16)] * 2,
        name="mem_kv",
        compiler_params=pltpu.CompilerParams(dimension_semantics=("arbitrary",)),
    )(mem2d, g, w_bf, gk)


def _proj_kernel(x_ref, pos_ref, g_ref, w_ref, invf_ref, sgn_ref, gq_ref, gk_ref, gcq_ref,
                 lng_ref, lnb_ref, bg_ref, out_ref, *, d_model):
    tm = x_ref.shape[0]
    h = _rms(x_ref[...], g_ref[...]).astype(BF16)

    ang = pos_ref[...].astype(F32) * invf_ref[...]
    cosv = jnp.cos(ang)
    sinv = jnp.sin(ang) * sgn_ref[...]
    lane = _lane_iota((tm, LANES))
    first_head = lane < HEAD_DIM
    lo_half = (lane % HEAD_DIM) < (HEAD_DIM // 2)

    def head_norm_rope(blk, g):
        y = blk * blk
        s_lo = jnp.sum(jnp.where(first_head, y, 0.0), axis=-1, keepdims=True)
        s_hi = jnp.sum(jnp.where(first_head, 0.0, y), axis=-1, keepdims=True)
        ss = jnp.where(first_head, s_lo, s_hi)
        n = blk * lax.rsqrt(ss * (1.0 / HEAD_DIM) + EPS) * g
        rot = jnp.where(lo_half, pltpu.roll(n, LANES - HEAD_DIM // 2, 1),
                        pltpu.roll(n, HEAD_DIM // 2, 1))
        return n * cosv + rot * sinv

    def proj(a, b):
        return jnp.dot(h, w_ref[:, a:b], preferred_element_type=F32)

    pq = proj(0, A_Q)
    for c in range(A_Q // LANES):
        sl = slice(c * LANES, (c + 1) * LANES)
        out_ref[:, OFF_Q + c * LANES:OFF_Q + (c + 1) * LANES] = (
            head_norm_rope(pq[:, sl], gq_ref[...]) * (HEAD_DIM ** -0.5)).astype(BF16)

    pkv = proj(A_Q, A_Q + 2 * A_KV)
    kn = head_norm_rope(pkv[:, :LANES], gk_ref[...])
    vv = pkv[:, LANES:]
    for off, t in ((OFF_K4, kn), (OFF_V4, vv)):
        tr = pltpu.roll(t, HEAD_DIM, 1)
        parts = (jnp.where(first_head, t, 0.0), jnp.where(first_head, 0.0, tr),
                 jnp.where(first_head, tr, 0.0), jnp.where(first_head, 0.0, t))
        for j, p in enumerate(parts):
            out_ref[:, off + j * LANES:off + (j + 1) * LANES] = p.astype(BF16)

    o0 = A_Q + 2 * A_KV
    out_ref[:, OFF_U:OFF_U + GMLP_WIDTH] = _gelu(proj(o0, o0 + GMLP_WIDTH)).astype(BF16)
    gv = _gelu(proj(o0 + GMLP_WIDTH, o0 + 2 * GMLP_WIDTH))
    mu = jnp.mean(gv, axis=-1, keepdims=True)
    var = jnp.mean(jnp.square(gv - mu), axis=-1, keepdims=True)
    out_ref[:, OFF_VN:OFF_VN + GMLP_WIDTH] = (
        (gv - mu) * lax.rsqrt(var + EPS) * lng_ref[...] + lnb_ref[...]).astype(BF16)

    o1 = o0 + 2 * GMLP_WIDTH
    pc = proj(o1, o1 + C_Q)
    for hh in range(X_HEADS):
        sl = slice(hh * X_HEAD_DIM, (hh + 1) * X_HEAD_DIM)
        out_ref[:, OFF_QC + hh * X_HEAD_DIM:OFF_QC + (hh + 1) * X_HEAD_DIM] = (
            _rms(pc[:, sl], gcq_ref[...]) * (X_HEAD_DIM ** -0.5)).astype(BF16)

    o2 = o1 + C_Q
    for j in range(3):
        sl = slice(j * d_model, (j + 1) * d_model)
        z = proj(o2 + j * d_model, o2 + (j + 1) * d_model) + bg_ref[:, sl]
        out_ref[:, OFF_GA + j * d_model:OFF_GA + (j + 1) * d_model] = (
            0.5 * jnp.tanh(0.5 * z) + 0.5).astype(BF16)


def _proj_after_kernel(after_ref, *refs, d_model):
    del after_ref
    _proj_kernel(*refs, d_model=d_model)


def _proj(x2d, pos2d, g, w_bf, invf, sgn, gq, gk, gcq, lng, lnb, bg, tm, row0, t, after):
    d = x2d.shape[1]
    d_in = w_bf.shape[1]
    pw = PROJ_W_BASE + 3 * d
    blk0 = row0 // tm
    full = lambda shape: pl.BlockSpec(shape, lambda i: (0,) * len(shape))
    body = functools.partial(_proj_kernel, d_model=d)
    lead_specs, lead_args = [], []
    if after is not None:
        body = functools.partial(_proj_after_kernel, d_model=d)
        lead_specs, lead_args = [pl.BlockSpec(memory_space=pl.ANY)], [after]
    return pl.pallas_call(
        body,
        grid=(t // tm,),
        in_specs=lead_specs + [
            pl.BlockSpec((tm, d), lambda i: (blk0 + i, 0)),
            pl.BlockSpec((tm, 1), lambda i: (blk0 + i, 0)),
            full((1, d)), full((d, d_in)), full((1, LANES)), full((1, LANES)),
            full((1, LANES)), full((1, LANES)), full((1, X_HEAD_DIM)),
            full((1, GMLP_WIDTH)), full((1, GMLP_WIDTH)), full((1, 3 * d)),
        ],
        out_specs=pl.BlockSpec((tm, pw), lambda i: (i, 0)),
        out_shape=jax.ShapeDtypeStruct((t, pw), BF16),
        name="proj",
        cost_estimate=pl.CostEstimate(
            flops=2 * t * d * d_in, transcendentals=t * (3 * d + 2 * GMLP_WIDTH + 2 * LANES),
            bytes_accessed=4 * t * d + 2 * t * pw + 2 * d * d_in),
        compiler_params=pltpu.CompilerParams(
            dimension_semantics=("arbitrary",), vmem_limit_bytes=VMEM_LIMIT),
    )(*lead_args, x2d, pos2d, g, w_bf, invf, sgn, gq, gk, gcq, lng, lnb, bg)


def _mix_kernel(sinks_ref, proj_ref, kprev_ref, vprev_ref, x_ref, kc_ref, vc_ref, ws_ref, bst_ref,
                woa_ref, wob_ref, woc_ref, wout_ref, gffn_ref, rw_ref, rb_ref,
                x1_ref, h2_ref, idx_ref, rank_ref, gate_ref, cnt_ref,
                run_ref, oa_ref, ob_ref, oc_ref, *, d_model):
    ts = x_ref.shape[0]
    s_idx = pl.program_id(1)
    first_step = jnp.logical_and(pl.program_id(0) == 0, s_idx == 0)
    neg_inf = float("-inf")

    cpk = A_Q // LANES // N_KV_HEADS
    stack = cpk * ATT_BLOCK
    srow = lax.broadcasted_iota(I32, (stack, 2 * ATT_BLOCK), 0)
    qi = srow % ATT_BLOCK
    kj = lax.broadcasted_iota(I32, (stack, 2 * ATT_BLOCK), 1)
    band = jnp.logical_and(kj <= ATT_BLOCK + qi, kj > qi)
    chunk_of_row = lax.broadcasted_iota(I32, (stack, 1), 0) // ATT_BLOCK
    for qb in range(ts // ATT_BLOCK):
        r0 = qb * ATT_BLOCK
        rows = slice(r0, r0 + ATT_BLOCK)
        if qb == 0:
            kp, vp = kprev_ref[...], vprev_ref[...]
            mask = jnp.logical_and(band, jnp.logical_or(kj >= ATT_BLOCK, s_idx > 0))
        else:
            prow = slice(r0 - ATT_BLOCK, r0)
            kp = proj_ref[prow, OFF_K4:OFF_K4 + 4 * LANES]
            vp = proj_ref[prow, OFF_V4:OFF_V4 + 4 * LANES]
            mask = band
        k4 = jnp.concatenate([kp, proj_ref[rows, OFF_K4:OFF_K4 + 4 * LANES]], axis=0)
        v4 = jnp.concatenate([vp, proj_ref[rows, OFF_V4:OFF_V4 + 4 * LANES]], axis=0)
        for kvh in range(N_KV_HEADS):
            c0 = kvh * cpk
            q4 = jnp.concatenate(
                [proj_ref[rows, OFF_Q + (c0 + c) * LANES:OFF_Q + (c0 + c + 1) * LANES]
                 for c in range(cpk)], axis=0)
            o = jnp.zeros((stack, LANES), F32)
            for half in range(2):
                col = slice((2 * kvh + half) * LANES, (2 * kvh + half + 1) * LANES)
                s = lax.dot_general(q4, k4[:, col], (((1,), (1,)), ((), ())),
                                    preferred_element_type=F32)
                s = jnp.where(mask, s, neg_inf)
                sink = jnp.zeros((stack, 1), F32)
                for c in range(cpk):
                    sink = jnp.where(chunk_of_row == c, sinks_ref[2 * (c0 + c) + half], sink)
                m = jnp.maximum(jnp.max(s, axis=-1, keepdims=True), sink)
                p = jnp.exp(s - m)
                den = jnp.sum(p, axis=-1, keepdims=True) + jnp.exp(sink - m)
                o = o + jnp.dot(p.astype(BF16), v4[:, col], preferred_element_type=F32) / den
            for c in range(cpk):
                oa_ref[rows, (c0 + c) * LANES:(c0 + c + 1) * LANES] = (
                    o[c * ATT_BLOCK:(c + 1) * ATT_BLOCK].astype(BF16))

    ti = lax.broadcasted_iota(I32, (GMLP_CHUNK, GMLP_CHUNK), 0)
    si = lax.broadcasted_iota(I32, (GMLP_CHUNK, GMLP_CHUNK), 1)
    for g in range(GMLP_GROUPS):
        wt = jnp.where(si <= ti, ws_ref[g], 0.0).astype(BF16)
        bcol = bst_ref[:, g:g + 1]
        for ch in range(ts // GMLP_CHUNK):
            rows = slice(ch * GMLP_CHUNK, (ch + 1) * GMLP_CHUNK)
            vn = proj_ref[rows, OFF_VN + g * LANES:OFF_VN + (g + 1) * LANES]
            u = proj_ref[rows, OFF_U + g * LANES:OFF_U + (g + 1) * LANES].astype(F32)
            mixed = jnp.dot(wt, vn, preferred_element_type=F32) + bcol
            ob_ref[rows, g * LANES:(g + 1) * LANES] = (u * mixed).astype(BF16)

    for hh in range(X_HEADS):
        sl = slice(hh * X_HEAD_DIM, (hh + 1) * X_HEAD_DIM)
        qc = proj_ref[:, OFF_QC + hh * X_HEAD_DIM:OFF_QC + (hh + 1) * X_HEAD_DIM]
        s = lax.dot_general(qc, kc_ref[:, sl], (((1,), (1,)), ((), ())),
                            preferred_element_type=F32)
        p = jnp.exp(s - jnp.max(s, axis=-1, keepdims=True))
        den = jnp.sum(p, axis=-1, keepdims=True)
        oc_ref[:, sl] = (jnp.dot(p.astype(BF16), vc_ref[:, sl],
                                 preferred_element_type=F32) / den).astype(BF16)

    def gate(j):
        return proj_ref[:, OFF_GA + j * d_model:OFF_GA + (j + 1) * d_model].astype(F32)

    merged = gate(0) * jnp.dot(oa_ref[...], woa_ref[...], preferred_element_type=F32)
    merged = merged + gate(1) * jnp.dot(ob_ref[...], wob_ref[...], preferred_element_type=F32)
    merged = merged + gate(2) * jnp.dot(oc_ref[...], woc_ref[...], preferred_element_type=F32)
    x1 = x_ref[...] + jnp.dot(merged.astype(BF16), wout_ref[...], preferred_element_type=F32)
    x1_ref[...] = x1

    h2 = _rms(x1, gffn_ref[...])
    h2_words = _pack_bf16_pairs(h2)
    for c in range(h2_ref.shape[0]):
        h2_ref[c] = h2_words[:, c * SC_CHUNK:(c + 1) * SC_CHUNK]
    lane = _lane_iota((ts, LANES))
    h2_hi = h2.astype(BF16)
    h2_lo = (h2 - h2_hi.astype(F32)).astype(BF16)
    part = jnp.dot(h2_hi, rw_ref[...], preferred_element_type=F32)
    logits = (part[:, :LANES] + part[:, LANES:]
              + jnp.dot(h2_lo, rw_ref[:, :LANES], preferred_element_type=F32) + rb_ref[...])
    logits = jnp.where(lane < N_EXPERTS, logits, neg_inf)
    vals, idxs = [], []
    for _ in range(TOP_K):
        m = jnp.max(logits, axis=-1, keepdims=True)
        i = jnp.min(jnp.where(logits == m, lane, LANES), axis=-1, keepdims=True)
        vals.append(m)
        idxs.append(i)
        logits = jnp.where(lane == i, neg_inf, logits)
    es = [jnp.exp(v - vals[0]) for v in vals]
    den = es[0] + es[1] + es[2] + es[3]

    @pl.when(first_step)
    def _():
        run_ref[...] = jnp.zeros_like(run_ref)

    hot = [lane == i for i in idxs]
    multihot = jnp.where(jnp.logical_or(jnp.logical_or(hot[0], hot[1]),
                                        jnp.logical_or(hot[2], hot[3])), 1.0, 0.0)
    tr = lax.broadcasted_iota(I32, (ts, ts), 0)
    tc = lax.broadcasted_iota(I32, (ts, ts), 1)
    ltri = jnp.where(tc < tr, 1.0, 0.0).astype(BF16)
    before = jnp.dot(ltri, multihot.astype(BF16), preferred_element_type=F32) + run_ref[...]
    idx_out = jnp.zeros((ts, LANES), I32)
    rank_out = jnp.zeros((ts, LANES), I32)
    gate_out = jnp.zeros((ts, LANES), F32)
    for k in range(TOP_K):
        rk = jnp.sum(jnp.where(hot[k], before, 0.0), axis=-1, keepdims=True)
        idx_out = jnp.where(lane == k, idxs[k], idx_out)
        rank_out = jnp.where(lane == k, rk.astype(I32), rank_out)
        gate_out = jnp.where(lane == k, es[k] / den, gate_out)
    idx_ref[...] = jnp.transpose(idx_out)[:IDX_ROWS]
    rank_ref[...] = jnp.transpose(rank_out)[:IDX_ROWS]
    gate_ref[...] = gate_out
    run_ref[...] = run_ref[...] + jnp.sum(multihot, axis=0, keepdims=True)
    cnt_ref[...] = run_ref[...]


def _mix(sinks, proj, x2d, kc, vc, w_s, bst, woa, wob, woc, wout, gffn, rw, rb,
         n_batch, seq, m_len, ts, batch0):
    d = x2d.shape[1]
    t = n_batch * seq
    pw = proj.shape[1]
    ns = seq // ts
    nblk = seq // ATT_BLOCK
    per = ts // ATT_BLOCK
    n_chunks = d // (2 * SC_CHUNK)
    full = lambda shape: pl.BlockSpec(shape, lambda b, s: (0,) * len(shape))
    row = lambda width: pl.BlockSpec((ts, width), lambda b, s: (b * ns + s, 0))
    prev = lambda colblk: pl.BlockSpec(
        (ATT_BLOCK, 4 * LANES), lambda b, s: (b * nblk + jnp.maximum(s * per - 1, 0), colblk))
    return pl.pallas_call(
        functools.partial(_mix_kernel, d_model=d),
        grid=(n_batch, ns),
        in_specs=[
            pl.BlockSpec(memory_space=pltpu.SMEM),
            row(pw), prev(OFF_K4 // (4 * LANES)), prev(OFF_V4 // (4 * LANES)),
            pl.BlockSpec((ts, d), lambda b, s: ((batch0 + b) * ns + s, 0)),
            pl.BlockSpec((m_len, C_Q), lambda b, s: (batch0 + b, 0)),
            pl.BlockSpec((m_len, C_Q), lambda b, s: (batch0 + b, 0)),
            full((GMLP_GROUPS, GMLP_CHUNK, GMLP_CHUNK)), full((GMLP_CHUNK, GMLP_GROUPS)),
            full((A_Q, d)), full((GMLP_WIDTH, d)), full((C_Q, d)), full((d, d)),
            full((1, d)), full((d, 2 * LANES)), full((1, LANES)),
        ],
        out_specs=[row(d),
                   pl.BlockSpec((n_chunks, ts, SC_CHUNK), lambda b, s: (0, b * ns + s, 0)),
                   pl.BlockSpec((IDX_ROWS, ts), lambda b, s: (0, b * ns + s)),
                   pl.BlockSpec((IDX_ROWS, ts), lambda b, s: (0, b * ns + s)),
                   row(LANES), full((1, LANES))],
        out_shape=[
            jax.ShapeDtypeStruct((t, d), F32),
            jax.ShapeDtypeStruct((n_chunks, t, SC_CHUNK), U32),
            jax.ShapeDtypeStruct((IDX_ROWS, t), I32), jax.ShapeDtypeStruct((IDX_ROWS, t), I32),
            jax.ShapeDtypeStruct((t, LANES), F32), jax.ShapeDtypeStruct((1, LANES), F32),
        ],
        scratch_shapes=[
            pltpu.VMEM((1, LANES), F32),
            pltpu.VMEM((ts, A_Q), BF16), pltpu.VMEM((ts, GMLP_WIDTH), BF16),
            pltpu.VMEM((ts, C_Q), BF16),
        ],
        name="mix",
        cost_estimate=pl.CostEstimate(
            flops=2 * t * (d * (A_Q + GMLP_WIDTH + C_Q + d) + 4 * ATT_BLOCK * A_Q
                           + GMLP_CHUNK * GMLP_WIDTH + 2 * m_len * C_Q + 3 * d * LANES),
            transcendentals=t * (2 * ATT_BLOCK * N_Q_HEADS + m_len * X_HEADS),
            bytes_accessed=t * (2 * pw + 4 * d + 4 * d + 2 * d + 12 * LANES)),
        compiler_params=pltpu.CompilerParams(
            dimension_semantics=("arbitrary", "arbitrary"), vmem_limit_bytes=VMEM_LIMIT),
    )(sinks, proj, proj, proj, x2d, kc, vc, w_s, bst, woa, wob, woc, wout, gffn, rw, rb)


def _pos_kernel(pstart_ref, idx_ref, rank_ref, pos_ref):
    idx = idx_ref[...]
    pos = rank_ref[...]
    for e in range(N_EXPERTS):
        pos = pos + jnp.where(idx == e, pstart_ref[e], 0)
    pos_ref[...] = pos


def _pos(pstart, idx, rank, tp):
    t = idx.shape[1]
    return pl.pallas_call(
        _pos_kernel,
        grid_spec=pltpu.PrefetchScalarGridSpec(
            num_scalar_prefetch=1,
            grid=(t // tp,),
            in_specs=[pl.BlockSpec((IDX_ROWS, tp), lambda i, ps: (0, i)),
                      pl.BlockSpec((IDX_ROWS, tp), lambda i, ps: (0, i))],
            out_specs=pl.BlockSpec((IDX_ROWS, tp), lambda i, ps: (0, i)),
        ),
        out_shape=jax.ShapeDtypeStruct((IDX_ROWS, t), I32),
        name="slot_pos",
        compiler_params=pltpu.CompilerParams(dimension_semantics=("arbitrary",)),
    )(pstart, idx, rank)


def _sc_mesh():
    return plsc.VectorSubcoreMesh(core_axis_name="core", subcore_axis_name="subcore")


def _sc_scatter_rows(src, idx, n_rows):
    n_chunks, t, w = src.shape
    n_idx = idx.shape[1]
    src_blocks = t // SC_ROWS
    idx_blocks = n_idx // SC_ROWS

    @pl.kernel(out_type=jax.ShapeDtypeStruct((n_chunks * n_rows, w), src.dtype), mesh=_sc_mesh(),
               scratch_types=[], name="sc_dispatch",
               cost_estimate=pl.CostEstimate(
                   flops=0, transcendentals=0,
                   bytes_accessed=n_chunks * n_idx * (8 * w + 4)))
    def scatter(src_hbm, idx_hbm, out_hbm):
        def body(src_vmem, idx_vmem):
            pltpu.sync_copy(src_vmem, out_hbm.at[idx_vmem.at[0]])

        pltpu.emit_pipeline(
            body,
            grid=(n_chunks, idx_blocks),
            in_specs=[pl.BlockSpec((SC_ROWS, w), lambda c, i: (c * src_blocks + i % src_blocks, 0)),
                      pl.BlockSpec((1, SC_ROWS), lambda c, i: (0, c * idx_blocks + i))],
            out_specs=[],
            core_axis_name=("core", "subcore"),
            dimension_semantics=(pltpu.PARALLEL, pltpu.PARALLEL),
        )(src_hbm, idx_hbm)

    return scatter(src.reshape(n_chunks * t, w), idx.reshape(1, n_chunks * n_idx)).reshape(
        n_chunks, n_rows, w)


def _sc_gather_rows(table, idx):
    n_chunks, p, w = table.shape
    n_idx = idx.shape[1]
    idx_blocks = n_idx // SC_ROWS

    @pl.kernel(out_type=jax.ShapeDtypeStruct((n_chunks * n_idx, w), table.dtype), mesh=_sc_mesh(),
               scratch_types=[], name="sc_gather",
               cost_estimate=pl.CostEstimate(
                   flops=0, transcendentals=0,
                   bytes_accessed=n_chunks * n_idx * (8 * w + 4)))
    def gather(table_hbm, idx_hbm, out_hbm):
        def body(idx_vmem, out_vmem):
            pltpu.sync_copy(table_hbm.at[idx_vmem.at[0]], out_vmem)

        pltpu.emit_pipeline(
            body,
            grid=(n_chunks, idx_blocks),
            in_specs=[pl.BlockSpec((1, SC_ROWS), lambda c, i: (0, c * idx_blocks + i))],
            out_specs=[pl.BlockSpec((SC_ROWS, w), lambda c, i: (c * idx_blocks + i, 0))],
            core_axis_name=("core", "subcore"),
            dimension_semantics=(pltpu.PARALLEL, pltpu.PARALLEL),
        )(idx_hbm, out_hbm)

    return gather(table.reshape(n_chunks * p, w), idx.reshape(1, n_chunks * n_idx)).reshape(
        n_chunks, n_idx, w)


def _expert_kernel(bexp_ref, nvalid_ref, xs_ref, wgu_ref, bgu_ref, wd_ref, bd_ref, ys_ref,
                   wgu_bf_ref, wd_bf_ref, *, d_exp):
    b = pl.program_id(0)

    @pl.when(b < nvalid_ref[0])
    def _():
        @pl.when(jnp.logical_or(b == 0, bexp_ref[b] != bexp_ref[jnp.maximum(b - 1, 0)]))
        def _():
            wgu_bf_ref[...] = wgu_ref[...].astype(BF16)
            wd_bf_ref[...] = wd_ref[...].astype(BF16)

        n_chunks = xs_ref.shape[0]
        lo, hi = _unpack_bf16_pairs(
            jnp.concatenate([xs_ref[c] for c in range(n_chunks)], axis=1))
        xb = jnp.concatenate([lo, hi], axis=1).astype(BF16)
        gu = jnp.dot(xb, wgu_bf_ref[...], preferred_element_type=F32) + bgu_ref[...]
        gate = jnp.minimum(gu[:, :d_exp], SWIGLU_LIMIT)
        up = jnp.clip(gu[:, d_exp:], -SWIGLU_LIMIT, SWIGLU_LIMIT)
        glu = gate * jax.nn.sigmoid(gate * SWIGLU_ALPHA)
        act = ((up + 1.0) * glu).astype(BF16)
        y = jnp.dot(act, wd_bf_ref[...], preferred_element_type=F32) + bd_ref[...]
        y_words = _pack_bf16_pairs(y)
        for c in range(n_chunks):
            ys_ref[c] = y_words[:, c * SC_CHUNK:(c + 1) * SC_CHUNK]


def _experts(bexp, nvalid, xs, wgu, bgu, wd, bd):
    n_chunks, n_rows, _ = xs.shape
    _, d, d_exp2 = wgu.shape
    d_exp = d_exp2 // 2
    nb = n_rows // MOE_BLOCK
    blk = lambda b, be, nv: jnp.minimum(b, nv[0] - 1)
    return pl.pallas_call(
        functools.partial(_expert_kernel, d_exp=d_exp),
        grid_spec=pltpu.PrefetchScalarGridSpec(
            num_scalar_prefetch=2,
            grid=(nb,),
            in_specs=[
                pl.BlockSpec((n_chunks, MOE_BLOCK, SC_CHUNK),
                             lambda b, be, nv: (0, blk(b, be, nv), 0)),
                pl.BlockSpec((None, d, 2 * d_exp), lambda b, be, nv: (be[blk(b, be, nv)], 0, 0)),
                pl.BlockSpec((None, 1, 2 * d_exp), lambda b, be, nv: (be[blk(b, be, nv)], 0, 0)),
                pl.BlockSpec((None, d_exp, d), lambda b, be, nv: (be[blk(b, be, nv)], 0, 0)),
                pl.BlockSpec((None, 1, d), lambda b, be, nv: (be[blk(b, be, nv)], 0, 0)),
            ],
            out_specs=pl.BlockSpec((n_chunks, MOE_BLOCK, SC_CHUNK),
                                   lambda b, be, nv: (0, blk(b, be, nv), 0)),
            scratch_shapes=[pltpu.VMEM((d, 2 * d_exp), BF16), pltpu.VMEM((d_exp, d), BF16)],
        ),
        out_shape=jax.ShapeDtypeStruct((n_chunks, n_rows, SC_CHUNK), U32),
        name="experts",
        cost_estimate=pl.CostEstimate(
            flops=6 * n_rows * d * d_exp, transcendentals=n_rows * d_exp,
            bytes_accessed=4 * n_rows * d + 4 * N_EXPERTS * 3 * d * d_exp),
        compiler_params=pltpu.CompilerParams(
            dimension_semantics=("arbitrary",), vmem_limit_bytes=VMEM_LIMIT),
    )(bexp, nvalid, xs, wgu, bgu, wd, bd)


def _combine_kernel(prev_ref, x1_ref, gate_ref, yg_ref, out_ref):
    del prev_ref
    n_chunks = yg_ref.shape[0]
    half = n_chunks * SC_CHUNK
    for c in range(n_chunks):
        sl_lo = slice(c * SC_CHUNK, (c + 1) * SC_CHUNK)
        sl_hi = slice(half + c * SC_CHUNK, half + (c + 1) * SC_CHUNK)
        acc_lo = x1_ref[:, sl_lo]
        acc_hi = x1_ref[:, sl_hi]
        for k in range(TOP_K):
            lo, hi = _unpack_bf16_pairs(yg_ref[c, k])
            g = gate_ref[:, k:k + 1]
            acc_lo = acc_lo + g * lo
            acc_hi = acc_hi + g * hi
        out_ref[:, sl_lo] = acc_lo
        out_ref[:, sl_hi] = acc_hi


def _combine(x1, gates, yg, out_prev, tcb, row0, t_total):
    t, d = x1.shape
    n_chunks = yg.shape[0]
    blk0 = row0 // tcb
    in_specs = [
        pl.BlockSpec((tcb, d), lambda i: (i, 0)),
        pl.BlockSpec((tcb, LANES), lambda i: (i, 0)),
        pl.BlockSpec((n_chunks, TOP_K, tcb, SC_CHUNK), lambda i: (0, 0, i, 0)),
    ]
    args = [x1, gates, yg]
    aliases = {}
    body = functools.partial(_combine_kernel, None)
    if out_prev is not None:
        in_specs.append(pl.BlockSpec(memory_space=pl.ANY))
        args.append(out_prev)
        aliases = {3: 0}
        body = lambda a, b, c, prev, o: _combine_kernel(prev, a, b, c, o)
    return pl.pallas_call(
        body,
        grid=(t // tcb,),
        in_specs=in_specs,
        out_specs=pl.BlockSpec((tcb, d), lambda i: (blk0 + i, 0)),
        out_shape=jax.ShapeDtypeStruct((t_total, d), F32),
        input_output_aliases=aliases,
        name="combine",
        cost_estimate=pl.CostEstimate(
            flops=2 * TOP_K * t * d, transcendentals=0,
            bytes_accessed=t * (4 * d + 4 * d + 2 * TOP_K * d + 4 * LANES)),
        compiler_params=pltpu.CompilerParams(dimension_semantics=("arbitrary",)),
    )(*args)


def _pick_tile(n, pref):
    t = min(pref, n)
    while n % t:
        t //= 2
    return t


def kernel(x, mem, positions, attn_norm_g, mem_norm_g, w_in, b_gates, a_q_norm_g, a_k_norm_g,
           a_sinks, w_o_a, gmlp_ln_g, gmlp_ln_b, gmlp_w_s, gmlp_b_s, w_o_b, w_mem_kv,
           c_q_norm_g, c_k_norm_g, w_o_c, w_out, ffn_norm_g, router_w, router_b,
           w_gate_up, b_gate_up, w_down, b_down):
    n_batch, seq, d = x.shape
    m_len = mem.shape[1]
    depth = w_in.shape[0]
    t = n_batch * seq
    n_groups = N_TOKEN_GROUPS if n_batch % N_TOKEN_GROUPS == 0 else 1
    gb = n_batch // n_groups
    tg = gb * seq
    n_asg = tg * TOP_K
    nb = -(-n_asg // MOE_BLOCK) + N_EXPERTS
    n_rows = nb * MOE_BLOCK

    inv_freq = ROPE_THETA ** (-jnp.arange(0, HEAD_DIM, 2, dtype=F32) / HEAD_DIM)
    invf = jnp.tile(inv_freq, LANES // (HEAD_DIM // 2))[None, :]
    sgn = jnp.tile(jnp.concatenate([-jnp.ones((HEAD_DIM // 2,), F32),
                                    jnp.ones((HEAD_DIM // 2,), F32)]), LANES // HEAD_DIM)[None, :]
    pos2d = positions.reshape(t, 1).astype(I32)
    mem2d = mem.reshape(n_batch * m_len, d)
    x2d = x.reshape(t, d)

    tm = _pick_tile(tg, 512)
    ts = _pick_tile(seq, 512)
    tp = _pick_tile(tg, 8192)
    tcb = _pick_tile(tg, 512)

    for l in range(depth):
        kc, vc = _mem_kv(mem2d, mem_norm_g[l][None, :], w_mem_kv[l].astype(BF16),
                         c_k_norm_g[l][None, :], n_batch, m_len)
        w_in_bf = w_in[l].astype(BF16)
        mix_w = (w_o_a[l].astype(BF16), w_o_b[l].astype(BF16), w_o_c[l].astype(BF16),
                 w_out[l].astype(BF16))
        gq = jnp.tile(a_q_norm_g[l], LANES // HEAD_DIM)[None, :]
        gk = jnp.tile(a_k_norm_g[l], LANES // HEAD_DIM)[None, :]
        rw32 = jnp.pad(router_w[l], ((0, 0), (0, LANES - N_EXPERTS)))
        rw_hi = rw32.astype(BF16)
        rw = jnp.concatenate([rw_hi, (rw32 - rw_hi.astype(F32)).astype(BF16)], axis=1)
        rb = jnp.pad(router_b[l], (0, LANES - N_EXPERTS))[None, :]

        routed = []
        scatter_idx = None
        for g in range(n_groups):
            proj = _proj(
                x2d, pos2d, attn_norm_g[l][None, :], w_in_bf, invf, sgn, gq, gk,
                c_q_norm_g[l][None, :], gmlp_ln_g[l][None, :], gmlp_ln_b[l][None, :],
                b_gates[l].reshape(1, 3 * d), tm, g * tg, tg, scatter_idx)
            x1, h2, idx, rank, gates, counts = _mix(
                a_sinks[l], proj, x2d, kc, vc, gmlp_w_s[l], gmlp_b_s[l].T, *mix_w,
                ffn_norm_g[l][None, :], rw, rb, gb, seq, m_len, ts, g * gb)

            cnt = counts[0, :N_EXPERTS].astype(I32)
            padded = (cnt + MOE_BLOCK - 1) // MOE_BLOCK * MOE_BLOCK
            pend = jnp.cumsum(padded)
            pstart = (pend - padded).astype(I32)
            nvalid = (pend[-1:] // MOE_BLOCK).astype(I32)
            blk_row = jnp.arange(nb, dtype=I32) * MOE_BLOCK
            bexp = jnp.minimum(jnp.sum((pend[None, :] <= blk_row[:, None]).astype(I32), axis=1),
                               N_EXPERTS - 1).astype(I32)
            j = jnp.arange(MOE_BLOCK, dtype=I32)[None, :]
            fill = jnp.where(j < (padded - cnt)[:, None], pend[:, None] - 1 - j,
                             n_rows - 1 - j).astype(I32)

            pos = _pos(pstart, idx, rank, tp)
            pos_km = pos[:TOP_K].reshape(n_asg)
            n_chunks = h2.shape[0]
            chunk_off = (jnp.arange(n_chunks, dtype=I32) * n_rows)[:, None]
            scatter_idx = jnp.concatenate([pos_km, fill.reshape(-1)])[None, :] + chunk_off
            xs = _sc_scatter_rows(h2, scatter_idx, n_rows)
            routed.append((x1, gates, xs, bexp, nvalid, pos_km, chunk_off))

        gathered = []
        for x1, gates, xs, bexp, nvalid, pos_km, chunk_off in routed:
            ys = _experts(bexp, nvalid, xs, w_gate_up[l], b_gate_up[l][:, None, :],
                          w_down[l], b_down[l][:, None, :])
            yg = _sc_gather_rows(ys, pos_km[None, :] + chunk_off)
            gathered.append((x1, gates, yg.reshape(yg.shape[0], TOP_K, tg, SC_CHUNK)))

        out = None
        for g, (x1, gates, yg) in enumerate(gathered):
            out = _combine(x1, gates, yg, out, tcb, g * tg, t)
        x2d = out
    return x2d.reshape(n_batch, seq, d)
```

```python
import functools

import numpy as np
import jax
import jax.numpy as jnp
from jax import lax
from jax.experimental import pallas as pl
from jax.experimental.pallas import tpu as pltpu
from jax.experimental.pallas import tpu_sc as plsc

F32 = jnp.float32
BF16 = jnp.bfloat16
I32 = jnp.int32
U32 = jnp.uint32
HI16 = np.uint32(0xFFFF0000)

EPS = 1e-6
LANES = 128
HEAD_DIM = 64
N_Q_HEADS = 16
N_KV_HEADS = 2
ATT_BLOCK = 128
ROPE_THETA = 10000.0
GMLP_WIDTH = 512
GMLP_GROUPS = 4
GMLP_CHUNK = 128
X_HEADS = 4
X_HEAD_DIM = 128
N_EXPERTS = 32
TOP_K = 4
SWIGLU_LIMIT = 7.0
SWIGLU_ALPHA = 1.702
MOE_BLOCK = 512
SC_ROWS = 128
SC_CHUNK = 256
N_TOKEN_GROUPS = 2
IDX_ROWS = 8

A_Q = N_Q_HEADS * HEAD_DIM
A_KV = N_KV_HEADS * HEAD_DIM
C_Q = X_HEADS * X_HEAD_DIM

OFF_Q = 0
OFF_K4 = OFF_Q + A_Q
OFF_V4 = OFF_K4 + 4 * LANES
OFF_U = OFF_V4 + 4 * LANES
OFF_VN = OFF_U + GMLP_WIDTH
OFF_QC = OFF_VN + GMLP_WIDTH
OFF_GA = OFF_QC + C_Q
PROJ_W_BASE = OFF_GA

VMEM_LIMIT = 56 * 1024 * 1024


def _lane_iota(shape):
    return lax.broadcasted_iota(I32, shape, len(shape) - 1)


def _rms(x, g):
    return x * lax.rsqrt(jnp.mean(x * x, axis=-1, keepdims=True) + EPS) * g


def _pack_bf16_pairs(x):
    n = x.shape[1] // 2
    bits = pltpu.bitcast(x.astype(BF16).astype(F32), U32)
    return (bits[:, :n] >> 16) | (bits[:, n:] & HI16)


def _unpack_bf16_pairs(w):
    return pltpu.bitcast(w << 16, F32), pltpu.bitcast(w & HI16, F32)


def _gelu(x):
    return 0.5 * x * (1.0 + lax.erf(x * np.float32(np.sqrt(0.5))))


def _memkv_kernel(mem_ref, g_ref, w_ref, gk_ref, kc_ref, vc_ref):
    h = _rms(mem_ref[...], g_ref[...]).astype(BF16)
    kv = jnp.dot(h, w_ref[...], preferred_element_type=F32)
    for hh in range(X_HEADS):
        sl = slice(hh * X_HEAD_DIM, (hh + 1) * X_HEAD_DIM)
        kc_ref[:, sl] = _rms(kv[:, sl], gk_ref[...]).astype(BF16)
    vc_ref[...] = kv[:, C_Q:].astype(BF16)


def _mem_kv(mem2d, g, w_bf, gk, n_batch, m_len):
    d = mem2d.shape[1]
    return pl.pallas_call(
        _memkv_kernel,
        grid=(n_batch,),
        in_specs=[
            pl.BlockSpec((m_len, d), lambda b: (b, 0)),
            pl.BlockSpec((1, d), lambda b: (0, 0)),
            pl.BlockSpec((d, 2 * C_Q), lambda b: (0, 0)),
            pl.BlockSpec((1, X_HEAD_DIM), lambda b: (0, 0)),
        ],
        out_specs=[
            pl.BlockSpec((m_len, C_Q), lambda b: (b, 0)),
            pl.BlockSpec((m_len, C_Q), lambda b: (b, 0)),
        ],
        out_shape=[jax.ShapeDtypeStruct((n_batch * m_len, C_Q), BF16)] * 2,
        name="mem_kv",
        compiler_params=pltpu.CompilerParams(dimension_semantics=("arbitrary",)),
    )(mem2d, g, w_bf, gk)


def _proj_kernel(x_ref, pos_ref, g_ref, w_ref, invf_ref, sgn_ref, gq_ref, gk_ref, gcq_ref,
                 lng_ref, lnb_ref, bg_ref, out_ref, *, d_model):
    tm = x_ref.shape[0]
    h = _rms(x_ref[...], g_ref[...]).astype(BF16)

    ang_t = invf_ref[...] * pos_ref[...].astype(F32)
    reps = LANES // (HEAD_DIM // 2)
    cosv = jnp.transpose(jnp.concatenate([jnp.cos(ang_t)] * reps, axis=0))
    sinv = jnp.transpose(jnp.concatenate([jnp.sin(ang_t)] * reps, axis=0)) * sgn_ref[...]
    lane = _lane_iota((tm, LANES))
    first_head = lane < HEAD_DIM
    lo_half = (lane % HEAD_DIM) < (HEAD_DIM // 2)

    def head_norm_rope(blk, g):
        y = blk * blk
        s_lo = jnp.sum(jnp.where(first_head, y, 0.0), axis=-1, keepdims=True)
        s_hi = jnp.sum(jnp.where(first_head, 0.0, y), axis=-1, keepdims=True)
        ss = jnp.where(first_head, s_lo, s_hi)
        n = blk * lax.rsqrt(ss * (1.0 / HEAD_DIM) + EPS) * g
        rot = jnp.where(lo_half, pltpu.roll(n, LANES - HEAD_DIM // 2, 1),
                        pltpu.roll(n, HEAD_DIM // 2, 1))
        return n * cosv + rot * sinv

    def proj(a, b):
        return jnp.dot(h, w_ref[:, a:b], preferred_element_type=F32)

    pq = proj(0, A_Q)
    for c in range(A_Q // LANES):
        sl = slice(c * LANES, (c + 1) * LANES)
        out_ref[:, OFF_Q + c * LANES:OFF_Q + (c + 1) * LANES] = (
            head_norm_rope(pq[:, sl], gq_ref[...]) * (HEAD_DIM ** -0.5)).astype(BF16)

    pkv = proj(A_Q, A_Q + 2 * A_KV)
    kn = head_norm_rope(pkv[:, :LANES], gk_ref[...])
    vv = pkv[:, LANES:]
    for off, t in ((OFF_K4, kn), (OFF_V4, vv)):
        tr = pltpu.roll(t, HEAD_DIM, 1)
        parts = (jnp.where(first_head, t, 0.0), jnp.where(first_head, 0.0, tr),
                 jnp.where(first_head, tr, 0.0), jnp.where(first_head, 0.0, t))
        for j, p in enumerate(parts):
            out_ref[:, off + j * LANES:off + (j + 1) * LANES] = p.astype(BF16)

    o0 = A_Q + 2 * A_KV
    out_ref[:, OFF_U:OFF_U + GMLP_WIDTH] = _gelu(proj(o0, o0 + GMLP_WIDTH)).astype(BF16)
    gv = _gelu(proj(o0 + GMLP_WIDTH, o0 + 2 * GMLP_WIDTH))
    mu = jnp.mean(gv, axis=-1, keepdims=True)
    var = jnp.mean(jnp.square(gv - mu), axis=-1, keepdims=True)
    out_ref[:, OFF_VN:OFF_VN + GMLP_WIDTH] = (
        (gv - mu) * lax.rsqrt(var + EPS) * lng_ref[...] + lnb_ref[...]).astype(BF16)

    o1 = o0 + 2 * GMLP_WIDTH
    pc = proj(o1, o1 + C_Q)
    for hh in range(X_HEADS):
        sl = slice(hh * X_HEAD_DIM, (hh + 1) * X_HEAD_DIM)
        out_ref[:, OFF_QC + hh * X_HEAD_DIM:OFF_QC + (hh + 1) * X_HEAD_DIM] = (
            _rms(pc[:, sl], gcq_ref[...]) * (X_HEAD_DIM ** -0.5)).astype(BF16)

    o2 = o1 + C_Q
    for j in range(3):
        sl = slice(j * d_model, (j + 1) * d_model)
        z = proj(o2 + j * d_model, o2 + (j + 1) * d_model) + bg_ref[:, sl]
        out_ref[:, OFF_GA + j * d_model:OFF_GA + (j + 1) * d_model] = (
            0.5 * jnp.tanh(0.5 * z) + 0.5).astype(BF16)


def _proj_after_kernel(after_ref, *refs, d_model):
    del after_ref
    _proj_kernel(*refs, d_model=d_model)


def _proj(x2d, pos2d, g, w_bf, invf, sgn, gq, gk, gcq, lng, lnb, bg, tm, row0, t, after):
    d = x2d.shape[1]
    d_in = w_bf.shape[1]
    pw = PROJ_W_BASE + 3 * d
    blk0 = row0 // tm
    full = lambda shape: pl.BlockSpec(shape, lambda i: (0,) * len(shape))
    body = functools.partial(_proj_kernel, d_model=d)
    lead_specs, lead_args = [], []
    if after is not None:
        body = functools.partial(_proj_after_kernel, d_model=d)
        lead_specs, lead_args = [pl.BlockSpec(memory_space=pl.ANY)], [after]
    return pl.pallas_call(
        body,
        grid=(t // tm,),
        in_specs=lead_specs + [
            pl.BlockSpec((tm, d), lambda i: (blk0 + i, 0)),
            pl.BlockSpec((None, 1, tm), lambda i: (blk0 + i, 0, 0)),
            full((1, d)), full((d, d_in)), full((HEAD_DIM // 2, 1)), full((1, LANES)),
            full((1, LANES)), full((1, LANES)), full((1, X_HEAD_DIM)),
            full((1, GMLP_WIDTH)), full((1, GMLP_WIDTH)), full((1, 3 * d)),
        ],
        out_specs=pl.BlockSpec((tm, pw), lambda i: (i, 0)),
        out_shape=jax.ShapeDtypeStruct((t, pw), BF16),
        name="proj",
        cost_estimate=pl.CostEstimate(
            flops=2 * t * d * d_in, transcendentals=t * (3 * d + 2 * GMLP_WIDTH + 2 * LANES),
            bytes_accessed=4 * t * d + 2 * t * pw + 2 * d * d_in),
        compiler_params=pltpu.CompilerParams(
            dimension_semantics=("arbitrary",), vmem_limit_bytes=VMEM_LIMIT),
    )(*lead_args, x2d, pos2d.reshape(-1, 1, tm), g, w_bf, invf, sgn, gq, gk, gcq, lng, lnb, bg)


def _mix_kernel(sinks_ref, proj_ref, kprev_ref, vprev_ref, x_ref, kc_ref, vc_ref, ws_ref, bst_ref,
                woa_ref, wob_ref, woc_ref, wout_ref, gffn_ref, rw_ref, rb_ref,
                x1_ref, h2_ref, idx_ref, rank_ref, gate_ref, cnt_ref,
                run_ref, oa_ref, ob_ref, oc_ref, *, d_model):
    ts = x_ref.shape[0]
    s_idx = pl.program_id(1)
    first_step = jnp.logical_and(pl.program_id(0) == 0, s_idx == 0)
    neg_inf = float("-inf")

    cpk = A_Q // LANES // N_KV_HEADS
    stack = cpk * ATT_BLOCK
    srow = lax.broadcasted_iota(I32, (stack, 2 * ATT_BLOCK), 0)
    qi = srow % ATT_BLOCK
    kj = lax.broadcasted_iota(I32, (stack, 2 * ATT_BLOCK), 1)
    band = jnp.logical_and(kj <= ATT_BLOCK + qi, kj > qi)
    chunk_of_row = lax.broadcasted_iota(I32, (stack, 1), 0) // ATT_BLOCK
    for qb in range(ts // ATT_BLOCK):
        r0 = qb * ATT_BLOCK
        rows = slice(r0, r0 + ATT_BLOCK)
        if qb == 0:
            kp, vp = kprev_ref[...], vprev_ref[...]
            mask = jnp.logical_and(band, jnp.logical_or(kj >= ATT_BLOCK, s_idx > 0))
        else:
            prow = slice(r0 - ATT_BLOCK, r0)
            kp = proj_ref[prow, OFF_K4:OFF_K4 + 4 * LANES]
            vp = proj_ref[prow, OFF_V4:OFF_V4 + 4 * LANES]
            mask = band
        k4 = jnp.concatenate([kp, proj_ref[rows, OFF_K4:OFF_K4 + 4 * LANES]], axis=0)
        v4 = jnp.concatenate([vp, proj_ref[rows, OFF_V4:OFF_V4 + 4 * LANES]], axis=0)
        for kvh in range(N_KV_HEADS):
            c0 = kvh * cpk
            q4 = jnp.concatenate(
                [proj_ref[rows, OFF_Q + (c0 + c) * LANES:OFF_Q + (c0 + c + 1) * LANES]
                 for c in range(cpk)], axis=0)
            o = jnp.zeros((stack, LANES), F32)
            for half in range(2):
                col = slice((2 * kvh + half) * LANES, (2 * kvh + half + 1) * LANES)
                s = lax.dot_general(q4, k4[:, col], (((1,), (1,)), ((), ())),
                                    preferred_element_type=F32)
                s = jnp.where(mask, s, neg_inf)
                sink = jnp.zeros((stack, 1), F32)
                for c in range(cpk):
                    sink = jnp.where(chunk_of_row == c, sinks_ref[2 * (c0 + c) + half], sink)
                m = jnp.maximum(jnp.max(s, axis=-1, keepdims=True), sink)
                p = jnp.exp(s - m)
                den = jnp.sum(p, axis=-1, keepdims=True) + jnp.exp(sink - m)
                o = o + jnp.dot(p.astype(BF16), v4[:, col], preferred_element_type=F32) / den
            for c in range(cpk):
                oa_ref[rows, (c0 + c) * LANES:(c0 + c + 1) * LANES] = (
                    o[c * ATT_BLOCK:(c + 1) * ATT_BLOCK].astype(BF16))

    ti = lax.broadcasted_iota(I32, (GMLP_CHUNK, GMLP_CHUNK), 0)
    si = lax.broadcasted_iota(I32, (GMLP_CHUNK, GMLP_CHUNK), 1)
    for g in range(GMLP_GROUPS):
        wt = jnp.where(si <= ti, ws_ref[g], 0.0).astype(BF16)
        bcol = bst_ref[:, g:g + 1]
        for ch in range(ts // GMLP_CHUNK):
            rows = slice(ch * GMLP_CHUNK, (ch + 1) * GMLP_CHUNK)
            vn = proj_ref[rows, OFF_VN + g * LANES:OFF_VN + (g + 1) * LANES]
            u = proj_ref[rows, OFF_U + g * LANES:OFF_U + (g + 1) * LANES].astype(F32)
            mixed = jnp.dot(wt, vn, preferred_element_type=F32) + bcol
            ob_ref[rows, g * LANES:(g + 1) * LANES] = (u * mixed).astype(BF16)

    for hh in range(X_HEADS):
        sl = slice(hh * X_HEAD_DIM, (hh + 1) * X_HEAD_DIM)
        qc = proj_ref[:, OFF_QC + hh * X_HEAD_DIM:OFF_QC + (hh + 1) * X_HEAD_DIM]
        s = lax.dot_general(qc, kc_ref[:, sl], (((1,), (1,)), ((), ())),
                            preferred_element_type=F32)
        p = jnp.exp(s - jnp.max(s, axis=-1, keepdims=True))
        den = jnp.sum(p, axis=-1, keepdims=True)
        oc_ref[:, sl] = (jnp.dot(p.astype(BF16), vc_ref[:, sl],
                                 preferred_element_type=F32) / den).astype(BF16)

    def gate(j):
        return proj_ref[:, OFF_GA + j * d_model:OFF_GA + (j + 1) * d_model].astype(F32)

    merged = gate(0) * jnp.dot(oa_ref[...], woa_ref[...], preferred_element_type=F32)
    merged = merged + gate(1) * jnp.dot(ob_ref[...], wob_ref[...], preferred_element_type=F32)
    merged = merged + gate(2) * jnp.dot(oc_ref[...], woc_ref[...], preferred_element_type=F32)
    x1 = x_ref[...] + jnp.dot(merged.astype(BF16), wout_ref[...], preferred_element_type=F32)
    x1_ref[...] = x1

    h2 = _rms(x1, gffn_ref[...])
    h2_words = _pack_bf16_pairs(h2)
    for c in range(h2_ref.shape[0]):
        h2_ref[c] = h2_words[:, c * SC_CHUNK:(c + 1) * SC_CHUNK]
    h2_hi = h2.astype(BF16)
    h2_lo = (h2 - h2_hi.astype(F32)).astype(BF16)
    part = jnp.dot(h2_hi, rw_ref[...], preferred_element_type=F32)
    logits = (part[:, :LANES] + part[:, LANES:]
              + jnp.dot(h2_lo, rw_ref[:, :LANES], preferred_element_type=F32) + rb_ref[...])
    lt = jnp.transpose(logits)[:N_EXPERTS]
    erow = lax.broadcasted_iota(I32, (N_EXPERTS, ts), 0)
    vals, idxs = [], []
    for _ in range(TOP_K):
        m = jnp.max(lt, axis=0, keepdims=True)
        i = jnp.min(jnp.where(lt == m, erow, N_EXPERTS), axis=0, keepdims=True)
        vals.append(m)
        idxs.append(i)
        lt = jnp.where(erow == i, neg_inf, lt)
    es = [jnp.exp(v - vals[0]) for v in vals]
    den = es[0] + es[1] + es[2] + es[3]

    @pl.when(first_step)
    def _():
        run_ref[...] = jnp.zeros_like(run_ref)

    hot = [erow == i for i in idxs]
    multihot = jnp.where(jnp.logical_or(jnp.logical_or(hot[0], hot[1]),
                                        jnp.logical_or(hot[2], hot[3])), 1.0, 0.0)
    tr = lax.broadcasted_iota(I32, (ts, ts), 0)
    tc = lax.broadcasted_iota(I32, (ts, ts), 1)
    earlier = jnp.where(tr < tc, 1.0, 0.0).astype(BF16)
    before = jnp.dot(multihot.astype(BF16), earlier, preferred_element_type=F32) + run_ref[...]
    krow = lax.broadcasted_iota(I32, (IDX_ROWS, ts), 0)
    idx_out = jnp.zeros((IDX_ROWS, ts), I32)
    rank_out = jnp.zeros((IDX_ROWS, ts), I32)
    gate_rows = jnp.zeros((IDX_ROWS, ts), F32)
    for k in range(TOP_K):
        rk = jnp.sum(jnp.where(hot[k], before, 0.0), axis=0, keepdims=True)
        idx_out = jnp.where(krow == k, idxs[k], idx_out)
        rank_out = jnp.where(krow == k, rk.astype(I32), rank_out)
        gate_rows = jnp.where(krow == k, es[k] / den, gate_rows)
    idx_ref[...] = idx_out
    rank_ref[...] = rank_out
    gate_ref[...] = jnp.transpose(jnp.concatenate(
        [gate_rows, jnp.zeros((LANES - IDX_ROWS, ts), F32)], axis=0))
    run_ref[...] = run_ref[...] + jnp.sum(multihot, axis=1, keepdims=True)
    cnt_ref[...] = run_ref[...]


def _mix(sinks, proj, x2d, kc, vc, w_s, bst, woa, wob, woc, wout, gffn, rw, rb,
         n_batch, seq, m_len, ts, batch0):
    d = x2d.shape[1]
    t = n_batch * seq
    pw = proj.shape[1]
    ns = seq // ts
    nblk = seq // ATT_BLOCK
    per = ts // ATT_BLOCK
    n_chunks = d // (2 * SC_CHUNK)
    full = lambda shape: pl.BlockSpec(shape, lambda b, s: (0,) * len(shape))
    row = lambda width: pl.BlockSpec((ts, width), lambda b, s: (b * ns + s, 0))
    prev = lambda colblk: pl.BlockSpec(
        (ATT_BLOCK, 4 * LANES), lambda b, s: (b * nblk + jnp.maximum(s * per - 1, 0), colblk))
    return pl.pallas_call(
        functools.partial(_mix_kernel, d_model=d),
        grid=(n_batch, ns),
        in_specs=[
            pl.BlockSpec(memory_space=pltpu.SMEM),
            row(pw), prev(OFF_K4 // (4 * LANES)), prev(OFF_V4 // (4 * LANES)),
            pl.BlockSpec((ts, d), lambda b, s: ((batch0 + b) * ns + s, 0)),
            pl.BlockSpec((m_len, C_Q), lambda b, s: (batch0 + b, 0)),
            pl.BlockSpec((m_len, C_Q), lambda b, s: (batch0 + b, 0)),
            full((GMLP_GROUPS, GMLP_CHUNK, GMLP_CHUNK)), full((GMLP_CHUNK, GMLP_GROUPS)),
            full((A_Q, d)), full((GMLP_WIDTH, d)), full((C_Q, d)), full((d, d)),
            full((1, d)), full((d, 2 * LANES)), full((1, LANES)),
        ],
        out_specs=[row(d),
                   pl.BlockSpec((n_chunks, ts, SC_CHUNK), lambda b, s: (0, b * ns + s, 0)),
                   pl.BlockSpec((IDX_ROWS, ts), lambda b, s: (0, b * ns + s)),
                   pl.BlockSpec((IDX_ROWS, ts), lambda b, s: (0, b * ns + s)),
                   row(LANES), full((N_EXPERTS, 1))],
        out_shape=[
            jax.ShapeDtypeStruct((t, d), F32),
            jax.ShapeDtypeStruct((n_chunks, t, SC_CHUNK), U32),
            jax.ShapeDtypeStruct((IDX_ROWS, t), I32), jax.ShapeDtypeStruct((IDX_ROWS, t), I32),
            jax.ShapeDtypeStruct((t, LANES), F32), jax.ShapeDtypeStruct((N_EXPERTS, 1), F32),
        ],
        scratch_shapes=[
            pltpu.VMEM((N_EXPERTS, 1), F32),
            pltpu.VMEM((ts, A_Q), BF16), pltpu.VMEM((ts, GMLP_WIDTH), BF16),
            pltpu.VMEM((ts, C_Q), BF16),
        ],
        name="mix",
        cost_estimate=pl.CostEstimate(
            flops=2 * t * (d * (A_Q + GMLP_WIDTH + C_Q + d) + 4 * ATT_BLOCK * A_Q
                           + GMLP_CHUNK * GMLP_WIDTH + 2 * m_len * C_Q + 3 * d * LANES),
            transcendentals=t * (2 * ATT_BLOCK * N_Q_HEADS + m_len * X_HEADS),
            bytes_accessed=t * (2 * pw + 4 * d + 4 * d + 2 * d + 12 * LANES)),
        compiler_params=pltpu.CompilerParams(
            dimension_semantics=("arbitrary", "arbitrary"), vmem_limit_bytes=VMEM_LIMIT),
    )(sinks, proj, proj, proj, x2d, kc, vc, w_s, bst, woa, wob, woc, wout, gffn, rw, rb)


def _pos_kernel(pstart_ref, idx_ref, rank_ref, pos_ref):
    idx = idx_ref[...]
    pos = rank_ref[...]
    for e in range(N_EXPERTS):
        pos = pos + jnp.where(idx == e, pstart_ref[e], 0)
    pos_ref[...] = pos


def _pos(pstart, idx, rank, tp):
    t = idx.shape[1]
    return pl.pallas_call(
        _pos_kernel,
        grid_spec=pltpu.PrefetchScalarGridSpec(
            num_scalar_prefetch=1,
            grid=(t // tp,),
            in_specs=[pl.BlockSpec((IDX_ROWS, tp), lambda i, ps: (0, i)),
                      pl.BlockSpec((IDX_ROWS, tp), lambda i, ps: (0, i))],
            out_specs=pl.BlockSpec((IDX_ROWS, tp), lambda i, ps: (0, i)),
        ),
        out_shape=jax.ShapeDtypeStruct((IDX_ROWS, t), I32),
        name="slot_pos",
        compiler_params=pltpu.CompilerParams(dimension_semantics=("arbitrary",)),
    )(pstart, idx, rank)


def _sc_mesh():
    return plsc.VectorSubcoreMesh(core_axis_name="core", subcore_axis_name="subcore")


def _sc_scatter_rows(src, idx, n_rows):
    n_chunks, t, w = src.shape
    n_idx = idx.shape[1]
    src_blocks = t // SC_ROWS
    idx_blocks = n_idx // SC_ROWS

    @pl.kernel(out_type=jax.ShapeDtypeStruct((n_chunks * n_rows, w), src.dtype), mesh=_sc_mesh(),
               scratch_types=[], name="sc_dispatch",
               cost_estimate=pl.CostEstimate(
                   flops=0, transcendentals=0,
                   bytes_accessed=n_chunks * n_idx * (8 * w + 4)))
    def scatter(src_hbm, idx_hbm, out_hbm):
        def body(src_vmem, idx_vmem):
            pltpu.sync_copy(src_vmem, out_hbm.at[idx_vmem.at[0]])

        pltpu.emit_pipeline(
            body,
            grid=(n_chunks, idx_blocks),
            in_specs=[pl.BlockSpec((SC_ROWS, w), lambda c, i: (c * src_blocks + i % src_blocks, 0)),
                      pl.BlockSpec((1, SC_ROWS), lambda c, i: (0, c * idx_blocks + i))],
            out_specs=[],
            core_axis_name=("core", "subcore"),
            dimension_semantics=(pltpu.PARALLEL, pltpu.PARALLEL),
        )(src_hbm, idx_hbm)

    return scatter(src.reshape(n_chunks * t, w), idx.reshape(1, n_chunks * n_idx)).reshape(
        n_chunks, n_rows, w)


def _sc_gather_rows(table, idx):
    n_chunks, p, w = table.shape
    n_idx = idx.shape[1]
    idx_blocks = n_idx // SC_ROWS

    @pl.kernel(out_type=jax.ShapeDtypeStruct((n_chunks * n_idx, w), table.dtype), mesh=_sc_mesh(),
               scratch_types=[], name="sc_gather",
               cost_estimate=pl.CostEstimate(
                   flops=0, transcendentals=0,
                   bytes_accessed=n_chunks * n_idx * (8 * w + 4)))
    def gather(table_hbm, idx_hbm, out_hbm):
        def body(idx_vmem, out_vmem):
            pltpu.sync_copy(table_hbm.at[idx_vmem.at[0]], out_vmem)

        pltpu.emit_pipeline(
            body,
            grid=(n_chunks, idx_blocks),
            in_specs=[pl.BlockSpec((1, SC_ROWS), lambda c, i: (0, c * idx_blocks + i))],
            out_specs=[pl.BlockSpec((SC_ROWS, w), lambda c, i: (c * idx_blocks + i, 0))],
            core_axis_name=("core", "subcore"),
            dimension_semantics=(pltpu.PARALLEL, pltpu.PARALLEL),
        )(idx_hbm, out_hbm)

    return gather(table.reshape(n_chunks * p, w), idx.reshape(1, n_chunks * n_idx)).reshape(
        n_chunks, n_idx, w)


def _expert_kernel(bexp_ref, nvalid_ref, xs_ref, wgu_ref, bgu_ref, wd_ref, bd_ref, ys_ref,
                   wgu_bf_ref, wd_bf_ref, *, d_exp):
    b = pl.program_id(0)

    @pl.when(b < nvalid_ref[0])
    def _():
        @pl.when(jnp.logical_or(b == 0, bexp_ref[b] != bexp_ref[jnp.maximum(b - 1, 0)]))
        def _():
            wgu_bf_ref[...] = wgu_ref[...].astype(BF16)
            wd_bf_ref[...] = wd_ref[...].astype(BF16)

        n_chunks = xs_ref.shape[0]
        lo, hi = _unpack_bf16_pairs(
            jnp.concatenate([xs_ref[c] for c in range(n_chunks)], axis=1))
        xb = jnp.concatenate([lo, hi], axis=1).astype(BF16)
        gu = jnp.dot(xb, wgu_bf_ref[...], preferred_element_type=F32) + bgu_ref[...]
        gate = jnp.minimum(gu[:, :d_exp], SWIGLU_LIMIT)
        up = jnp.clip(gu[:, d_exp:], -SWIGLU_LIMIT, SWIGLU_LIMIT)
        glu = gate * jax.nn.sigmoid(gate * SWIGLU_ALPHA)
        act = ((up + 1.0) * glu).astype(BF16)
        y = jnp.dot(act, wd_bf_ref[...], preferred_element_type=F32) + bd_ref[...]
        y_words = _pack_bf16_pairs(y)
        for c in range(n_chunks):
            ys_ref[c] = y_words[:, c * SC_CHUNK:(c + 1) * SC_CHUNK]


def _experts(bexp, nvalid, xs, wgu, bgu, wd, bd):
    n_chunks, n_rows, _ = xs.shape
    _, d, d_exp2 = wgu.shape
    d_exp = d_exp2 // 2
    nb = n_rows // MOE_BLOCK
    blk = lambda b, be, nv: jnp.minimum(b, nv[0] - 1)
    return pl.pallas_call(
        functools.partial(_expert_kernel, d_exp=d_exp),
        grid_spec=pltpu.PrefetchScalarGridSpec(
            num_scalar_prefetch=2,
            grid=(nb,),
            in_specs=[
                pl.BlockSpec((n_chunks, MOE_BLOCK, SC_CHUNK),
                             lambda b, be, nv: (0, blk(b, be, nv), 0)),
                pl.BlockSpec((None, d, 2 * d_exp), lambda b, be, nv: (be[blk(b, be, nv)], 0, 0)),
                pl.BlockSpec((None, 1, 2 * d_exp), lambda b, be, nv: (be[blk(b, be, nv)], 0, 0)),
                pl.BlockSpec((None, d_exp, d), lambda b, be, nv: (be[blk(b, be, nv)], 0, 0)),
                pl.BlockSpec((None, 1, d), lambda b, be, nv: (be[blk(b, be, nv)], 0, 0)),
            ],
            out_specs=pl.BlockSpec((n_chunks, MOE_BLOCK, SC_CHUNK),
                                   lambda b, be, nv: (0, blk(b, be, nv), 0)),
            scratch_shapes=[pltpu.VMEM((d, 2 * d_exp), BF16), pltpu.VMEM((d_exp, d), BF16)],
        ),
        out_shape=jax.ShapeDtypeStruct((n_chunks, n_rows, SC_CHUNK), U32),
        name="experts",
        cost_estimate=pl.CostEstimate(
            flops=6 * n_rows * d * d_exp, transcendentals=n_rows * d_exp,
            bytes_accessed=4 * n_rows * d + 4 * N_EXPERTS * 3 * d * d_exp),
        compiler_params=pltpu.CompilerParams(
            dimension_semantics=("arbitrary",), vmem_limit_bytes=VMEM_LIMIT),
    )(bexp, nvalid, xs, wgu, bgu, wd, bd)


def _combine_kernel(prev_ref, x1_ref, gate_ref, yg_ref, out_ref):
    del prev_ref
    n_chunks = yg_ref.shape[0]
    half = n_chunks * SC_CHUNK
    for c in range(n_chunks):
        sl_lo = slice(c * SC_CHUNK, (c + 1) * SC_CHUNK)
        sl_hi = slice(half + c * SC_CHUNK, half + (c + 1) * SC_CHUNK)
        acc_lo = x1_ref[:, sl_lo]
        acc_hi = x1_ref[:, sl_hi]
        for k in range(TOP_K):
            lo, hi = _unpack_bf16_pairs(yg_ref[c, k])
            g = gate_ref[:, k:k + 1]
            acc_lo = acc_lo + g * lo
            acc_hi = acc_hi + g * hi
        out_ref[:, sl_lo] = acc_lo
        out_ref[:, sl_hi] = acc_hi


def _combine(x1, gates, yg, out_prev, tcb, row0, t_total):
    t, d = x1.shape
    n_chunks = yg.shape[0]
    blk0 = row0 // tcb
    in_specs = [
        pl.BlockSpec((tcb, d), lambda i: (i, 0)),
        pl.BlockSpec((tcb, LANES), lambda i: (i, 0)),
        pl.BlockSpec((n_chunks, TOP_K, tcb, SC_CHUNK), lambda i: (0, 0, i, 0)),
    ]
    args = [x1, gates, yg]
    aliases = {}
    body = functools.partial(_combine_kernel, None)
    if out_prev is not None:
        in_specs.append(pl.BlockSpec(memory_space=pl.ANY))
        args.append(out_prev)
        aliases = {3: 0}
        body = lambda a, b, c, prev, o: _combine_kernel(prev, a, b, c, o)
    return pl.pallas_call(
        body,
        grid=(t // tcb,),
        in_specs=in_specs,
        out_specs=pl.BlockSpec((tcb, d), lambda i: (blk0 + i, 0)),
        out_shape=jax.ShapeDtypeStruct((t_total, d), F32),
        input_output_aliases=aliases,
        name="combine",
        cost_estimate=pl.CostEstimate(
            flops=2 * TOP_K * t * d, transcendentals=0,
            bytes_accessed=t * (4 * d + 4 * d + 2 * TOP_K * d + 4 * LANES)),
        compiler_params=pltpu.CompilerParams(dimension_semantics=("arbitrary",)),
    )(*args)


def _pick_tile(n, pref):
    t = min(pref, n)
    while n % t:
        t //= 2
    return t


def kernel(x, mem, positions, attn_norm_g, mem_norm_g, w_in, b_gates, a_q_norm_g, a_k_norm_g,
           a_sinks, w_o_a, gmlp_ln_g, gmlp_ln_b, gmlp_w_s, gmlp_b_s, w_o_b, w_mem_kv,
           c_q_norm_g, c_k_norm_g, w_o_c, w_out, ffn_norm_g, router_w, router_b,
           w_gate_up, b_gate_up, w_down, b_down):
    n_batch, seq, d = x.shape
    m_len = mem.shape[1]
    depth = w_in.shape[0]
    t = n_batch * seq
    n_groups = N_TOKEN_GROUPS if n_batch % N_TOKEN_GROUPS == 0 else 1
    gb = n_batch // n_groups
    tg = gb * seq
    n_asg = tg * TOP_K
    nb = -(-n_asg // MOE_BLOCK) + N_EXPERTS
    n_rows = nb * MOE_BLOCK

    inv_freq = ROPE_THETA ** (-jnp.arange(0, HEAD_DIM, 2, dtype=F32) / HEAD_DIM)
    invf = inv_freq[:, None]
    sgn = jnp.tile(jnp.concatenate([-jnp.ones((HEAD_DIM // 2,), F32),
                                    jnp.ones((HEAD_DIM // 2,), F32)]), LANES // HEAD_DIM)[None, :]
    pos2d = positions.reshape(t, 1).astype(I32)
    mem2d = mem.reshape(n_batch * m_len, d)
    x2d = x.reshape(t, d)

    tm = _pick_tile(tg, 512)
    ts = _pick_tile(seq, 512)
    tp = _pick_tile(tg, 8192)
    tcb = _pick_tile(tg, 512)

    for l in range(depth):
        kc, vc = _mem_kv(mem2d, mem_norm_g[l][None, :], w_mem_kv[l].astype(BF16),
                         c_k_norm_g[l][None, :], n_batch, m_len)
        w_in_bf = w_in[l].astype(BF16)
        mix_w = (w_o_a[l].astype(BF16), w_o_b[l].astype(BF16), w_o_c[l].astype(BF16),
                 w_out[l].astype(BF16))
        gq = jnp.tile(a_q_norm_g[l], LANES // HEAD_DIM)[None, :]
        gk = jnp.tile(a_k_norm_g[l], LANES // HEAD_DIM)[None, :]
        rw32 = jnp.pad(router_w[l], ((0, 0), (0, LANES - N_EXPERTS)))
        rw_hi = rw32.astype(BF16)
        rw = jnp.concatenate([rw_hi, (rw32 - rw_hi.astype(F32)).astype(BF16)], axis=1)
        rb = jnp.pad(router_b[l], (0, LANES - N_EXPERTS))[None, :]

        routed = []
        scatter_idx = None
        for g in range(n_groups):
            proj = _proj(
                x2d, pos2d, attn_norm_g[l][None, :], w_in_bf, invf, sgn, gq, gk,
                c_q_norm_g[l][None, :], gmlp_ln_g[l][None, :], gmlp_ln_b[l][None, :],
                b_gates[l].reshape(1, 3 * d), tm, g * tg, tg, scatter_idx)
            x1, h2, idx, rank, gates, counts = _mix(
                a_sinks[l], proj, x2d, kc, vc, gmlp_w_s[l], gmlp_b_s[l].T, *mix_w,
                ffn_norm_g[l][None, :], rw, rb, gb, seq, m_len, ts, g * gb)

            cnt = counts[:, 0].astype(I32)
            padded = (cnt + MOE_BLOCK - 1) // MOE_BLOCK * MOE_BLOCK
            pend = jnp.cumsum(padded)
            pstart = (pend - padded).astype(I32)
            nvalid = (pend[-1:] // MOE_BLOCK).astype(I32)
            blk_row = jnp.arange(nb, dtype=I32) * MOE_BLOCK
            bexp = jnp.minimum(jnp.sum((pend[None, :] <= blk_row[:, None]).astype(I32), axis=1),
                               N_EXPERTS - 1).astype(I32)
            j = jnp.arange(MOE_BLOCK, dtype=I32)[None, :]
            fill = jnp.where(j < (padded - cnt)[:, None], pend[:, None] - 1 - j,
                             n_rows - 1 - j).astype(I32)

            pos = _pos(pstart, idx, rank, tp)
            pos_km = pos[:TOP_K].reshape(n_asg)
            n_chunks = h2.shape[0]
            chunk_off = (jnp.arange(n_chunks, dtype=I32) * n_rows)[:, None]
            scatter_idx = jnp.concatenate([pos_km, fill.reshape(-1)])[None, :] + chunk_off
            xs = _sc_scatter_rows(h2, scatter_idx, n_rows)
            routed.append((x1, gates, xs, bexp, nvalid, pos_km, chunk_off))

        gathered = []
        for x1, gates, xs, bexp, nvalid, pos_km, chunk_off in routed:
            ys = _experts(bexp, nvalid, xs, w_gate_up[l], b_gate_up[l][:, None, :],
                          w_down[l], b_down[l][:, None, :])
            yg = _sc_gather_rows(ys, pos_km[None, :] + chunk_off)
            gathered.append((x1, gates, yg.reshape(yg.shape[0], TOP_K, tg, SC_CHUNK)))

        out = None
        for g, (x1, gates, yg) in enumerate(gathered):
            out = _combine(x1, gates, yg, out, tcb, g * tg, t)
        x2d = out
    return x2d.reshape(n_batch, seq, d)
```

```python
import functools

import numpy as np
import jax
import jax.numpy as jnp
from jax import lax
from jax.experimental import pallas as pl
from jax.experimental.pallas import tpu as pltpu
from jax.experimental.pallas import tpu_sc as plsc

F32 = jnp.float32
BF16 = jnp.bfloat16
I32 = jnp.int32
U32 = jnp.uint32
HI16 = np.uint32(0xFFFF0000)

EPS = 1e-6
LANES = 128
HEAD_DIM = 64
N_Q_HEADS = 16
N_KV_HEADS = 2
ATT_BLOCK = 128
ROPE_THETA = 10000.0
GMLP_WIDTH = 512
GMLP_GROUPS = 4
GMLP_CHUNK = 128
X_HEADS = 4
X_HEAD_DIM = 128
N_EXPERTS = 32
TOP_K = 4
SWIGLU_LIMIT = 7.0
SWIGLU_ALPHA = 1.702
MOE_BLOCK = 512
SC_ROWS = 128
SC_CHUNK = 256
N_TOKEN_GROUPS = 2
IDX_ROWS = 8

A_Q = N_Q_HEADS * HEAD_DIM
A_KV = N_KV_HEADS * HEAD_DIM
C_Q = X_HEADS * X_HEAD_DIM

OFF_Q = 0
OFF_K4 = OFF_Q + A_Q
OFF_V4 = OFF_K4 + 4 * LANES
OFF_U = OFF_V4 + 4 * LANES
OFF_VN = OFF_U + GMLP_WIDTH
OFF_QC = OFF_VN + GMLP_WIDTH
OFF_GA = OFF_QC + C_Q
PROJ_W_BASE = OFF_GA

VMEM_LIMIT = 56 * 1024 * 1024


def _lane_iota(shape):
    return lax.broadcasted_iota(I32, shape, len(shape) - 1)


def _rms(x, g):
    return x * lax.rsqrt(jnp.mean(x * x, axis=-1, keepdims=True) + EPS) * g


def _pack_bf16_pairs(x):
    n = x.shape[1] // 2
    bits = pltpu.bitcast(x.astype(BF16).astype(F32), U32)
    return (bits[:, :n] >> 16) | (bits[:, n:] & HI16)


def _unpack_bf16_pairs(w):
    return pltpu.bitcast(w << 16, F32), pltpu.bitcast(w & HI16, F32)


def _gelu(x):
    return 0.5 * x * (1.0 + lax.erf(x * np.float32(np.sqrt(0.5))))


def _memkv_kernel(mem_ref, g_ref, w_ref, gk_ref, kc_ref, vc_ref):
    h = _rms(mem_ref[...], g_ref[...]).astype(BF16)
    kv = jnp.dot(h, w_ref[...], preferred_element_type=F32)
    for hh in range(X_HEADS):
        sl = slice(hh * X_HEAD_DIM, (hh + 1) * X_HEAD_DIM)
        kc_ref[:, sl] = _rms(kv[:, sl], gk_ref[...]).astype(BF16)
    vc_ref[...] = kv[:, C_Q:].astype(BF16)


def _mem_kv(mem2d, g, w_bf, gk, n_batch, m_len):
    d = mem2d.shape[1]
    return pl.pallas_call(
        _memkv_kernel,
        grid=(n_batch,),
        in_specs=[
            pl.BlockSpec((m_len, d), lambda b: (b, 0)),
            pl.BlockSpec((1, d), lambda b: (0, 0)),
            pl.BlockSpec((d, 2 * C_Q), lambda b: (0, 0)),
            pl.BlockSpec((1, X_HEAD_DIM), lambda b: (0, 0)),
        ],
        out_specs=[
            pl.BlockSpec((m_len, C_Q), lambda b: (b, 0)),
            pl.BlockSpec((m_len, C_Q), lambda b: (b, 0)),
        ],
        out_shape=[jax.ShapeDtypeStruct((n_batch * m_len, C_Q), BF16)] * 2,
        name="mem_kv",
        compiler_params=pltpu.CompilerParams(dimension_semantics=("arbitrary",)),
    )(mem2d, g, w_bf, gk)


def _proj_kernel(x_ref, pos_ref, g_ref, w_ref, invf_ref, sgn_ref, gq_ref, gk_ref, gcq_ref,
                 lng_ref, lnb_ref, bg_ref, out_ref, *, d_model):
    tm = x_ref.shape[0]
    h = _rms(x_ref[...], g_ref[...]).astype(BF16)

    ang_t = invf_ref[...] * pos_ref[...].astype(F32)
    reps = LANES // (HEAD_DIM // 2)
    cosv = jnp.transpose(jnp.concatenate([jnp.cos(ang_t)] * reps, axis=0))
    sinv = jnp.transpose(jnp.concatenate([jnp.sin(ang_t)] * reps, axis=0)) * sgn_ref[...]
    lane = _lane_iota((tm, LANES))
    first_head = lane < HEAD_DIM
    lo_half = (lane % HEAD_DIM) < (HEAD_DIM // 2)

    def head_norm_rope(blk, g):
        y = blk * blk
        s_lo = jnp.sum(jnp.where(first_head, y, 0.0), axis=-1, keepdims=True)
        s_hi = jnp.sum(jnp.where(first_head, 0.0, y), axis=-1, keepdims=True)
        ss = jnp.where(first_head, s_lo, s_hi)
        n = blk * lax.rsqrt(ss * (1.0 / HEAD_DIM) + EPS) * g
        rot = jnp.where(lo_half, pltpu.roll(n, LANES - HEAD_DIM // 2, 1),
                        pltpu.roll(n, HEAD_DIM // 2, 1))
        return n * cosv + rot * sinv

    def proj(a, b):
        return jnp.dot(h, w_ref[:, a:b], preferred_element_type=F32)

    pq = proj(0, A_Q)
    for c in range(A_Q // LANES):
        sl = slice(c * LANES, (c + 1) * LANES)
        out_ref[:, OFF_Q + c * LANES:OFF_Q + (c + 1) * LANES] = (
            head_norm_rope(pq[:, sl], gq_ref[...]) * (HEAD_DIM ** -0.5)).astype(BF16)

    pkv = proj(A_Q, A_Q + 2 * A_KV)
    kn = head_norm_rope(pkv[:, :LANES], gk_ref[...])
    vv = pkv[:, LANES:]
    for off, t in ((OFF_K4, kn), (OFF_V4, vv)):
        tr = pltpu.roll(t, HEAD_DIM, 1)
        parts = (jnp.where(first_head, t, 0.0), jnp.where(first_head, 0.0, tr),
                 jnp.where(first_head, tr, 0.0), jnp.where(first_head, 0.0, t))
        for j, p in enumerate(parts):
            out_ref[:, off + j * LANES:off + (j + 1) * LANES] = p.astype(BF16)

    o0 = A_Q + 2 * A_KV
    out_ref[:, OFF_U:OFF_U + GMLP_WIDTH] = _gelu(proj(o0, o0 + GMLP_WIDTH)).astype(BF16)
    gv = _gelu(proj(o0 + GMLP_WIDTH, o0 + 2 * GMLP_WIDTH))
    mu = jnp.mean(gv, axis=-1, keepdims=True)
    var = jnp.mean(jnp.square(gv - mu), axis=-1, keepdims=True)
    out_ref[:, OFF_VN:OFF_VN + GMLP_WIDTH] = (
        (gv - mu) * lax.rsqrt(var + EPS) * lng_ref[...] + lnb_ref[...]).astype(BF16)

    o1 = o0 + 2 * GMLP_WIDTH
    pc = proj(o1, o1 + C_Q)
    for hh in range(X_HEADS):
        sl = slice(hh * X_HEAD_DIM, (hh + 1) * X_HEAD_DIM)
        out_ref[:, OFF_QC + hh * X_HEAD_DIM:OFF_QC + (hh + 1) * X_HEAD_DIM] = (
            _rms(pc[:, sl], gcq_ref[...]) * (X_HEAD_DIM ** -0.5)).astype(BF16)

    o2 = o1 + C_Q
    for j in range(3):
        sl = slice(j * d_model, (j + 1) * d_model)
        z = proj(o2 + j * d_model, o2 + (j + 1) * d_model) + bg_ref[:, sl]
        out_ref[:, OFF_GA + j * d_model:OFF_GA + (j + 1) * d_model] = (
            0.5 * jnp.tanh(0.5 * z) + 0.5).astype(BF16)


def _proj_after_kernel(after_ref, *refs, d_model):
    del after_ref
    _proj_kernel(*refs, d_model=d_model)


def _proj(x2d, pos2d, g, w_bf, invf, sgn, gq, gk, gcq, lng, lnb, bg, tm, row0, t, after):
    d = x2d.shape[1]
    d_in = w_bf.shape[1]
    pw = PROJ_W_BASE + 3 * d
    blk0 = row0 // tm
    full = lambda shape: pl.BlockSpec(shape, lambda i: (0,) * len(shape))
    body = functools.partial(_proj_kernel, d_model=d)
    lead_specs, lead_args = [], []
    if after is not None:
        body = functools.partial(_proj_after_kernel, d_model=d)
        lead_specs, lead_args = [pl.BlockSpec(memory_space=pl.ANY)], [after]
    return pl.pallas_call(
        body,
        grid=(t // tm,),
        in_specs=lead_specs + [
            pl.BlockSpec((tm, d), lambda i: (blk0 + i, 0)),
            pl.BlockSpec((None, 1, tm), lambda i: (blk0 + i, 0, 0)),
            full((1, d)), full((d, d_in)), full((HEAD_DIM // 2, 1)), full((1, LANES)),
            full((1, LANES)), full((1, LANES)), full((1, X_HEAD_DIM)),
            full((1, GMLP_WIDTH)), full((1, GMLP_WIDTH)), full((1, 3 * d)),
        ],
        out_specs=pl.BlockSpec((tm, pw), lambda i: (i, 0)),
        out_shape=jax.ShapeDtypeStruct((t, pw), BF16),
        name="proj",
        cost_estimate=pl.CostEstimate(
            flops=2 * t * d * d_in, transcendentals=t * (3 * d + 2 * GMLP_WIDTH + 2 * LANES),
            bytes_accessed=4 * t * d + 2 * t * pw + 2 * d * d_in),
        compiler_params=pltpu.CompilerParams(
            dimension_semantics=("arbitrary",), vmem_limit_bytes=VMEM_LIMIT),
    )(*lead_args, x2d, pos2d.reshape(-1, 1, tm), g, w_bf, invf, sgn, gq, gk, gcq, lng, lnb, bg)


def _mix_kernel(sinks_ref, proj_ref, kprev_ref, vprev_ref, x_ref, kc_ref, vc_ref, ws_ref, bst_ref,
                woa_ref, wob_ref, woc_ref, wout_ref, gffn_ref, rw_ref, rb_ref,
                x1_ref, h2_ref, idx_ref, rank_ref, gate_ref, cnt_ref,
                run_ref, oa_ref, ob_ref, oc_ref, *, d_model):
    ts = x_ref.shape[0]
    s_idx = pl.program_id(1)
    first_step = jnp.logical_and(pl.program_id(0) == 0, s_idx == 0)
    neg_inf = float("-inf")

    cpk = A_Q // LANES // N_KV_HEADS
    stack = cpk * ATT_BLOCK
    srow = lax.broadcasted_iota(I32, (stack, 2 * ATT_BLOCK), 0)
    qi = srow % ATT_BLOCK
    kj = lax.broadcasted_iota(I32, (stack, 2 * ATT_BLOCK), 1)
    band = jnp.logical_and(kj <= ATT_BLOCK + qi, kj > qi)
    chunk_of_row = lax.broadcasted_iota(I32, (stack, 1), 0) // ATT_BLOCK
    for qb in range(ts // ATT_BLOCK):
        r0 = qb * ATT_BLOCK
        rows = slice(r0, r0 + ATT_BLOCK)
        if qb == 0:
            kp, vp = kprev_ref[...], vprev_ref[...]
            mask = jnp.logical_and(band, jnp.logical_or(kj >= ATT_BLOCK, s_idx > 0))
        else:
            prow = slice(r0 - ATT_BLOCK, r0)
            kp = proj_ref[prow, OFF_K4:OFF_K4 + 4 * LANES]
            vp = proj_ref[prow, OFF_V4:OFF_V4 + 4 * LANES]
            mask = band
        k4 = jnp.concatenate([kp, proj_ref[rows, OFF_K4:OFF_K4 + 4 * LANES]], axis=0)
        v4 = jnp.concatenate([vp, proj_ref[rows, OFF_V4:OFF_V4 + 4 * LANES]], axis=0)
        for kvh in range(N_KV_HEADS):
            c0 = kvh * cpk
            q4 = jnp.concatenate(
                [proj_ref[rows, OFF_Q + (c0 + c) * LANES:OFF_Q + (c0 + c + 1) * LANES]
                 for c in range(cpk)], axis=0)
            o = jnp.zeros((stack, LANES), F32)
            for half in range(2):
                col = slice((2 * kvh + half) * LANES, (2 * kvh + half + 1) * LANES)
                s = lax.dot_general(q4, k4[:, col], (((1,), (1,)), ((), ())),
                                    preferred_element_type=F32)
                s = jnp.where(mask, s, neg_inf)
                sink = jnp.zeros((stack, 1), F32)
                for c in range(cpk):
                    sink = jnp.where(chunk_of_row == c, sinks_ref[2 * (c0 + c) + half], sink)
                m = jnp.maximum(jnp.max(s, axis=-1, keepdims=True), sink)
                p = jnp.exp(s - m)
                den = jnp.sum(p, axis=-1, keepdims=True) + jnp.exp(sink - m)
                o = o + jnp.dot(p.astype(BF16), v4[:, col], preferred_element_type=F32) / den
            for c in range(cpk):
                oa_ref[rows, (c0 + c) * LANES:(c0 + c + 1) * LANES] = (
                    o[c * ATT_BLOCK:(c + 1) * ATT_BLOCK].astype(BF16))

    ti = lax.broadcasted_iota(I32, (GMLP_CHUNK, GMLP_CHUNK), 0)
    si = lax.broadcasted_iota(I32, (GMLP_CHUNK, GMLP_CHUNK), 1)
    for g in range(GMLP_GROUPS):
        wt = jnp.where(si <= ti, ws_ref[g], 0.0).astype(BF16)
        bcol = bst_ref[:, g:g + 1]
        for ch in range(ts // GMLP_CHUNK):
            rows = slice(ch * GMLP_CHUNK, (ch + 1) * GMLP_CHUNK)
            vn = proj_ref[rows, OFF_VN + g * LANES:OFF_VN + (g + 1) * LANES]
            u = proj_ref[rows, OFF_U + g * LANES:OFF_U + (g + 1) * LANES].astype(F32)
            mixed = jnp.dot(wt, vn, preferred_element_type=F32) + bcol
            ob_ref[rows, g * LANES:(g + 1) * LANES] = (u * mixed).astype(BF16)

    for hh in range(X_HEADS):
        sl = slice(hh * X_HEAD_DIM, (hh + 1) * X_HEAD_DIM)
        qc = proj_ref[:, OFF_QC + hh * X_HEAD_DIM:OFF_QC + (hh + 1) * X_HEAD_DIM]
        s = lax.dot_general(qc, kc_ref[:, sl], (((1,), (1,)), ((), ())),
                            preferred_element_type=F32)
        p = jnp.exp(s - jnp.max(s, axis=-1, keepdims=True))
        den = jnp.sum(p, axis=-1, keepdims=True)
        oc_ref[:, sl] = (jnp.dot(p.astype(BF16), vc_ref[:, sl],
                                 preferred_element_type=F32) / den).astype(BF16)

    def gate(j):
        return proj_ref[:, OFF_GA + j * d_model:OFF_GA + (j + 1) * d_model].astype(F32)

    merged = gate(0) * jnp.dot(oa_ref[...], woa_ref[...], preferred_element_type=F32)
    merged = merged + gate(1) * jnp.dot(ob_ref[...], wob_ref[...], preferred_element_type=F32)
    merged = merged + gate(2) * jnp.dot(oc_ref[...], woc_ref[...], preferred_element_type=F32)
    x1 = x_ref[...] + jnp.dot(merged.astype(BF16), wout_ref[...], preferred_element_type=F32)
    x1_ref[...] = x1

    h2 = _rms(x1, gffn_ref[...])
    h2_words = _pack_bf16_pairs(h2)
    for c in range(h2_ref.shape[0]):
        h2_ref[c] = h2_words[:, c * SC_CHUNK:(c + 1) * SC_CHUNK]
    h2_hi = h2.astype(BF16)
    h2_lo = (h2 - h2_hi.astype(F32)).astype(BF16)
    part = jnp.dot(h2_hi, rw_ref[...], preferred_element_type=F32)
    logits = (part[:, :LANES] + part[:, LANES:]
              + jnp.dot(h2_lo, rw_ref[:, :LANES], preferred_element_type=F32) + rb_ref[...])
    lt = jnp.transpose(logits)[:N_EXPERTS]
    erow = lax.broadcasted_iota(I32, (N_EXPERTS, ts), 0)
    vals, idxs = [], []
    for _ in range(TOP_K):
        m = jnp.max(lt, axis=0, keepdims=True)
        i = jnp.min(jnp.where(lt == m, erow, N_EXPERTS), axis=0, keepdims=True)
        vals.append(m)
        idxs.append(i)
        lt = jnp.where(erow == i, neg_inf, lt)
    es = [jnp.exp(v - vals[0]) for v in vals]
    den = es[0] + es[1] + es[2] + es[3]

    @pl.when(first_step)
    def _():
        run_ref[...] = jnp.zeros_like(run_ref)

    hot = [erow == i for i in idxs]
    multihot = jnp.where(jnp.logical_or(jnp.logical_or(hot[0], hot[1]),
                                        jnp.logical_or(hot[2], hot[3])), 1.0, 0.0)
    tr = lax.broadcasted_iota(I32, (ts, ts), 0)
    tc = lax.broadcasted_iota(I32, (ts, ts), 1)
    earlier = jnp.where(tr < tc, 1.0, 0.0).astype(BF16)
    before = jnp.dot(multihot.astype(BF16), earlier, preferred_element_type=F32) + run_ref[...]
    krow = lax.broadcasted_iota(I32, (IDX_ROWS, ts), 0)
    idx_out = jnp.zeros((IDX_ROWS, ts), I32)
    rank_out = jnp.zeros((IDX_ROWS, ts), I32)
    gate_rows = jnp.zeros((IDX_ROWS, ts), F32)
    for k in range(TOP_K):
        rk = jnp.sum(jnp.where(hot[k], before, 0.0), axis=0, keepdims=True)
        idx_out = jnp.where(krow == k, idxs[k], idx_out)
        rank_out = jnp.where(krow == k, rk.astype(I32), rank_out)
        gate_rows = jnp.where(krow == k, es[k] / den, gate_rows)
    idx_ref[...] = idx_out
    rank_ref[...] = rank_out
    gate_ref[...] = jnp.transpose(jnp.concatenate(
        [gate_rows, jnp.zeros((LANES - IDX_ROWS, ts), F32)], axis=0))
    run_ref[...] = run_ref[...] + jnp.sum(multihot, axis=1, keepdims=True)
    cnt_ref[...] = run_ref[...]


def _mix(sinks, proj, x2d, kc, vc, w_s, bst, woa, wob, woc, wout, gffn, rw, rb,
         n_batch, seq, m_len, ts, batch0):
    d = x2d.shape[1]
    t = n_batch * seq
    pw = proj.shape[1]
    ns = seq // ts
    nblk = seq // ATT_BLOCK
    per = ts // ATT_BLOCK
    n_chunks = d // (2 * SC_CHUNK)
    full = lambda shape: pl.BlockSpec(shape, lambda b, s: (0,) * len(shape))
    row = lambda width: pl.BlockSpec((ts, width), lambda b, s: (b * ns + s, 0))
    prev = lambda colblk: pl.BlockSpec(
        (ATT_BLOCK, 4 * LANES), lambda b, s: (b * nblk + jnp.maximum(s * per - 1, 0), colblk))
    return pl.pallas_call(
        functools.partial(_mix_kernel, d_model=d),
        grid=(n_batch, ns),
        in_specs=[
            pl.BlockSpec(memory_space=pltpu.SMEM),
            row(pw), prev(OFF_K4 // (4 * LANES)), prev(OFF_V4 // (4 * LANES)),
            pl.BlockSpec((ts, d), lambda b, s: ((batch0 + b) * ns + s, 0)),
            pl.BlockSpec((m_len, C_Q), lambda b, s: (batch0 + b, 0)),
            pl.BlockSpec((m_len, C_Q), lambda b, s: (batch0 + b, 0)),
            full((GMLP_GROUPS, GMLP_CHUNK, GMLP_CHUNK)), full((GMLP_CHUNK, GMLP_GROUPS)),
            full((A_Q, d)), full((GMLP_WIDTH, d)), full((C_Q, d)), full((d, d)),
            full((1, d)), full((d, 2 * LANES)), full((1, LANES)),
        ],
        out_specs=[row(d),
                   pl.BlockSpec((n_chunks, ts, SC_CHUNK), lambda b, s: (0, b * ns + s, 0)),
                   pl.BlockSpec((IDX_ROWS, ts), lambda b, s: (0, b * ns + s)),
                   pl.BlockSpec((IDX_ROWS, ts), lambda b, s: (0, b * ns + s)),
                   row(LANES), full((N_EXPERTS, 1))],
        out_shape=[
            jax.ShapeDtypeStruct((t, d), F32),
            jax.ShapeDtypeStruct((n_chunks, t, SC_CHUNK), U32),
            jax.ShapeDtypeStruct((IDX_ROWS, t), I32), jax.ShapeDtypeStruct((IDX_ROWS, t), I32),
            jax.ShapeDtypeStruct((t, LANES), F32), jax.ShapeDtypeStruct((N_EXPERTS, 1), F32),
        ],
        scratch_shapes=[
            pltpu.VMEM((N_EXPERTS, 1), F32),
            pltpu.VMEM((ts, A_Q), BF16), pltpu.VMEM((ts, GMLP_WIDTH), BF16),
            pltpu.VMEM((ts, C_Q), BF16),
        ],
        name="mix",
        cost_estimate=pl.CostEstimate(
            flops=2 * t * (d * (A_Q + GMLP_WIDTH + C_Q + d) + 4 * ATT_BLOCK * A_Q
                           + GMLP_CHUNK * GMLP_WIDTH + 2 * m_len * C_Q + 3 * d * LANES),
            transcendentals=t * (2 * ATT_BLOCK * N_Q_HEADS + m_len * X_HEADS),
            bytes_accessed=t * (2 * pw + 4 * d + 4 * d + 2 * d + 12 * LANES)),
        compiler_params=pltpu.CompilerParams(
            dimension_semantics=("arbitrary", "arbitrary"), vmem_limit_bytes=VMEM_LIMIT),
    )(sinks, proj, proj, proj, x2d, kc, vc, w_s, bst, woa, wob, woc, wout, gffn, rw, rb)


def _pos_kernel(pstart_ref, idx_ref, rank_ref, pos_ref):
    idx = idx_ref[...]
    pos = rank_ref[...]
    for e in range(N_EXPERTS):
        pos = pos + jnp.where(idx == e, pstart_ref[e], 0)
    pos_ref[...] = pos


def _pos(pstart, idx, rank, tp):
    t = idx.shape[1]
    return pl.pallas_call(
        _pos_kernel,
        grid_spec=pltpu.PrefetchScalarGridSpec(
            num_scalar_prefetch=1,
            grid=(t // tp,),
            in_specs=[pl.BlockSpec((IDX_ROWS, tp), lambda i, ps: (0, i)),
                      pl.BlockSpec((IDX_ROWS, tp), lambda i, ps: (0, i))],
            out_specs=pl.BlockSpec((IDX_ROWS, tp), lambda i, ps: (0, i)),
        ),
        out_shape=jax.ShapeDtypeStruct((IDX_ROWS, t), I32),
        name="slot_pos",
        compiler_params=pltpu.CompilerParams(dimension_semantics=("arbitrary",)),
    )(pstart, idx, rank)


def _sc_mesh():
    return plsc.VectorSubcoreMesh(core_axis_name="core", subcore_axis_name="subcore")


def _sc_scatter_rows(src, idx, n_rows):
    n_chunks, t, w = src.shape
    n_idx = idx.shape[1]
    src_blocks = t // SC_ROWS
    idx_blocks = n_idx // SC_ROWS

    @pl.kernel(out_type=jax.ShapeDtypeStruct((n_chunks * n_rows, w), src.dtype), mesh=_sc_mesh(),
               scratch_types=[], name="sc_dispatch",
               cost_estimate=pl.CostEstimate(
                   flops=0, transcendentals=0,
                   bytes_accessed=n_chunks * n_idx * (8 * w + 4)))
    def scatter(src_hbm, idx_hbm, out_hbm):
        def body(src_vmem, idx_vmem):
            pltpu.sync_copy(src_vmem, out_hbm.at[idx_vmem.at[0]])

        pltpu.emit_pipeline(
            body,
            grid=(n_chunks, idx_blocks),
            in_specs=[pl.BlockSpec((SC_ROWS, w), lambda c, i: (c * src_blocks + i % src_blocks, 0)),
                      pl.BlockSpec((1, SC_ROWS), lambda c, i: (0, c * idx_blocks + i))],
            out_specs=[],
            core_axis_name=("core", "subcore"),
            dimension_semantics=(pltpu.PARALLEL, pltpu.PARALLEL),
        )(src_hbm, idx_hbm)

    return scatter(src.reshape(n_chunks * t, w), idx.reshape(1, n_chunks * n_idx)).reshape(
        n_chunks, n_rows, w)


def _sc_gather_rows(table, idx):
    n_chunks, p, w = table.shape
    n_idx = idx.shape[1]
    idx_blocks = n_idx // SC_ROWS

    @pl.kernel(out_type=jax.ShapeDtypeStruct((n_chunks * n_idx, w), table.dtype), mesh=_sc_mesh(),
               scratch_types=[], name="sc_gather",
               cost_estimate=pl.CostEstimate(
                   flops=0, transcendentals=0,
                   bytes_accessed=n_chunks * n_idx * (8 * w + 4)))
    def gather(table_hbm, idx_hbm, out_hbm):
        def body(idx_vmem, out_vmem):
            pltpu.sync_copy(table_hbm.at[idx_vmem.at[0]], out_vmem)

        pltpu.emit_pipeline(
            body,
            grid=(n_chunks, idx_blocks),
            in_specs=[pl.BlockSpec((1, SC_ROWS), lambda c, i: (0, c * idx_blocks + i))],
            out_specs=[pl.BlockSpec((SC_ROWS, w), lambda c, i: (c * idx_blocks + i, 0))],
            core_axis_name=("core", "subcore"),
            dimension_semantics=(pltpu.PARALLEL, pltpu.PARALLEL),
        )(idx_hbm, out_hbm)

    return gather(table.reshape(n_chunks * p, w), idx.reshape(1, n_chunks * n_idx)).reshape(
        n_chunks, n_idx, w)


def _expert_kernel(bexp_ref, nvalid_ref, next_ref, xs_ref, wgu_hbm, bgu_ref, wd_hbm, bd_ref, ys_ref,
                   wgu_stage, wd_stage, wgu_bf_ref, wd_bf_ref, sems, *, d_exp):
    b = pl.program_id(0)

    def weight_copies(e):
        return (pltpu.make_async_copy(wgu_hbm.at[e], wgu_stage, sems.at[0]),
                pltpu.make_async_copy(wd_hbm.at[e], wd_stage, sems.at[1]))

    @pl.when(b < nvalid_ref[0])
    def _():
        e = bexp_ref[b]

        @pl.when(jnp.logical_or(b == 0, e != bexp_ref[jnp.maximum(b - 1, 0)]))
        def _():
            @pl.when(b == 0)
            def _():
                for cp in weight_copies(e):
                    cp.start()

            for cp in weight_copies(e):
                cp.wait()
            wgu_bf_ref[...] = wgu_stage[...].astype(BF16)
            wd_bf_ref[...] = wd_stage[...].astype(BF16)
            e_next = next_ref[e]

            @pl.when(e_next >= 0)
            def _():
                for cp in weight_copies(e_next):
                    cp.start()

        n_chunks = xs_ref.shape[0]
        lo, hi = _unpack_bf16_pairs(
            jnp.concatenate([xs_ref[c] for c in range(n_chunks)], axis=1))
        xb = jnp.concatenate([lo, hi], axis=1).astype(BF16)
        gu = jnp.dot(xb, wgu_bf_ref[...], preferred_element_type=F32) + bgu_ref[...]
        gate = jnp.minimum(gu[:, :d_exp], SWIGLU_LIMIT)
        up = jnp.clip(gu[:, d_exp:], -SWIGLU_LIMIT, SWIGLU_LIMIT)
        glu = gate * jax.nn.sigmoid(gate * SWIGLU_ALPHA)
        act = ((up + 1.0) * glu).astype(BF16)
        y = jnp.dot(act, wd_bf_ref[...], preferred_element_type=F32) + bd_ref[...]
        y_words = _pack_bf16_pairs(y)
        for c in range(n_chunks):
            ys_ref[c] = y_words[:, c * SC_CHUNK:(c + 1) * SC_CHUNK]


def _experts(bexp, nvalid, next_expert, xs, wgu, bgu, wd, bd):
    n_chunks, n_rows, _ = xs.shape
    _, d, d_exp2 = wgu.shape
    d_exp = d_exp2 // 2
    nb = n_rows // MOE_BLOCK
    blk = lambda b, be, nv, nx: jnp.minimum(b, nv[0] - 1)
    return pl.pallas_call(
        functools.partial(_expert_kernel, d_exp=d_exp),
        grid_spec=pltpu.PrefetchScalarGridSpec(
            num_scalar_prefetch=3,
            grid=(nb,),
            in_specs=[
                pl.BlockSpec((n_chunks, MOE_BLOCK, SC_CHUNK),
                             lambda b, be, nv, nx: (0, blk(b, be, nv, nx), 0)),
                pl.BlockSpec(memory_space=pl.ANY),
                pl.BlockSpec((None, 1, 2 * d_exp),
                             lambda b, be, nv, nx: (be[blk(b, be, nv, nx)], 0, 0)),
                pl.BlockSpec(memory_space=pl.ANY),
                pl.BlockSpec((None, 1, d), lambda b, be, nv, nx: (be[blk(b, be, nv, nx)], 0, 0)),
            ],
            out_specs=pl.BlockSpec((n_chunks, MOE_BLOCK, SC_CHUNK),
                                   lambda b, be, nv, nx: (0, blk(b, be, nv, nx), 0)),
            scratch_shapes=[pltpu.VMEM((d, 2 * d_exp), F32), pltpu.VMEM((d_exp, d), F32),
                            pltpu.VMEM((d, 2 * d_exp), BF16), pltpu.VMEM((d_exp, d), BF16),
                            pltpu.SemaphoreType.DMA((2,))],
        ),
        out_shape=jax.ShapeDtypeStruct((n_chunks, n_rows, SC_CHUNK), U32),
        name="experts",
        cost_estimate=pl.CostEstimate(
            flops=6 * n_rows * d * d_exp, transcendentals=n_rows * d_exp,
            bytes_accessed=4 * n_rows * d + 4 * N_EXPERTS * 3 * d * d_exp),
        compiler_params=pltpu.CompilerParams(
            dimension_semantics=("arbitrary",), vmem_limit_bytes=VMEM_LIMIT),
    )(bexp, nvalid, next_expert, xs, wgu, bgu, wd, bd)


def _combine_kernel(prev_ref, x1_ref, gate_ref, yg_ref, out_ref):
    del prev_ref
    n_chunks = yg_ref.shape[0]
    half = n_chunks * SC_CHUNK
    for c in range(n_chunks):
        sl_lo = slice(c * SC_CHUNK, (c + 1) * SC_CHUNK)
        sl_hi = slice(half + c * SC_CHUNK, half + (c + 1) * SC_CHUNK)
        acc_lo = x1_ref[:, sl_lo]
        acc_hi = x1_ref[:, sl_hi]
        for k in range(TOP_K):
            lo, hi = _unpack_bf16_pairs(yg_ref[c, k])
            g = gate_ref[:, k:k + 1]
            acc_lo = acc_lo + g * lo
            acc_hi = acc_hi + g * hi
        out_ref[:, sl_lo] = acc_lo
        out_ref[:, sl_hi] = acc_hi


def _combine(x1, gates, yg, out_prev, tcb, row0, t_total):
    t, d = x1.shape
    n_chunks = yg.shape[0]
    blk0 = row0 // tcb
    in_specs = [
        pl.BlockSpec((tcb, d), lambda i: (i, 0)),
        pl.BlockSpec((tcb, LANES), lambda i: (i, 0)),
        pl.BlockSpec((n_chunks, TOP_K, tcb, SC_CHUNK), lambda i: (0, 0, i, 0)),
    ]
    args = [x1, gates, yg]
    aliases = {}
    body = functools.partial(_combine_kernel, None)
    if out_prev is not None:
        in_specs.append(pl.BlockSpec(memory_space=pl.ANY))
        args.append(out_prev)
        aliases = {3: 0}
        body = lambda a, b, c, prev, o: _combine_kernel(prev, a, b, c, o)
    return pl.pallas_call(
        body,
        grid=(t // tcb,),
        in_specs=in_specs,
        out_specs=pl.BlockSpec((tcb, d), lambda i: (blk0 + i, 0)),
        out_shape=jax.ShapeDtypeStruct((t_total, d), F32),
        input_output_aliases=aliases,
        name="combine",
        cost_estimate=pl.CostEstimate(
            flops=2 * TOP_K * t * d, transcendentals=0,
            bytes_accessed=t * (4 * d + 4 * d + 2 * TOP_K * d + 4 * LANES)),
        compiler_params=pltpu.CompilerParams(dimension_semantics=("arbitrary",)),
    )(*args)


def _pick_tile(n, pref):
    t = min(pref, n)
    while n % t:
        t //= 2
    return t


def kernel(x, mem, positions, attn_norm_g, mem_norm_g, w_in, b_gates, a_q_norm_g, a_k_norm_g,
           a_sinks, w_o_a, gmlp_ln_g, gmlp_ln_b, gmlp_w_s, gmlp_b_s, w_o_b, w_mem_kv,
           c_q_norm_g, c_k_norm_g, w_o_c, w_out, ffn_norm_g, router_w, router_b,
           w_gate_up, b_gate_up, w_down, b_down):
    n_batch, seq, d = x.shape
    m_len = mem.shape[1]
    depth = w_in.shape[0]
    t = n_batch * seq
    n_groups = N_TOKEN_GROUPS if n_batch % N_TOKEN_GROUPS == 0 else 1
    gb = n_batch // n_groups
    tg = gb * seq
    n_asg = tg * TOP_K
    nb = -(-n_asg // MOE_BLOCK) + N_EXPERTS
    n_rows = nb * MOE_BLOCK

    inv_freq = ROPE_THETA ** (-jnp.arange(0, HEAD_DIM, 2, dtype=F32) / HEAD_DIM)
    invf = inv_freq[:, None]
    sgn = jnp.tile(jnp.concatenate([-jnp.ones((HEAD_DIM // 2,), F32),
                                    jnp.ones((HEAD_DIM // 2,), F32)]), LANES // HEAD_DIM)[None, :]
    pos2d = positions.reshape(t, 1).astype(I32)
    mem2d = mem.reshape(n_batch * m_len, d)
    x2d = x.reshape(t, d)

    tm = _pick_tile(tg, 512)
    ts = _pick_tile(seq, 512)
    tp = _pick_tile(tg, 8192)
    tcb = _pick_tile(tg, 512)

    for l in range(depth):
        kc, vc = _mem_kv(mem2d, mem_norm_g[l][None, :], w_mem_kv[l].astype(BF16),
                         c_k_norm_g[l][None, :], n_batch, m_len)
        w_in_bf = w_in[l].astype(BF16)
        mix_w = (w_o_a[l].astype(BF16), w_o_b[l].astype(BF16), w_o_c[l].astype(BF16),
                 w_out[l].astype(BF16))
        gq = jnp.tile(a_q_norm_g[l], LANES // HEAD_DIM)[None, :]
        gk = jnp.tile(a_k_norm_g[l], LANES // HEAD_DIM)[None, :]
        rw32 = jnp.pad(router_w[l], ((0, 0), (0, LANES - N_EXPERTS)))
        rw_hi = rw32.astype(BF16)
        rw = jnp.concatenate([rw_hi, (rw32 - rw_hi.astype(F32)).astype(BF16)], axis=1)
        rb = jnp.pad(router_b[l], (0, LANES - N_EXPERTS))[None, :]

        routed = []
        scatter_idx = None
        for g in range(n_groups):
            proj = _proj(
                x2d, pos2d, attn_norm_g[l][None, :], w_in_bf, invf, sgn, gq, gk,
                c_q_norm_g[l][None, :], gmlp_ln_g[l][None, :], gmlp_ln_b[l][None, :],
                b_gates[l].reshape(1, 3 * d), tm, g * tg, tg, scatter_idx)
            x1, h2, idx, rank, gates, counts = _mix(
                a_sinks[l], proj, x2d, kc, vc, gmlp_w_s[l], gmlp_b_s[l].T, *mix_w,
                ffn_norm_g[l][None, :], rw, rb, gb, seq, m_len, ts, g * gb)

            cnt = counts[:, 0].astype(I32)
            padded = (cnt + MOE_BLOCK - 1) // MOE_BLOCK * MOE_BLOCK
            pend = jnp.cumsum(padded)
            pstart = (pend - padded).astype(I32)
            nvalid = (pend[-1:] // MOE_BLOCK).astype(I32)
            blk_row = jnp.arange(nb, dtype=I32) * MOE_BLOCK
            bexp = jnp.minimum(jnp.sum((pend[None, :] <= blk_row[:, None]).astype(I32), axis=1),
                               N_EXPERTS - 1).astype(I32)
            eid = jnp.arange(N_EXPERTS, dtype=I32)
            later = jnp.logical_and(eid[None, :] > eid[:, None], (cnt > 0)[None, :])
            next_expert = jnp.min(jnp.where(later, eid[None, :], N_EXPERTS), axis=1)
            next_expert = jnp.where(next_expert == N_EXPERTS, -1, next_expert).astype(I32)
            j = jnp.arange(MOE_BLOCK, dtype=I32)[None, :]
            fill = jnp.where(j < (padded - cnt)[:, None], pend[:, None] - 1 - j,
                             n_rows - 1 - j).astype(I32)

            pos = _pos(pstart, idx, rank, tp)
            pos_km = pos[:TOP_K].reshape(n_asg)
            n_chunks = h2.shape[0]
            chunk_off = (jnp.arange(n_chunks, dtype=I32) * n_rows)[:, None]
            scatter_idx = jnp.concatenate([pos_km, fill.reshape(-1)])[None, :] + chunk_off
            xs = _sc_scatter_rows(h2, scatter_idx, n_rows)
            routed.append((x1, gates, xs, bexp, nvalid, next_expert, pos_km, chunk_off))

        gathered = []
        for x1, gates, xs, bexp, nvalid, next_expert, pos_km, chunk_off in routed:
            ys = _experts(bexp, nvalid, next_expert, xs, w_gate_up[l], b_gate_up[l][:, None, :],
                          w_down[l], b_down[l][:, None, :])
            yg = _sc_gather_rows(ys, pos_km[None, :] + chunk_off)
            gathered.append((x1, gates, yg.reshape(yg.shape[0], TOP_K, tg, SC_CHUNK)))

        out = None
        for g, (x1, gates, yg) in enumerate(gathered):
            out = _combine(x1, gates, yg, out, tcb, g * tg, t)
        x2d = out
    return x2d.reshape(n_batch, seq, d)
```

```python
import functools

import numpy as np
import jax
import jax.numpy as jnp
from jax import lax
from jax.experimental import pallas as pl
from jax.experimental.pallas import tpu as pltpu
from jax.experimental.pallas import tpu_sc as plsc

F32 = jnp.float32
BF16 = jnp.bfloat16
I32 = jnp.int32
U32 = jnp.uint32
HI16 = np.uint32(0xFFFF0000)

EPS = 1e-6
LANES = 128
HEAD_DIM = 64
N_Q_HEADS = 16
N_KV_HEADS = 2
ATT_BLOCK = 128
ROPE_THETA = 10000.0
GMLP_WIDTH = 512
GMLP_GROUPS = 4
GMLP_CHUNK = 128
X_HEADS = 4
X_HEAD_DIM = 128
N_EXPERTS = 32
TOP_K = 4
SWIGLU_LIMIT = 7.0
SWIGLU_ALPHA = 1.702
MOE_BLOCK = 512
SC_ROWS = 128
SC_CHUNK = 256
N_TOKEN_GROUPS = 2
EXPERT_BLOCKS_PER_STEP = 2
IDX_ROWS = 8

A_Q = N_Q_HEADS * HEAD_DIM
A_KV = N_KV_HEADS * HEAD_DIM
C_Q = X_HEADS * X_HEAD_DIM

OFF_Q = 0
OFF_K4 = OFF_Q + A_Q
OFF_V4 = OFF_K4 + 4 * LANES
OFF_U = OFF_V4 + 4 * LANES
OFF_VN = OFF_U + GMLP_WIDTH
OFF_QC = OFF_VN + GMLP_WIDTH
OFF_GA = OFF_QC + C_Q
PROJ_W_BASE = OFF_GA

VMEM_LIMIT = 56 * 1024 * 1024


def _lane_iota(shape):
    return lax.broadcasted_iota(I32, shape, len(shape) - 1)


def _rms(x, g):
    return x * lax.rsqrt(jnp.mean(x * x, axis=-1, keepdims=True) + EPS) * g


def _pack_bf16_pairs(x):
    n = x.shape[1] // 2
    bits = pltpu.bitcast(x.astype(BF16).astype(F32), U32)
    return (bits[:, :n] >> 16) | (bits[:, n:] & HI16)


def _unpack_bf16_pairs(w):
    return pltpu.bitcast(w << 16, F32), pltpu.bitcast(w & HI16, F32)


def _gelu(x):
    return 0.5 * x * (1.0 + lax.erf(x * np.float32(np.sqrt(0.5))))


def _memkv_kernel(mem_ref, g_ref, w_ref, gk_ref, kc_ref, vc_ref):
    h = _rms(mem_ref[...], g_ref[...]).astype(BF16)
    kv = jnp.dot(h, w_ref[...], preferred_element_type=F32)
    for hh in range(X_HEADS):
        sl = slice(hh * X_HEAD_DIM, (hh + 1) * X_HEAD_DIM)
        kc_ref[:, sl] = _rms(kv[:, sl], gk_ref[...]).astype(BF16)
    vc_ref[...] = kv[:, C_Q:].astype(BF16)


def _mem_kv(mem2d, g, w_bf, gk, n_batch, m_len):
    d = mem2d.shape[1]
    return pl.pallas_call(
        _memkv_kernel,
        grid=(n_batch,),
        in_specs=[
            pl.BlockSpec((m_len, d), lambda b: (b, 0)),
            pl.BlockSpec((1, d), lambda b: (0, 0)),
            pl.BlockSpec((d, 2 * C_Q), lambda b: (0, 0)),
            pl.BlockSpec((1, X_HEAD_DIM), lambda b: (0, 0)),
        ],
        out_specs=[
            pl.BlockSpec((m_len, C_Q), lambda b: (b, 0)),
            pl.BlockSpec((m_len, C_Q), lambda b: (b, 0)),
        ],
        out_shape=[jax.ShapeDtypeStruct((n_batch * m_len, C_Q), BF16)] * 2,
        name="mem_kv",
        compiler_params=pltpu.CompilerParams(dimension_semantics=("arbitrary",)),
    )(mem2d, g, w_bf, gk)


def _proj_kernel(x_ref, pos_ref, g_ref, w_ref, invf_ref, sgn_ref, gq_ref, gk_ref, gcq_ref,
                 lng_ref, lnb_ref, bg_ref, out_ref, *, d_model):
    tm = x_ref.shape[0]
    h = _rms(x_ref[...], g_ref[...]).astype(BF16)

    ang_t = invf_ref[...] * pos_ref[...].astype(F32)
    reps = LANES // (HEAD_DIM // 2)
    cosv = jnp.transpose(jnp.concatenate([jnp.cos(ang_t)] * reps, axis=0))
    sinv = jnp.transpose(jnp.concatenate([jnp.sin(ang_t)] * reps, axis=0)) * sgn_ref[...]
    lane = _lane_iota((tm, LANES))
    first_head = lane < HEAD_DIM
    lo_half = (lane % HEAD_DIM) < (HEAD_DIM // 2)

    def head_norm_rope(blk, g):
        y = blk * blk
        s_lo = jnp.sum(jnp.where(first_head, y, 0.0), axis=-1, keepdims=True)
        s_hi = jnp.sum(jnp.where(first_head, 0.0, y), axis=-1, keepdims=True)
        ss = jnp.where(first_head, s_lo, s_hi)
        n = blk * lax.rsqrt(ss * (1.0 / HEAD_DIM) + EPS) * g
        rot = jnp.where(lo_half, pltpu.roll(n, LANES - HEAD_DIM // 2, 1),
                        pltpu.roll(n, HEAD_DIM // 2, 1))
        return n * cosv + rot * sinv

    def proj(a, b):
        return jnp.dot(h, w_ref[:, a:b], preferred_element_type=F32)

    pq = proj(0, A_Q)
    for c in range(A_Q // LANES):
        sl = slice(c * LANES, (c + 1) * LANES)
        out_ref[:, OFF_Q + c * LANES:OFF_Q + (c + 1) * LANES] = (
            head_norm_rope(pq[:, sl], gq_ref[...]) * (HEAD_DIM ** -0.5)).astype(BF16)

    pkv = proj(A_Q, A_Q + 2 * A_KV)
    kn = head_norm_rope(pkv[:, :LANES], gk_ref[...])
    vv = pkv[:, LANES:]
    for off, t in ((OFF_K4, kn), (OFF_V4, vv)):
        tr = pltpu.roll(t, HEAD_DIM, 1)
        parts = (jnp.where(first_head, t, 0.0), jnp.where(first_head, 0.0, tr),
                 jnp.where(first_head, tr, 0.0), jnp.where(first_head, 0.0, t))
        for j, p in enumerate(parts):
            out_ref[:, off + j * LANES:off + (j + 1) * LANES] = p.astype(BF16)

    o0 = A_Q + 2 * A_KV
    out_ref[:, OFF_U:OFF_U + GMLP_WIDTH] = _gelu(proj(o0, o0 + GMLP_WIDTH)).astype(BF16)
    gv = _gelu(proj(o0 + GMLP_WIDTH, o0 + 2 * GMLP_WIDTH))
    mu = jnp.mean(gv, axis=-1, keepdims=True)
    var = jnp.mean(jnp.square(gv - mu), axis=-1, keepdims=True)
    out_ref[:, OFF_VN:OFF_VN + GMLP_WIDTH] = (
        (gv - mu) * lax.rsqrt(var + EPS) * lng_ref[...] + lnb_ref[...]).astype(BF16)

    o1 = o0 + 2 * GMLP_WIDTH
    pc = proj(o1, o1 + C_Q)
    for hh in range(X_HEADS):
        sl = slice(hh * X_HEAD_DIM, (hh + 1) * X_HEAD_DIM)
        out_ref[:, OFF_QC + hh * X_HEAD_DIM:OFF_QC + (hh + 1) * X_HEAD_DIM] = (
            _rms(pc[:, sl], gcq_ref[...]) * (X_HEAD_DIM ** -0.5)).astype(BF16)

    o2 = o1 + C_Q
    for j in range(3):
        sl = slice(j * d_model, (j + 1) * d_model)
        z = proj(o2 + j * d_model, o2 + (j + 1) * d_model) + bg_ref[:, sl]
        out_ref[:, OFF_GA + j * d_model:OFF_GA + (j + 1) * d_model] = (
            0.5 * jnp.tanh(0.5 * z) + 0.5).astype(BF16)


def _proj_after_kernel(after_ref, *refs, d_model):
    del after_ref
    _proj_kernel(*refs, d_model=d_model)


def _proj(x2d, pos2d, g, w_bf, invf, sgn, gq, gk, gcq, lng, lnb, bg, tm, row0, t, after):
    d = x2d.shape[1]
    d_in = w_bf.shape[1]
    pw = PROJ_W_BASE + 3 * d
    blk0 = row0 // tm
    full = lambda shape: pl.BlockSpec(shape, lambda i: (0,) * len(shape))
    body = functools.partial(_proj_kernel, d_model=d)
    lead_specs, lead_args = [], []
    if after is not None:
        body = functools.partial(_proj_after_kernel, d_model=d)
        lead_specs, lead_args = [pl.BlockSpec(memory_space=pl.ANY)], [after]
    return pl.pallas_call(
        body,
        grid=(t // tm,),
        in_specs=lead_specs + [
            pl.BlockSpec((tm, d), lambda i: (blk0 + i, 0)),
            pl.BlockSpec((None, 1, tm), lambda i: (blk0 + i, 0, 0)),
            full((1, d)), full((d, d_in)), full((HEAD_DIM // 2, 1)), full((1, LANES)),
            full((1, LANES)), full((1, LANES)), full((1, X_HEAD_DIM)),
            full((1, GMLP_WIDTH)), full((1, GMLP_WIDTH)), full((1, 3 * d)),
        ],
        out_specs=pl.BlockSpec((tm, pw), lambda i: (i, 0)),
        out_shape=jax.ShapeDtypeStruct((t, pw), BF16),
        name="proj",
        cost_estimate=pl.CostEstimate(
            flops=2 * t * d * d_in, transcendentals=t * (3 * d + 2 * GMLP_WIDTH + 2 * LANES),
            bytes_accessed=4 * t * d + 2 * t * pw + 2 * d * d_in),
        compiler_params=pltpu.CompilerParams(
            dimension_semantics=("arbitrary",), vmem_limit_bytes=VMEM_LIMIT),
    )(*lead_args, x2d, pos2d.reshape(-1, 1, tm), g, w_bf, invf, sgn, gq, gk, gcq, lng, lnb, bg)


def _mix_kernel(sinks_ref, proj_ref, kprev_ref, vprev_ref, x_ref, kc_ref, vc_ref, ws_ref, bst_ref,
                woa_ref, wob_ref, woc_ref, wout_ref, gffn_ref, rw_ref, rb_ref,
                x1_ref, h2_ref, idx_ref, rank_ref, gate_ref, cnt_ref,
                run_ref, oa_ref, ob_ref, oc_ref, *, d_model):
    ts = x_ref.shape[0]
    s_idx = pl.program_id(1)
    first_step = jnp.logical_and(pl.program_id(0) == 0, s_idx == 0)
    neg_inf = float("-inf")

    cpk = A_Q // LANES // N_KV_HEADS
    stack = cpk * ATT_BLOCK
    srow = lax.broadcasted_iota(I32, (stack, 2 * ATT_BLOCK), 0)
    qi = srow % ATT_BLOCK
    kj = lax.broadcasted_iota(I32, (stack, 2 * ATT_BLOCK), 1)
    band = jnp.logical_and(kj <= ATT_BLOCK + qi, kj > qi)
    chunk_of_row = lax.broadcasted_iota(I32, (stack, 1), 0) // ATT_BLOCK
    for qb in range(ts // ATT_BLOCK):
        r0 = qb * ATT_BLOCK
        rows = slice(r0, r0 + ATT_BLOCK)
        if qb == 0:
            kp, vp = kprev_ref[...], vprev_ref[...]
            mask = jnp.logical_and(band, jnp.logical_or(kj >= ATT_BLOCK, s_idx > 0))
        else:
            prow = slice(r0 - ATT_BLOCK, r0)
            kp = proj_ref[prow, OFF_K4:OFF_K4 + 4 * LANES]
            vp = proj_ref[prow, OFF_V4:OFF_V4 + 4 * LANES]
            mask = band
        k4 = jnp.concatenate([kp, proj_ref[rows, OFF_K4:OFF_K4 + 4 * LANES]], axis=0)
        v4 = jnp.concatenate([vp, proj_ref[rows, OFF_V4:OFF_V4 + 4 * LANES]], axis=0)
        for kvh in range(N_KV_HEADS):
            c0 = kvh * cpk
            q4 = jnp.concatenate(
                [proj_ref[rows, OFF_Q + (c0 + c) * LANES:OFF_Q + (c0 + c + 1) * LANES]
                 for c in range(cpk)], axis=0)
            o = jnp.zeros((stack, LANES), F32)
            for half in range(2):
                col = slice((2 * kvh + half) * LANES, (2 * kvh + half + 1) * LANES)
                s = lax.dot_general(q4, k4[:, col], (((1,), (1,)), ((), ())),
                                    preferred_element_type=F32)
                s = jnp.where(mask, s, neg_inf)
                sink = jnp.zeros((stack, 1), F32)
                for c in range(cpk):
                    sink = jnp.where(chunk_of_row == c, sinks_ref[2 * (c0 + c) + half], sink)
                m = jnp.maximum(jnp.max(s, axis=-1, keepdims=True), sink)
                p = jnp.exp(s - m)
                den = jnp.sum(p, axis=-1, keepdims=True) + jnp.exp(sink - m)
                o = o + jnp.dot(p.astype(BF16), v4[:, col], preferred_element_type=F32) / den
            for c in range(cpk):
                oa_ref[rows, (c0 + c) * LANES:(c0 + c + 1) * LANES] = (
                    o[c * ATT_BLOCK:(c + 1) * ATT_BLOCK].astype(BF16))

    ti = lax.broadcasted_iota(I32, (GMLP_CHUNK, GMLP_CHUNK), 0)
    si = lax.broadcasted_iota(I32, (GMLP_CHUNK, GMLP_CHUNK), 1)
    for g in range(GMLP_GROUPS):
        wt = jnp.where(si <= ti, ws_ref[g], 0.0).astype(BF16)
        bcol = bst_ref[:, g:g + 1]
        for ch in range(ts // GMLP_CHUNK):
            rows = slice(ch * GMLP_CHUNK, (ch + 1) * GMLP_CHUNK)
            vn = proj_ref[rows, OFF_VN + g * LANES:OFF_VN + (g + 1) * LANES]
            u = proj_ref[rows, OFF_U + g * LANES:OFF_U + (g + 1) * LANES].astype(F32)
            mixed = jnp.dot(wt, vn, preferred_element_type=F32) + bcol
            ob_ref[rows, g * LANES:(g + 1) * LANES] = (u * mixed).astype(BF16)

    for hh in range(X_HEADS):
        sl = slice(hh * X_HEAD_DIM, (hh + 1) * X_HEAD_DIM)
        qc = proj_ref[:, OFF_QC + hh * X_HEAD_DIM:OFF_QC + (hh + 1) * X_HEAD_DIM]
        s = lax.dot_general(qc, kc_ref[:, sl], (((1,), (1,)), ((), ())),
                            preferred_element_type=F32)
        p = jnp.exp(s - jnp.max(s, axis=-1, keepdims=True))
        den = jnp.sum(p, axis=-1, keepdims=True)
        oc_ref[:, sl] = (jnp.dot(p.astype(BF16), vc_ref[:, sl],
                                 preferred_element_type=F32) / den).astype(BF16)

    def gate(j):
        return proj_ref[:, OFF_GA + j * d_model:OFF_GA + (j + 1) * d_model].astype(F32)

    merged = gate(0) * jnp.dot(oa_ref[...], woa_ref[...], preferred_element_type=F32)
    merged = merged + gate(1) * jnp.dot(ob_ref[...], wob_ref[...], preferred_element_type=F32)
    merged = merged + gate(2) * jnp.dot(oc_ref[...], woc_ref[...], preferred_element_type=F32)
    x1 = x_ref[...] + jnp.dot(merged.astype(BF16), wout_ref[...], preferred_element_type=F32)
    x1_ref[...] = x1

    h2 = _rms(x1, gffn_ref[...])
    h2_words = _pack_bf16_pairs(h2)
    for c in range(h2_ref.shape[0]):
        h2_ref[c] = h2_words[:, c * SC_CHUNK:(c + 1) * SC_CHUNK]
    h2_hi = h2.astype(BF16)
    h2_lo = (h2 - h2_hi.astype(F32)).astype(BF16)
    part = jnp.dot(h2_hi, rw_ref[...], preferred_element_type=F32)
    logits = (part[:, :LANES] + part[:, LANES:]
              + jnp.dot(h2_lo, rw_ref[:, :LANES], preferred_element_type=F32) + rb_ref[...])
    lt = jnp.transpose(logits)[:N_EXPERTS]
    erow = lax.broadcasted_iota(I32, (N_EXPERTS, ts), 0)
    vals, idxs = [], []
    for _ in range(TOP_K):
        m = jnp.max(lt, axis=0, keepdims=True)
        i = jnp.min(jnp.where(lt == m, erow, N_EXPERTS), axis=0, keepdims=True)
        vals.append(m)
        idxs.append(i)
        lt = jnp.where(erow == i, neg_inf, lt)
    es = [jnp.exp(v - vals[0]) for v in vals]
    den = es[0] + es[1] + es[2] + es[3]

    @pl.when(first_step)
    def _():
        run_ref[...] = jnp.zeros_like(run_ref)

    hot = [erow == i for i in idxs]
    multihot = jnp.where(jnp.logical_or(jnp.logical_or(hot[0], hot[1]),
                                        jnp.logical_or(hot[2], hot[3])), 1.0, 0.0)
    tr = lax.broadcasted_iota(I32, (ts, ts), 0)
    tc = lax.broadcasted_iota(I32, (ts, ts), 1)
    earlier = jnp.where(tr < tc, 1.0, 0.0).astype(BF16)
    before = jnp.dot(multihot.astype(BF16), earlier, preferred_element_type=F32) + run_ref[...]
    krow = lax.broadcasted_iota(I32, (IDX_ROWS, ts), 0)
    idx_out = jnp.zeros((IDX_ROWS, ts), I32)
    rank_out = jnp.zeros((IDX_ROWS, ts), I32)
    gate_rows = jnp.zeros((IDX_ROWS, ts), F32)
    for k in range(TOP_K):
        rk = jnp.sum(jnp.where(hot[k], before, 0.0), axis=0, keepdims=True)
        idx_out = jnp.where(krow == k, idxs[k], idx_out)
        rank_out = jnp.where(krow == k, rk.astype(I32), rank_out)
        gate_rows = jnp.where(krow == k, es[k] / den, gate_rows)
    idx_ref[...] = idx_out
    rank_ref[...] = rank_out
    gate_ref[...] = jnp.transpose(jnp.concatenate(
        [gate_rows, jnp.zeros((LANES - IDX_ROWS, ts), F32)], axis=0))
    run_ref[...] = run_ref[...] + jnp.sum(multihot, axis=1, keepdims=True)
    cnt_ref[...] = run_ref[...]


def _mix(sinks, proj, x2d, kc, vc, w_s, bst, woa, wob, woc, wout, gffn, rw, rb,
         n_batch, seq, m_len, ts, batch0):
    d = x2d.shape[1]
    t = n_batch * seq
    pw = proj.shape[1]
    ns = seq // ts
    nblk = seq // ATT_BLOCK
    per = ts // ATT_BLOCK
    n_chunks = d // (2 * SC_CHUNK)
    full = lambda shape: pl.BlockSpec(shape, lambda b, s: (0,) * len(shape))
    row = lambda width: pl.BlockSpec((ts, width), lambda b, s: (b * ns + s, 0))
    prev = lambda colblk: pl.BlockSpec(
        (ATT_BLOCK, 4 * LANES), lambda b, s: (b * nblk + jnp.maximum(s * per - 1, 0), colblk))
    return pl.pallas_call(
        functools.partial(_mix_kernel, d_model=d),
        grid=(n_batch, ns),
        in_specs=[
            pl.BlockSpec(memory_space=pltpu.SMEM),
            row(pw), prev(OFF_K4 // (4 * LANES)), prev(OFF_V4 // (4 * LANES)),
            pl.BlockSpec((ts, d), lambda b, s: ((batch0 + b) * ns + s, 0)),
            pl.BlockSpec((m_len, C_Q), lambda b, s: (batch0 + b, 0)),
            pl.BlockSpec((m_len, C_Q), lambda b, s: (batch0 + b, 0)),
            full((GMLP_GROUPS, GMLP_CHUNK, GMLP_CHUNK)), full((GMLP_CHUNK, GMLP_GROUPS)),
            full((A_Q, d)), full((GMLP_WIDTH, d)), full((C_Q, d)), full((d, d)),
            full((1, d)), full((d, 2 * LANES)), full((1, LANES)),
        ],
        out_specs=[row(d),
                   pl.BlockSpec((n_chunks, ts, SC_CHUNK), lambda b, s: (0, b * ns + s, 0)),
                   pl.BlockSpec((IDX_ROWS, ts), lambda b, s: (0, b * ns + s)),
                   pl.BlockSpec((IDX_ROWS, ts), lambda b, s: (0, b * ns + s)),
                   row(LANES), full((N_EXPERTS, 1))],
        out_shape=[
            jax.ShapeDtypeStruct((t, d), F32),
            jax.ShapeDtypeStruct((n_chunks, t, SC_CHUNK), U32),
            jax.ShapeDtypeStruct((IDX_ROWS, t), I32), jax.ShapeDtypeStruct((IDX_ROWS, t), I32),
            jax.ShapeDtypeStruct((t, LANES), F32), jax.ShapeDtypeStruct((N_EXPERTS, 1), F32),
        ],
        scratch_shapes=[
            pltpu.VMEM((N_EXPERTS, 1), F32),
            pltpu.VMEM((ts, A_Q), BF16), pltpu.VMEM((ts, GMLP_WIDTH), BF16),
            pltpu.VMEM((ts, C_Q), BF16),
        ],
        name="mix",
        cost_estimate=pl.CostEstimate(
            flops=2 * t * (d * (A_Q + GMLP_WIDTH + C_Q + d) + 4 * ATT_BLOCK * A_Q
                           + GMLP_CHUNK * GMLP_WIDTH + 2 * m_len * C_Q + 3 * d * LANES),
            transcendentals=t * (2 * ATT_BLOCK * N_Q_HEADS + m_len * X_HEADS),
            bytes_accessed=t * (2 * pw + 4 * d + 4 * d + 2 * d + 12 * LANES)),
        compiler_params=pltpu.CompilerParams(
            dimension_semantics=("arbitrary", "arbitrary"), vmem_limit_bytes=VMEM_LIMIT),
    )(sinks, proj, proj, proj, x2d, kc, vc, w_s, bst, woa, wob, woc, wout, gffn, rw, rb)


def _pos_kernel(pstart_ref, idx_ref, rank_ref, pos_ref):
    idx = idx_ref[...]
    pos = rank_ref[...]
    for e in range(N_EXPERTS):
        pos = pos + jnp.where(idx == e, pstart_ref[e], 0)
    pos_ref[...] = pos


def _pos(pstart, idx, rank, tp):
    t = idx.shape[1]
    return pl.pallas_call(
        _pos_kernel,
        grid_spec=pltpu.PrefetchScalarGridSpec(
            num_scalar_prefetch=1,
            grid=(t // tp,),
            in_specs=[pl.BlockSpec((IDX_ROWS, tp), lambda i, ps: (0, i)),
                      pl.BlockSpec((IDX_ROWS, tp), lambda i, ps: (0, i))],
            out_specs=pl.BlockSpec((IDX_ROWS, tp), lambda i, ps: (0, i)),
        ),
        out_shape=jax.ShapeDtypeStruct((IDX_ROWS, t), I32),
        name="slot_pos",
        compiler_params=pltpu.CompilerParams(dimension_semantics=("arbitrary",)),
    )(pstart, idx, rank)


def _sc_mesh():
    return plsc.VectorSubcoreMesh(core_axis_name="core", subcore_axis_name="subcore")


def _sc_scatter_rows(src, idx, n_rows):
    n_chunks, t, w = src.shape
    n_idx = idx.shape[1]
    src_blocks = t // SC_ROWS
    idx_blocks = n_idx // SC_ROWS

    @pl.kernel(out_type=jax.ShapeDtypeStruct((n_chunks * n_rows, w), src.dtype), mesh=_sc_mesh(),
               scratch_types=[], name="sc_dispatch",
               cost_estimate=pl.CostEstimate(
                   flops=0, transcendentals=0,
                   bytes_accessed=n_chunks * n_idx * (8 * w + 4)))
    def scatter(src_hbm, idx_hbm, out_hbm):
        def body(src_vmem, idx_vmem):
            pltpu.sync_copy(src_vmem, out_hbm.at[idx_vmem.at[0]])

        pltpu.emit_pipeline(
            body,
            grid=(n_chunks, idx_blocks),
            in_specs=[pl.BlockSpec((SC_ROWS, w), lambda c, i: (c * src_blocks + i % src_blocks, 0)),
                      pl.BlockSpec((1, SC_ROWS), lambda c, i: (0, c * idx_blocks + i))],
            out_specs=[],
            core_axis_name=("core", "subcore"),
            dimension_semantics=(pltpu.PARALLEL, pltpu.PARALLEL),
        )(src_hbm, idx_hbm)

    return scatter(src.reshape(n_chunks * t, w), idx.reshape(1, n_chunks * n_idx)).reshape(
        n_chunks, n_rows, w)


def _sc_gather_rows(table, idx):
    n_chunks, p, w = table.shape
    n_idx = idx.shape[1]
    idx_blocks = n_idx // SC_ROWS

    @pl.kernel(out_type=jax.ShapeDtypeStruct((n_chunks * n_idx, w), table.dtype), mesh=_sc_mesh(),
               scratch_types=[], name="sc_gather",
               cost_estimate=pl.CostEstimate(
                   flops=0, transcendentals=0,
                   bytes_accessed=n_chunks * n_idx * (8 * w + 4)))
    def gather(table_hbm, idx_hbm, out_hbm):
        def body(idx_vmem, out_vmem):
            pltpu.sync_copy(table_hbm.at[idx_vmem.at[0]], out_vmem)

        pltpu.emit_pipeline(
            body,
            grid=(n_chunks, idx_blocks),
            in_specs=[pl.BlockSpec((1, SC_ROWS), lambda c, i: (0, c * idx_blocks + i))],
            out_specs=[pl.BlockSpec((SC_ROWS, w), lambda c, i: (c * idx_blocks + i, 0))],
            core_axis_name=("core", "subcore"),
            dimension_semantics=(pltpu.PARALLEL, pltpu.PARALLEL),
        )(idx_hbm, out_hbm)

    return gather(table.reshape(n_chunks * p, w), idx.reshape(1, n_chunks * n_idx)).reshape(
        n_chunks, n_idx, w)


def _expert_kernel(bexp_ref, nvalid_ref, next_ref, xs_ref, wgu_hbm, bgu_ref, wd_hbm, bd_ref, ys_ref,
                   wgu_stage, wd_stage, wgu_bf_ref, wd_bf_ref, sems, *, d_exp):
    n_chunks = xs_ref.shape[0]

    def weight_copies(e):
        return (pltpu.make_async_copy(wgu_hbm.at[e], wgu_stage, sems.at[0]),
                pltpu.make_async_copy(wd_hbm.at[e], wd_stage, sems.at[1]))

    def one_block(b, rows):
        e = bexp_ref[b]

        @pl.when(jnp.logical_or(b == 0, e != bexp_ref[jnp.maximum(b - 1, 0)]))
        def _():
            @pl.when(b == 0)
            def _():
                for cp in weight_copies(e):
                    cp.start()

            for cp in weight_copies(e):
                cp.wait()
            wgu_bf_ref[...] = wgu_stage[...].astype(BF16)
            wd_bf_ref[...] = wd_stage[...].astype(BF16)
            e_next = next_ref[e]

            @pl.when(e_next >= 0)
            def _():
                for cp in weight_copies(e_next):
                    cp.start()

        lo, hi = _unpack_bf16_pairs(
            jnp.concatenate([xs_ref[c, rows, :] for c in range(n_chunks)], axis=1))
        xb = jnp.concatenate([lo, hi], axis=1).astype(BF16)
        gu = jnp.dot(xb, wgu_bf_ref[...], preferred_element_type=F32) + bgu_ref[e]
        gate = jnp.minimum(gu[:, :d_exp], SWIGLU_LIMIT)
        up = jnp.clip(gu[:, d_exp:], -SWIGLU_LIMIT, SWIGLU_LIMIT)
        glu = gate * jax.nn.sigmoid(gate * SWIGLU_ALPHA)
        act = ((up + 1.0) * glu).astype(BF16)
        y = jnp.dot(act, wd_bf_ref[...], preferred_element_type=F32) + bd_ref[e]
        y_words = _pack_bf16_pairs(y)
        for c in range(n_chunks):
            ys_ref[c, rows, :] = y_words[:, c * SC_CHUNK:(c + 1) * SC_CHUNK]

    for h in range(EXPERT_BLOCKS_PER_STEP):
        b = pl.program_id(0) * EXPERT_BLOCKS_PER_STEP + h

        @pl.when(b < nvalid_ref[0])
        def _():
            one_block(b, slice(h * MOE_BLOCK, (h + 1) * MOE_BLOCK))


def _experts(bexp, nvalid, next_expert, xs, wgu, bgu, wd, bd):
    n_chunks, n_rows, _ = xs.shape
    n_exp, d, d_exp2 = wgu.shape
    d_exp = d_exp2 // 2
    step_rows = EXPERT_BLOCKS_PER_STEP * MOE_BLOCK
    blk = lambda s, be, nv, nx: jnp.minimum(s, (nv[0] - 1) // EXPERT_BLOCKS_PER_STEP)
    return pl.pallas_call(
        functools.partial(_expert_kernel, d_exp=d_exp),
        grid_spec=pltpu.PrefetchScalarGridSpec(
            num_scalar_prefetch=3,
            grid=(n_rows // step_rows,),
            in_specs=[
                pl.BlockSpec((n_chunks, step_rows, SC_CHUNK),
                             lambda s, be, nv, nx: (0, blk(s, be, nv, nx), 0)),
                pl.BlockSpec(memory_space=pl.ANY),
                pl.BlockSpec((n_exp, 1, 2 * d_exp), lambda s, be, nv, nx: (0, 0, 0)),
                pl.BlockSpec(memory_space=pl.ANY),
                pl.BlockSpec((n_exp, 1, d), lambda s, be, nv, nx: (0, 0, 0)),
            ],
            out_specs=pl.BlockSpec((n_chunks, step_rows, SC_CHUNK),
                                   lambda s, be, nv, nx: (0, blk(s, be, nv, nx), 0)),
            scratch_shapes=[pltpu.VMEM((d, 2 * d_exp), F32), pltpu.VMEM((d_exp, d), F32),
                            pltpu.VMEM((d, 2 * d_exp), BF16), pltpu.VMEM((d_exp, d), BF16),
                            pltpu.SemaphoreType.DMA((2,))],
        ),
        out_shape=jax.ShapeDtypeStruct((n_chunks, n_rows, SC_CHUNK), U32),
        name="experts",
        cost_estimate=pl.CostEstimate(
            flops=6 * n_rows * d * d_exp, transcendentals=n_rows * d_exp,
            bytes_accessed=4 * n_rows * d + 4 * N_EXPERTS * 3 * d * d_exp),
        compiler_params=pltpu.CompilerParams(
            dimension_semantics=("arbitrary",), vmem_limit_bytes=VMEM_LIMIT),
    )(bexp, nvalid, next_expert, xs, wgu, bgu, wd, bd)


def _combine_kernel(prev_ref, x1_ref, gate_ref, yg_ref, out_ref):
    del prev_ref
    n_chunks = yg_ref.shape[0]
    half = n_chunks * SC_CHUNK
    for c in range(n_chunks):
        sl_lo = slice(c * SC_CHUNK, (c + 1) * SC_CHUNK)
        sl_hi = slice(half + c * SC_CHUNK, half + (c + 1) * SC_CHUNK)
        acc_lo = x1_ref[:, sl_lo]
        acc_hi = x1_ref[:, sl_hi]
        for k in range(TOP_K):
            lo, hi = _unpack_bf16_pairs(yg_ref[c, k])
            g = gate_ref[:, k:k + 1]
            acc_lo = acc_lo + g * lo
            acc_hi = acc_hi + g * hi
        out_ref[:, sl_lo] = acc_lo
        out_ref[:, sl_hi] = acc_hi


def _combine(x1, gates, yg, out_prev, tcb, row0, t_total):
    t, d = x1.shape
    n_chunks = yg.shape[0]
    blk0 = row0 // tcb
    in_specs = [
        pl.BlockSpec((tcb, d), lambda i: (i, 0)),
        pl.BlockSpec((tcb, LANES), lambda i: (i, 0)),
        pl.BlockSpec((n_chunks, TOP_K, tcb, SC_CHUNK), lambda i: (0, 0, i, 0)),
    ]
    args = [x1, gates, yg]
    aliases = {}
    body = functools.partial(_combine_kernel, None)
    if out_prev is not None:
        in_specs.append(pl.BlockSpec(memory_space=pl.ANY))
        args.append(out_prev)
        aliases = {3: 0}
        body = lambda a, b, c, prev, o: _combine_kernel(prev, a, b, c, o)
    return pl.pallas_call(
        body,
        grid=(t // tcb,),
        in_specs=in_specs,
        out_specs=pl.BlockSpec((tcb, d), lambda i: (blk0 + i, 0)),
        out_shape=jax.ShapeDtypeStruct((t_total, d), F32),
        input_output_aliases=aliases,
        name="combine",
        cost_estimate=pl.CostEstimate(
            flops=2 * TOP_K * t * d, transcendentals=0,
            bytes_accessed=t * (4 * d + 4 * d + 2 * TOP_K * d + 4 * LANES)),
        compiler_params=pltpu.CompilerParams(dimension_semantics=("arbitrary",)),
    )(*args)


def _pick_tile(n, pref):
    t = min(pref, n)
    while n % t:
        t //= 2
    return t


def kernel(x, mem, positions, attn_norm_g, mem_norm_g, w_in, b_gates, a_q_norm_g, a_k_norm_g,
           a_sinks, w_o_a, gmlp_ln_g, gmlp_ln_b, gmlp_w_s, gmlp_b_s, w_o_b, w_mem_kv,
           c_q_norm_g, c_k_norm_g, w_o_c, w_out, ffn_norm_g, router_w, router_b,
           w_gate_up, b_gate_up, w_down, b_down):
    n_batch, seq, d = x.shape
    m_len = mem.shape[1]
    depth = w_in.shape[0]
    t = n_batch * seq
    n_groups = N_TOKEN_GROUPS if n_batch % N_TOKEN_GROUPS == 0 else 1
    gb = n_batch // n_groups
    tg = gb * seq
    n_asg = tg * TOP_K
    nb = -(-n_asg // MOE_BLOCK) + N_EXPERTS
    nb = -(-nb // EXPERT_BLOCKS_PER_STEP) * EXPERT_BLOCKS_PER_STEP
    n_rows = nb * MOE_BLOCK

    inv_freq = ROPE_THETA ** (-jnp.arange(0, HEAD_DIM, 2, dtype=F32) / HEAD_DIM)
    invf = inv_freq[:, None]
    sgn = jnp.tile(jnp.concatenate([-jnp.ones((HEAD_DIM // 2,), F32),
                                    jnp.ones((HEAD_DIM // 2,), F32)]), LANES // HEAD_DIM)[None, :]
    pos2d = positions.reshape(t, 1).astype(I32)
    mem2d = mem.reshape(n_batch * m_len, d)
    x2d = x.reshape(t, d)

    tm = _pick_tile(tg, 512)
    ts = _pick_tile(seq, 512)
    tp = _pick_tile(tg, 8192)
    tcb = _pick_tile(tg, 512)

    for l in range(depth):
        kc, vc = _mem_kv(mem2d, mem_norm_g[l][None, :], w_mem_kv[l].astype(BF16),
                         c_k_norm_g[l][None, :], n_batch, m_len)
        w_in_bf = w_in[l].astype(BF16)
        mix_w = (w_o_a[l].astype(BF16), w_o_b[l].astype(BF16), w_o_c[l].astype(BF16),
                 w_out[l].astype(BF16))
        gq = jnp.tile(a_q_norm_g[l], LANES // HEAD_DIM)[None, :]
        gk = jnp.tile(a_k_norm_g[l], LANES // HEAD_DIM)[None, :]
        rw32 = jnp.pad(router_w[l], ((0, 0), (0, LANES - N_EXPERTS)))
        rw_hi = rw32.astype(BF16)
        rw = jnp.concatenate([rw_hi, (rw32 - rw_hi.astype(F32)).astype(BF16)], axis=1)
        rb = jnp.pad(router_b[l], (0, LANES - N_EXPERTS))[None, :]

        routed = []
        scatter_idx = None
        for g in range(n_groups):
            proj = _proj(
                x2d, pos2d, attn_norm_g[l][None, :], w_in_bf, invf, sgn, gq, gk,
                c_q_norm_g[l][None, :], gmlp_ln_g[l][None, :], gmlp_ln_b[l][None, :],
                b_gates[l].reshape(1, 3 * d), tm, g * tg, tg, scatter_idx)
            x1, h2, idx, rank, gates, counts = _mix(
                a_sinks[l], proj, x2d, kc, vc, gmlp_w_s[l], gmlp_b_s[l].T, *mix_w,
                ffn_norm_g[l][None, :], rw, rb, gb, seq, m_len, ts, g * gb)

            cnt = counts[:, 0].astype(I32)
            padded = (cnt + MOE_BLOCK - 1) // MOE_BLOCK * MOE_BLOCK
            pend = jnp.cumsum(padded)
            pstart = (pend - padded).astype(I32)
            nvalid = (pend[-1:] // MOE_BLOCK).astype(I32)
            blk_row = jnp.arange(nb, dtype=I32) * MOE_BLOCK
            bexp = jnp.minimum(jnp.sum((pend[None, :] <= blk_row[:, None]).astype(I32), axis=1),
                               N_EXPERTS - 1).astype(I32)
            eid = jnp.arange(N_EXPERTS, dtype=I32)
            later = jnp.logical_and(eid[None, :] > eid[:, None], (cnt > 0)[None, :])
            next_expert = jnp.min(jnp.where(later, eid[None, :], N_EXPERTS), axis=1)
            next_expert = jnp.where(next_expert == N_EXPERTS, -1, next_expert).astype(I32)
            j = jnp.arange(MOE_BLOCK, dtype=I32)[None, :]
            fill = jnp.where(j < (padded - cnt)[:, None], pend[:, None] - 1 - j,
                             n_rows - 1 - j).astype(I32)

            pos = _pos(pstart, idx, rank, tp)
            pos_km = pos[:TOP_K].reshape(n_asg)
            n_chunks = h2.shape[0]
            chunk_off = (jnp.arange(n_chunks, dtype=I32) * n_rows)[:, None]
            scatter_idx = jnp.concatenate([pos_km, fill.reshape(-1)])[None, :] + chunk_off
            xs = _sc_scatter_rows(h2, scatter_idx, n_rows)
            routed.append((x1, gates, xs, bexp, nvalid, next_expert, pos_km, chunk_off))

        gathered = []
        for x1, gates, xs, bexp, nvalid, next_expert, pos_km, chunk_off in routed:
            ys = _experts(bexp, nvalid, next_expert, xs, w_gate_up[l], b_gate_up[l][:, None, :],
                          w_down[l], b_down[l][:, None, :])
            yg = _sc_gather_rows(ys, pos_km[None, :] + chunk_off)
            gathered.append((x1, gates, yg.reshape(yg.shape[0], TOP_K, tg, SC_CHUNK)))

        out = None
        for g, (x1, gates, yg) in enumerate(gathered):
            out = _combine(x1, gates, yg, out, tcb, g * tg, t)
        x2d = out
    return x2d.reshape(n_batch, seq, d)
```

```python
import functools

import numpy as np
import jax
import jax.numpy as jnp
from jax import lax
from jax.experimental import pallas as pl
from jax.experimental.pallas import tpu as pltpu
from jax.experimental.pallas import tpu_sc as plsc

F32 = jnp.float32
BF16 = jnp.bfloat16
I32 = jnp.int32
U32 = jnp.uint32
HI16 = np.uint32(0xFFFF0000)
LOG2_E = float(np.log2(np.e))

EPS = 1e-6
LANES = 128
HEAD_DIM = 64
N_Q_HEADS = 16
N_KV_HEADS = 2
ATT_BLOCK = 128
ROPE_THETA = 10000.0
GMLP_WIDTH = 512
GMLP_GROUPS = 4
GMLP_CHUNK = 128
X_HEADS = 4
X_HEAD_DIM = 128
N_EXPERTS = 32
TOP_K = 4
SWIGLU_LIMIT = 7.0
SWIGLU_ALPHA = 1.702
MOE_BLOCK = 512
SC_ROWS = 128
SC_CHUNK = 256
EXPERT_BLOCKS_PER_STEP = 2
IDX_ROWS = 8

A_Q = N_Q_HEADS * HEAD_DIM
A_KV = N_KV_HEADS * HEAD_DIM
C_Q = X_HEADS * X_HEAD_DIM

OFF_Q = 0
OFF_K4 = OFF_Q + A_Q
OFF_V4 = OFF_K4 + 4 * LANES
OFF_U = OFF_V4 + 4 * LANES
OFF_VN = OFF_U + GMLP_WIDTH
OFF_QC = OFF_VN + GMLP_WIDTH
OFF_GA = OFF_QC + C_Q
PROJ_W_BASE = OFF_GA

VMEM_LIMIT = 56 * 1024 * 1024


def _lane_iota(shape):
    return lax.broadcasted_iota(I32, shape, len(shape) - 1)


def _rms(x, g):
    return x * lax.rsqrt(jnp.mean(x * x, axis=-1, keepdims=True) + EPS) * g


def _pack_bf16_pairs(x):
    n = x.shape[1] // 2
    bits = pltpu.bitcast(x.astype(BF16).astype(F32), U32)
    return (bits[:, :n] >> 16) | (bits[:, n:] & HI16)


def _unpack_bf16_pairs(w):
    return pltpu.bitcast(w << 16, F32), pltpu.bitcast(w & HI16, F32)


def _gelu(x):
    return 0.5 * x * (1.0 + lax.erf(x * np.float32(np.sqrt(0.5))))


def _memkv_kernel(mem_ref, g_ref, w_ref, gk_ref, kc_ref, vc_ref):
    h = _rms(mem_ref[...], g_ref[...]).astype(BF16)
    kv = jnp.dot(h, w_ref[...], preferred_element_type=F32)
    for hh in range(X_HEADS):
        sl = slice(hh * X_HEAD_DIM, (hh + 1) * X_HEAD_DIM)
        kc_ref[:, sl] = _rms(kv[:, sl], gk_ref[...]).astype(BF16)
    vc_ref[...] = kv[:, C_Q:].astype(BF16)


def _mem_kv(mem2d, g, w_bf, gk, n_batch, m_len):
    d = mem2d.shape[1]
    return pl.pallas_call(
        _memkv_kernel,
        grid=(n_batch,),
        in_specs=[
            pl.BlockSpec((m_len, d), lambda b: (b, 0)),
            pl.BlockSpec((1, d), lambda b: (0, 0)),
            pl.BlockSpec((d, 2 * C_Q), lambda b: (0, 0)),
            pl.BlockSpec((1, X_HEAD_DIM), lambda b: (0, 0)),
        ],
        out_specs=[
            pl.BlockSpec((m_len, C_Q), lambda b: (b, 0)),
            pl.BlockSpec((m_len, C_Q), lambda b: (b, 0)),
        ],
        out_shape=[jax.ShapeDtypeStruct((n_batch * m_len, C_Q), BF16)] * 2,
        name="mem_kv",
        compiler_params=pltpu.CompilerParams(dimension_semantics=("arbitrary",)),
    )(mem2d, g, w_bf, gk)


def _proj_kernel(x_ref, pos_ref, g_ref, w_ref, invf_ref, sgn_ref, gq_ref, gk_ref, gcq_ref,
                 lng_ref, lnb_ref, bg_ref, out_ref, *, d_model):
    tm = x_ref.shape[0]
    h = _rms(x_ref[...], g_ref[...]).astype(BF16)

    ang_t = invf_ref[...] * pos_ref[...].astype(F32)
    reps = LANES // (HEAD_DIM // 2)
    cosv = jnp.transpose(jnp.concatenate([jnp.cos(ang_t)] * reps, axis=0))
    sinv = jnp.transpose(jnp.concatenate([jnp.sin(ang_t)] * reps, axis=0)) * sgn_ref[...]
    lane = _lane_iota((tm, LANES))
    first_head = lane < HEAD_DIM
    lo_half = (lane % HEAD_DIM) < (HEAD_DIM // 2)

    def head_norm_rope(blk, g):
        y = blk * blk
        s_lo = jnp.sum(jnp.where(first_head, y, 0.0), axis=-1, keepdims=True)
        s_hi = jnp.sum(jnp.where(first_head, 0.0, y), axis=-1, keepdims=True)
        ss = jnp.where(first_head, s_lo, s_hi)
        n = blk * lax.rsqrt(ss * (1.0 / HEAD_DIM) + EPS) * g
        rot = jnp.where(lo_half, pltpu.roll(n, LANES - HEAD_DIM // 2, 1),
                        pltpu.roll(n, HEAD_DIM // 2, 1))
        return n * cosv + rot * sinv

    def proj(a, b):
        return jnp.dot(h, w_ref[:, a:b], preferred_element_type=F32)

    pq = proj(0, A_Q)
    for c in range(A_Q // LANES):
        sl = slice(c * LANES, (c + 1) * LANES)
        out_ref[:, OFF_Q + c * LANES:OFF_Q + (c + 1) * LANES] = (
            head_norm_rope(pq[:, sl], gq_ref[...]) * (HEAD_DIM ** -0.5 * LOG2_E)).astype(BF16)

    pkv = proj(A_Q, A_Q + 2 * A_KV)
    kn = head_norm_rope(pkv[:, :LANES], gk_ref[...])
    vv = pkv[:, LANES:]
    for off, t in ((OFF_K4, kn), (OFF_V4, vv)):
        tr = pltpu.roll(t, HEAD_DIM, 1)
        parts = (jnp.where(first_head, t, 0.0), jnp.where(first_head, 0.0, tr),
                 jnp.where(first_head, tr, 0.0), jnp.where(first_head, 0.0, t))
        for j, p in enumerate(parts):
            out_ref[:, off + j * LANES:off + (j + 1) * LANES] = p.astype(BF16)

    o0 = A_Q + 2 * A_KV
    out_ref[:, OFF_U:OFF_U + GMLP_WIDTH] = _gelu(proj(o0, o0 + GMLP_WIDTH)).astype(BF16)
    gv = _gelu(proj(o0 + GMLP_WIDTH, o0 + 2 * GMLP_WIDTH))
    mu = jnp.mean(gv, axis=-1, keepdims=True)
    var = jnp.mean(jnp.square(gv - mu), axis=-1, keepdims=True)
    out_ref[:, OFF_VN:OFF_VN + GMLP_WIDTH] = (
        (gv - mu) * lax.rsqrt(var + EPS) * lng_ref[...] + lnb_ref[...]).astype(BF16)

    o1 = o0 + 2 * GMLP_WIDTH
    pc = proj(o1, o1 + C_Q)
    for hh in range(X_HEADS):
        sl = slice(hh * X_HEAD_DIM, (hh + 1) * X_HEAD_DIM)
        out_ref[:, OFF_QC + hh * X_HEAD_DIM:OFF_QC + (hh + 1) * X_HEAD_DIM] = (
            _rms(pc[:, sl], gcq_ref[...]) * (X_HEAD_DIM ** -0.5 * LOG2_E)).astype(BF16)

    o2 = o1 + C_Q
    for j in range(3):
        sl = slice(j * d_model, (j + 1) * d_model)
        z = proj(o2 + j * d_model, o2 + (j + 1) * d_model) + bg_ref[:, sl]
        out_ref[:, OFF_GA + j * d_model:OFF_GA + (j + 1) * d_model] = (
            0.5 * jnp.tanh(0.5 * z) + 0.5).astype(BF16)


def _proj_after_kernel(after_ref, *refs, d_model):
    del after_ref
    _proj_kernel(*refs, d_model=d_model)


def _proj(x2d, pos2d, g, w_bf, invf, sgn, gq, gk, gcq, lng, lnb, bg, tm, row0, t, after):
    d = x2d.shape[1]
    d_in = w_bf.shape[1]
    pw = PROJ_W_BASE + 3 * d
    blk0 = row0 // tm
    full = lambda shape: pl.BlockSpec(shape, lambda i: (0,) * len(shape))
    body = functools.partial(_proj_kernel, d_model=d)
    lead_specs, lead_args = [], []
    if after is not None:
        body = functools.partial(_proj_after_kernel, d_model=d)
        lead_specs, lead_args = [pl.BlockSpec(memory_space=pl.ANY)], [after]
    return pl.pallas_call(
        body,
        grid=(t // tm,),
        in_specs=lead_specs + [
            pl.BlockSpec((tm, d), lambda i: (blk0 + i, 0)),
            pl.BlockSpec((None, 1, tm), lambda i: (blk0 + i, 0, 0)),
            full((1, d)), full((d, d_in)), full((HEAD_DIM // 2, 1)), full((1, LANES)),
            full((1, LANES)), full((1, LANES)), full((1, X_HEAD_DIM)),
            full((1, GMLP_WIDTH)), full((1, GMLP_WIDTH)), full((1, 3 * d)),
        ],
        out_specs=pl.BlockSpec((tm, pw), lambda i: (i, 0)),
        out_shape=jax.ShapeDtypeStruct((t, pw), BF16),
        name="proj",
        cost_estimate=pl.CostEstimate(
            flops=2 * t * d * d_in, transcendentals=t * (3 * d + 2 * GMLP_WIDTH + 2 * LANES),
            bytes_accessed=4 * t * d + 2 * t * pw + 2 * d * d_in),
        compiler_params=pltpu.CompilerParams(
            dimension_semantics=("arbitrary",), vmem_limit_bytes=VMEM_LIMIT),
    )(*lead_args, x2d, pos2d.reshape(-1, 1, tm), g, w_bf, invf, sgn, gq, gk, gcq, lng, lnb, bg)


def _mix_kernel(sinks_ref, proj_ref, kprev_ref, vprev_ref, x_ref, kc_ref, vc_ref, ws_ref, bst_ref,
                woa_ref, wob_ref, woc_ref, wout_ref, gffn_ref, rw_ref, rb_ref,
                x1_ref, h2_ref, idx_ref, rank_ref, gate_ref, cnt_ref,
                run_ref, oa_ref, ob_ref, oc_ref, *, d_model):
    ts = x_ref.shape[0]
    s_idx = pl.program_id(1)
    first_step = jnp.logical_and(pl.program_id(0) == 0, s_idx == 0)
    neg_inf = float("-inf")

    cpk = A_Q // LANES // N_KV_HEADS
    stack = cpk * ATT_BLOCK
    srow = lax.broadcasted_iota(I32, (stack, 2 * ATT_BLOCK), 0)
    qi = srow % ATT_BLOCK
    kj = lax.broadcasted_iota(I32, (stack, 2 * ATT_BLOCK), 1)
    band = jnp.logical_and(kj <= ATT_BLOCK + qi, kj > qi)
    chunk_of_row = lax.broadcasted_iota(I32, (stack, 1), 0) // ATT_BLOCK
    for qb in range(ts // ATT_BLOCK):
        r0 = qb * ATT_BLOCK
        rows = slice(r0, r0 + ATT_BLOCK)
        if qb == 0:
            kp, vp = kprev_ref[...], vprev_ref[...]
            mask = jnp.logical_and(band, jnp.logical_or(kj >= ATT_BLOCK, s_idx > 0))
        else:
            prow = slice(r0 - ATT_BLOCK, r0)
            kp = proj_ref[prow, OFF_K4:OFF_K4 + 4 * LANES]
            vp = proj_ref[prow, OFF_V4:OFF_V4 + 4 * LANES]
            mask = band
        k4 = jnp.concatenate([kp, proj_ref[rows, OFF_K4:OFF_K4 + 4 * LANES]], axis=0)
        v4 = jnp.concatenate([vp, proj_ref[rows, OFF_V4:OFF_V4 + 4 * LANES]], axis=0)
        for kvh in range(N_KV_HEADS):
            c0 = kvh * cpk
            q4 = jnp.concatenate(
                [proj_ref[rows, OFF_Q + (c0 + c) * LANES:OFF_Q + (c0 + c + 1) * LANES]
                 for c in range(cpk)], axis=0)
            o = jnp.zeros((stack, LANES), F32)
            for half in range(2):
                col = slice((2 * kvh + half) * LANES, (2 * kvh + half + 1) * LANES)
                s = lax.dot_general(q4, k4[:, col], (((1,), (1,)), ((), ())),
                                    preferred_element_type=F32)
                s = jnp.where(mask, s, neg_inf)
                sink = jnp.zeros((stack, 1), F32)
                for c in range(cpk):
                    sink = jnp.where(chunk_of_row == c,
                                     sinks_ref[2 * (c0 + c) + half] * LOG2_E, sink)
                m = jnp.maximum(jnp.max(s, axis=-1, keepdims=True), sink)
                p = jnp.exp2(s - m)
                den = jnp.sum(p, axis=-1, keepdims=True) + jnp.exp2(sink - m)
                o = o + jnp.dot(p.astype(BF16), v4[:, col], preferred_element_type=F32) / den
            for c in range(cpk):
                oa_ref[rows, (c0 + c) * LANES:(c0 + c + 1) * LANES] = (
                    o[c * ATT_BLOCK:(c + 1) * ATT_BLOCK].astype(BF16))

    ti = lax.broadcasted_iota(I32, (GMLP_CHUNK, GMLP_CHUNK), 0)
    si = lax.broadcasted_iota(I32, (GMLP_CHUNK, GMLP_CHUNK), 1)
    for g in range(GMLP_GROUPS):
        wt = jnp.where(si <= ti, ws_ref[g], 0.0).astype(BF16)
        bcol = bst_ref[:, g:g + 1]
        for ch in range(ts // GMLP_CHUNK):
            rows = slice(ch * GMLP_CHUNK, (ch + 1) * GMLP_CHUNK)
            vn = proj_ref[rows, OFF_VN + g * LANES:OFF_VN + (g + 1) * LANES]
            u = proj_ref[rows, OFF_U + g * LANES:OFF_U + (g + 1) * LANES].astype(F32)
            mixed = jnp.dot(wt, vn, preferred_element_type=F32) + bcol
            ob_ref[rows, g * LANES:(g + 1) * LANES] = (u * mixed).astype(BF16)

    for hh in range(X_HEADS):
        sl = slice(hh * X_HEAD_DIM, (hh + 1) * X_HEAD_DIM)
        qc = proj_ref[:, OFF_QC + hh * X_HEAD_DIM:OFF_QC + (hh + 1) * X_HEAD_DIM]
        s = lax.dot_general(qc, kc_ref[:, sl], (((1,), (1,)), ((), ())),
                            preferred_element_type=F32)
        p = jnp.exp2(s - jnp.max(s, axis=-1, keepdims=True))
        den = jnp.sum(p, axis=-1, keepdims=True)
        oc_ref[:, sl] = (jnp.dot(p.astype(BF16), vc_ref[:, sl],
                                 preferred_element_type=F32) / den).astype(BF16)

    def gate(j):
        return proj_ref[:, OFF_GA + j * d_model:OFF_GA + (j + 1) * d_model].astype(F32)

    merged = gate(0) * jnp.dot(oa_ref[...], woa_ref[...], preferred_element_type=F32)
    merged = merged + gate(1) * jnp.dot(ob_ref[...], wob_ref[...], preferred_element_type=F32)
    merged = merged + gate(2) * jnp.dot(oc_ref[...], woc_ref[...], preferred_element_type=F32)
    x1 = x_ref[...] + jnp.dot(merged.astype(BF16), wout_ref[...], preferred_element_type=F32)
    x1_ref[...] = x1

    h2 = _rms(x1, gffn_ref[...])
    h2_words = _pack_bf16_pairs(h2)
    for c in range(h2_ref.shape[0]):
        h2_ref[c] = h2_words[:, c * SC_CHUNK:(c + 1) * SC_CHUNK]
    h2_hi = h2.astype(BF16)
    h2_lo = (h2 - h2_hi.astype(F32)).astype(BF16)
    part = jnp.dot(h2_hi, rw_ref[...], preferred_element_type=F32)
    logits = (part[:, :LANES] + part[:, LANES:]
              + jnp.dot(h2_lo, rw_ref[:, :LANES], preferred_element_type=F32) + rb_ref[...])
    lt = jnp.transpose(logits)[:N_EXPERTS]
    erow = lax.broadcasted_iota(I32, (N_EXPERTS, ts), 0)
    vals, idxs = [], []
    for _ in range(TOP_K):
        m = jnp.max(lt, axis=0, keepdims=True)
        i = jnp.min(jnp.where(lt == m, erow, N_EXPERTS), axis=0, keepdims=True)
        vals.append(m)
        idxs.append(i)
        lt = jnp.where(erow == i, neg_inf, lt)
    es = [jnp.exp(v - vals[0]) for v in vals]
    den = es[0] + es[1] + es[2] + es[3]

    @pl.when(first_step)
    def _():
        run_ref[...] = jnp.zeros_like(run_ref)

    hot = [erow == i for i in idxs]
    multihot = jnp.where(jnp.logical_or(jnp.logical_or(hot[0], hot[1]),
                                        jnp.logical_or(hot[2], hot[3])), 1.0, 0.0)
    tr = lax.broadcasted_iota(I32, (ts, ts), 0)
    tc = lax.broadcasted_iota(I32, (ts, ts), 1)
    earlier = jnp.where(tr < tc, 1.0, 0.0).astype(BF16)
    before = jnp.dot(multihot.astype(BF16), earlier, preferred_element_type=F32) + run_ref[...]
    krow = lax.broadcasted_iota(I32, (IDX_ROWS, ts), 0)
    idx_out = jnp.zeros((IDX_ROWS, ts), I32)
    rank_out = jnp.zeros((IDX_ROWS, ts), I32)
    gate_rows = jnp.zeros((IDX_ROWS, ts), F32)
    for k in range(TOP_K):
        rk = jnp.sum(jnp.where(hot[k], before, 0.0), axis=0, keepdims=True)
        idx_out = jnp.where(krow == k, idxs[k], idx_out)
        rank_out = jnp.where(krow == k, rk.astype(I32), rank_out)
        gate_rows = jnp.where(krow == k, es[k] / den, gate_rows)
    idx_ref[...] = idx_out
    rank_ref[...] = rank_out
    gate_ref[...] = jnp.transpose(jnp.concatenate(
        [gate_rows, jnp.zeros((LANES - IDX_ROWS, ts), F32)], axis=0))
    run_ref[...] = run_ref[...] + jnp.sum(multihot, axis=1, keepdims=True)
    cnt_ref[...] = run_ref[...]


def _mix(sinks, proj, x2d, kc, vc, w_s, bst, woa, wob, woc, wout, gffn, rw, rb,
         n_batch, seq, m_len, ts, batch0):
    d = x2d.shape[1]
    t = n_batch * seq
    pw = proj.shape[1]
    ns = seq // ts
    nblk = seq // ATT_BLOCK
    per = ts // ATT_BLOCK
    n_chunks = d // (2 * SC_CHUNK)
    full = lambda shape: pl.BlockSpec(shape, lambda b, s: (0,) * len(shape))
    row = lambda width: pl.BlockSpec((ts, width), lambda b, s: (b * ns + s, 0))
    prev = lambda colblk: pl.BlockSpec(
        (ATT_BLOCK, 4 * LANES), lambda b, s: (b * nblk + jnp.maximum(s * per - 1, 0), colblk))
    return pl.pallas_call(
        functools.partial(_mix_kernel, d_model=d),
        grid=(n_batch, ns),
        in_specs=[
            pl.BlockSpec(memory_space=pltpu.SMEM),
            row(pw), prev(OFF_K4 // (4 * LANES)), prev(OFF_V4 // (4 * LANES)),
            pl.BlockSpec((ts, d), lambda b, s: ((batch0 + b) * ns + s, 0)),
            pl.BlockSpec((m_len, C_Q), lambda b, s: (batch0 + b, 0)),
            pl.BlockSpec((m_len, C_Q), lambda b, s: (batch0 + b, 0)),
            full((GMLP_GROUPS, GMLP_CHUNK, GMLP_CHUNK)), full((GMLP_CHUNK, GMLP_GROUPS)),
            full((A_Q, d)), full((GMLP_WIDTH, d)), full((C_Q, d)), full((d, d)),
            full((1, d)), full((d, 2 * LANES)), full((1, LANES)),
        ],
        out_specs=[row(d),
                   pl.BlockSpec((n_chunks, ts, SC_CHUNK), lambda b, s: (0, b * ns + s, 0)),
                   pl.BlockSpec((IDX_ROWS, ts), lambda b, s: (0, b * ns + s)),
                   pl.BlockSpec((IDX_ROWS, ts), lambda b, s: (0, b * ns + s)),
                   row(LANES), full((N_EXPERTS, 1))],
        out_shape=[
            jax.ShapeDtypeStruct((t, d), F32),
            jax.ShapeDtypeStruct((n_chunks, t, SC_CHUNK), U32),
            jax.ShapeDtypeStruct((IDX_ROWS, t), I32), jax.ShapeDtypeStruct((IDX_ROWS, t), I32),
            jax.ShapeDtypeStruct((t, LANES), F32), jax.ShapeDtypeStruct((N_EXPERTS, 1), F32),
        ],
        scratch_shapes=[
            pltpu.VMEM((N_EXPERTS, 1), F32),
            pltpu.VMEM((ts, A_Q), BF16), pltpu.VMEM((ts, GMLP_WIDTH), BF16),
            pltpu.VMEM((ts, C_Q), BF16),
        ],
        name="mix",
        cost_estimate=pl.CostEstimate(
            flops=2 * t * (d * (A_Q + GMLP_WIDTH + C_Q + d) + 4 * ATT_BLOCK * A_Q
                           + GMLP_CHUNK * GMLP_WIDTH + 2 * m_len * C_Q + 3 * d * LANES),
            transcendentals=t * (2 * ATT_BLOCK * N_Q_HEADS + m_len * X_HEADS),
            bytes_accessed=t * (2 * pw + 4 * d + 4 * d + 2 * d + 12 * LANES)),
        compiler_params=pltpu.CompilerParams(
            dimension_semantics=("arbitrary", "arbitrary"), vmem_limit_bytes=VMEM_LIMIT),
    )(sinks, proj, proj, proj, x2d, kc, vc, w_s, bst, woa, wob, woc, wout, gffn, rw, rb)


def _pos_kernel(pstart_ref, idx_ref, rank_ref, pos_ref):
    idx = idx_ref[...]
    pos = rank_ref[...]
    for e in range(N_EXPERTS):
        pos = pos + jnp.where(idx == e, pstart_ref[e], 0)
    pos_ref[...] = pos


def _pos(pstart, idx, rank, tp):
    t = idx.shape[1]
    return pl.pallas_call(
        _pos_kernel,
        grid_spec=pltpu.PrefetchScalarGridSpec(
            num_scalar_prefetch=1,
            grid=(t // tp,),
            in_specs=[pl.BlockSpec((IDX_ROWS, tp), lambda i, ps: (0, i)),
                      pl.BlockSpec((IDX_ROWS, tp), lambda i, ps: (0, i))],
            out_specs=pl.BlockSpec((IDX_ROWS, tp), lambda i, ps: (0, i)),
        ),
        out_shape=jax.ShapeDtypeStruct((IDX_ROWS, t), I32),
        name="slot_pos",
        compiler_params=pltpu.CompilerParams(dimension_semantics=("arbitrary",)),
    )(pstart, idx, rank)


def _sc_mesh():
    return plsc.VectorSubcoreMesh(core_axis_name="core", subcore_axis_name="subcore")


def _sc_scatter_rows(src, idx, n_rows):
    n_chunks, t, w = src.shape
    n_idx = idx.shape[1]
    src_blocks = t // SC_ROWS
    idx_blocks = n_idx // SC_ROWS

    @pl.kernel(out_type=jax.ShapeDtypeStruct((n_chunks * n_rows, w), src.dtype), mesh=_sc_mesh(),
               scratch_types=[], name="sc_dispatch",
               cost_estimate=pl.CostEstimate(
                   flops=0, transcendentals=0,
                   bytes_accessed=n_chunks * n_idx * (8 * w + 4)))
    def scatter(src_hbm, idx_hbm, out_hbm):
        def body(src_vmem, idx_vmem):
            pltpu.sync_copy(src_vmem, out_hbm.at[idx_vmem.at[0]])

        pltpu.emit_pipeline(
            body,
            grid=(n_chunks, idx_blocks),
            in_specs=[pl.BlockSpec((SC_ROWS, w), lambda c, i: (c * src_blocks + i % src_blocks, 0)),
                      pl.BlockSpec((1, SC_ROWS), lambda c, i: (0, c * idx_blocks + i))],
            out_specs=[],
            core_axis_name=("core", "subcore"),
            dimension_semantics=(pltpu.PARALLEL, pltpu.PARALLEL),
        )(src_hbm, idx_hbm)

    return scatter(src.reshape(n_chunks * t, w), idx.reshape(1, n_chunks * n_idx)).reshape(
        n_chunks, n_rows, w)


def _sc_gather_rows(table, idx):
    n_chunks, p, w = table.shape
    n_idx = idx.shape[1]
    idx_blocks = n_idx // SC_ROWS

    @pl.kernel(out_type=jax.ShapeDtypeStruct((n_chunks * n_idx, w), table.dtype), mesh=_sc_mesh(),
               scratch_types=[], name="sc_gather",
               cost_estimate=pl.CostEstimate(
                   flops=0, transcendentals=0,
                   bytes_accessed=n_chunks * n_idx * (8 * w + 4)))
    def gather(table_hbm, idx_hbm, out_hbm):
        def body(idx_vmem, out_vmem):
            pltpu.sync_copy(table_hbm.at[idx_vmem.at[0]], out_vmem)

        pltpu.emit_pipeline(
            body,
            grid=(n_chunks, idx_blocks),
            in_specs=[pl.BlockSpec((1, SC_ROWS), lambda c, i: (0, c * idx_blocks + i))],
            out_specs=[pl.BlockSpec((SC_ROWS, w), lambda c, i: (c * idx_blocks + i, 0))],
            core_axis_name=("core", "subcore"),
            dimension_semantics=(pltpu.PARALLEL, pltpu.PARALLEL),
        )(idx_hbm, out_hbm)

    return gather(table.reshape(n_chunks * p, w), idx.reshape(1, n_chunks * n_idx)).reshape(
        n_chunks, n_idx, w)


def _expert_kernel(bexp_ref, nvalid_ref, next_ref, xs_ref, wgu_hbm, bgu_ref, wd_hbm, bd_ref, ys_ref,
                   wgu_stage, wd_stage, wgu_bf_ref, wd_bf_ref, sems, *, d_exp):
    n_chunks = xs_ref.shape[0]

    def weight_copies(e):
        return (pltpu.make_async_copy(wgu_hbm.at[e], wgu_stage, sems.at[0]),
                pltpu.make_async_copy(wd_hbm.at[e], wd_stage, sems.at[1]))

    def switch_weights(b):
        e = bexp_ref[b]

        @pl.when(jnp.logical_or(b == 0, e != bexp_ref[jnp.maximum(b - 1, 0)]))
        def _():
            @pl.when(b == 0)
            def _():
                for cp in weight_copies(e):
                    cp.start()

            for cp in weight_copies(e):
                cp.wait()
            wgu_bf_ref[...] = wgu_stage[...].astype(BF16)
            wd_bf_ref[...] = wd_stage[...].astype(BF16)
            e_next = next_ref[e]

            @pl.when(e_next >= 0)
            def _():
                for cp in weight_copies(e_next):
                    cp.start()

    def compute(e, rows):
        lo, hi = _unpack_bf16_pairs(
            jnp.concatenate([xs_ref[c, rows, :] for c in range(n_chunks)], axis=1))
        xb = jnp.concatenate([lo, hi], axis=1).astype(BF16)
        gu = jnp.dot(xb, wgu_bf_ref[...], preferred_element_type=F32) + bgu_ref[e]
        gate = jnp.minimum(gu[:, :d_exp], SWIGLU_LIMIT)
        up = jnp.clip(gu[:, d_exp:], -SWIGLU_LIMIT, SWIGLU_LIMIT)
        glu = gate * jax.nn.sigmoid(gate * SWIGLU_ALPHA)
        act = ((up + 1.0) * glu).astype(BF16)
        y = jnp.dot(act, wd_bf_ref[...], preferred_element_type=F32) + bd_ref[e]
        y_words = _pack_bf16_pairs(y)
        for c in range(n_chunks):
            ys_ref[c, rows, :] = y_words[:, c * SC_CHUNK:(c + 1) * SC_CHUNK]

    for h in range(EXPERT_BLOCKS_PER_STEP):
        b = pl.program_id(0) * EXPERT_BLOCKS_PER_STEP + h

        @pl.when(b < nvalid_ref[0])
        def _():
            switch_weights(b)
            compute(bexp_ref[b], slice(h * MOE_BLOCK, (h + 1) * MOE_BLOCK))


def _experts(bexp, nvalid, next_expert, xs, wgu, bgu, wd, bd):
    n_chunks, n_rows, _ = xs.shape
    n_exp, d, d_exp2 = wgu.shape
    d_exp = d_exp2 // 2
    step_rows = EXPERT_BLOCKS_PER_STEP * MOE_BLOCK
    blk = lambda s, be, nv, nx: jnp.minimum(s, (nv[0] - 1) // EXPERT_BLOCKS_PER_STEP)
    return pl.pallas_call(
        functools.partial(_expert_kernel, d_exp=d_exp),
        grid_spec=pltpu.PrefetchScalarGridSpec(
            num_scalar_prefetch=3,
            grid=(n_rows // step_rows,),
            in_specs=[
                pl.BlockSpec((n_chunks, step_rows, SC_CHUNK),
                             lambda s, be, nv, nx: (0, blk(s, be, nv, nx), 0)),
                pl.BlockSpec(memory_space=pl.ANY),
                pl.BlockSpec((n_exp, 1, 2 * d_exp), lambda s, be, nv, nx: (0, 0, 0)),
                pl.BlockSpec(memory_space=pl.ANY),
                pl.BlockSpec((n_exp, 1, d), lambda s, be, nv, nx: (0, 0, 0)),
            ],
            out_specs=pl.BlockSpec((n_chunks, step_rows, SC_CHUNK),
                                   lambda s, be, nv, nx: (0, blk(s, be, nv, nx), 0)),
            scratch_shapes=[pltpu.VMEM((d, 2 * d_exp), F32), pltpu.VMEM((d_exp, d), F32),
                            pltpu.VMEM((d, 2 * d_exp), BF16), pltpu.VMEM((d_exp, d), BF16),
                            pltpu.SemaphoreType.DMA((2,))],
        ),
        out_shape=jax.ShapeDtypeStruct((n_chunks, n_rows, SC_CHUNK), U32),
        name="experts",
        cost_estimate=pl.CostEstimate(
            flops=6 * n_rows * d * d_exp, transcendentals=n_rows * d_exp,
            bytes_accessed=4 * n_rows * d + 4 * N_EXPERTS * 3 * d * d_exp),
        compiler_params=pltpu.CompilerParams(
            dimension_semantics=("arbitrary",), vmem_limit_bytes=VMEM_LIMIT),
    )(bexp, nvalid, next_expert, xs, wgu, bgu, wd, bd)


def _combine_kernel(prev_ref, x1_ref, gate_ref, yg_ref, out_ref):
    del prev_ref
    n_chunks = yg_ref.shape[0]
    half = n_chunks * SC_CHUNK
    for c in range(n_chunks):
        sl_lo = slice(c * SC_CHUNK, (c + 1) * SC_CHUNK)
        sl_hi = slice(half + c * SC_CHUNK, half + (c + 1) * SC_CHUNK)
        acc_lo = x1_ref[:, sl_lo]
        acc_hi = x1_ref[:, sl_hi]
        for k in range(TOP_K):
            lo, hi = _unpack_bf16_pairs(yg_ref[c, k])
            g = gate_ref[:, k:k + 1]
            acc_lo = acc_lo + g * lo
            acc_hi = acc_hi + g * hi
        out_ref[:, sl_lo] = acc_lo
        out_ref[:, sl_hi] = acc_hi


def _combine(x1, gates, yg, out_prev, tcb, row0, t_total):
    t, d = x1.shape
    n_chunks = yg.shape[0]
    blk0 = row0 // tcb
    in_specs = [
        pl.BlockSpec((tcb, d), lambda i: (i, 0)),
        pl.BlockSpec((tcb, LANES), lambda i: (i, 0)),
        pl.BlockSpec((n_chunks, TOP_K, tcb, SC_CHUNK), lambda i: (0, 0, i, 0)),
    ]
    args = [x1, gates, yg]
    aliases = {}
    body = functools.partial(_combine_kernel, None)
    if out_prev is not None:
        in_specs.append(pl.BlockSpec(memory_space=pl.ANY))
        args.append(out_prev)
        aliases = {3: 0}
        body = lambda a, b, c, prev, o: _combine_kernel(prev, a, b, c, o)
    return pl.pallas_call(
        body,
        grid=(t // tcb,),
        in_specs=in_specs,
        out_specs=pl.BlockSpec((tcb, d), lambda i: (blk0 + i, 0)),
        out_shape=jax.ShapeDtypeStruct((t_total, d), F32),
        input_output_aliases=aliases,
        name="combine",
        cost_estimate=pl.CostEstimate(
            flops=2 * TOP_K * t * d, transcendentals=0,
            bytes_accessed=t * (4 * d + 4 * d + 2 * TOP_K * d + 4 * LANES)),
        compiler_params=pltpu.CompilerParams(dimension_semantics=("arbitrary",)),
    )(*args)


def _pick_tile(n, pref):
    t = min(pref, n)
    while n % t:
        t //= 2
    return t


def _token_groups(n_batch):
    if n_batch % 4 == 0:
        return [3 * n_batch // 4, n_batch // 4]
    if n_batch % 2 == 0:
        return [n_batch // 2, n_batch // 2]
    return [n_batch]


def kernel(x, mem, positions, attn_norm_g, mem_norm_g, w_in, b_gates, a_q_norm_g, a_k_norm_g,
           a_sinks, w_o_a, gmlp_ln_g, gmlp_ln_b, gmlp_w_s, gmlp_b_s, w_o_b, w_mem_kv,
           c_q_norm_g, c_k_norm_g, w_o_c, w_out, ffn_norm_g, router_w, router_b,
           w_gate_up, b_gate_up, w_down, b_down):
    n_batch, seq, d = x.shape
    m_len = mem.shape[1]
    depth = w_in.shape[0]
    t = n_batch * seq
    group_batches = _token_groups(n_batch)

    inv_freq = ROPE_THETA ** (-jnp.arange(0, HEAD_DIM, 2, dtype=F32) / HEAD_DIM)
    invf = inv_freq[:, None]
    sgn = jnp.tile(jnp.concatenate([-jnp.ones((HEAD_DIM // 2,), F32),
                                    jnp.ones((HEAD_DIM // 2,), F32)]), LANES // HEAD_DIM)[None, :]
    pos2d = positions.reshape(t, 1).astype(I32)
    mem2d = mem.reshape(n_batch * m_len, d)
    x2d = x.reshape(t, d)

    ts = _pick_tile(seq, 512)

    for l in range(depth):
        kc, vc = _mem_kv(mem2d, mem_norm_g[l][None, :], w_mem_kv[l].astype(BF16),
                         c_k_norm_g[l][None, :], n_batch, m_len)
        w_in_bf = w_in[l].astype(BF16)
        mix_w = (w_o_a[l].astype(BF16), w_o_b[l].astype(BF16), w_o_c[l].astype(BF16),
                 w_out[l].astype(BF16))
        gq = jnp.tile(a_q_norm_g[l], LANES // HEAD_DIM)[None, :]
        gk = jnp.tile(a_k_norm_g[l], LANES // HEAD_DIM)[None, :]
        rw32 = jnp.pad(router_w[l], ((0, 0), (0, LANES - N_EXPERTS)))
        rw_hi = rw32.astype(BF16)
        rw = jnp.concatenate([rw_hi, (rw32 - rw_hi.astype(F32)).astype(BF16)], axis=1)
        rb = jnp.pad(router_b[l], (0, LANES - N_EXPERTS))[None, :]

        routed = []
        scatter_idx = None
        batch0 = 0
        for gb in group_batches:
            tg = gb * seq
            row0 = batch0 * seq
            n_asg = tg * TOP_K
            nb = -(-n_asg // MOE_BLOCK) + N_EXPERTS
            nb = -(-nb // EXPERT_BLOCKS_PER_STEP) * EXPERT_BLOCKS_PER_STEP
            n_rows = nb * MOE_BLOCK
            tm = _pick_tile(tg, 512)
            proj = _proj(
                x2d, pos2d, attn_norm_g[l][None, :], w_in_bf, invf, sgn, gq, gk,
                c_q_norm_g[l][None, :], gmlp_ln_g[l][None, :], gmlp_ln_b[l][None, :],
                b_gates[l].reshape(1, 3 * d), tm, row0, tg, scatter_idx)
            x1, h2, idx, rank, gates, counts = _mix(
                a_sinks[l], proj, x2d, kc, vc, gmlp_w_s[l], gmlp_b_s[l].T, *mix_w,
                ffn_norm_g[l][None, :], rw, rb, gb, seq, m_len, ts, batch0)

            cnt = counts[:, 0].astype(I32)
            padded = (cnt + MOE_BLOCK - 1) // MOE_BLOCK * MOE_BLOCK
            pend = jnp.cumsum(padded)
            pstart = (pend - padded).astype(I32)
            nvalid = (pend[-1:] // MOE_BLOCK).astype(I32)
            blk_row = jnp.arange(nb, dtype=I32) * MOE_BLOCK
            bexp = jnp.minimum(jnp.sum((pend[None, :] <= blk_row[:, None]).astype(I32), axis=1),
                               N_EXPERTS - 1).astype(I32)
            eid = jnp.arange(N_EXPERTS, dtype=I32)
            later = jnp.logical_and(eid[None, :] > eid[:, None], (cnt > 0)[None, :])
            next_expert = jnp.min(jnp.where(later, eid[None, :], N_EXPERTS), axis=1)
            next_expert = jnp.where(next_expert == N_EXPERTS, -1, next_expert).astype(I32)
            j = jnp.arange(MOE_BLOCK, dtype=I32)[None, :]
            fill = jnp.where(j < (padded - cnt)[:, None], pend[:, None] - 1 - j,
                             n_rows - 1 - j).astype(I32)

            pos = _pos(pstart, idx, rank, _pick_tile(tg, 8192))
            pos_km = pos[:TOP_K].reshape(n_asg)
            n_chunks = h2.shape[0]
            chunk_off = (jnp.arange(n_chunks, dtype=I32) * n_rows)[:, None]
            scatter_idx = jnp.concatenate([pos_km, fill.reshape(-1)])[None, :] + chunk_off
            xs = _sc_scatter_rows(h2, scatter_idx, n_rows)
            routed.append((row0, x1, gates, xs, bexp, nvalid, next_expert, pos_km, chunk_off))
            batch0 += gb

        gathered = []
        for row0, x1, gates, xs, bexp, nvalid, next_expert, pos_km, chunk_off in routed:
            ys = _experts(bexp, nvalid, next_expert, xs, w_gate_up[l], b_gate_up[l][:, None, :],
                          w_down[l], b_down[l][:, None, :])
            yg = _sc_gather_rows(ys, pos_km[None, :] + chunk_off)
            gathered.append(
                (row0, x1, gates, yg.reshape(yg.shape[0], TOP_K, x1.shape[0], SC_CHUNK)))

        out = None
        for row0, x1, gates, yg in gathered:
            out = _combine(x1, gates, yg, out, _pick_tile(x1.shape[0], 512), row0, t)
        x2d = out
    return x2d.reshape(n_batch, seq, d)
```

```python
import functools

import numpy as np
import jax
import jax.numpy as jnp
from jax import lax
from jax.experimental import pallas as pl
from jax.experimental.pallas import tpu as pltpu
from jax.experimental.pallas import tpu_sc as plsc

F32 = jnp.float32
BF16 = jnp.bfloat16
I32 = jnp.int32
U32 = jnp.uint32
HI16 = np.uint32(0xFFFF0000)
LOG2_E = float(np.log2(np.e))

EPS = 1e-6
LANES = 128
HEAD_DIM = 64
N_Q_HEADS = 16
N_KV_HEADS = 2
ATT_BLOCK = 128
ROPE_THETA = 10000.0
GMLP_WIDTH = 512
GMLP_GROUPS = 4
GMLP_CHUNK = 128
X_HEADS = 4
X_HEAD_DIM = 128
N_EXPERTS = 32
TOP_K = 4
SWIGLU_LIMIT = 7.0
SWIGLU_ALPHA = 1.702
MOE_BLOCK = 512
SC_ROWS = 128
SC_CHUNK = 256
EXPERT_BLOCKS_PER_STEP = 4
IDX_ROWS = 8

A_Q = N_Q_HEADS * HEAD_DIM
A_KV = N_KV_HEADS * HEAD_DIM
C_Q = X_HEADS * X_HEAD_DIM

OFF_Q = 0
OFF_K4 = OFF_Q + A_Q
OFF_V4 = OFF_K4 + 4 * LANES
OFF_U = OFF_V4 + 4 * LANES
OFF_VN = OFF_U + GMLP_WIDTH
OFF_QC = OFF_VN + GMLP_WIDTH
OFF_GA = OFF_QC + C_Q
PROJ_W_BASE = OFF_GA

VMEM_LIMIT = 56 * 1024 * 1024


def _lane_iota(shape):
    return lax.broadcasted_iota(I32, shape, len(shape) - 1)


def _rms(x, g):
    return x * lax.rsqrt(jnp.mean(x * x, axis=-1, keepdims=True) + EPS) * g


def _pack_bf16_pairs(x):
    n = x.shape[1] // 2
    bits = pltpu.bitcast(x.astype(BF16).astype(F32), U32)
    return (bits[:, :n] >> 16) | (bits[:, n:] & HI16)


def _unpack_bf16_pairs(w):
    return pltpu.bitcast(w << 16, F32), pltpu.bitcast(w & HI16, F32)


def _gelu(x):
    return 0.5 * x * (1.0 + lax.erf(x * np.float32(np.sqrt(0.5))))


def _memkv_kernel(mem_ref, g_ref, w_ref, gk_ref, kc_ref, vc_ref):
    h = _rms(mem_ref[...], g_ref[...]).astype(BF16)
    kv = jnp.dot(h, w_ref[...], preferred_element_type=F32)
    for hh in range(X_HEADS):
        sl = slice(hh * X_HEAD_DIM, (hh + 1) * X_HEAD_DIM)
        kc_ref[:, sl] = _rms(kv[:, sl], gk_ref[...]).astype(BF16)
    vc_ref[...] = kv[:, C_Q:].astype(BF16)


def _mem_kv(mem2d, g, w_bf, gk, n_batch, m_len):
    d = mem2d.shape[1]
    return pl.pallas_call(
        _memkv_kernel,
        grid=(n_batch,),
        in_specs=[
            pl.BlockSpec((m_len, d), lambda b: (b, 0)),
            pl.BlockSpec((1, d), lambda b: (0, 0)),
            pl.BlockSpec((d, 2 * C_Q), lambda b: (0, 0)),
            pl.BlockSpec((1, X_HEAD_DIM), lambda b: (0, 0)),
        ],
        out_specs=[
            pl.BlockSpec((m_len, C_Q), lambda b: (b, 0)),
            pl.BlockSpec((m_len, C_Q), lambda b: (b, 0)),
        ],
        out_shape=[jax.ShapeDtypeStruct((n_batch * m_len, C_Q), BF16)] * 2,
        name="mem_kv",
        compiler_params=pltpu.CompilerParams(dimension_semantics=("arbitrary",)),
    )(mem2d, g, w_bf, gk)


def _proj_kernel(x_ref, pos_ref, g_ref, w_ref, invf_ref, sgn_ref, gq_ref, gk_ref, gcq_ref,
                 lng_ref, lnb_ref, bg_ref, out_ref, *, d_model):
    tm = x_ref.shape[0]
    h = _rms(x_ref[...], g_ref[...]).astype(BF16)

    ang_t = invf_ref[...] * pos_ref[...].astype(F32)
    reps = LANES // (HEAD_DIM // 2)
    cosv = jnp.transpose(jnp.concatenate([jnp.cos(ang_t)] * reps, axis=0))
    sinv = jnp.transpose(jnp.concatenate([jnp.sin(ang_t)] * reps, axis=0)) * sgn_ref[...]
    lane = _lane_iota((tm, LANES))
    first_head = lane < HEAD_DIM
    lo_half = (lane % HEAD_DIM) < (HEAD_DIM // 2)

    def head_norm_rope(blk, g):
        y = blk * blk
        s_lo = jnp.sum(jnp.where(first_head, y, 0.0), axis=-1, keepdims=True)
        s_hi = jnp.sum(jnp.where(first_head, 0.0, y), axis=-1, keepdims=True)
        ss = jnp.where(first_head, s_lo, s_hi)
        n = blk * lax.rsqrt(ss * (1.0 / HEAD_DIM) + EPS) * g
        rot = jnp.where(lo_half, pltpu.roll(n, LANES - HEAD_DIM // 2, 1),
                        pltpu.roll(n, HEAD_DIM // 2, 1))
        return n * cosv + rot * sinv

    def proj(a, b):
        return jnp.dot(h, w_ref[:, a:b], preferred_element_type=F32)

    pq = proj(0, A_Q)
    for c in range(A_Q // LANES):
        sl = slice(c * LANES, (c + 1) * LANES)
        out_ref[:, OFF_Q + c * LANES:OFF_Q + (c + 1) * LANES] = (
            head_norm_rope(pq[:, sl], gq_ref[...])).astype(BF16)

    pkv = proj(A_Q, A_Q + 2 * A_KV)
    kn = head_norm_rope(pkv[:, :LANES], gk_ref[...])
    vv = pkv[:, LANES:]
    for off, t in ((OFF_K4, kn), (OFF_V4, vv)):
        tr = pltpu.roll(t, HEAD_DIM, 1)
        parts = (jnp.where(first_head, t, 0.0), jnp.where(first_head, 0.0, tr),
                 jnp.where(first_head, tr, 0.0), jnp.where(first_head, 0.0, t))
        for j, p in enumerate(parts):
            out_ref[:, off + j * LANES:off + (j + 1) * LANES] = p.astype(BF16)

    o0 = A_Q + 2 * A_KV
    out_ref[:, OFF_U:OFF_U + GMLP_WIDTH] = _gelu(proj(o0, o0 + GMLP_WIDTH)).astype(BF16)
    gv = _gelu(proj(o0 + GMLP_WIDTH, o0 + 2 * GMLP_WIDTH))
    mu = jnp.mean(gv, axis=-1, keepdims=True)
    var = jnp.mean(jnp.square(gv - mu), axis=-1, keepdims=True)
    out_ref[:, OFF_VN:OFF_VN + GMLP_WIDTH] = (
        (gv - mu) * lax.rsqrt(var + EPS) * lng_ref[...] + lnb_ref[...]).astype(BF16)

    o1 = o0 + 2 * GMLP_WIDTH
    pc = proj(o1, o1 + C_Q)
    for hh in range(X_HEADS):
        sl = slice(hh * X_HEAD_DIM, (hh + 1) * X_HEAD_DIM)
        out_ref[:, OFF_QC + hh * X_HEAD_DIM:OFF_QC + (hh + 1) * X_HEAD_DIM] = (
            _rms(pc[:, sl], gcq_ref[...])).astype(BF16)

    o2 = o1 + C_Q
    for j in range(3):
        sl = slice(j * d_model, (j + 1) * d_model)
        z = proj(o2 + j * d_model, o2 + (j + 1) * d_model) + bg_ref[:, sl]
        out_ref[:, OFF_GA + j * d_model:OFF_GA + (j + 1) * d_model] = (
            0.5 * jnp.tanh(0.5 * z) + 0.5).astype(BF16)


def _proj_after_kernel(after_ref, *refs, d_model):
    del after_ref
    _proj_kernel(*refs, d_model=d_model)


def _proj(x2d, pos2d, g, w_bf, invf, sgn, gq, gk, gcq, lng, lnb, bg, tm, row0, t, after):
    d = x2d.shape[1]
    d_in = w_bf.shape[1]
    pw = PROJ_W_BASE + 3 * d
    blk0 = row0 // tm
    full = lambda shape: pl.BlockSpec(shape, lambda i: (0,) * len(shape))
    body = functools.partial(_proj_kernel, d_model=d)
    lead_specs, lead_args = [], []
    if after is not None:
        body = functools.partial(_proj_after_kernel, d_model=d)
        lead_specs, lead_args = [pl.BlockSpec(memory_space=pl.ANY)], [after]
    return pl.pallas_call(
        body,
        grid=(t // tm,),
        in_specs=lead_specs + [
            pl.BlockSpec((tm, d), lambda i: (blk0 + i, 0)),
            pl.BlockSpec((None, 1, tm), lambda i: (blk0 + i, 0, 0)),
            full((1, d)), full((d, d_in)), full((HEAD_DIM // 2, 1)), full((1, LANES)),
            full((1, LANES)), full((1, LANES)), full((1, X_HEAD_DIM)),
            full((1, GMLP_WIDTH)), full((1, GMLP_WIDTH)), full((1, 3 * d)),
        ],
        out_specs=pl.BlockSpec((tm, pw), lambda i: (i, 0)),
        out_shape=jax.ShapeDtypeStruct((t, pw), BF16),
        name="proj",
        cost_estimate=pl.CostEstimate(
            flops=2 * t * d * d_in, transcendentals=t * (3 * d + 2 * GMLP_WIDTH + 2 * LANES),
            bytes_accessed=4 * t * d + 2 * t * pw + 2 * d * d_in),
        compiler_params=pltpu.CompilerParams(
            dimension_semantics=("arbitrary",), vmem_limit_bytes=VMEM_LIMIT),
    )(*lead_args, x2d, pos2d.reshape(-1, 1, tm), g, w_bf, invf, sgn, gq, gk, gcq, lng, lnb, bg)


def _mix_kernel(sinks_ref, proj_ref, kprev_ref, vprev_ref, x_ref, kc_ref, vc_ref, ws_ref, bst_ref,
                woa_ref, wob_ref, woc_ref, wout_ref, gffn_ref, rw_ref, rb_ref,
                x1_ref, h2_ref, idx_ref, rank_ref, gate_ref, cnt_ref,
                run_ref, oa_ref, ob_ref, oc_ref, *, d_model):
    ts = x_ref.shape[0]
    s_idx = pl.program_id(1)
    first_step = jnp.logical_and(pl.program_id(0) == 0, s_idx == 0)
    neg_inf = float("-inf")

    cpk = A_Q // LANES // N_KV_HEADS
    stack = cpk * ATT_BLOCK
    srow = lax.broadcasted_iota(I32, (stack, 2 * ATT_BLOCK), 0)
    qi = srow % ATT_BLOCK
    kj = lax.broadcasted_iota(I32, (stack, 2 * ATT_BLOCK), 1)
    band = jnp.logical_and(kj <= ATT_BLOCK + qi, kj > qi)
    chunk_of_row = lax.broadcasted_iota(I32, (stack, 1), 0) // ATT_BLOCK
    for qb in range(ts // ATT_BLOCK):
        r0 = qb * ATT_BLOCK
        rows = slice(r0, r0 + ATT_BLOCK)
        if qb == 0:
            kp, vp = kprev_ref[...], vprev_ref[...]
            mask = jnp.logical_and(band, jnp.logical_or(kj >= ATT_BLOCK, s_idx > 0))
        else:
            prow = slice(r0 - ATT_BLOCK, r0)
            kp = proj_ref[prow, OFF_K4:OFF_K4 + 4 * LANES]
            vp = proj_ref[prow, OFF_V4:OFF_V4 + 4 * LANES]
            mask = band
        k4 = jnp.concatenate([kp, proj_ref[rows, OFF_K4:OFF_K4 + 4 * LANES]], axis=0)
        v4 = jnp.concatenate([vp, proj_ref[rows, OFF_V4:OFF_V4 + 4 * LANES]], axis=0)
        for kvh in range(N_KV_HEADS):
            c0 = kvh * cpk
            q4 = jnp.concatenate(
                [proj_ref[rows, OFF_Q + (c0 + c) * LANES:OFF_Q + (c0 + c + 1) * LANES]
                 for c in range(cpk)], axis=0)
            o = jnp.zeros((stack, LANES), F32)
            for half in range(2):
                col = slice((2 * kvh + half) * LANES, (2 * kvh + half + 1) * LANES)
                s = lax.dot_general(q4, k4[:, col], (((1,), (1,)), ((), ())),
                                    preferred_element_type=F32)
                s = jnp.where(mask, s, neg_inf)
                sink = jnp.zeros((stack, 1), F32)
                for c in range(cpk):
                    sink = jnp.where(chunk_of_row == c,
                                     sinks_ref[2 * (c0 + c) + half] * LOG2_E, sink)
                m = jnp.maximum(jnp.max(s, axis=-1, keepdims=True), sink)
                p = jnp.exp2(s - m)
                den = jnp.sum(p, axis=-1, keepdims=True) + jnp.exp2(sink - m)
                o = o + jnp.dot(p.astype(BF16), v4[:, col], preferred_element_type=F32) / den
            for c in range(cpk):
                oa_ref[rows, (c0 + c) * LANES:(c0 + c + 1) * LANES] = (
                    o[c * ATT_BLOCK:(c + 1) * ATT_BLOCK].astype(BF16))

    ti = lax.broadcasted_iota(I32, (GMLP_CHUNK, GMLP_CHUNK), 0)
    si = lax.broadcasted_iota(I32, (GMLP_CHUNK, GMLP_CHUNK), 1)
    for g in range(GMLP_GROUPS):
        wt = jnp.where(si <= ti, ws_ref[g], 0.0).astype(BF16)
        bcol = bst_ref[:, g:g + 1]
        for ch in range(ts // GMLP_CHUNK):
            rows = slice(ch * GMLP_CHUNK, (ch + 1) * GMLP_CHUNK)
            vn = proj_ref[rows, OFF_VN + g * LANES:OFF_VN + (g + 1) * LANES]
            u = proj_ref[rows, OFF_U + g * LANES:OFF_U + (g + 1) * LANES].astype(F32)
            mixed = jnp.dot(wt, vn, preferred_element_type=F32) + bcol
            ob_ref[rows, g * LANES:(g + 1) * LANES] = (u * mixed).astype(BF16)

    for hh in range(X_HEADS):
        sl = slice(hh * X_HEAD_DIM, (hh + 1) * X_HEAD_DIM)
        qc = proj_ref[:, OFF_QC + hh * X_HEAD_DIM:OFF_QC + (hh + 1) * X_HEAD_DIM]
        s = lax.dot_general(qc, kc_ref[:, sl], (((1,), (1,)), ((), ())),
                            preferred_element_type=F32)
        p = jnp.exp2(s - jnp.max(s, axis=-1, keepdims=True))
        den = jnp.sum(p, axis=-1, keepdims=True)
        oc_ref[:, sl] = (jnp.dot(p.astype(BF16), vc_ref[:, sl],
                                 preferred_element_type=F32) / den).astype(BF16)

    def gate(j):
        return proj_ref[:, OFF_GA + j * d_model:OFF_GA + (j + 1) * d_model].astype(F32)

    merged = gate(0) * jnp.dot(oa_ref[...], woa_ref[...], preferred_element_type=F32)
    merged = merged + gate(1) * jnp.dot(ob_ref[...], wob_ref[...], preferred_element_type=F32)
    merged = merged + gate(2) * jnp.dot(oc_ref[...], woc_ref[...], preferred_element_type=F32)
    x1 = x_ref[...] + jnp.dot(merged.astype(BF16), wout_ref[...], preferred_element_type=F32)
    x1_ref[...] = x1

    h2 = _rms(x1, gffn_ref[...])
    h2_words = _pack_bf16_pairs(h2)
    for c in range(h2_ref.shape[0]):
        h2_ref[c] = h2_words[:, c * SC_CHUNK:(c + 1) * SC_CHUNK]
    h2_hi = h2.astype(BF16)
    h2_lo = (h2 - h2_hi.astype(F32)).astype(BF16)
    part = jnp.dot(h2_hi, rw_ref[...], preferred_element_type=F32)
    logits = (part[:, :LANES] + part[:, LANES:]
              + jnp.dot(h2_lo, rw_ref[:, :LANES], preferred_element_type=F32) + rb_ref[...])
    lt = jnp.transpose(logits)[:N_EXPERTS]
    erow = lax.broadcasted_iota(I32, (N_EXPERTS, ts), 0)
    vals, idxs = [], []
    for _ in range(TOP_K):
        m = jnp.max(lt, axis=0, keepdims=True)
        i = jnp.min(jnp.where(lt == m, erow, N_EXPERTS), axis=0, keepdims=True)
        vals.append(m)
        idxs.append(i)
        lt = jnp.where(erow == i, neg_inf, lt)
    es = [jnp.exp(v - vals[0]) for v in vals]
    den = es[0] + es[1] + es[2] + es[3]

    @pl.when(first_step)
    def _():
        run_ref[...] = jnp.zeros_like(run_ref)

    hot = [erow == i for i in idxs]
    multihot = jnp.where(jnp.logical_or(jnp.logical_or(hot[0], hot[1]),
                                        jnp.logical_or(hot[2], hot[3])), 1.0, 0.0)
    tr = lax.broadcasted_iota(I32, (ts, ts), 0)
    tc = lax.broadcasted_iota(I32, (ts, ts), 1)
    earlier = jnp.where(tr < tc, 1.0, 0.0).astype(BF16)
    before = jnp.dot(multihot.astype(BF16), earlier, preferred_element_type=F32) + run_ref[...]
    krow = lax.broadcasted_iota(I32, (IDX_ROWS, ts), 0)
    idx_out = jnp.zeros((IDX_ROWS, ts), I32)
    rank_out = jnp.zeros((IDX_ROWS, ts), I32)
    gate_rows = jnp.zeros((IDX_ROWS, ts), F32)
    for k in range(TOP_K):
        rk = jnp.sum(jnp.where(hot[k], before, 0.0), axis=0, keepdims=True)
        idx_out = jnp.where(krow == k, idxs[k], idx_out)
        rank_out = jnp.where(krow == k, rk.astype(I32), rank_out)
        gate_rows = jnp.where(krow == k, es[k] / den, gate_rows)
    idx_ref[...] = idx_out
    rank_ref[...] = rank_out
    gate_ref[...] = jnp.transpose(jnp.concatenate(
        [gate_rows, jnp.zeros((LANES - IDX_ROWS, ts), F32)], axis=0))
    run_ref[...] = run_ref[...] + jnp.sum(multihot, axis=1, keepdims=True)
    cnt_ref[...] = run_ref[...]


def _mix(sinks, proj, x2d, kc, vc, w_s, bst, woa, wob, woc, wout, gffn, rw, rb,
         n_batch, seq, m_len, ts, batch0):
    d = x2d.shape[1]
    t = n_batch * seq
    pw = proj.shape[1]
    ns = seq // ts
    nblk = seq // ATT_BLOCK
    per = ts // ATT_BLOCK
    n_chunks = d // (2 * SC_CHUNK)
    full = lambda shape: pl.BlockSpec(shape, lambda b, s: (0,) * len(shape))
    row = lambda width: pl.BlockSpec((ts, width), lambda b, s: (b * ns + s, 0))
    prev = lambda colblk: pl.BlockSpec(
        (ATT_BLOCK, 4 * LANES), lambda b, s: (b * nblk + jnp.maximum(s * per - 1, 0), colblk))
    return pl.pallas_call(
        functools.partial(_mix_kernel, d_model=d),
        grid=(n_batch, ns),
        in_specs=[
            pl.BlockSpec(memory_space=pltpu.SMEM),
            row(pw), prev(OFF_K4 // (4 * LANES)), prev(OFF_V4 // (4 * LANES)),
            pl.BlockSpec((ts, d), lambda b, s: ((batch0 + b) * ns + s, 0)),
            pl.BlockSpec((m_len, C_Q), lambda b, s: (batch0 + b, 0)),
            pl.BlockSpec((m_len, C_Q), lambda b, s: (batch0 + b, 0)),
            full((GMLP_GROUPS, GMLP_CHUNK, GMLP_CHUNK)), full((GMLP_CHUNK, GMLP_GROUPS)),
            full((A_Q, d)), full((GMLP_WIDTH, d)), full((C_Q, d)), full((d, d)),
            full((1, d)), full((d, 2 * LANES)), full((1, LANES)),
        ],
        out_specs=[row(d),
                   pl.BlockSpec((n_chunks, ts, SC_CHUNK), lambda b, s: (0, b * ns + s, 0)),
                   pl.BlockSpec((IDX_ROWS, ts), lambda b, s: (0, b * ns + s)),
                   pl.BlockSpec((IDX_ROWS, ts), lambda b, s: (0, b * ns + s)),
                   row(LANES), full((N_EXPERTS, 1))],
        out_shape=[
            jax.ShapeDtypeStruct((t, d), F32),
            jax.ShapeDtypeStruct((n_chunks, t, SC_CHUNK), U32),
            jax.ShapeDtypeStruct((IDX_ROWS, t), I32), jax.ShapeDtypeStruct((IDX_ROWS, t), I32),
            jax.ShapeDtypeStruct((t, LANES), F32), jax.ShapeDtypeStruct((N_EXPERTS, 1), F32),
        ],
        scratch_shapes=[
            pltpu.VMEM((N_EXPERTS, 1), F32),
            pltpu.VMEM((ts, A_Q), BF16), pltpu.VMEM((ts, GMLP_WIDTH), BF16),
            pltpu.VMEM((ts, C_Q), BF16),
        ],
        name="mix",
        cost_estimate=pl.CostEstimate(
            flops=2 * t * (d * (A_Q + GMLP_WIDTH + C_Q + d) + 4 * ATT_BLOCK * A_Q
                           + GMLP_CHUNK * GMLP_WIDTH + 2 * m_len * C_Q + 3 * d * LANES),
            transcendentals=t * (2 * ATT_BLOCK * N_Q_HEADS + m_len * X_HEADS),
            bytes_accessed=t * (2 * pw + 4 * d + 4 * d + 2 * d + 12 * LANES)),
        compiler_params=pltpu.CompilerParams(
            dimension_semantics=("arbitrary", "arbitrary"), vmem_limit_bytes=VMEM_LIMIT),
    )(sinks, proj, proj, proj, x2d, kc, vc, w_s, bst, woa, wob, woc, wout, gffn, rw, rb)


def _pos_kernel(pstart_ref, idx_ref, rank_ref, pos_ref):
    idx = idx_ref[...]
    pos = rank_ref[...]
    for e in range(N_EXPERTS):
        pos = pos + jnp.where(idx == e, pstart_ref[e], 0)
    pos_ref[...] = pos


def _pos(pstart, idx, rank, tp):
    t = idx.shape[1]
    return pl.pallas_call(
        _pos_kernel,
        grid_spec=pltpu.PrefetchScalarGridSpec(
            num_scalar_prefetch=1,
            grid=(t // tp,),
            in_specs=[pl.BlockSpec((IDX_ROWS, tp), lambda i, ps: (0, i)),
                      pl.BlockSpec((IDX_ROWS, tp), lambda i, ps: (0, i))],
            out_specs=pl.BlockSpec((IDX_ROWS, tp), lambda i, ps: (0, i)),
        ),
        out_shape=jax.ShapeDtypeStruct((IDX_ROWS, t), I32),
        name="slot_pos",
        compiler_params=pltpu.CompilerParams(dimension_semantics=("arbitrary",)),
    )(pstart, idx, rank)


def _sc_mesh():
    return plsc.VectorSubcoreMesh(core_axis_name="core", subcore_axis_name="subcore")


def _sc_scatter_rows(src, idx, n_rows):
    n_chunks, t, w = src.shape
    n_idx = idx.shape[1]
    src_blocks = t // SC_ROWS
    idx_blocks = n_idx // SC_ROWS

    @pl.kernel(out_type=jax.ShapeDtypeStruct((n_chunks * n_rows, w), src.dtype), mesh=_sc_mesh(),
               scratch_types=[], name="sc_dispatch",
               cost_estimate=pl.CostEstimate(
                   flops=0, transcendentals=0,
                   bytes_accessed=n_chunks * n_idx * (8 * w + 4)))
    def scatter(src_hbm, idx_hbm, out_hbm):
        def body(src_vmem, idx_vmem):
            pltpu.sync_copy(src_vmem, out_hbm.at[idx_vmem.at[0]])

        pltpu.emit_pipeline(
            body,
            grid=(n_chunks, idx_blocks),
            in_specs=[pl.BlockSpec((SC_ROWS, w), lambda c, i: (c * src_blocks + i % src_blocks, 0)),
                      pl.BlockSpec((1, SC_ROWS), lambda c, i: (0, c * idx_blocks + i))],
            out_specs=[],
            core_axis_name=("core", "subcore"),
            dimension_semantics=(pltpu.PARALLEL, pltpu.PARALLEL),
        )(src_hbm, idx_hbm)

    return scatter(src.reshape(n_chunks * t, w), idx.reshape(1, n_chunks * n_idx)).reshape(
        n_chunks, n_rows, w)


def _sc_gather_rows(table, idx):
    n_chunks, p, w = table.shape
    n_idx = idx.shape[1]
    idx_blocks = n_idx // SC_ROWS

    @pl.kernel(out_type=jax.ShapeDtypeStruct((n_chunks * n_idx, w), table.dtype), mesh=_sc_mesh(),
               scratch_types=[], name="sc_gather",
               cost_estimate=pl.CostEstimate(
                   flops=0, transcendentals=0,
                   bytes_accessed=n_chunks * n_idx * (8 * w + 4)))
    def gather(table_hbm, idx_hbm, out_hbm):
        def body(idx_vmem, out_vmem):
            pltpu.sync_copy(table_hbm.at[idx_vmem.at[0]], out_vmem)

        pltpu.emit_pipeline(
            body,
            grid=(n_chunks, idx_blocks),
            in_specs=[pl.BlockSpec((1, SC_ROWS), lambda c, i: (0, c * idx_blocks + i))],
            out_specs=[pl.BlockSpec((SC_ROWS, w), lambda c, i: (c * idx_blocks + i, 0))],
            core_axis_name=("core", "subcore"),
            dimension_semantics=(pltpu.PARALLEL, pltpu.PARALLEL),
        )(idx_hbm, out_hbm)

    return gather(table.reshape(n_chunks * p, w), idx.reshape(1, n_chunks * n_idx)).reshape(
        n_chunks, n_idx, w)


def _expert_kernel(bexp_ref, nvalid_ref, next_ref, xs_ref, wgu_hbm, bgu_ref, wd_hbm, bd_ref, ys_ref,
                   wgu_stage, wd_stage, wgu_bf_ref, wd_bf_ref, sems, *, d_exp):
    n_chunks = xs_ref.shape[0]

    def weight_copies(e):
        return (pltpu.make_async_copy(wgu_hbm.at[e], wgu_stage, sems.at[0]),
                pltpu.make_async_copy(wd_hbm.at[e], wd_stage, sems.at[1]))

    def switch_weights(b):
        e = bexp_ref[b]

        @pl.when(jnp.logical_or(b == 0, e != bexp_ref[jnp.maximum(b - 1, 0)]))
        def _():
            @pl.when(b == 0)
            def _():
                for cp in weight_copies(e):
                    cp.start()

            for cp in weight_copies(e):
                cp.wait()
            wgu_bf_ref[...] = wgu_stage[...].astype(BF16)
            wd_bf_ref[...] = wd_stage[...].astype(BF16)
            e_next = next_ref[e]

            @pl.when(e_next >= 0)
            def _():
                for cp in weight_copies(e_next):
                    cp.start()

    def compute(e, rows):
        lo, hi = _unpack_bf16_pairs(
            jnp.concatenate([xs_ref[c, rows, :] for c in range(n_chunks)], axis=1))
        xb = jnp.concatenate([lo, hi], axis=1).astype(BF16)
        gu = jnp.dot(xb, wgu_bf_ref[...], preferred_element_type=F32) + bgu_ref[e]
        gate = jnp.minimum(gu[:, :d_exp], SWIGLU_LIMIT)
        up = jnp.clip(gu[:, d_exp:], -SWIGLU_LIMIT, SWIGLU_LIMIT)
        half_gate = 0.5 * gate
        glu = half_gate + half_gate * jnp.tanh(gate * (0.5 * SWIGLU_ALPHA))
        act = ((up + 1.0) * glu).astype(BF16)
        y = jnp.dot(act, wd_bf_ref[...], preferred_element_type=F32) + bd_ref[e]
        y_words = _pack_bf16_pairs(y)
        for c in range(n_chunks):
            ys_ref[c, rows, :] = y_words[:, c * SC_CHUNK:(c + 1) * SC_CHUNK]

    for h in range(EXPERT_BLOCKS_PER_STEP):
        b = pl.program_id(0) * EXPERT_BLOCKS_PER_STEP + h

        @pl.when(b < nvalid_ref[0])
        def _():
            switch_weights(b)
            compute(bexp_ref[b], slice(h * MOE_BLOCK, (h + 1) * MOE_BLOCK))


def _experts(bexp, nvalid, next_expert, xs, wgu, bgu, wd, bd):
    n_chunks, n_rows, _ = xs.shape
    n_exp, d, d_exp2 = wgu.shape
    d_exp = d_exp2 // 2
    step_rows = EXPERT_BLOCKS_PER_STEP * MOE_BLOCK
    blk = lambda s, be, nv, nx: jnp.minimum(s, (nv[0] - 1) // EXPERT_BLOCKS_PER_STEP)
    return pl.pallas_call(
        functools.partial(_expert_kernel, d_exp=d_exp),
        grid_spec=pltpu.PrefetchScalarGridSpec(
            num_scalar_prefetch=3,
            grid=(n_rows // step_rows,),
            in_specs=[
                pl.BlockSpec((n_chunks, step_rows, SC_CHUNK),
                             lambda s, be, nv, nx: (0, blk(s, be, nv, nx), 0)),
                pl.BlockSpec(memory_space=pl.ANY),
                pl.BlockSpec((n_exp, 1, 2 * d_exp), lambda s, be, nv, nx: (0, 0, 0)),
                pl.BlockSpec(memory_space=pl.ANY),
                pl.BlockSpec((n_exp, 1, d), lambda s, be, nv, nx: (0, 0, 0)),
            ],
            out_specs=pl.BlockSpec((n_chunks, step_rows, SC_CHUNK),
                                   lambda s, be, nv, nx: (0, blk(s, be, nv, nx), 0)),
            scratch_shapes=[pltpu.VMEM((d, 2 * d_exp), F32), pltpu.VMEM((d_exp, d), F32),
                            pltpu.VMEM((d, 2 * d_exp), BF16), pltpu.VMEM((d_exp, d), BF16),
                            pltpu.SemaphoreType.DMA((2,))],
        ),
        out_shape=jax.ShapeDtypeStruct((n_chunks, n_rows, SC_CHUNK), U32),
        name="experts",
        cost_estimate=pl.CostEstimate(
            flops=6 * n_rows * d * d_exp, transcendentals=n_rows * d_exp,
            bytes_accessed=4 * n_rows * d + 4 * N_EXPERTS * 3 * d * d_exp),
        compiler_params=pltpu.CompilerParams(
            dimension_semantics=("arbitrary",), vmem_limit_bytes=VMEM_LIMIT),
    )(bexp, nvalid, next_expert, xs, wgu, bgu, wd, bd)


def _combine_kernel(prev_ref, x1_ref, gate_ref, yg_ref, out_ref):
    del prev_ref
    n_chunks = yg_ref.shape[0]
    half = n_chunks * SC_CHUNK
    for c in range(n_chunks):
        sl_lo = slice(c * SC_CHUNK, (c + 1) * SC_CHUNK)
        sl_hi = slice(half + c * SC_CHUNK, half + (c + 1) * SC_CHUNK)
        acc_lo = x1_ref[:, sl_lo]
        acc_hi = x1_ref[:, sl_hi]
        for k in range(TOP_K):
            lo, hi = _unpack_bf16_pairs(yg_ref[c, k])
            g = gate_ref[:, k:k + 1]
            acc_lo = acc_lo + g * lo
            acc_hi = acc_hi + g * hi
        out_ref[:, sl_lo] = acc_lo
        out_ref[:, sl_hi] = acc_hi


def _combine(x1, gates, yg, out_prev, tcb, row0, t_total):
    t, d = x1.shape
    n_chunks = yg.shape[0]
    blk0 = row0 // tcb
    in_specs = [
        pl.BlockSpec((tcb, d), lambda i: (i, 0)),
        pl.BlockSpec((tcb, LANES), lambda i: (i, 0)),
        pl.BlockSpec((n_chunks, TOP_K, tcb, SC_CHUNK), lambda i: (0, 0, i, 0)),
    ]
    args = [x1, gates, yg]
    aliases = {}
    body = functools.partial(_combine_kernel, None)
    if out_prev is not None:
        in_specs.append(pl.BlockSpec(memory_space=pl.ANY))
        args.append(out_prev)
        aliases = {3: 0}
        body = lambda a, b, c, prev, o: _combine_kernel(prev, a, b, c, o)
    return pl.pallas_call(
        body,
        grid=(t // tcb,),
        in_specs=in_specs,
        out_specs=pl.BlockSpec((tcb, d), lambda i: (blk0 + i, 0)),
        out_shape=jax.ShapeDtypeStruct((t_total, d), F32),
        input_output_aliases=aliases,
        name="combine",
        cost_estimate=pl.CostEstimate(
            flops=2 * TOP_K * t * d, transcendentals=0,
            bytes_accessed=t * (4 * d + 4 * d + 2 * TOP_K * d + 4 * LANES)),
        compiler_params=pltpu.CompilerParams(
            dimension_semantics=("arbitrary",), vmem_limit_bytes=VMEM_LIMIT),
    )(*args)


def _pick_tile(n, pref):
    t = min(pref, n)
    while n % t:
        t //= 2
    return t


def _token_groups(n_batch):
    if n_batch % 4 == 0:
        return [3 * n_batch // 4, n_batch // 4]
    if n_batch % 2 == 0:
        return [n_batch // 2, n_batch // 2]
    return [n_batch]


def kernel(x, mem, positions, attn_norm_g, mem_norm_g, w_in, b_gates, a_q_norm_g, a_k_norm_g,
           a_sinks, w_o_a, gmlp_ln_g, gmlp_ln_b, gmlp_w_s, gmlp_b_s, w_o_b, w_mem_kv,
           c_q_norm_g, c_k_norm_g, w_o_c, w_out, ffn_norm_g, router_w, router_b,
           w_gate_up, b_gate_up, w_down, b_down):
    n_batch, seq, d = x.shape
    m_len = mem.shape[1]
    depth = w_in.shape[0]
    t = n_batch * seq
    group_batches = _token_groups(n_batch)

    inv_freq = ROPE_THETA ** (-jnp.arange(0, HEAD_DIM, 2, dtype=F32) / HEAD_DIM)
    invf = inv_freq[:, None]
    sgn = jnp.tile(jnp.concatenate([-jnp.ones((HEAD_DIM // 2,), F32),
                                    jnp.ones((HEAD_DIM // 2,), F32)]), LANES // HEAD_DIM)[None, :]
    pos2d = positions.reshape(t, 1).astype(I32)
    mem2d = mem.reshape(n_batch * m_len, d)
    x2d = x.reshape(t, d)

    ts = _pick_tile(seq, 512)

    for l in range(depth):
        kc, vc = _mem_kv(mem2d, mem_norm_g[l][None, :], w_mem_kv[l].astype(BF16),
                         c_k_norm_g[l][None, :], n_batch, m_len)
        w_in_bf = w_in[l].astype(BF16)
        mix_w = (w_o_a[l].astype(BF16), w_o_b[l].astype(BF16), w_o_c[l].astype(BF16),
                 w_out[l].astype(BF16))
        gq = jnp.tile(a_q_norm_g[l], LANES // HEAD_DIM)[None, :] * (HEAD_DIM ** -0.5 * LOG2_E)
        gcq = c_q_norm_g[l][None, :] * (X_HEAD_DIM ** -0.5 * LOG2_E)
        gk = jnp.tile(a_k_norm_g[l], LANES // HEAD_DIM)[None, :]
        rw32 = jnp.pad(router_w[l], ((0, 0), (0, LANES - N_EXPERTS)))
        rw_hi = rw32.astype(BF16)
        rw = jnp.concatenate([rw_hi, (rw32 - rw_hi.astype(F32)).astype(BF16)], axis=1)
        rb = jnp.pad(router_b[l], (0, LANES - N_EXPERTS))[None, :]

        routed = []
        scatter_idx = None
        batch0 = 0
        for gb in group_batches:
            tg = gb * seq
            row0 = batch0 * seq
            n_asg = tg * TOP_K
            nb = -(-n_asg // MOE_BLOCK) + N_EXPERTS
            nb = -(-nb // EXPERT_BLOCKS_PER_STEP) * EXPERT_BLOCKS_PER_STEP
            n_rows = nb * MOE_BLOCK
            tm = _pick_tile(tg, 512)
            proj = _proj(
                x2d, pos2d, attn_norm_g[l][None, :], w_in_bf, invf, sgn, gq, gk,
                gcq, gmlp_ln_g[l][None, :], gmlp_ln_b[l][None, :],
                b_gates[l].reshape(1, 3 * d), tm, row0, tg, scatter_idx)
            x1, h2, idx, rank, gates, counts = _mix(
                a_sinks[l], proj, x2d, kc, vc, gmlp_w_s[l], gmlp_b_s[l].T, *mix_w,
                ffn_norm_g[l][None, :], rw, rb, gb, seq, m_len, ts, batch0)

            cnt = counts[:, 0].astype(I32)
            padded = (cnt + MOE_BLOCK - 1) // MOE_BLOCK * MOE_BLOCK
            pend = jnp.cumsum(padded)
            pstart = (pend - padded).astype(I32)
            nvalid = (pend[-1:] // MOE_BLOCK).astype(I32)
            blk_row = jnp.arange(nb, dtype=I32) * MOE_BLOCK
            bexp = jnp.minimum(jnp.sum((pend[None, :] <= blk_row[:, None]).astype(I32), axis=1),
                               N_EXPERTS - 1).astype(I32)
            eid = jnp.arange(N_EXPERTS, dtype=I32)
            later = jnp.logical_and(eid[None, :] > eid[:, None], (cnt > 0)[None, :])
            next_expert = jnp.min(jnp.where(later, eid[None, :], N_EXPERTS), axis=1)
            next_expert = jnp.where(next_expert == N_EXPERTS, -1, next_expert).astype(I32)
            j = jnp.arange(MOE_BLOCK, dtype=I32)[None, :]
            fill = jnp.where(j < (padded - cnt)[:, None], pend[:, None] - 1 - j,
                             n_rows - 1 - j).astype(I32)

            pos = _pos(pstart, idx, rank, _pick_tile(tg, 8192))
            pos_km = pos[:TOP_K].reshape(n_asg)
            n_chunks = h2.shape[0]
            chunk_off = (jnp.arange(n_chunks, dtype=I32) * n_rows)[:, None]
            scatter_idx = jnp.concatenate([pos_km, fill.reshape(-1)])[None, :] + chunk_off
            xs = _sc_scatter_rows(h2, scatter_idx, n_rows)
            routed.append((row0, x1, gates, xs, bexp, nvalid, next_expert, pos_km, chunk_off))
            batch0 += gb

        gathered = []
        for row0, x1, gates, xs, bexp, nvalid, next_expert, pos_km, chunk_off in routed:
            ys = _experts(bexp, nvalid, next_expert, xs, w_gate_up[l], b_gate_up[l][:, None, :],
                          w_down[l], b_down[l][:, None, :])
            yg = _sc_gather_rows(ys, pos_km[None, :] + chunk_off)
            gathered.append(
                (row0, x1, gates, yg.reshape(yg.shape[0], TOP_K, x1.shape[0], SC_CHUNK)))

        out = None
        for row0, x1, gates, yg in gathered:
            out = _combine(x1, gates, yg, out, _pick_tile(x1.shape[0], 1024), row0, t)
        x2d = out
    return x2d.reshape(n_batch, seq, d)
```

```python
import functools

import numpy as np
import jax
import jax.numpy as jnp
from jax import lax
from jax.experimental import pallas as pl
from jax.experimental.pallas import tpu as pltpu
from jax.experimental.pallas import tpu_sc as plsc

F32 = jnp.float32
BF16 = jnp.bfloat16
I32 = jnp.int32
U32 = jnp.uint32
HI16 = np.uint32(0xFFFF0000)
LOG2_E = float(np.log2(np.e))

EPS = 1e-6
LANES = 128
HEAD_DIM = 64
N_Q_HEADS = 16
N_KV_HEADS = 2
ATT_BLOCK = 128
ROPE_THETA = 10000.0
GMLP_WIDTH = 512
GMLP_GROUPS = 4
GMLP_CHUNK = 128
X_HEADS = 4
X_HEAD_DIM = 128
N_EXPERTS = 32
TOP_K = 4
SWIGLU_LIMIT = 7.0
SWIGLU_ALPHA = 1.702
MOE_BLOCK = 512
SC_ROWS = 128
SC_CHUNK = 256
PROJ_SUB_ROWS = 512
EXPERT_BLOCKS_PER_STEP = 4
IDX_ROWS = 8

A_Q = N_Q_HEADS * HEAD_DIM
A_KV = N_KV_HEADS * HEAD_DIM
C_Q = X_HEADS * X_HEAD_DIM

OFF_Q = 0
OFF_K4 = OFF_Q + A_Q
OFF_V4 = OFF_K4 + 4 * LANES
OFF_U = OFF_V4 + 4 * LANES
OFF_VN = OFF_U + GMLP_WIDTH
OFF_QC = OFF_VN + GMLP_WIDTH
OFF_GA = OFF_QC + C_Q
PROJ_W_BASE = OFF_GA

VMEM_LIMIT = 56 * 1024 * 1024


def _lane_iota(shape):
    return lax.broadcasted_iota(I32, shape, len(shape) - 1)


def _rms(x, g):
    return x * lax.rsqrt(jnp.mean(x * x, axis=-1, keepdims=True) + EPS) * g


def _pack_bf16_pairs(x):
    n = x.shape[1] // 2
    bits = pltpu.bitcast(x.astype(BF16).astype(F32), U32)
    return (bits[:, :n] >> 16) | (bits[:, n:] & HI16)


def _unpack_bf16_pairs(w):
    return pltpu.bitcast(w << 16, F32), pltpu.bitcast(w & HI16, F32)


def _gelu(x):
    return 0.5 * x * (1.0 + lax.erf(x * np.float32(np.sqrt(0.5))))


def _memkv_kernel(mem_ref, g_ref, w_ref, gk_ref, kc_ref, vc_ref):
    h = _rms(mem_ref[...], g_ref[...]).astype(BF16)
    kv = jnp.dot(h, w_ref[...], preferred_element_type=F32)
    for hh in range(X_HEADS):
        sl = slice(hh * X_HEAD_DIM, (hh + 1) * X_HEAD_DIM)
        kc_ref[:, sl] = _rms(kv[:, sl], gk_ref[...]).astype(BF16)
    vc_ref[...] = kv[:, C_Q:].astype(BF16)


def _mem_kv(mem2d, g, w_bf, gk, n_batch, m_len):
    d = mem2d.shape[1]
    return pl.pallas_call(
        _memkv_kernel,
        grid=(n_batch,),
        in_specs=[
            pl.BlockSpec((m_len, d), lambda b: (b, 0)),
            pl.BlockSpec((1, d), lambda b: (0, 0)),
            pl.BlockSpec((d, 2 * C_Q), lambda b: (0, 0)),
            pl.BlockSpec((1, X_HEAD_DIM), lambda b: (0, 0)),
        ],
        out_specs=[
            pl.BlockSpec((m_len, C_Q), lambda b: (b, 0)),
            pl.BlockSpec((m_len, C_Q), lambda b: (b, 0)),
        ],
        out_shape=[jax.ShapeDtypeStruct((n_batch * m_len, C_Q), BF16)] * 2,
        name="mem_kv",
        compiler_params=pltpu.CompilerParams(dimension_semantics=("arbitrary",)),
    )(mem2d, g, w_bf, gk)


def _proj_kernel(x_ref, pos_ref, g_ref, w_ref, invf_ref, sgn_ref, gq_ref, gk_ref, gcq_ref,
                 lng_ref, lnb_ref, bg_ref, out_ref, *, d_model):
    for r in range(x_ref.shape[0] // PROJ_SUB_ROWS):
        rows = pl.ds(r * PROJ_SUB_ROWS, PROJ_SUB_ROWS)
        _proj_rows(x_ref.at[rows], pos_ref.at[:, rows], g_ref, w_ref, invf_ref, sgn_ref, gq_ref,
                   gk_ref, gcq_ref, lng_ref, lnb_ref, bg_ref, out_ref.at[rows], d_model=d_model)


def _proj_rows(x_ref, pos_ref, g_ref, w_ref, invf_ref, sgn_ref, gq_ref, gk_ref, gcq_ref,
               lng_ref, lnb_ref, bg_ref, out_ref, *, d_model):
    tm = x_ref.shape[0]
    h = _rms(x_ref[...], g_ref[...]).astype(BF16)

    ang_t = invf_ref[...] * pos_ref[...].astype(F32)
    reps = LANES // (HEAD_DIM // 2)
    cosv = jnp.transpose(jnp.concatenate([jnp.cos(ang_t)] * reps, axis=0))
    sinv = jnp.transpose(jnp.concatenate([jnp.sin(ang_t)] * reps, axis=0)) * sgn_ref[...]
    lane = _lane_iota((tm, LANES))
    first_head = lane < HEAD_DIM
    lo_half = (lane % HEAD_DIM) < (HEAD_DIM // 2)

    def head_norm_rope(blk, g):
        y = blk * blk
        s_lo = jnp.sum(jnp.where(first_head, y, 0.0), axis=-1, keepdims=True)
        s_hi = jnp.sum(jnp.where(first_head, 0.0, y), axis=-1, keepdims=True)
        ss = jnp.where(first_head, s_lo, s_hi)
        n = blk * lax.rsqrt(ss * (1.0 / HEAD_DIM) + EPS) * g
        rot = jnp.where(lo_half, pltpu.roll(n, LANES - HEAD_DIM // 2, 1),
                        pltpu.roll(n, HEAD_DIM // 2, 1))
        return n * cosv + rot * sinv

    def proj(a, b):
        return jnp.dot(h, w_ref[:, a:b], preferred_element_type=F32)

    pq = proj(0, A_Q)
    for c in range(A_Q // LANES):
        sl = slice(c * LANES, (c + 1) * LANES)
        out_ref[:, OFF_Q + c * LANES:OFF_Q + (c + 1) * LANES] = (
            head_norm_rope(pq[:, sl], gq_ref[...])).astype(BF16)

    pkv = proj(A_Q, A_Q + 2 * A_KV)
    kn = head_norm_rope(pkv[:, :LANES], gk_ref[...])
    vv = pkv[:, LANES:]
    for off, t in ((OFF_K4, kn), (OFF_V4, vv)):
        tr = pltpu.roll(t, HEAD_DIM, 1)
        parts = (jnp.where(first_head, t, 0.0), jnp.where(first_head, 0.0, tr),
                 jnp.where(first_head, tr, 0.0), jnp.where(first_head, 0.0, t))
        for j, p in enumerate(parts):
            out_ref[:, off + j * LANES:off + (j + 1) * LANES] = p.astype(BF16)

    o0 = A_Q + 2 * A_KV
    out_ref[:, OFF_U:OFF_U + GMLP_WIDTH] = _gelu(proj(o0, o0 + GMLP_WIDTH)).astype(BF16)
    gv = _gelu(proj(o0 + GMLP_WIDTH, o0 + 2 * GMLP_WIDTH))
    mu = jnp.mean(gv, axis=-1, keepdims=True)
    var = jnp.mean(jnp.square(gv - mu), axis=-1, keepdims=True)
    out_ref[:, OFF_VN:OFF_VN + GMLP_WIDTH] = (
        (gv - mu) * lax.rsqrt(var + EPS) * lng_ref[...] + lnb_ref[...]).astype(BF16)

    o1 = o0 + 2 * GMLP_WIDTH
    pc = proj(o1, o1 + C_Q)
    for hh in range(X_HEADS):
        sl = slice(hh * X_HEAD_DIM, (hh + 1) * X_HEAD_DIM)
        out_ref[:, OFF_QC + hh * X_HEAD_DIM:OFF_QC + (hh + 1) * X_HEAD_DIM] = (
            _rms(pc[:, sl], gcq_ref[...])).astype(BF16)

    o2 = o1 + C_Q
    for j in range(3):
        sl = slice(j * d_model, (j + 1) * d_model)
        z = proj(o2 + j * d_model, o2 + (j + 1) * d_model) + bg_ref[:, sl]
        out_ref[:, OFF_GA + j * d_model:OFF_GA + (j + 1) * d_model] = (
            0.5 * jnp.tanh(0.5 * z) + 0.5).astype(BF16)


def _proj_after_kernel(after_ref, *refs, d_model):
    del after_ref
    _proj_kernel(*refs, d_model=d_model)


def _proj(x2d, pos2d, g, w_bf, invf, sgn, gq, gk, gcq, lng, lnb, bg, tm, row0, t, after):
    d = x2d.shape[1]
    d_in = w_bf.shape[1]
    pw = PROJ_W_BASE + 3 * d
    blk0 = row0 // tm
    full = lambda shape: pl.BlockSpec(shape, lambda i: (0,) * len(shape))
    body = functools.partial(_proj_kernel, d_model=d)
    lead_specs, lead_args = [], []
    if after is not None:
        body = functools.partial(_proj_after_kernel, d_model=d)
        lead_specs, lead_args = [pl.BlockSpec(memory_space=pl.ANY)], [after]
    return pl.pallas_call(
        body,
        grid=(t // tm,),
        in_specs=lead_specs + [
            pl.BlockSpec((tm, d), lambda i: (blk0 + i, 0)),
            pl.BlockSpec((None, 1, tm), lambda i: (blk0 + i, 0, 0)),
            full((1, d)),
            pl.BlockSpec((d, d_in), lambda i: (0, 0), pipeline_mode=pl.Buffered(1)),
            full((HEAD_DIM // 2, 1)), full((1, LANES)),
            full((1, LANES)), full((1, LANES)), full((1, X_HEAD_DIM)),
            full((1, GMLP_WIDTH)), full((1, GMLP_WIDTH)), full((1, 3 * d)),
        ],
        out_specs=pl.BlockSpec((tm, pw), lambda i: (i, 0)),
        out_shape=jax.ShapeDtypeStruct((t, pw), BF16),
        name="proj",
        cost_estimate=pl.CostEstimate(
            flops=2 * t * d * d_in, transcendentals=t * (3 * d + 2 * GMLP_WIDTH + 2 * LANES),
            bytes_accessed=4 * t * d + 2 * t * pw + 2 * d * d_in),
        compiler_params=pltpu.CompilerParams(
            dimension_semantics=("arbitrary",), vmem_limit_bytes=VMEM_LIMIT),
    )(*lead_args, x2d, pos2d.reshape(-1, 1, tm), g, w_bf, invf, sgn, gq, gk, gcq, lng, lnb, bg)


def _mix_kernel(sinks_ref, proj_ref, kprev_ref, vprev_ref, x_ref, kc_ref, vc_ref, ws_ref, bst_ref,
                woa_ref, wob_ref, woc_ref, wout_ref, gffn_ref, rw_ref, rb_ref,
                x1_ref, h2_ref, idx_ref, rank_ref, gate_ref, cnt_ref,
                run_ref, oa_ref, ob_ref, oc_ref, *, d_model):
    ts = x_ref.shape[0]
    s_idx = pl.program_id(1)
    first_step = jnp.logical_and(pl.program_id(0) == 0, s_idx == 0)
    neg_inf = float("-inf")

    cpk = A_Q // LANES // N_KV_HEADS
    stack = cpk * ATT_BLOCK
    srow = lax.broadcasted_iota(I32, (stack, 2 * ATT_BLOCK), 0)
    qi = srow % ATT_BLOCK
    kj = lax.broadcasted_iota(I32, (stack, 2 * ATT_BLOCK), 1)
    band = jnp.logical_and(kj <= ATT_BLOCK + qi, kj > qi)
    chunk_of_row = lax.broadcasted_iota(I32, (stack, 1), 0) // ATT_BLOCK
    for qb in range(ts // ATT_BLOCK):
        r0 = qb * ATT_BLOCK
        rows = slice(r0, r0 + ATT_BLOCK)
        if qb == 0:
            kp, vp = kprev_ref[...], vprev_ref[...]
            mask = jnp.logical_and(band, jnp.logical_or(kj >= ATT_BLOCK, s_idx > 0))
        else:
            prow = slice(r0 - ATT_BLOCK, r0)
            kp = proj_ref[prow, OFF_K4:OFF_K4 + 4 * LANES]
            vp = proj_ref[prow, OFF_V4:OFF_V4 + 4 * LANES]
            mask = band
        k4 = jnp.concatenate([kp, proj_ref[rows, OFF_K4:OFF_K4 + 4 * LANES]], axis=0)
        v4 = jnp.concatenate([vp, proj_ref[rows, OFF_V4:OFF_V4 + 4 * LANES]], axis=0)
        for kvh in range(N_KV_HEADS):
            c0 = kvh * cpk
            q4 = jnp.concatenate(
                [proj_ref[rows, OFF_Q + (c0 + c) * LANES:OFF_Q + (c0 + c + 1) * LANES]
                 for c in range(cpk)], axis=0)
            o = jnp.zeros((stack, LANES), F32)
            for half in range(2):
                col = slice((2 * kvh + half) * LANES, (2 * kvh + half + 1) * LANES)
                s = lax.dot_general(q4, k4[:, col], (((1,), (1,)), ((), ())),
                                    preferred_element_type=F32)
                s = jnp.where(mask, s, neg_inf)
                sink = jnp.zeros((stack, 1), F32)
                for c in range(cpk):
                    sink = jnp.where(chunk_of_row == c,
                                     sinks_ref[2 * (c0 + c) + half] * LOG2_E, sink)
                m = jnp.maximum(jnp.max(s, axis=-1, keepdims=True), sink)
                p = jnp.exp2(s - m)
                den = jnp.sum(p, axis=-1, keepdims=True) + jnp.exp2(sink - m)
                o = o + jnp.dot(p.astype(BF16), v4[:, col], preferred_element_type=F32) / den
            for c in range(cpk):
                oa_ref[rows, (c0 + c) * LANES:(c0 + c + 1) * LANES] = (
                    o[c * ATT_BLOCK:(c + 1) * ATT_BLOCK].astype(BF16))

    ti = lax.broadcasted_iota(I32, (GMLP_CHUNK, GMLP_CHUNK), 0)
    si = lax.broadcasted_iota(I32, (GMLP_CHUNK, GMLP_CHUNK), 1)
    for g in range(GMLP_GROUPS):
        wt = jnp.where(si <= ti, ws_ref[g], 0.0).astype(BF16)
        bcol = bst_ref[:, g:g + 1]
        for ch in range(ts // GMLP_CHUNK):
            rows = slice(ch * GMLP_CHUNK, (ch + 1) * GMLP_CHUNK)
            vn = proj_ref[rows, OFF_VN + g * LANES:OFF_VN + (g + 1) * LANES]
            u = proj_ref[rows, OFF_U + g * LANES:OFF_U + (g + 1) * LANES].astype(F32)
            mixed = jnp.dot(wt, vn, preferred_element_type=F32) + bcol
            ob_ref[rows, g * LANES:(g + 1) * LANES] = (u * mixed).astype(BF16)

    for hh in range(X_HEADS):
        sl = slice(hh * X_HEAD_DIM, (hh + 1) * X_HEAD_DIM)
        qc = proj_ref[:, OFF_QC + hh * X_HEAD_DIM:OFF_QC + (hh + 1) * X_HEAD_DIM]
        s = lax.dot_general(qc, kc_ref[:, sl], (((1,), (1,)), ((), ())),
                            preferred_element_type=F32)
        p = jnp.exp2(s - jnp.max(s, axis=-1, keepdims=True))
        den = jnp.sum(p, axis=-1, keepdims=True)
        oc_ref[:, sl] = (jnp.dot(p.astype(BF16), vc_ref[:, sl],
                                 preferred_element_type=F32) / den).astype(BF16)

    def gate(j):
        return proj_ref[:, OFF_GA + j * d_model:OFF_GA + (j + 1) * d_model].astype(F32)

    merged = gate(0) * jnp.dot(oa_ref[...], woa_ref[...], preferred_element_type=F32)
    merged = merged + gate(1) * jnp.dot(ob_ref[...], wob_ref[...], preferred_element_type=F32)
    merged = merged + gate(2) * jnp.dot(oc_ref[...], woc_ref[...], preferred_element_type=F32)
    x1 = x_ref[...] + jnp.dot(merged.astype(BF16), wout_ref[...], preferred_element_type=F32)
    x1_ref[...] = x1

    h2 = _rms(x1, gffn_ref[...])
    h2_words = _pack_bf16_pairs(h2)
    for c in range(h2_ref.shape[0]):
        h2_ref[c] = h2_words[:, c * SC_CHUNK:(c + 1) * SC_CHUNK]
    h2_hi = h2.astype(BF16)
    h2_lo = (h2 - h2_hi.astype(F32)).astype(BF16)
    part = jnp.dot(h2_hi, rw_ref[...], preferred_element_type=F32)
    logits = (part[:, :LANES] + part[:, LANES:]
              + jnp.dot(h2_lo, rw_ref[:, :LANES], preferred_element_type=F32) + rb_ref[...])
    lt = jnp.transpose(logits)[:N_EXPERTS]
    erow = lax.broadcasted_iota(I32, (N_EXPERTS, ts), 0)
    vals, idxs = [], []
    for _ in range(TOP_K):
        m = jnp.max(lt, axis=0, keepdims=True)
        i = jnp.min(jnp.where(lt == m, erow, N_EXPERTS), axis=0, keepdims=True)
        vals.append(m)
        idxs.append(i)
        lt = jnp.where(erow == i, neg_inf, lt)
    es = [jnp.exp(v - vals[0]) for v in vals]
    den = es[0] + es[1] + es[2] + es[3]

    @pl.when(first_step)
    def _():
        run_ref[...] = jnp.zeros_like(run_ref)

    hot = [erow == i for i in idxs]
    multihot = jnp.where(jnp.logical_or(jnp.logical_or(hot[0], hot[1]),
                                        jnp.logical_or(hot[2], hot[3])), 1.0, 0.0)
    tr = lax.broadcasted_iota(I32, (ts, ts), 0)
    tc = lax.broadcasted_iota(I32, (ts, ts), 1)
    earlier = jnp.where(tr < tc, 1.0, 0.0).astype(BF16)
    before = jnp.dot(multihot.astype(BF16), earlier, preferred_element_type=F32) + run_ref[...]
    krow = lax.broadcasted_iota(I32, (IDX_ROWS, ts), 0)
    idx_out = jnp.zeros((IDX_ROWS, ts), I32)
    rank_out = jnp.zeros((IDX_ROWS, ts), I32)
    gate_rows = jnp.zeros((IDX_ROWS, ts), F32)
    for k in range(TOP_K):
        rk = jnp.sum(jnp.where(hot[k], before, 0.0), axis=0, keepdims=True)
        idx_out = jnp.where(krow == k, idxs[k], idx_out)
        rank_out = jnp.where(krow == k, rk.astype(I32), rank_out)
        gate_rows = jnp.where(krow == k, es[k] / den, gate_rows)
    idx_ref[...] = idx_out
    rank_ref[...] = rank_out
    gate_ref[...] = jnp.transpose(jnp.concatenate(
        [gate_rows, jnp.zeros((LANES - IDX_ROWS, ts), F32)], axis=0))
    run_ref[...] = run_ref[...] + jnp.sum(multihot, axis=1, keepdims=True)
    cnt_ref[...] = run_ref[...]


def _mix(sinks, proj, x2d, kc, vc, w_s, bst, woa, wob, woc, wout, gffn, rw, rb,
         n_batch, seq, m_len, ts, batch0):
    d = x2d.shape[1]
    t = n_batch * seq
    pw = proj.shape[1]
    ns = seq // ts
    nblk = seq // ATT_BLOCK
    per = ts // ATT_BLOCK
    n_chunks = d // (2 * SC_CHUNK)
    full = lambda shape: pl.BlockSpec(shape, lambda b, s: (0,) * len(shape))
    row = lambda width: pl.BlockSpec((ts, width), lambda b, s: (b * ns + s, 0))
    prev = lambda colblk: pl.BlockSpec(
        (ATT_BLOCK, 4 * LANES), lambda b, s: (b * nblk + jnp.maximum(s * per - 1, 0), colblk))
    return pl.pallas_call(
        functools.partial(_mix_kernel, d_model=d),
        grid=(n_batch, ns),
        in_specs=[
            pl.BlockSpec(memory_space=pltpu.SMEM),
            row(pw), prev(OFF_K4 // (4 * LANES)), prev(OFF_V4 // (4 * LANES)),
            pl.BlockSpec((ts, d), lambda b, s: ((batch0 + b) * ns + s, 0)),
            pl.BlockSpec((m_len, C_Q), lambda b, s: (batch0 + b, 0)),
            pl.BlockSpec((m_len, C_Q), lambda b, s: (batch0 + b, 0)),
            full((GMLP_GROUPS, GMLP_CHUNK, GMLP_CHUNK)), full((GMLP_CHUNK, GMLP_GROUPS)),
            full((A_Q, d)), full((GMLP_WIDTH, d)), full((C_Q, d)), full((d, d)),
            full((1, d)), full((d, 2 * LANES)), full((1, LANES)),
        ],
        out_specs=[row(d),
                   pl.BlockSpec((n_chunks, ts, SC_CHUNK), lambda b, s: (0, b * ns + s, 0)),
                   pl.BlockSpec((IDX_ROWS, ts), lambda b, s: (0, b * ns + s)),
                   pl.BlockSpec((IDX_ROWS, ts), lambda b, s: (0, b * ns + s)),
                   row(LANES), full((N_EXPERTS, 1))],
        out_shape=[
            jax.ShapeDtypeStruct((t, d), F32),
            jax.ShapeDtypeStruct((n_chunks, t, SC_CHUNK), U32),
            jax.ShapeDtypeStruct((IDX_ROWS, t), I32), jax.ShapeDtypeStruct((IDX_ROWS, t), I32),
            jax.ShapeDtypeStruct((t, LANES), F32), jax.ShapeDtypeStruct((N_EXPERTS, 1), F32),
        ],
        scratch_shapes=[
            pltpu.VMEM((N_EXPERTS, 1), F32),
            pltpu.VMEM((ts, A_Q), BF16), pltpu.VMEM((ts, GMLP_WIDTH), BF16),
            pltpu.VMEM((ts, C_Q), BF16),
        ],
        name="mix",
        cost_estimate=pl.CostEstimate(
            flops=2 * t * (d * (A_Q + GMLP_WIDTH + C_Q + d) + 4 * ATT_BLOCK * A_Q
                           + GMLP_CHUNK * GMLP_WIDTH + 2 * m_len * C_Q + 3 * d * LANES),
            transcendentals=t * (2 * ATT_BLOCK * N_Q_HEADS + m_len * X_HEADS),
            bytes_accessed=t * (2 * pw + 4 * d + 4 * d + 2 * d + 12 * LANES)),
        compiler_params=pltpu.CompilerParams(
            dimension_semantics=("arbitrary", "arbitrary"), vmem_limit_bytes=VMEM_LIMIT),
    )(sinks, proj, proj, proj, x2d, kc, vc, w_s, bst, woa, wob, woc, wout, gffn, rw, rb)


def _pos_kernel(pstart_ref, idx_ref, rank_ref, pos_ref):
    idx = idx_ref[...]
    pos = rank_ref[...]
    for e in range(N_EXPERTS):
        pos = pos + jnp.where(idx == e, pstart_ref[e], 0)
    pos_ref[...] = pos


def _pos(pstart, idx, rank, tp):
    t = idx.shape[1]
    return pl.pallas_call(
        _pos_kernel,
        grid_spec=pltpu.PrefetchScalarGridSpec(
            num_scalar_prefetch=1,
            grid=(t // tp,),
            in_specs=[pl.BlockSpec((IDX_ROWS, tp), lambda i, ps: (0, i)),
                      pl.BlockSpec((IDX_ROWS, tp), lambda i, ps: (0, i))],
            out_specs=pl.BlockSpec((IDX_ROWS, tp), lambda i, ps: (0, i)),
        ),
        out_shape=jax.ShapeDtypeStruct((IDX_ROWS, t), I32),
        name="slot_pos",
        compiler_params=pltpu.CompilerParams(dimension_semantics=("arbitrary",)),
    )(pstart, idx, rank)


def _sc_mesh():
    return plsc.VectorSubcoreMesh(core_axis_name="core", subcore_axis_name="subcore")


def _sc_scatter_rows(src, idx, n_rows):
    n_chunks, t, w = src.shape
    n_idx = idx.shape[1]
    src_blocks = t // SC_ROWS
    idx_blocks = n_idx // SC_ROWS

    @pl.kernel(out_type=jax.ShapeDtypeStruct((n_chunks * n_rows, w), src.dtype), mesh=_sc_mesh(),
               scratch_types=[], name="sc_dispatch",
               cost_estimate=pl.CostEstimate(
                   flops=0, transcendentals=0,
                   bytes_accessed=n_chunks * n_idx * (8 * w + 4)))
    def scatter(src_hbm, idx_hbm, out_hbm):
        def body(src_vmem, idx_vmem):
            pltpu.sync_copy(src_vmem, out_hbm.at[idx_vmem.at[0]])

        pltpu.emit_pipeline(
            body,
            grid=(n_chunks, idx_blocks),
            in_specs=[pl.BlockSpec((SC_ROWS, w), lambda c, i: (c * src_blocks + i % src_blocks, 0)),
                      pl.BlockSpec((1, SC_ROWS), lambda c, i: (0, c * idx_blocks + i))],
            out_specs=[],
            core_axis_name=("core", "subcore"),
            dimension_semantics=(pltpu.PARALLEL, pltpu.PARALLEL),
        )(src_hbm, idx_hbm)

    return scatter(src.reshape(n_chunks * t, w), idx.reshape(1, n_chunks * n_idx)).reshape(
        n_chunks, n_rows, w)


def _sc_gather_rows(table, idx):
    n_chunks, p, w = table.shape
    n_idx = idx.shape[1]
    idx_blocks = n_idx // SC_ROWS

    @pl.kernel(out_type=jax.ShapeDtypeStruct((n_chunks * n_idx, w), table.dtype), mesh=_sc_mesh(),
               scratch_types=[], name="sc_gather",
               cost_estimate=pl.CostEstimate(
                   flops=0, transcendentals=0,
                   bytes_accessed=n_chunks * n_idx * (8 * w + 4)))
    def gather(table_hbm, idx_hbm, out_hbm):
        def body(idx_vmem, out_vmem):
            pltpu.sync_copy(table_hbm.at[idx_vmem.at[0]], out_vmem)

        pltpu.emit_pipeline(
            body,
            grid=(n_chunks, idx_blocks),
            in_specs=[pl.BlockSpec((1, SC_ROWS), lambda c, i: (0, c * idx_blocks + i))],
            out_specs=[pl.BlockSpec((SC_ROWS, w), lambda c, i: (c * idx_blocks + i, 0))],
            core_axis_name=("core", "subcore"),
            dimension_semantics=(pltpu.PARALLEL, pltpu.PARALLEL),
        )(idx_hbm, out_hbm)

    return gather(table.reshape(n_chunks * p, w), idx.reshape(1, n_chunks * n_idx)).reshape(
        n_chunks, n_idx, w)


def _expert_kernel(bexp_ref, nvalid_ref, next_ref, xs_ref, wgu_hbm, bgu_ref, wd_hbm, bd_ref, ys_ref,
                   wgu_stage, wd_stage, wgu_bf_ref, wd_bf_ref, sems, *, d_exp):
    n_chunks = xs_ref.shape[0]

    def weight_copies(e):
        return (pltpu.make_async_copy(wgu_hbm.at[e], wgu_stage, sems.at[0]),
                pltpu.make_async_copy(wd_hbm.at[e], wd_stage, sems.at[1]))

    def switch_weights(b):
        e = bexp_ref[b]

        @pl.when(jnp.logical_or(b == 0, e != bexp_ref[jnp.maximum(b - 1, 0)]))
        def _():
            @pl.when(b == 0)
            def _():
                for cp in weight_copies(e):
                    cp.start()

            for cp in weight_copies(e):
                cp.wait()
            wgu_bf_ref[...] = wgu_stage[...].astype(BF16)
            wd_bf_ref[...] = wd_stage[...].astype(BF16)
            e_next = next_ref[e]

            @pl.when(e_next >= 0)
            def _():
                for cp in weight_copies(e_next):
                    cp.start()

    def compute(e, rows):
        lo, hi = _unpack_bf16_pairs(
            jnp.concatenate([xs_ref[c, rows, :] for c in range(n_chunks)], axis=1))
        xb = jnp.concatenate([lo, hi], axis=1).astype(BF16)
        gu = jnp.dot(xb, wgu_bf_ref[...], preferred_element_type=F32) + bgu_ref[e]
        gate = jnp.minimum(gu[:, :d_exp], SWIGLU_LIMIT)
        up = jnp.clip(gu[:, d_exp:], -SWIGLU_LIMIT, SWIGLU_LIMIT)
        half_gate = 0.5 * gate
        glu = half_gate + half_gate * jnp.tanh(gate * (0.5 * SWIGLU_ALPHA))
        act = ((up + 1.0) * glu).astype(BF16)
        y = jnp.dot(act, wd_bf_ref[...], preferred_element_type=F32) + bd_ref[e]
        y_words = _pack_bf16_pairs(y)
        for c in range(n_chunks):
            ys_ref[c, rows, :] = y_words[:, c * SC_CHUNK:(c + 1) * SC_CHUNK]

    for h in range(EXPERT_BLOCKS_PER_STEP):
        b = pl.program_id(0) * EXPERT_BLOCKS_PER_STEP + h

        @pl.when(b < nvalid_ref[0])
        def _():
            switch_weights(b)
            compute(bexp_ref[b], slice(h * MOE_BLOCK, (h + 1) * MOE_BLOCK))


def _experts(bexp, nvalid, next_expert, xs, wgu, bgu, wd, bd):
    n_chunks, n_rows, _ = xs.shape
    n_exp, d, d_exp2 = wgu.shape
    d_exp = d_exp2 // 2
    step_rows = EXPERT_BLOCKS_PER_STEP * MOE_BLOCK
    blk = lambda s, be, nv, nx: jnp.minimum(s, (nv[0] - 1) // EXPERT_BLOCKS_PER_STEP)
    return pl.pallas_call(
        functools.partial(_expert_kernel, d_exp=d_exp),
        grid_spec=pltpu.PrefetchScalarGridSpec(
            num_scalar_prefetch=3,
            grid=(n_rows // step_rows,),
            in_specs=[
                pl.BlockSpec((n_chunks, step_rows, SC_CHUNK),
                             lambda s, be, nv, nx: (0, blk(s, be, nv, nx), 0)),
                pl.BlockSpec(memory_space=pl.ANY),
                pl.BlockSpec((n_exp, 1, 2 * d_exp), lambda s, be, nv, nx: (0, 0, 0)),
                pl.BlockSpec(memory_space=pl.ANY),
                pl.BlockSpec((n_exp, 1, d), lambda s, be, nv, nx: (0, 0, 0)),
            ],
            out_specs=pl.BlockSpec((n_chunks, step_rows, SC_CHUNK),
                                   lambda s, be, nv, nx: (0, blk(s, be, nv, nx), 0)),
            scratch_shapes=[pltpu.VMEM((d, 2 * d_exp), F32), pltpu.VMEM((d_exp, d), F32),
                            pltpu.VMEM((d, 2 * d_exp), BF16), pltpu.VMEM((d_exp, d), BF16),
                            pltpu.SemaphoreType.DMA((2,))],
        ),
        out_shape=jax.ShapeDtypeStruct((n_chunks, n_rows, SC_CHUNK), U32),
        name="experts",
        cost_estimate=pl.CostEstimate(
            flops=6 * n_rows * d * d_exp, transcendentals=n_rows * d_exp,
            bytes_accessed=4 * n_rows * d + 4 * N_EXPERTS * 3 * d * d_exp),
        compiler_params=pltpu.CompilerParams(
            dimension_semantics=("arbitrary",), vmem_limit_bytes=VMEM_LIMIT),
    )(bexp, nvalid, next_expert, xs, wgu, bgu, wd, bd)


def _combine_kernel(prev_ref, x1_ref, gate_ref, yg_ref, out_ref):
    del prev_ref
    n_chunks = yg_ref.shape[0]
    half = n_chunks * SC_CHUNK
    for c in range(n_chunks):
        sl_lo = slice(c * SC_CHUNK, (c + 1) * SC_CHUNK)
        sl_hi = slice(half + c * SC_CHUNK, half + (c + 1) * SC_CHUNK)
        acc_lo = x1_ref[:, sl_lo]
        acc_hi = x1_ref[:, sl_hi]
        for k in range(TOP_K):
            lo, hi = _unpack_bf16_pairs(yg_ref[c, k])
            g = gate_ref[:, k:k + 1]
            acc_lo = acc_lo + g * lo
            acc_hi = acc_hi + g * hi
        out_ref[:, sl_lo] = acc_lo
        out_ref[:, sl_hi] = acc_hi


def _combine(x1, gates, yg, out_prev, tcb, row0, t_total):
    t, d = x1.shape
    n_chunks = yg.shape[0]
    blk0 = row0 // tcb
    in_specs = [
        pl.BlockSpec((tcb, d), lambda i: (i, 0)),
        pl.BlockSpec((tcb, LANES), lambda i: (i, 0)),
        pl.BlockSpec((n_chunks, TOP_K, tcb, SC_CHUNK), lambda i: (0, 0, i, 0)),
    ]
    args = [x1, gates, yg]
    aliases = {}
    body = functools.partial(_combine_kernel, None)
    if out_prev is not None:
        in_specs.append(pl.BlockSpec(memory_space=pl.ANY))
        args.append(out_prev)
        aliases = {3: 0}
        body = lambda a, b, c, prev, o: _combine_kernel(prev, a, b, c, o)
    return pl.pallas_call(
        body,
        grid=(t // tcb,),
        in_specs=in_specs,
        out_specs=pl.BlockSpec((tcb, d), lambda i: (blk0 + i, 0)),
        out_shape=jax.ShapeDtypeStruct((t_total, d), F32),
        input_output_aliases=aliases,
        name="combine",
        cost_estimate=pl.CostEstimate(
            flops=2 * TOP_K * t * d, transcendentals=0,
            bytes_accessed=t * (4 * d + 4 * d + 2 * TOP_K * d + 4 * LANES)),
        compiler_params=pltpu.CompilerParams(
            dimension_semantics=("arbitrary",), vmem_limit_bytes=VMEM_LIMIT),
    )(*args)


def _pick_tile(n, pref):
    t = min(pref, n)
    while n % t:
        t //= 2
    return t


def _token_groups(n_batch):
    if n_batch % 4 == 0:
        return [3 * n_batch // 4, n_batch // 4]
    if n_batch % 2 == 0:
        return [n_batch // 2, n_batch // 2]
    return [n_batch]


def kernel(x, mem, positions, attn_norm_g, mem_norm_g, w_in, b_gates, a_q_norm_g, a_k_norm_g,
           a_sinks, w_o_a, gmlp_ln_g, gmlp_ln_b, gmlp_w_s, gmlp_b_s, w_o_b, w_mem_kv,
           c_q_norm_g, c_k_norm_g, w_o_c, w_out, ffn_norm_g, router_w, router_b,
           w_gate_up, b_gate_up, w_down, b_down):
    n_batch, seq, d = x.shape
    m_len = mem.shape[1]
    depth = w_in.shape[0]
    t = n_batch * seq
    group_batches = _token_groups(n_batch)

    inv_freq = ROPE_THETA ** (-jnp.arange(0, HEAD_DIM, 2, dtype=F32) / HEAD_DIM)
    invf = inv_freq[:, None]
    sgn = jnp.tile(jnp.concatenate([-jnp.ones((HEAD_DIM // 2,), F32),
                                    jnp.ones((HEAD_DIM // 2,), F32)]), LANES // HEAD_DIM)[None, :]
    pos2d = positions.reshape(t, 1).astype(I32)
    mem2d = mem.reshape(n_batch * m_len, d)
    x2d = x.reshape(t, d)

    ts = _pick_tile(seq, 512)

    for l in range(depth):
        kc, vc = _mem_kv(mem2d, mem_norm_g[l][None, :], w_mem_kv[l].astype(BF16),
                         c_k_norm_g[l][None, :], n_batch, m_len)
        w_in_bf = w_in[l].astype(BF16)
        mix_w = (w_o_a[l].astype(BF16), w_o_b[l].astype(BF16), w_o_c[l].astype(BF16),
                 w_out[l].astype(BF16))
        gq = jnp.tile(a_q_norm_g[l], LANES // HEAD_DIM)[None, :] * (HEAD_DIM ** -0.5 * LOG2_E)
        gcq = c_q_norm_g[l][None, :] * (X_HEAD_DIM ** -0.5 * LOG2_E)
        gk = jnp.tile(a_k_norm_g[l], LANES // HEAD_DIM)[None, :]
        rw32 = jnp.pad(router_w[l], ((0, 0), (0, LANES - N_EXPERTS)))
        rw_hi = rw32.astype(BF16)
        rw = jnp.concatenate([rw_hi, (rw32 - rw_hi.astype(F32)).astype(BF16)], axis=1)
        rb = jnp.pad(router_b[l], (0, LANES - N_EXPERTS))[None, :]

        routed = []
        scatter_idx = None
        batch0 = 0
        for gb in group_batches:
            tg = gb * seq
            row0 = batch0 * seq
            n_asg = tg * TOP_K
            nb = -(-n_asg // MOE_BLOCK) + N_EXPERTS
            nb = -(-nb // EXPERT_BLOCKS_PER_STEP) * EXPERT_BLOCKS_PER_STEP
            n_rows = nb * MOE_BLOCK
            tm = _pick_tile(tg, 2 * PROJ_SUB_ROWS)
            proj = _proj(
                x2d, pos2d, attn_norm_g[l][None, :], w_in_bf, invf, sgn, gq, gk,
                gcq, gmlp_ln_g[l][None, :], gmlp_ln_b[l][None, :],
                b_gates[l].reshape(1, 3 * d), tm, row0, tg, scatter_idx)
            x1, h2, idx, rank, gates, counts = _mix(
                a_sinks[l], proj, x2d, kc, vc, gmlp_w_s[l], gmlp_b_s[l].T, *mix_w,
                ffn_norm_g[l][None, :], rw, rb, gb, seq, m_len, ts, batch0)

            cnt = counts[:, 0].astype(I32)
            padded = (cnt + MOE_BLOCK - 1) // MOE_BLOCK * MOE_BLOCK
            pend = jnp.cumsum(padded)
            pstart = (pend - padded).astype(I32)
            nvalid = (pend[-1:] // MOE_BLOCK).astype(I32)
            blk_row = jnp.arange(nb, dtype=I32) * MOE_BLOCK
            bexp = jnp.minimum(jnp.sum((pend[None, :] <= blk_row[:, None]).astype(I32), axis=1),
                               N_EXPERTS - 1).astype(I32)
            eid = jnp.arange(N_EXPERTS, dtype=I32)
            later = jnp.logical_and(eid[None, :] > eid[:, None], (cnt > 0)[None, :])
            next_expert = jnp.min(jnp.where(later, eid[None, :], N_EXPERTS), axis=1)
            next_expert = jnp.where(next_expert == N_EXPERTS, -1, next_expert).astype(I32)
            j = jnp.arange(MOE_BLOCK, dtype=I32)[None, :]
            fill = jnp.where(j < (padded - cnt)[:, None], pend[:, None] - 1 - j,
                             n_rows - 1 - j).astype(I32)

            pos = _pos(pstart, idx, rank, _pick_tile(tg, 8192))
            pos_km = pos[:TOP_K].reshape(n_asg)
            n_chunks = h2.shape[0]
            chunk_off = (jnp.arange(n_chunks, dtype=I32) * n_rows)[:, None]
            scatter_idx = jnp.concatenate([pos_km, fill.reshape(-1)])[None, :] + chunk_off
            xs = _sc_scatter_rows(h2, scatter_idx, n_rows)
            routed.append((row0, x1, gates, xs, bexp, nvalid, next_expert, pos_km, chunk_off))
            batch0 += gb

        gathered = []
        for row0, x1, gates, xs, bexp, nvalid, next_expert, pos_km, chunk_off in routed:
            ys = _experts(bexp, nvalid, next_expert, xs, w_gate_up[l], b_gate_up[l][:, None, :],
                          w_down[l], b_down[l][:, None, :])
            yg = _sc_gather_rows(ys, pos_km[None, :] + chunk_off)
            gathered.append(
                (row0, x1, gates, yg.reshape(yg.shape[0], TOP_K, x1.shape[0], SC_CHUNK)))

        out = None
        for row0, x1, gates, yg in gathered:
            out = _combine(x1, gates, yg, out, _pick_tile(x1.shape[0], 1024), row0, t)
        x2d = out
    return x2d.reshape(n_batch, seq, d)
```

```python
import functools

import numpy as np
import jax
import jax.numpy as jnp
from jax import lax
from jax.experimental import pallas as pl
from jax.experimental.pallas import tpu as pltpu
from jax.experimental.pallas import tpu_sc as plsc

F32 = jnp.float32
BF16 = jnp.bfloat16
I32 = jnp.int32
U32 = jnp.uint32
HI16 = np.uint32(0xFFFF0000)
LOG2_E = float(np.log2(np.e))

EPS = 1e-6
LANES = 128
HEAD_DIM = 64
N_Q_HEADS = 16
N_KV_HEADS = 2
ATT_BLOCK = 128
ROPE_THETA = 10000.0
GMLP_WIDTH = 512
GMLP_GROUPS = 4
GMLP_CHUNK = 128
X_HEADS = 4
X_HEAD_DIM = 128
N_EXPERTS = 32
TOP_K = 4
SWIGLU_LIMIT = 7.0
SWIGLU_ALPHA = 1.702
MOE_BLOCK = 512
SC_ROWS = 128
SC_CHUNK = 256
PROJ_SUB_ROWS = 512
EXPERT_BLOCKS_PER_STEP = 6
IDX_ROWS = 8

A_Q = N_Q_HEADS * HEAD_DIM
A_KV = N_KV_HEADS * HEAD_DIM
C_Q = X_HEADS * X_HEAD_DIM

OFF_Q = 0
OFF_K4 = OFF_Q + A_Q
OFF_V4 = OFF_K4 + 4 * LANES
OFF_U = OFF_V4 + 4 * LANES
OFF_VN = OFF_U + GMLP_WIDTH
OFF_QC = OFF_VN + GMLP_WIDTH
OFF_GA = OFF_QC + C_Q
PROJ_W_BASE = OFF_GA

VMEM_LIMIT = 56 * 1024 * 1024


def _lane_iota(shape):
    return lax.broadcasted_iota(I32, shape, len(shape) - 1)


def _rms(x, g):
    return x * lax.rsqrt(jnp.mean(x * x, axis=-1, keepdims=True) + EPS) * g


def _pack_bf16_pairs(x):
    n = x.shape[1] // 2
    bits = pltpu.bitcast(x.astype(BF16).astype(F32), U32)
    return (bits[:, :n] >> 16) | (bits[:, n:] & HI16)


def _unpack_bf16_pairs(w):
    return pltpu.bitcast(w << 16, F32), pltpu.bitcast(w & HI16, F32)


def _gelu(x):
    return 0.5 * x * (1.0 + lax.erf(x * np.float32(np.sqrt(0.5))))


def _memkv_kernel(mem_ref, g_ref, w_ref, gk_ref, kc_ref, vc_ref):
    h = _rms(mem_ref[...], g_ref[...]).astype(BF16)
    kv = jnp.dot(h, w_ref[...], preferred_element_type=F32)
    for hh in range(X_HEADS):
        sl = slice(hh * X_HEAD_DIM, (hh + 1) * X_HEAD_DIM)
        kc_ref[:, sl] = _rms(kv[:, sl], gk_ref[...]).astype(BF16)
    vc_ref[...] = kv[:, C_Q:].astype(BF16)


def _mem_kv(mem2d, g, w_bf, gk, n_batch, m_len):
    d = mem2d.shape[1]
    rows = _pick_tile(n_batch * m_len, 1024)
    return pl.pallas_call(
        _memkv_kernel,
        grid=(n_batch * m_len // rows,),
        in_specs=[
            pl.BlockSpec((rows, d), lambda b: (b, 0)),
            pl.BlockSpec((1, d), lambda b: (0, 0)),
            pl.BlockSpec((d, 2 * C_Q), lambda b: (0, 0)),
            pl.BlockSpec((1, X_HEAD_DIM), lambda b: (0, 0)),
        ],
        out_specs=[
            pl.BlockSpec((rows, C_Q), lambda b: (b, 0)),
            pl.BlockSpec((rows, C_Q), lambda b: (b, 0)),
        ],
        out_shape=[jax.ShapeDtypeStruct((n_batch * m_len, C_Q), BF16)] * 2,
        name="mem_kv",
        compiler_params=pltpu.CompilerParams(dimension_semantics=("arbitrary",)),
    )(mem2d, g, w_bf, gk)


def _proj_kernel(x_ref, pos_ref, g_ref, w_ref, invf_ref, sgn_ref, gq_ref, gk_ref, gcq_ref,
                 lng_ref, lnb_ref, bg_ref, out_ref, *, d_model):
    sub = min(PROJ_SUB_ROWS, x_ref.shape[0])
    for r in range(x_ref.shape[0] // sub):
        rows = pl.ds(r * sub, sub)
        _proj_rows(x_ref.at[rows], pos_ref.at[:, rows], g_ref, w_ref, invf_ref, sgn_ref, gq_ref,
                   gk_ref, gcq_ref, lng_ref, lnb_ref, bg_ref, out_ref.at[rows], d_model=d_model)


def _proj_rows(x_ref, pos_ref, g_ref, w_ref, invf_ref, sgn_ref, gq_ref, gk_ref, gcq_ref,
               lng_ref, lnb_ref, bg_ref, out_ref, *, d_model):
    tm = x_ref.shape[0]
    h = _rms(x_ref[...], g_ref[...]).astype(BF16)

    ang_t = invf_ref[...] * pos_ref[...].astype(F32)
    reps = LANES // (HEAD_DIM // 2)
    cosv = jnp.transpose(jnp.concatenate([jnp.cos(ang_t)] * reps, axis=0))
    sinv = jnp.transpose(jnp.concatenate([jnp.sin(ang_t)] * reps, axis=0)) * sgn_ref[...]
    lane = _lane_iota((tm, LANES))
    first_head = lane < HEAD_DIM
    lo_half = (lane % HEAD_DIM) < (HEAD_DIM // 2)

    def head_norm_rope(blk, g):
        y = blk * blk
        s_lo = jnp.sum(jnp.where(first_head, y, 0.0), axis=-1, keepdims=True)
        s_hi = jnp.sum(jnp.where(first_head, 0.0, y), axis=-1, keepdims=True)
        ss = jnp.where(first_head, s_lo, s_hi)
        n = blk * lax.rsqrt(ss * (1.0 / HEAD_DIM) + EPS) * g
        rot = jnp.where(lo_half, pltpu.roll(n, LANES - HEAD_DIM // 2, 1),
                        pltpu.roll(n, HEAD_DIM // 2, 1))
        return n * cosv + rot * sinv

    def proj(a, b):
        return jnp.dot(h, w_ref[:, a:b], preferred_element_type=F32)

    pq = proj(0, A_Q)
    for c in range(A_Q // LANES):
        sl = slice(c * LANES, (c + 1) * LANES)
        out_ref[:, OFF_Q + c * LANES:OFF_Q + (c + 1) * LANES] = (
            head_norm_rope(pq[:, sl], gq_ref[...])).astype(BF16)

    pkv = proj(A_Q, A_Q + 2 * A_KV)
    kn = head_norm_rope(pkv[:, :LANES], gk_ref[...])
    vv = pkv[:, LANES:]
    for off, t in ((OFF_K4, kn), (OFF_V4, vv)):
        tr = pltpu.roll(t, HEAD_DIM, 1)
        parts = (jnp.where(first_head, t, 0.0), jnp.where(first_head, 0.0, tr),
                 jnp.where(first_head, tr, 0.0), jnp.where(first_head, 0.0, t))
        for j, p in enumerate(parts):
            out_ref[:, off + j * LANES:off + (j + 1) * LANES] = p.astype(BF16)

    o0 = A_Q + 2 * A_KV
    out_ref[:, OFF_U:OFF_U + GMLP_WIDTH] = _gelu(proj(o0, o0 + GMLP_WIDTH)).astype(BF16)
    gv = _gelu(proj(o0 + GMLP_WIDTH, o0 + 2 * GMLP_WIDTH))
    mu = jnp.mean(gv, axis=-1, keepdims=True)
    var = jnp.mean(jnp.square(gv - mu), axis=-1, keepdims=True)
    out_ref[:, OFF_VN:OFF_VN + GMLP_WIDTH] = (
        (gv - mu) * lax.rsqrt(var + EPS) * lng_ref[...] + lnb_ref[...]).astype(BF16)

    o1 = o0 + 2 * GMLP_WIDTH
    pc = proj(o1, o1 + C_Q)
    for hh in range(X_HEADS):
        sl = slice(hh * X_HEAD_DIM, (hh + 1) * X_HEAD_DIM)
        out_ref[:, OFF_QC + hh * X_HEAD_DIM:OFF_QC + (hh + 1) * X_HEAD_DIM] = (
            _rms(pc[:, sl], gcq_ref[...])).astype(BF16)

    o2 = o1 + C_Q
    for j in range(3):
        sl = slice(j * d_model, (j + 1) * d_model)
        z = proj(o2 + j * d_model, o2 + (j + 1) * d_model) + bg_ref[:, sl]
        out_ref[:, OFF_GA + j * d_model:OFF_GA + (j + 1) * d_model] = (
            0.5 * jnp.tanh(0.5 * z) + 0.5).astype(BF16)


def _proj_after_kernel(after_ref, *refs, d_model):
    del after_ref
    _proj_kernel(*refs, d_model=d_model)


def _proj(x2d, pos2d, g, w_bf, invf, sgn, gq, gk, gcq, lng, lnb, bg, tm, row0, t, after):
    d = x2d.shape[1]
    d_in = w_bf.shape[1]
    pw = PROJ_W_BASE + 3 * d
    blk0 = row0 // tm
    full = lambda shape: pl.BlockSpec(shape, lambda i: (0,) * len(shape))
    body = functools.partial(_proj_kernel, d_model=d)
    lead_specs, lead_args = [], []
    if after is not None:
        body = functools.partial(_proj_after_kernel, d_model=d)
        lead_specs, lead_args = [pl.BlockSpec(memory_space=pl.ANY)], [after]
    return pl.pallas_call(
        body,
        grid=(t // tm,),
        in_specs=lead_specs + [
            pl.BlockSpec((tm, d), lambda i: (blk0 + i, 0)),
            pl.BlockSpec((None, 1, tm), lambda i: (blk0 + i, 0, 0)),
            full((1, d)),
            pl.BlockSpec((d, d_in), lambda i: (0, 0), pipeline_mode=pl.Buffered(1)),
            full((HEAD_DIM // 2, 1)), full((1, LANES)),
            full((1, LANES)), full((1, LANES)), full((1, X_HEAD_DIM)),
            full((1, GMLP_WIDTH)), full((1, GMLP_WIDTH)), full((1, 3 * d)),
        ],
        out_specs=pl.BlockSpec((tm, pw), lambda i: (i, 0)),
        out_shape=jax.ShapeDtypeStruct((t, pw), BF16),
        name="proj",
        cost_estimate=pl.CostEstimate(
            flops=2 * t * d * d_in, transcendentals=t * (3 * d + 2 * GMLP_WIDTH + 2 * LANES),
            bytes_accessed=4 * t * d + 2 * t * pw + 2 * d * d_in),
        compiler_params=pltpu.CompilerParams(
            dimension_semantics=("arbitrary",), vmem_limit_bytes=VMEM_LIMIT),
    )(*lead_args, x2d, pos2d.reshape(-1, 1, tm), g, w_bf, invf, sgn, gq, gk, gcq, lng, lnb, bg)


def _mix_kernel(sinks_ref, proj_ref, kprev_ref, vprev_ref, x_ref, kc_ref, vc_ref, ws_ref, bst_ref,
                woa_ref, wob_ref, woc_ref, wout_ref, gffn_ref, rw_ref, rb_ref,
                x1_ref, h2_ref, idx_ref, rank_ref, gate_ref, cnt_ref,
                run_ref, oa_ref, ob_ref, oc_ref, *, d_model):
    ts = x_ref.shape[0]
    s_idx = pl.program_id(1)
    first_step = jnp.logical_and(pl.program_id(0) == 0, s_idx == 0)
    neg_inf = float("-inf")

    cpk = A_Q // LANES // N_KV_HEADS
    stack = cpk * ATT_BLOCK
    srow = lax.broadcasted_iota(I32, (stack, 2 * ATT_BLOCK), 0)
    qi = srow % ATT_BLOCK
    kj = lax.broadcasted_iota(I32, (stack, 2 * ATT_BLOCK), 1)
    band = jnp.logical_and(kj <= ATT_BLOCK + qi, kj > qi)
    chunk_of_row = lax.broadcasted_iota(I32, (stack, 1), 0) // ATT_BLOCK
    for qb in range(ts // ATT_BLOCK):
        r0 = qb * ATT_BLOCK
        rows = slice(r0, r0 + ATT_BLOCK)
        if qb == 0:
            kp, vp = kprev_ref[...], vprev_ref[...]
            mask = jnp.logical_and(band, jnp.logical_or(kj >= ATT_BLOCK, s_idx > 0))
        else:
            prow = slice(r0 - ATT_BLOCK, r0)
            kp = proj_ref[prow, OFF_K4:OFF_K4 + 4 * LANES]
            vp = proj_ref[prow, OFF_V4:OFF_V4 + 4 * LANES]
            mask = band
        k4 = jnp.concatenate([kp, proj_ref[rows, OFF_K4:OFF_K4 + 4 * LANES]], axis=0)
        v4 = jnp.concatenate([vp, proj_ref[rows, OFF_V4:OFF_V4 + 4 * LANES]], axis=0)
        for kvh in range(N_KV_HEADS):
            c0 = kvh * cpk
            q4 = jnp.concatenate(
                [proj_ref[rows, OFF_Q + (c0 + c) * LANES:OFF_Q + (c0 + c + 1) * LANES]
                 for c in range(cpk)], axis=0)
            o = jnp.zeros((stack, LANES), F32)
            for half in range(2):
                col = slice((2 * kvh + half) * LANES, (2 * kvh + half + 1) * LANES)
                s = lax.dot_general(q4, k4[:, col], (((1,), (1,)), ((), ())),
                                    preferred_element_type=F32)
                s = jnp.where(mask, s, neg_inf)
                sink = jnp.zeros((stack, 1), F32)
                for c in range(cpk):
                    sink = jnp.where(chunk_of_row == c,
                                     sinks_ref[2 * (c0 + c) + half] * LOG2_E, sink)
                m = jnp.maximum(jnp.max(s, axis=-1, keepdims=True), sink)
                p = jnp.exp2(s - m)
                den = jnp.sum(p, axis=-1, keepdims=True) + jnp.exp2(sink - m)
                o = o + jnp.dot(p.astype(BF16), v4[:, col], preferred_element_type=F32) / den
            for c in range(cpk):
                oa_ref[rows, (c0 + c) * LANES:(c0 + c + 1) * LANES] = (
                    o[c * ATT_BLOCK:(c + 1) * ATT_BLOCK].astype(BF16))

    ti = lax.broadcasted_iota(I32, (GMLP_CHUNK, GMLP_CHUNK), 0)
    si = lax.broadcasted_iota(I32, (GMLP_CHUNK, GMLP_CHUNK), 1)
    for g in range(GMLP_GROUPS):
        wt = jnp.where(si <= ti, ws_ref[g], 0.0).astype(BF16)
        bcol = bst_ref[:, g:g + 1]
        for ch in range(ts // GMLP_CHUNK):
            rows = slice(ch * GMLP_CHUNK, (ch + 1) * GMLP_CHUNK)
            vn = proj_ref[rows, OFF_VN + g * LANES:OFF_VN + (g + 1) * LANES]
            u = proj_ref[rows, OFF_U + g * LANES:OFF_U + (g + 1) * LANES].astype(F32)
            mixed = jnp.dot(wt, vn, preferred_element_type=F32) + bcol
            ob_ref[rows, g * LANES:(g + 1) * LANES] = (u * mixed).astype(BF16)

    for hh in range(X_HEADS):
        sl = slice(hh * X_HEAD_DIM, (hh + 1) * X_HEAD_DIM)
        qc = proj_ref[:, OFF_QC + hh * X_HEAD_DIM:OFF_QC + (hh + 1) * X_HEAD_DIM]
        s = lax.dot_general(qc, kc_ref[:, sl], (((1,), (1,)), ((), ())),
                            preferred_element_type=F32)
        p = jnp.exp2(s - jnp.max(s, axis=-1, keepdims=True))
        den = jnp.sum(p, axis=-1, keepdims=True)
        oc_ref[:, sl] = (jnp.dot(p.astype(BF16), vc_ref[:, sl],
                                 preferred_element_type=F32) / den).astype(BF16)

    def gate(j):
        return proj_ref[:, OFF_GA + j * d_model:OFF_GA + (j + 1) * d_model].astype(F32)

    merged = gate(0) * jnp.dot(oa_ref[...], woa_ref[...], preferred_element_type=F32)
    merged = merged + gate(1) * jnp.dot(ob_ref[...], wob_ref[...], preferred_element_type=F32)
    merged = merged + gate(2) * jnp.dot(oc_ref[...], woc_ref[...], preferred_element_type=F32)
    x1 = x_ref[...] + jnp.dot(merged.astype(BF16), wout_ref[...], preferred_element_type=F32)
    x1_ref[...] = x1

    h2 = _rms(x1, gffn_ref[...])
    h2_words = _pack_bf16_pairs(h2)
    for c in range(h2_ref.shape[0]):
        h2_ref[c] = h2_words[:, c * SC_CHUNK:(c + 1) * SC_CHUNK]
    h2_hi = h2.astype(BF16)
    h2_lo = (h2 - h2_hi.astype(F32)).astype(BF16)
    part = jnp.dot(h2_hi, rw_ref[...], preferred_element_type=F32)
    logits = (part[:, :LANES] + part[:, LANES:]
              + jnp.dot(h2_lo, rw_ref[:, :LANES], preferred_element_type=F32) + rb_ref[...])
    lt = jnp.transpose(logits)[:N_EXPERTS]
    erow = lax.broadcasted_iota(I32, (N_EXPERTS, ts), 0)
    vals, idxs = [], []
    for _ in range(TOP_K):
        m = jnp.max(lt, axis=0, keepdims=True)
        i = jnp.min(jnp.where(lt == m, erow, N_EXPERTS), axis=0, keepdims=True)
        vals.append(m)
        idxs.append(i)
        lt = jnp.where(erow == i, neg_inf, lt)
    es = [jnp.exp(v - vals[0]) for v in vals]
    den = es[0] + es[1] + es[2] + es[3]

    @pl.when(first_step)
    def _():
        run_ref[...] = jnp.zeros_like(run_ref)

    hot = [erow == i for i in idxs]
    multihot = jnp.where(jnp.logical_or(jnp.logical_or(hot[0], hot[1]),
                                        jnp.logical_or(hot[2], hot[3])), 1.0, 0.0)
    tr = lax.broadcasted_iota(I32, (ts, ts), 0)
    tc = lax.broadcasted_iota(I32, (ts, ts), 1)
    earlier = jnp.where(tr < tc, 1.0, 0.0).astype(BF16)
    before = jnp.dot(multihot.astype(BF16), earlier, preferred_element_type=F32) + run_ref[...]
    krow = lax.broadcasted_iota(I32, (IDX_ROWS, ts), 0)
    idx_out = jnp.zeros((IDX_ROWS, ts), I32)
    rank_out = jnp.zeros((IDX_ROWS, ts), I32)
    gate_rows = jnp.zeros((IDX_ROWS, ts), F32)
    for k in range(TOP_K):
        rk = jnp.sum(jnp.where(hot[k], before, 0.0), axis=0, keepdims=True)
        idx_out = jnp.where(krow == k, idxs[k], idx_out)
        rank_out = jnp.where(krow == k, rk.astype(I32), rank_out)
        gate_rows = jnp.where(krow == k, es[k] / den, gate_rows)
    idx_ref[...] = idx_out
    rank_ref[...] = rank_out
    gate_ref[...] = jnp.transpose(jnp.concatenate(
        [gate_rows, jnp.zeros((LANES - IDX_ROWS, ts), F32)], axis=0))
    run_ref[...] = run_ref[...] + jnp.sum(multihot, axis=1, keepdims=True)
    cnt_ref[...] = run_ref[...]


def _mix(sinks, proj, x2d, kc, vc, w_s, bst, woa, wob, woc, wout, gffn, rw, rb,
         n_batch, seq, m_len, ts, batch0):
    d = x2d.shape[1]
    t = n_batch * seq
    pw = proj.shape[1]
    ns = seq // ts
    nblk = seq // ATT_BLOCK
    per = ts // ATT_BLOCK
    n_chunks = d // (2 * SC_CHUNK)
    full = lambda shape: pl.BlockSpec(shape, lambda b, s: (0,) * len(shape))
    row = lambda width: pl.BlockSpec((ts, width), lambda b, s: (b * ns + s, 0))
    prev = lambda colblk: pl.BlockSpec(
        (ATT_BLOCK, 4 * LANES), lambda b, s: (b * nblk + jnp.maximum(s * per - 1, 0), colblk))
    return pl.pallas_call(
        functools.partial(_mix_kernel, d_model=d),
        grid=(n_batch, ns),
        in_specs=[
            pl.BlockSpec(memory_space=pltpu.SMEM),
            row(pw), prev(OFF_K4 // (4 * LANES)), prev(OFF_V4 // (4 * LANES)),
            pl.BlockSpec((ts, d), lambda b, s: ((batch0 + b) * ns + s, 0)),
            pl.BlockSpec((m_len, C_Q), lambda b, s: (batch0 + b, 0)),
            pl.BlockSpec((m_len, C_Q), lambda b, s: (batch0 + b, 0)),
            full((GMLP_GROUPS, GMLP_CHUNK, GMLP_CHUNK)), full((GMLP_CHUNK, GMLP_GROUPS)),
            full((A_Q, d)), full((GMLP_WIDTH, d)), full((C_Q, d)), full((d, d)),
            full((1, d)), full((d, 2 * LANES)), full((1, LANES)),
        ],
        out_specs=[row(d),
                   pl.BlockSpec((n_chunks, ts, SC_CHUNK), lambda b, s: (0, b * ns + s, 0)),
                   pl.BlockSpec((IDX_ROWS, ts), lambda b, s: (0, b * ns + s)),
                   pl.BlockSpec((IDX_ROWS, ts), lambda b, s: (0, b * ns + s)),
                   row(LANES), full((N_EXPERTS, 1))],
        out_shape=[
            jax.ShapeDtypeStruct((t, d), F32),
            jax.ShapeDtypeStruct((n_chunks, t, SC_CHUNK), U32),
            jax.ShapeDtypeStruct((IDX_ROWS, t), I32), jax.ShapeDtypeStruct((IDX_ROWS, t), I32),
            jax.ShapeDtypeStruct((t, LANES), F32), jax.ShapeDtypeStruct((N_EXPERTS, 1), F32),
        ],
        scratch_shapes=[
            pltpu.VMEM((N_EXPERTS, 1), F32),
            pltpu.VMEM((ts, A_Q), BF16), pltpu.VMEM((ts, GMLP_WIDTH), BF16),
            pltpu.VMEM((ts, C_Q), BF16),
        ],
        name="mix",
        cost_estimate=pl.CostEstimate(
            flops=2 * t * (d * (A_Q + GMLP_WIDTH + C_Q + d) + 4 * ATT_BLOCK * A_Q
                           + GMLP_CHUNK * GMLP_WIDTH + 2 * m_len * C_Q + 3 * d * LANES),
            transcendentals=t * (2 * ATT_BLOCK * N_Q_HEADS + m_len * X_HEADS),
            bytes_accessed=t * (2 * pw + 4 * d + 4 * d + 2 * d + 12 * LANES)),
        compiler_params=pltpu.CompilerParams(
            dimension_semantics=("arbitrary", "arbitrary"), vmem_limit_bytes=VMEM_LIMIT),
    )(sinks, proj, proj, proj, x2d, kc, vc, w_s, bst, woa, wob, woc, wout, gffn, rw, rb)


def _pos_kernel(pstart_ref, idx_ref, rank_ref, pos_ref):
    idx = idx_ref[...]
    pos = rank_ref[...]
    for e in range(N_EXPERTS):
        pos = pos + jnp.where(idx == e, pstart_ref[e], 0)
    pos_ref[...] = pos


def _pos(pstart, idx, rank, tp):
    t = idx.shape[1]
    return pl.pallas_call(
        _pos_kernel,
        grid_spec=pltpu.PrefetchScalarGridSpec(
            num_scalar_prefetch=1,
            grid=(t // tp,),
            in_specs=[pl.BlockSpec((IDX_ROWS, tp), lambda i, ps: (0, i)),
                      pl.BlockSpec((IDX_ROWS, tp), lambda i, ps: (0, i))],
            out_specs=pl.BlockSpec((IDX_ROWS, tp), lambda i, ps: (0, i)),
        ),
        out_shape=jax.ShapeDtypeStruct((IDX_ROWS, t), I32),
        name="slot_pos",
        compiler_params=pltpu.CompilerParams(dimension_semantics=("arbitrary",)),
    )(pstart, idx, rank)


def _sc_mesh():
    return plsc.VectorSubcoreMesh(core_axis_name="core", subcore_axis_name="subcore")


def _sc_scatter_rows(src, idx, n_rows):
    n_chunks, t, w = src.shape
    n_idx = idx.shape[1]
    src_blocks = t // SC_ROWS
    idx_blocks = n_idx // SC_ROWS

    @pl.kernel(out_type=jax.ShapeDtypeStruct((n_chunks * n_rows, w), src.dtype), mesh=_sc_mesh(),
               scratch_types=[], name="sc_dispatch",
               cost_estimate=pl.CostEstimate(
                   flops=0, transcendentals=0,
                   bytes_accessed=n_chunks * n_idx * (8 * w + 4)))
    def scatter(src_hbm, idx_hbm, out_hbm):
        def body(src_vmem, idx_vmem):
            pltpu.sync_copy(src_vmem, out_hbm.at[idx_vmem.at[0]])

        pltpu.emit_pipeline(
            body,
            grid=(n_chunks, idx_blocks),
            in_specs=[pl.BlockSpec((SC_ROWS, w), lambda c, i: (c * src_blocks + i % src_blocks, 0)),
                      pl.BlockSpec((1, SC_ROWS), lambda c, i: (0, c * idx_blocks + i))],
            out_specs=[],
            core_axis_name=("core", "subcore"),
            dimension_semantics=(pltpu.PARALLEL, pltpu.PARALLEL),
        )(src_hbm, idx_hbm)

    return scatter(src.reshape(n_chunks * t, w), idx.reshape(1, n_chunks * n_idx)).reshape(
        n_chunks, n_rows, w)


def _sc_gather_rows(table, idx):
    n_chunks, p, w = table.shape
    n_idx = idx.shape[1]
    idx_blocks = n_idx // SC_ROWS

    @pl.kernel(out_type=jax.ShapeDtypeStruct((n_chunks * n_idx, w), table.dtype), mesh=_sc_mesh(),
               scratch_types=[], name="sc_gather",
               cost_estimate=pl.CostEstimate(
                   flops=0, transcendentals=0,
                   bytes_accessed=n_chunks * n_idx * (8 * w + 4)))
    def gather(table_hbm, idx_hbm, out_hbm):
        def body(idx_vmem, out_vmem):
            pltpu.sync_copy(table_hbm.at[idx_vmem.at[0]], out_vmem)

        pltpu.emit_pipeline(
            body,
            grid=(n_chunks, idx_blocks),
            in_specs=[pl.BlockSpec((1, SC_ROWS), lambda c, i: (0, c * idx_blocks + i))],
            out_specs=[pl.BlockSpec((SC_ROWS, w), lambda c, i: (c * idx_blocks + i, 0))],
            core_axis_name=("core", "subcore"),
            dimension_semantics=(pltpu.PARALLEL, pltpu.PARALLEL),
        )(idx_hbm, out_hbm)

    return gather(table.reshape(n_chunks * p, w), idx.reshape(1, n_chunks * n_idx)).reshape(
        n_chunks, n_idx, w)


def _expert_kernel(bexp_ref, nvalid_ref, next_ref, xs_ref, wgu_hbm, bgu_ref, wd_hbm, bd_ref, ys_ref,
                   wgu_stage, wd_stage, wgu_bf_ref, wd_bf_ref, sems, *, d_exp):
    n_chunks = xs_ref.shape[0]

    def weight_copies(e):
        return (pltpu.make_async_copy(wgu_hbm.at[e], wgu_stage, sems.at[0]),
                pltpu.make_async_copy(wd_hbm.at[e], wd_stage, sems.at[1]))

    def switch_weights(b):
        e = bexp_ref[b]

        @pl.when(jnp.logical_or(b == 0, e != bexp_ref[jnp.maximum(b - 1, 0)]))
        def _():
            @pl.when(b == 0)
            def _():
                for cp in weight_copies(e):
                    cp.start()

            for cp in weight_copies(e):
                cp.wait()
            wgu_bf_ref[...] = wgu_stage[...].astype(BF16)
            wd_bf_ref[...] = wd_stage[...].astype(BF16)
            e_next = next_ref[e]

            @pl.when(e_next >= 0)
            def _():
                for cp in weight_copies(e_next):
                    cp.start()

    def compute(e, rows):
        lo, hi = _unpack_bf16_pairs(
            jnp.concatenate([xs_ref[c, rows, :] for c in range(n_chunks)], axis=1))
        xb = jnp.concatenate([lo, hi], axis=1).astype(BF16)
        gu = jnp.dot(xb, wgu_bf_ref[...], preferred_element_type=F32) + bgu_ref[e]
        gate = jnp.minimum(gu[:, :d_exp], SWIGLU_LIMIT)
        up = jnp.clip(gu[:, d_exp:], -SWIGLU_LIMIT, SWIGLU_LIMIT)
        half_gate = 0.5 * gate
        glu = half_gate + half_gate * jnp.tanh(gate * (0.5 * SWIGLU_ALPHA))
        act = ((up + 1.0) * glu).astype(BF16)
        y = jnp.dot(act, wd_bf_ref[...], preferred_element_type=F32) + bd_ref[e]
        y_words = _pack_bf16_pairs(y)
        for c in range(n_chunks):
            ys_ref[c, rows, :] = y_words[:, c * SC_CHUNK:(c + 1) * SC_CHUNK]

    for h in range(EXPERT_BLOCKS_PER_STEP):
        b = pl.program_id(0) * EXPERT_BLOCKS_PER_STEP + h

        @pl.when(b < nvalid_ref[0])
        def _():
            switch_weights(b)
            compute(bexp_ref[b], slice(h * MOE_BLOCK, (h + 1) * MOE_BLOCK))


def _experts(bexp, nvalid, next_expert, xs, wgu, bgu, wd, bd):
    n_chunks, n_rows, _ = xs.shape
    n_exp, d, d_exp2 = wgu.shape
    d_exp = d_exp2 // 2
    step_rows = EXPERT_BLOCKS_PER_STEP * MOE_BLOCK
    blk = lambda s, be, nv, nx: jnp.minimum(s, (nv[0] - 1) // EXPERT_BLOCKS_PER_STEP)
    return pl.pallas_call(
        functools.partial(_expert_kernel, d_exp=d_exp),
        grid_spec=pltpu.PrefetchScalarGridSpec(
            num_scalar_prefetch=3,
            grid=(n_rows // step_rows,),
            in_specs=[
                pl.BlockSpec((n_chunks, step_rows, SC_CHUNK),
                             lambda s, be, nv, nx: (0, blk(s, be, nv, nx), 0)),
                pl.BlockSpec(memory_space=pl.ANY),
                pl.BlockSpec((n_exp, 1, 2 * d_exp), lambda s, be, nv, nx: (0, 0, 0)),
                pl.BlockSpec(memory_space=pl.ANY),
                pl.BlockSpec((n_exp, 1, d), lambda s, be, nv, nx: (0, 0, 0)),
            ],
            out_specs=pl.BlockSpec((n_chunks, step_rows, SC_CHUNK),
                                   lambda s, be, nv, nx: (0, blk(s, be, nv, nx), 0)),
            scratch_shapes=[pltpu.VMEM((d, 2 * d_exp), F32), pltpu.VMEM((d_exp, d), F32),
                            pltpu.VMEM((d, 2 * d_exp), BF16), pltpu.VMEM((d_exp, d), BF16),
                            pltpu.SemaphoreType.DMA((2,))],
        ),
        out_shape=jax.ShapeDtypeStruct((n_chunks, n_rows, SC_CHUNK), U32),
        name="experts",
        cost_estimate=pl.CostEstimate(
            flops=6 * n_rows * d * d_exp, transcendentals=n_rows * d_exp,
            bytes_accessed=4 * n_rows * d + 4 * N_EXPERTS * 3 * d * d_exp),
        compiler_params=pltpu.CompilerParams(
            dimension_semantics=("arbitrary",), vmem_limit_bytes=VMEM_LIMIT),
    )(bexp, nvalid, next_expert, xs, wgu, bgu, wd, bd)


def _combine_kernel(prev_ref, x1_ref, gate_ref, yg_ref, out_ref):
    del prev_ref
    n_chunks = yg_ref.shape[0]
    half = n_chunks * SC_CHUNK
    for c in range(n_chunks):
        sl_lo = slice(c * SC_CHUNK, (c + 1) * SC_CHUNK)
        sl_hi = slice(half + c * SC_CHUNK, half + (c + 1) * SC_CHUNK)
        acc_lo = x1_ref[:, sl_lo]
        acc_hi = x1_ref[:, sl_hi]
        for k in range(TOP_K):
            lo, hi = _unpack_bf16_pairs(yg_ref[c, k])
            g = gate_ref[:, k:k + 1]
            acc_lo = acc_lo + g * lo
            acc_hi = acc_hi + g * hi
        out_ref[:, sl_lo] = acc_lo
        out_ref[:, sl_hi] = acc_hi


def _combine(x1, gates, yg, out_prev, tcb, row0, t_total):
    t, d = x1.shape
    n_chunks = yg.shape[0]
    blk0 = row0 // tcb
    in_specs = [
        pl.BlockSpec((tcb, d), lambda i: (i, 0)),
        pl.BlockSpec((tcb, LANES), lambda i: (i, 0)),
        pl.BlockSpec((n_chunks, TOP_K, tcb, SC_CHUNK), lambda i: (0, 0, i, 0)),
    ]
    args = [x1, gates, yg]
    aliases = {}
    body = functools.partial(_combine_kernel, None)
    if out_prev is not None:
        in_specs.append(pl.BlockSpec(memory_space=pl.ANY))
        args.append(out_prev)
        aliases = {3: 0}
        body = lambda a, b, c, prev, o: _combine_kernel(prev, a, b, c, o)
    return pl.pallas_call(
        body,
        grid=(t // tcb,),
        in_specs=in_specs,
        out_specs=pl.BlockSpec((tcb, d), lambda i: (blk0 + i, 0)),
        out_shape=jax.ShapeDtypeStruct((t_total, d), F32),
        input_output_aliases=aliases,
        name="combine",
        cost_estimate=pl.CostEstimate(
            flops=2 * TOP_K * t * d, transcendentals=0,
            bytes_accessed=t * (4 * d + 4 * d + 2 * TOP_K * d + 4 * LANES)),
        compiler_params=pltpu.CompilerParams(
            dimension_semantics=("arbitrary",), vmem_limit_bytes=VMEM_LIMIT),
    )(*args)


def _pick_tile(n, pref, *also):
    t = pref
    while any(v % t for v in (n,) + also):
        t //= 2
    return t


def _token_groups(n_batch):
    if n_batch % 4 == 0:
        return [3 * n_batch // 4, n_batch // 4]
    if n_batch % 2 == 0:
        return [n_batch // 2, n_batch // 2]
    return [n_batch]


def kernel(x, mem, positions, attn_norm_g, mem_norm_g, w_in, b_gates, a_q_norm_g, a_k_norm_g,
           a_sinks, w_o_a, gmlp_ln_g, gmlp_ln_b, gmlp_w_s, gmlp_b_s, w_o_b, w_mem_kv,
           c_q_norm_g, c_k_norm_g, w_o_c, w_out, ffn_norm_g, router_w, router_b,
           w_gate_up, b_gate_up, w_down, b_down):
    n_batch, seq, d = x.shape
    m_len = mem.shape[1]
    depth = w_in.shape[0]
    t = n_batch * seq
    group_batches = _token_groups(n_batch)

    inv_freq = ROPE_THETA ** (-jnp.arange(0, HEAD_DIM, 2, dtype=F32) / HEAD_DIM)
    invf = inv_freq[:, None]
    sgn = jnp.tile(jnp.concatenate([-jnp.ones((HEAD_DIM // 2,), F32),
                                    jnp.ones((HEAD_DIM // 2,), F32)]), LANES // HEAD_DIM)[None, :]
    pos2d = positions.reshape(t, 1).astype(I32)
    mem2d = mem.reshape(n_batch * m_len, d)
    x2d = x.reshape(t, d)

    ts = _pick_tile(seq, 512)

    for l in range(depth):
        kc, vc = _mem_kv(mem2d, mem_norm_g[l][None, :], w_mem_kv[l].astype(BF16),
                         c_k_norm_g[l][None, :], n_batch, m_len)
        w_in_bf = w_in[l].astype(BF16)
        mix_w = (w_o_a[l].astype(BF16), w_o_b[l].astype(BF16), w_o_c[l].astype(BF16),
                 w_out[l].astype(BF16))
        gq = jnp.tile(a_q_norm_g[l], LANES // HEAD_DIM)[None, :] * (HEAD_DIM ** -0.5 * LOG2_E)
        gcq = c_q_norm_g[l][None, :] * (X_HEAD_DIM ** -0.5 * LOG2_E)
        gk = jnp.tile(a_k_norm_g[l], LANES // HEAD_DIM)[None, :]
        rw32 = jnp.pad(router_w[l], ((0, 0), (0, LANES - N_EXPERTS)))
        rw_hi = rw32.astype(BF16)
        rw = jnp.concatenate([rw_hi, (rw32 - rw_hi.astype(F32)).astype(BF16)], axis=1)
        rb = jnp.pad(router_b[l], (0, LANES - N_EXPERTS))[None, :]

        routed = []
        scatter_idx = None
        batch0 = 0
        for gb in group_batches:
            tg = gb * seq
            row0 = batch0 * seq
            n_asg = tg * TOP_K
            nb = -(-n_asg // MOE_BLOCK) + N_EXPERTS
            nb = -(-nb // EXPERT_BLOCKS_PER_STEP) * EXPERT_BLOCKS_PER_STEP
            n_rows = nb * MOE_BLOCK
            tm = _pick_tile(tg, 2 * PROJ_SUB_ROWS, row0, t)
            proj = _proj(
                x2d, pos2d, attn_norm_g[l][None, :], w_in_bf, invf, sgn, gq, gk,
                gcq, gmlp_ln_g[l][None, :], gmlp_ln_b[l][None, :],
                b_gates[l].reshape(1, 3 * d), tm, row0, tg, scatter_idx)
            x1, h2, idx, rank, gates, counts = _mix(
                a_sinks[l], proj, x2d, kc, vc, gmlp_w_s[l], gmlp_b_s[l].T, *mix_w,
                ffn_norm_g[l][None, :], rw, rb, gb, seq, m_len, ts, batch0)

            cnt = counts[:, 0].astype(I32)
            padded = (cnt + MOE_BLOCK - 1) // MOE_BLOCK * MOE_BLOCK
            pend = jnp.cumsum(padded)
            pstart = (pend - padded).astype(I32)
            nvalid = (pend[-1:] // MOE_BLOCK).astype(I32)
            blk_row = jnp.arange(nb, dtype=I32) * MOE_BLOCK
            bexp = jnp.minimum(jnp.sum((pend[None, :] <= blk_row[:, None]).astype(I32), axis=1),
                               N_EXPERTS - 1).astype(I32)
            eid = jnp.arange(N_EXPERTS, dtype=I32)
            later = jnp.logical_and(eid[None, :] > eid[:, None], (cnt > 0)[None, :])
            next_expert = jnp.min(jnp.where(later, eid[None, :], N_EXPERTS), axis=1)
            next_expert = jnp.where(next_expert == N_EXPERTS, -1, next_expert).astype(I32)
            j = jnp.arange(MOE_BLOCK, dtype=I32)[None, :]
            fill = jnp.where(j < (padded - cnt)[:, None], pend[:, None] - 1 - j,
                             n_rows - 1 - j).astype(I32)

            pos = _pos(pstart, idx, rank, _pick_tile(tg, 8192))
            pos_km = pos[:TOP_K].reshape(n_asg)
            n_chunks = h2.shape[0]
            chunk_off = (jnp.arange(n_chunks, dtype=I32) * n_rows)[:, None]
            scatter_idx = jnp.concatenate([pos_km, fill.reshape(-1)])[None, :] + chunk_off
            xs = _sc_scatter_rows(h2, scatter_idx, n_rows)
            routed.append((row0, x1, gates, xs, bexp, nvalid, next_expert, pos_km, chunk_off))
            batch0 += gb

        gathered = []
        for row0, x1, gates, xs, bexp, nvalid, next_expert, pos_km, chunk_off in routed:
            ys = _experts(bexp, nvalid, next_expert, xs, w_gate_up[l], b_gate_up[l][:, None, :],
                          w_down[l], b_down[l][:, None, :])
            yg = _sc_gather_rows(ys, pos_km[None, :] + chunk_off)
            gathered.append(
                (row0, x1, gates, yg.reshape(yg.shape[0], TOP_K, x1.shape[0], SC_CHUNK)))

        out = None
        for row0, x1, gates, yg in gathered:
            out = _combine(x1, gates, yg, out, _pick_tile(x1.shape[0], 1024, row0), row0, t)
        x2d = out
    return x2d.reshape(n_batch, seq, d)
```

```python
import functools

import numpy as np
import jax
import jax.numpy as jnp
from jax import lax
from jax.experimental import pallas as pl
from jax.experimental.pallas import tpu as pltpu
from jax.experimental.pallas import tpu_sc as plsc

F32 = jnp.float32
BF16 = jnp.bfloat16
I32 = jnp.int32
U32 = jnp.uint32
HI16 = np.uint32(0xFFFF0000)
LOG2_E = float(np.log2(np.e))

EPS = 1e-6
LANES = 128
HEAD_DIM = 64
N_Q_HEADS = 16
N_KV_HEADS = 2
ATT_BLOCK = 128
ROPE_THETA = 10000.0
GMLP_WIDTH = 512
GMLP_GROUPS = 4
GMLP_CHUNK = 128
X_HEADS = 4
X_HEAD_DIM = 128
N_EXPERTS = 32
TOP_K = 4
SWIGLU_LIMIT = 7.0
SWIGLU_ALPHA = 1.702
MOE_BLOCK = 512
SC_ROWS = 128
SC_CHUNK = 256
PROJ_SUB_ROWS = 512
EXPERT_BLOCKS_PER_STEP = 4
IDX_ROWS = 8

A_Q = N_Q_HEADS * HEAD_DIM
A_KV = N_KV_HEADS * HEAD_DIM
C_Q = X_HEADS * X_HEAD_DIM

OFF_Q = 0
OFF_K4 = OFF_Q + A_Q
OFF_V4 = OFF_K4 + 4 * LANES
OFF_U = OFF_V4 + 4 * LANES
OFF_VN = OFF_U + GMLP_WIDTH
OFF_QC = OFF_VN + GMLP_WIDTH
OFF_GA = OFF_QC + C_Q
PROJ_W_BASE = OFF_GA

VMEM_LIMIT = 56 * 1024 * 1024


def _lane_iota(shape):
    return lax.broadcasted_iota(I32, shape, len(shape) - 1)


def _rms(x, g):
    return x * lax.rsqrt(jnp.mean(x * x, axis=-1, keepdims=True) + EPS) * g


def _pack_bf16_pairs(x):
    n = x.shape[1] // 2
    bits = pltpu.bitcast(x.astype(BF16).astype(F32), U32)
    return (bits[:, :n] >> 16) | (bits[:, n:] & HI16)


def _unpack_bf16_pairs(w):
    return pltpu.bitcast(w << 16, F32), pltpu.bitcast(w & HI16, F32)


def _gelu(x):
    return 0.5 * x * (1.0 + lax.erf(x * np.float32(np.sqrt(0.5))))


def _memkv_kernel(mem_ref, g_ref, w_ref, gk_ref, kc_ref, vc_ref):
    h = _rms(mem_ref[...], g_ref[...]).astype(BF16)
    kv = jnp.dot(h, w_ref[...], preferred_element_type=F32)
    for hh in range(X_HEADS):
        sl = slice(hh * X_HEAD_DIM, (hh + 1) * X_HEAD_DIM)
        kc_ref[:, sl] = _rms(kv[:, sl], gk_ref[...]).astype(BF16)
    vc_ref[...] = kv[:, C_Q:].astype(BF16)


def _mem_kv(mem2d, g, w_bf, gk, n_batch, m_len):
    d = mem2d.shape[1]
    rows = _pick_tile(n_batch * m_len, 1024)
    return pl.pallas_call(
        _memkv_kernel,
        grid=(n_batch * m_len // rows,),
        in_specs=[
            pl.BlockSpec((rows, d), lambda b: (b, 0)),
            pl.BlockSpec((1, d), lambda b: (0, 0)),
            pl.BlockSpec((d, 2 * C_Q), lambda b: (0, 0)),
            pl.BlockSpec((1, X_HEAD_DIM), lambda b: (0, 0)),
        ],
        out_specs=[
            pl.BlockSpec((rows, C_Q), lambda b: (b, 0)),
            pl.BlockSpec((rows, C_Q), lambda b: (b, 0)),
        ],
        out_shape=[jax.ShapeDtypeStruct((n_batch * m_len, C_Q), BF16)] * 2,
        name="mem_kv",
        compiler_params=pltpu.CompilerParams(dimension_semantics=("arbitrary",)),
    )(mem2d, g, w_bf, gk)


def _proj_kernel(x_ref, pos_ref, g_ref, w_ref, invf_ref, sgn_ref, gq_ref, gk_ref, gcq_ref,
                 lng_ref, lnb_ref, bg_ref, out_ref, *, d_model):
    sub = min(PROJ_SUB_ROWS, x_ref.shape[0])
    for r in range(x_ref.shape[0] // sub):
        rows = pl.ds(r * sub, sub)
        _proj_rows(x_ref.at[rows], pos_ref.at[:, rows], g_ref, w_ref, invf_ref, sgn_ref, gq_ref,
                   gk_ref, gcq_ref, lng_ref, lnb_ref, bg_ref, out_ref.at[rows], d_model=d_model)


def _proj_rows(x_ref, pos_ref, g_ref, w_ref, invf_ref, sgn_ref, gq_ref, gk_ref, gcq_ref,
               lng_ref, lnb_ref, bg_ref, out_ref, *, d_model):
    tm = x_ref.shape[0]
    h = _rms(x_ref[...], g_ref[...]).astype(BF16)

    ang_t = invf_ref[...] * pos_ref[...].astype(F32)
    reps = LANES // (HEAD_DIM // 2)
    cosv = jnp.transpose(jnp.concatenate([jnp.cos(ang_t)] * reps, axis=0))
    sinv = jnp.transpose(jnp.concatenate([jnp.sin(ang_t)] * reps, axis=0)) * sgn_ref[...]
    lane = _lane_iota((tm, LANES))
    first_head = lane < HEAD_DIM
    lo_half = (lane % HEAD_DIM) < (HEAD_DIM // 2)

    def head_norm_rope(blk, g):
        y = blk * blk
        s_lo = jnp.sum(jnp.where(first_head, y, 0.0), axis=-1, keepdims=True)
        s_hi = jnp.sum(jnp.where(first_head, 0.0, y), axis=-1, keepdims=True)
        ss = jnp.where(first_head, s_lo, s_hi)
        n = blk * lax.rsqrt(ss * (1.0 / HEAD_DIM) + EPS) * g
        rot = jnp.where(lo_half, pltpu.roll(n, LANES - HEAD_DIM // 2, 1),
                        pltpu.roll(n, HEAD_DIM // 2, 1))
        return n * cosv + rot * sinv

    def proj(a, b):
        return jnp.dot(h, w_ref[:, a:b], preferred_element_type=F32)

    pq = proj(0, A_Q)
    for c in range(A_Q // LANES):
        sl = slice(c * LANES, (c + 1) * LANES)
        out_ref[:, OFF_Q + c * LANES:OFF_Q + (c + 1) * LANES] = (
            head_norm_rope(pq[:, sl], gq_ref[...])).astype(BF16)

    pkv = proj(A_Q, A_Q + 2 * A_KV)
    kn = head_norm_rope(pkv[:, :LANES], gk_ref[...])
    vv = pkv[:, LANES:]
    for off, t in ((OFF_K4, kn), (OFF_V4, vv)):
        tr = pltpu.roll(t, HEAD_DIM, 1)
        parts = (jnp.where(first_head, t, 0.0), jnp.where(first_head, 0.0, tr),
                 jnp.where(first_head, tr, 0.0), jnp.where(first_head, 0.0, t))
        for j, p in enumerate(parts):
            out_ref[:, off + j * LANES:off + (j + 1) * LANES] = p.astype(BF16)

    o0 = A_Q + 2 * A_KV
    out_ref[:, OFF_U:OFF_U + GMLP_WIDTH] = _gelu(proj(o0, o0 + GMLP_WIDTH)).astype(BF16)
    gv = _gelu(proj(o0 + GMLP_WIDTH, o0 + 2 * GMLP_WIDTH))
    mu = jnp.mean(gv, axis=-1, keepdims=True)
    var = jnp.mean(jnp.square(gv - mu), axis=-1, keepdims=True)
    out_ref[:, OFF_VN:OFF_VN + GMLP_WIDTH] = (
        (gv - mu) * lax.rsqrt(var + EPS) * lng_ref[...] + lnb_ref[...]).astype(BF16)

    o1 = o0 + 2 * GMLP_WIDTH
    pc = proj(o1, o1 + C_Q)
    for hh in range(X_HEADS):
        sl = slice(hh * X_HEAD_DIM, (hh + 1) * X_HEAD_DIM)
        out_ref[:, OFF_QC + hh * X_HEAD_DIM:OFF_QC + (hh + 1) * X_HEAD_DIM] = (
            _rms(pc[:, sl], gcq_ref[...])).astype(BF16)

    o2 = o1 + C_Q
    for j in range(3):
        sl = slice(j * d_model, (j + 1) * d_model)
        z = proj(o2 + j * d_model, o2 + (j + 1) * d_model) + bg_ref[:, sl]
        out_ref[:, OFF_GA + j * d_model:OFF_GA + (j + 1) * d_model] = (
            0.5 * jnp.tanh(0.5 * z) + 0.5).astype(BF16)


def _proj_after_kernel(after_ref, *refs, d_model):
    del after_ref
    _proj_kernel(*refs, d_model=d_model)


def _proj(x2d, pos2d, g, w_bf, invf, sgn, gq, gk, gcq, lng, lnb, bg, tm, row0, t, after):
    d = x2d.shape[1]
    d_in = w_bf.shape[1]
    pw = PROJ_W_BASE + 3 * d
    blk0 = row0 // tm
    full = lambda shape: pl.BlockSpec(shape, lambda i: (0,) * len(shape))
    body = functools.partial(_proj_kernel, d_model=d)
    lead_specs, lead_args = [], []
    if after is not None:
        body = functools.partial(_proj_after_kernel, d_model=d)
        lead_specs, lead_args = [pl.BlockSpec(memory_space=pl.ANY)], [after]
    return pl.pallas_call(
        body,
        grid=(t // tm,),
        in_specs=lead_specs + [
            pl.BlockSpec((tm, d), lambda i: (blk0 + i, 0)),
            pl.BlockSpec((None, 1, tm), lambda i: (blk0 + i, 0, 0)),
            full((1, d)),
            pl.BlockSpec((d, d_in), lambda i: (0, 0), pipeline_mode=pl.Buffered(1)),
            full((HEAD_DIM // 2, 1)), full((1, LANES)),
            full((1, LANES)), full((1, LANES)), full((1, X_HEAD_DIM)),
            full((1, GMLP_WIDTH)), full((1, GMLP_WIDTH)), full((1, 3 * d)),
        ],
        out_specs=pl.BlockSpec((tm, pw), lambda i: (i, 0)),
        out_shape=jax.ShapeDtypeStruct((t, pw), BF16),
        name="proj",
        cost_estimate=pl.CostEstimate(
            flops=2 * t * d * d_in, transcendentals=t * (3 * d + 2 * GMLP_WIDTH + 2 * LANES),
            bytes_accessed=4 * t * d + 2 * t * pw + 2 * d * d_in),
        compiler_params=pltpu.CompilerParams(
            dimension_semantics=("arbitrary",), vmem_limit_bytes=VMEM_LIMIT),
    )(*lead_args, x2d, pos2d.reshape(-1, 1, tm), g, w_bf, invf, sgn, gq, gk, gcq, lng, lnb, bg)


def _mix_kernel(sinks_ref, proj_ref, kprev_ref, vprev_ref, x_ref, kc_ref, vc_ref, ws_ref, bst_ref,
                woa_ref, wob_ref, woc_ref, wout_ref, gffn_ref, rw_ref, rb_ref,
                x1_ref, h2_ref, idx_ref, rank_ref, gate_ref, cnt_ref,
                run_ref, oa_ref, ob_ref, oc_ref, *, d_model):
    ts = x_ref.shape[0]
    s_idx = pl.program_id(1)
    first_step = jnp.logical_and(pl.program_id(0) == 0, s_idx == 0)
    neg_inf = float("-inf")

    cpk = A_Q // LANES // N_KV_HEADS
    stack = cpk * ATT_BLOCK
    srow = lax.broadcasted_iota(I32, (stack, 2 * ATT_BLOCK), 0)
    qi = srow % ATT_BLOCK
    kj = lax.broadcasted_iota(I32, (stack, 2 * ATT_BLOCK), 1)
    band = jnp.logical_and(kj <= ATT_BLOCK + qi, kj > qi)
    chunk_of_row = lax.broadcasted_iota(I32, (stack, 1), 0) // ATT_BLOCK
    for qb in range(ts // ATT_BLOCK):
        r0 = qb * ATT_BLOCK
        rows = slice(r0, r0 + ATT_BLOCK)
        if qb == 0:
            kp, vp = kprev_ref[...], vprev_ref[...]
            mask = jnp.logical_and(band, jnp.logical_or(kj >= ATT_BLOCK, s_idx > 0))
        else:
            prow = slice(r0 - ATT_BLOCK, r0)
            kp = proj_ref[prow, OFF_K4:OFF_K4 + 4 * LANES]
            vp = proj_ref[prow, OFF_V4:OFF_V4 + 4 * LANES]
            mask = band
        k4 = jnp.concatenate([kp, proj_ref[rows, OFF_K4:OFF_K4 + 4 * LANES]], axis=0)
        v4 = jnp.concatenate([vp, proj_ref[rows, OFF_V4:OFF_V4 + 4 * LANES]], axis=0)
        for kvh in range(N_KV_HEADS):
            c0 = kvh * cpk
            q4 = jnp.concatenate(
                [proj_ref[rows, OFF_Q + (c0 + c) * LANES:OFF_Q + (c0 + c + 1) * LANES]
                 for c in range(cpk)], axis=0)
            o = jnp.zeros((stack, LANES), F32)
            for half in range(2):
                col = slice((2 * kvh + half) * LANES, (2 * kvh + half + 1) * LANES)
                s = lax.dot_general(q4, k4[:, col], (((1,), (1,)), ((), ())),
                                    preferred_element_type=F32)
                s = jnp.where(mask, s, neg_inf)
                sink = jnp.zeros((stack, 1), F32)
                for c in range(cpk):
                    sink = jnp.where(chunk_of_row == c,
                                     sinks_ref[2 * (c0 + c) + half] * LOG2_E, sink)
                m = jnp.maximum(jnp.max(s, axis=-1, keepdims=True), sink)
                p = jnp.exp2(s - m)
                den = jnp.sum(p, axis=-1, keepdims=True) + jnp.exp2(sink - m)
                o = o + jnp.dot(p.astype(BF16), v4[:, col], preferred_element_type=F32) / den
            for c in range(cpk):
                oa_ref[rows, (c0 + c) * LANES:(c0 + c + 1) * LANES] = (
                    o[c * ATT_BLOCK:(c + 1) * ATT_BLOCK].astype(BF16))

    ti = lax.broadcasted_iota(I32, (GMLP_CHUNK, GMLP_CHUNK), 0)
    si = lax.broadcasted_iota(I32, (GMLP_CHUNK, GMLP_CHUNK), 1)
    for g in range(GMLP_GROUPS):
        wt = jnp.where(si <= ti, ws_ref[g], 0.0).astype(BF16)
        bcol = bst_ref[:, g:g + 1]
        for ch in range(ts // GMLP_CHUNK):
            rows = slice(ch * GMLP_CHUNK, (ch + 1) * GMLP_CHUNK)
            vn = proj_ref[rows, OFF_VN + g * LANES:OFF_VN + (g + 1) * LANES]
            u = proj_ref[rows, OFF_U + g * LANES:OFF_U + (g + 1) * LANES].astype(F32)
            mixed = jnp.dot(wt, vn, preferred_element_type=F32) + bcol
            ob_ref[rows, g * LANES:(g + 1) * LANES] = (u * mixed).astype(BF16)

    for hh in range(X_HEADS):
        sl = slice(hh * X_HEAD_DIM, (hh + 1) * X_HEAD_DIM)
        qc = proj_ref[:, OFF_QC + hh * X_HEAD_DIM:OFF_QC + (hh + 1) * X_HEAD_DIM]
        s = lax.dot_general(qc, kc_ref[:, sl], (((1,), (1,)), ((), ())),
                            preferred_element_type=F32)
        p = jnp.exp2(s - jnp.max(s, axis=-1, keepdims=True))
        den = jnp.sum(p, axis=-1, keepdims=True)
        oc_ref[:, sl] = (jnp.dot(p.astype(BF16), vc_ref[:, sl],
                                 preferred_element_type=F32) / den).astype(BF16)

    def gate(j):
        return proj_ref[:, OFF_GA + j * d_model:OFF_GA + (j + 1) * d_model].astype(F32)

    merged = gate(0) * jnp.dot(oa_ref[...], woa_ref[...], preferred_element_type=F32)
    merged = merged + gate(1) * jnp.dot(ob_ref[...], wob_ref[...], preferred_element_type=F32)
    merged = merged + gate(2) * jnp.dot(oc_ref[...], woc_ref[...], preferred_element_type=F32)
    x1 = x_ref[...] + jnp.dot(merged.astype(BF16), wout_ref[...], preferred_element_type=F32)
    x1_ref[...] = x1

    h2 = _rms(x1, gffn_ref[...])
    h2_words = _pack_bf16_pairs(h2)
    for c in range(h2_ref.shape[0]):
        h2_ref[c] = h2_words[:, c * SC_CHUNK:(c + 1) * SC_CHUNK]
    h2_hi = h2.astype(BF16)
    h2_lo = (h2 - h2_hi.astype(F32)).astype(BF16)
    part = jnp.dot(h2_hi, rw_ref[...], preferred_element_type=F32)
    logits = (part[:, :LANES] + part[:, LANES:]
              + jnp.dot(h2_lo, rw_ref[:, :LANES], preferred_element_type=F32) + rb_ref[...])
    lt = jnp.transpose(logits)[:N_EXPERTS]
    erow = lax.broadcasted_iota(I32, (N_EXPERTS, ts), 0)
    vals, idxs = [], []
    for _ in range(TOP_K):
        m = jnp.max(lt, axis=0, keepdims=True)
        i = jnp.min(jnp.where(lt == m, erow, N_EXPERTS), axis=0, keepdims=True)
        vals.append(m)
        idxs.append(i)
        lt = jnp.where(erow == i, neg_inf, lt)
    es = [jnp.exp(v - vals[0]) for v in vals]
    den = es[0] + es[1] + es[2] + es[3]

    @pl.when(first_step)
    def _():
        run_ref[...] = jnp.zeros_like(run_ref)

    hot = [erow == i for i in idxs]
    multihot = jnp.where(jnp.logical_or(jnp.logical_or(hot[0], hot[1]),
                                        jnp.logical_or(hot[2], hot[3])), 1.0, 0.0)
    tr = lax.broadcasted_iota(I32, (ts, ts), 0)
    tc = lax.broadcasted_iota(I32, (ts, ts), 1)
    earlier = jnp.where(tr < tc, 1.0, 0.0).astype(BF16)
    before = jnp.dot(multihot.astype(BF16), earlier, preferred_element_type=F32) + run_ref[...]
    krow = lax.broadcasted_iota(I32, (IDX_ROWS, ts), 0)
    idx_out = jnp.zeros((IDX_ROWS, ts), I32)
    rank_out = jnp.zeros((IDX_ROWS, ts), I32)
    gate_rows = jnp.zeros((IDX_ROWS, ts), F32)
    for k in range(TOP_K):
        rk = jnp.sum(jnp.where(hot[k], before, 0.0), axis=0, keepdims=True)
        idx_out = jnp.where(krow == k, idxs[k], idx_out)
        rank_out = jnp.where(krow == k, rk.astype(I32), rank_out)
        gate_rows = jnp.where(krow == k, es[k] / den, gate_rows)
    idx_ref[...] = idx_out
    rank_ref[...] = rank_out
    gate_ref[...] = jnp.transpose(jnp.concatenate(
        [gate_rows, jnp.zeros((LANES - IDX_ROWS, ts), F32)], axis=0))
    run_ref[...] = run_ref[...] + jnp.sum(multihot, axis=1, keepdims=True)
    cnt_ref[...] = run_ref[...]


def _mix(sinks, proj, x2d, kc, vc, w_s, bst, woa, wob, woc, wout, gffn, rw, rb,
         n_batch, seq, m_len, ts, batch0):
    d = x2d.shape[1]
    t = n_batch * seq
    pw = proj.shape[1]
    ns = seq // ts
    nblk = seq // ATT_BLOCK
    per = ts // ATT_BLOCK
    n_chunks = d // (2 * SC_CHUNK)
    full = lambda shape: pl.BlockSpec(shape, lambda b, s: (0,) * len(shape))
    row = lambda width: pl.BlockSpec((ts, width), lambda b, s: (b * ns + s, 0))
    prev = lambda colblk: pl.BlockSpec(
        (ATT_BLOCK, 4 * LANES), lambda b, s: (b * nblk + jnp.maximum(s * per - 1, 0), colblk))
    return pl.pallas_call(
        functools.partial(_mix_kernel, d_model=d),
        grid=(n_batch, ns),
        in_specs=[
            pl.BlockSpec(memory_space=pltpu.SMEM),
            row(pw), prev(OFF_K4 // (4 * LANES)), prev(OFF_V4 // (4 * LANES)),
            pl.BlockSpec((ts, d), lambda b, s: ((batch0 + b) * ns + s, 0)),
            pl.BlockSpec((m_len, C_Q), lambda b, s: (batch0 + b, 0)),
            pl.BlockSpec((m_len, C_Q), lambda b, s: (batch0 + b, 0)),
            full((GMLP_GROUPS, GMLP_CHUNK, GMLP_CHUNK)), full((GMLP_CHUNK, GMLP_GROUPS)),
            full((A_Q, d)), full((GMLP_WIDTH, d)), full((C_Q, d)), full((d, d)),
            full((1, d)), full((d, 2 * LANES)), full((1, LANES)),
        ],
        out_specs=[row(d),
                   pl.BlockSpec((n_chunks, ts, SC_CHUNK), lambda b, s: (0, b * ns + s, 0)),
                   pl.BlockSpec((IDX_ROWS, ts), lambda b, s: (0, b * ns + s)),
                   pl.BlockSpec((IDX_ROWS, ts), lambda b, s: (0, b * ns + s)),
                   row(LANES), full((N_EXPERTS, 1))],
        out_shape=[
            jax.ShapeDtypeStruct((t, d), F32),
            jax.ShapeDtypeStruct((n_chunks, t, SC_CHUNK), U32),
            jax.ShapeDtypeStruct((IDX_ROWS, t), I32), jax.ShapeDtypeStruct((IDX_ROWS, t), I32),
            jax.ShapeDtypeStruct((t, LANES), F32), jax.ShapeDtypeStruct((N_EXPERTS, 1), F32),
        ],
        scratch_shapes=[
            pltpu.VMEM((N_EXPERTS, 1), F32),
            pltpu.VMEM((ts, A_Q), BF16), pltpu.VMEM((ts, GMLP_WIDTH), BF16),
            pltpu.VMEM((ts, C_Q), BF16),
        ],
        name="mix",
        cost_estimate=pl.CostEstimate(
            flops=2 * t * (d * (A_Q + GMLP_WIDTH + C_Q + d) + 4 * ATT_BLOCK * A_Q
                           + GMLP_CHUNK * GMLP_WIDTH + 2 * m_len * C_Q + 3 * d * LANES),
            transcendentals=t * (2 * ATT_BLOCK * N_Q_HEADS + m_len * X_HEADS),
            bytes_accessed=t * (2 * pw + 4 * d + 4 * d + 2 * d + 12 * LANES)),
        compiler_params=pltpu.CompilerParams(
            dimension_semantics=("arbitrary", "arbitrary"), vmem_limit_bytes=VMEM_LIMIT),
    )(sinks, proj, proj, proj, x2d, kc, vc, w_s, bst, woa, wob, woc, wout, gffn, rw, rb)


def _pos_kernel(pstart_ref, idx_ref, rank_ref, pos_ref):
    idx = idx_ref[...]
    pos = rank_ref[...]
    for e in range(N_EXPERTS):
        pos = pos + jnp.where(idx == e, pstart_ref[e], 0)
    pos_ref[...] = pos


def _pos(pstart, idx, rank, tp):
    t = idx.shape[1]
    return pl.pallas_call(
        _pos_kernel,
        grid_spec=pltpu.PrefetchScalarGridSpec(
            num_scalar_prefetch=1,
            grid=(t // tp,),
            in_specs=[pl.BlockSpec((IDX_ROWS, tp), lambda i, ps: (0, i)),
                      pl.BlockSpec((IDX_ROWS, tp), lambda i, ps: (0, i))],
            out_specs=pl.BlockSpec((IDX_ROWS, tp), lambda i, ps: (0, i)),
        ),
        out_shape=jax.ShapeDtypeStruct((IDX_ROWS, t), I32),
        name="slot_pos",
        compiler_params=pltpu.CompilerParams(dimension_semantics=("arbitrary",)),
    )(pstart, idx, rank)


def _sc_mesh():
    return plsc.VectorSubcoreMesh(core_axis_name="core", subcore_axis_name="subcore")


def _sc_scatter_rows(src, idx, n_rows):
    n_chunks, t, w = src.shape
    n_idx = idx.shape[1]
    src_blocks = t // SC_ROWS
    idx_blocks = n_idx // SC_ROWS

    @pl.kernel(out_type=jax.ShapeDtypeStruct((n_chunks * n_rows, w), src.dtype), mesh=_sc_mesh(),
               scratch_types=[], name="sc_dispatch",
               cost_estimate=pl.CostEstimate(
                   flops=0, transcendentals=0,
                   bytes_accessed=n_chunks * n_idx * (8 * w + 4)))
    def scatter(src_hbm, idx_hbm, out_hbm):
        def body(src_vmem, idx_vmem):
            pltpu.sync_copy(src_vmem, out_hbm.at[idx_vmem.at[0]])

        pltpu.emit_pipeline(
            body,
            grid=(n_chunks, idx_blocks),
            in_specs=[pl.BlockSpec((SC_ROWS, w), lambda c, i: (c * src_blocks + i % src_blocks, 0)),
                      pl.BlockSpec((1, SC_ROWS), lambda c, i: (0, c * idx_blocks + i))],
            out_specs=[],
            core_axis_name=("core", "subcore"),
            dimension_semantics=(pltpu.PARALLEL, pltpu.PARALLEL),
        )(src_hbm, idx_hbm)

    return scatter(src.reshape(n_chunks * t, w), idx.reshape(1, n_chunks * n_idx)).reshape(
        n_chunks, n_rows, w)


def _sc_gather_rows(table, idx):
    n_chunks, p, w = table.shape
    n_idx = idx.shape[1]
    idx_blocks = n_idx // SC_ROWS

    @pl.kernel(out_type=jax.ShapeDtypeStruct((n_chunks * n_idx, w), table.dtype), mesh=_sc_mesh(),
               scratch_types=[], name="sc_gather",
               cost_estimate=pl.CostEstimate(
                   flops=0, transcendentals=0,
                   bytes_accessed=n_chunks * n_idx * (8 * w + 4)))
    def gather(table_hbm, idx_hbm, out_hbm):
        def body(idx_vmem, out_vmem):
            pltpu.sync_copy(table_hbm.at[idx_vmem.at[0]], out_vmem)

        pltpu.emit_pipeline(
            body,
            grid=(n_chunks, idx_blocks),
            in_specs=[pl.BlockSpec((1, SC_ROWS), lambda c, i: (0, c * idx_blocks + i))],
            out_specs=[pl.BlockSpec((SC_ROWS, w), lambda c, i: (c * idx_blocks + i, 0))],
            core_axis_name=("core", "subcore"),
            dimension_semantics=(pltpu.PARALLEL, pltpu.PARALLEL),
        )(idx_hbm, out_hbm)

    return gather(table.reshape(n_chunks * p, w), idx.reshape(1, n_chunks * n_idx)).reshape(
        n_chunks, n_idx, w)


def _expert_kernel(bexp_ref, nvalid_ref, next_ref, xs_ref, wgu_hbm, bgu_ref, wd_hbm, bd_ref, ys_ref,
                   wgu_stage, wd_stage, wgu_bf_ref, wd_bf_ref, sems, *, d_exp):
    n_chunks = xs_ref.shape[0]

    def weight_copies(e):
        return (pltpu.make_async_copy(wgu_hbm.at[e], wgu_stage, sems.at[0]),
                pltpu.make_async_copy(wd_hbm.at[e], wd_stage, sems.at[1]))

    def switch_weights(b):
        e = bexp_ref[b]

        @pl.when(jnp.logical_or(b == 0, e != bexp_ref[jnp.maximum(b - 1, 0)]))
        def _():
            @pl.when(b == 0)
            def _():
                for cp in weight_copies(e):
                    cp.start()

            for cp in weight_copies(e):
                cp.wait()
            wgu_bf_ref[...] = wgu_stage[...].astype(BF16)
            wd_bf_ref[...] = wd_stage[...].astype(BF16)
            e_next = next_ref[e]

            @pl.when(e_next >= 0)
            def _():
                for cp in weight_copies(e_next):
                    cp.start()

    def compute(e, rows):
        lo, hi = _unpack_bf16_pairs(
            jnp.concatenate([xs_ref[c, rows, :] for c in range(n_chunks)], axis=1))
        xb = jnp.concatenate([lo, hi], axis=1).astype(BF16)
        gu = jnp.dot(xb, wgu_bf_ref[...], preferred_element_type=F32) + bgu_ref[e]
        gate = jnp.minimum(gu[:, :d_exp], SWIGLU_LIMIT)
        up = jnp.clip(gu[:, d_exp:], -SWIGLU_LIMIT, SWIGLU_LIMIT)
        half_gate = 0.5 * gate
        glu = half_gate + half_gate * jnp.tanh(gate * (0.5 * SWIGLU_ALPHA))
        act = ((up + 1.0) * glu).astype(BF16)
        y = jnp.dot(act, wd_bf_ref[...], preferred_element_type=F32) + bd_ref[e]
        y_words = _pack_bf16_pairs(y)
        for c in range(n_chunks):
            ys_ref[c, rows, :] = y_words[:, c * SC_CHUNK:(c + 1) * SC_CHUNK]

    b0 = pl.program_id(0) * EXPERT_BLOCKS_PER_STEP
    b_last = b0 + EXPERT_BLOCKS_PER_STEP - 1
    block_rows = [slice(h * MOE_BLOCK, (h + 1) * MOE_BLOCK) for h in range(EXPERT_BLOCKS_PER_STEP)]
    one_expert = jnp.logical_and(b_last < nvalid_ref[0], bexp_ref[b_last] == bexp_ref[b0])

    @pl.when(b0 < nvalid_ref[0])
    def _():
        switch_weights(b0)

        @pl.when(one_expert)
        def _():
            for rows in block_rows:
                compute(bexp_ref[b0], rows)

        @pl.when(jnp.logical_not(one_expert))
        def _():
            compute(bexp_ref[b0], block_rows[0])
            for h in range(1, EXPERT_BLOCKS_PER_STEP):
                @pl.when(b0 + h < nvalid_ref[0])
                def _():
                    switch_weights(b0 + h)
                    compute(bexp_ref[b0 + h], block_rows[h])


def _experts(bexp, nvalid, next_expert, xs, wgu, bgu, wd, bd):
    n_chunks, n_rows, _ = xs.shape
    n_exp, d, d_exp2 = wgu.shape
    d_exp = d_exp2 // 2
    step_rows = EXPERT_BLOCKS_PER_STEP * MOE_BLOCK
    blk = lambda s, be, nv, nx: jnp.minimum(s, (nv[0] - 1) // EXPERT_BLOCKS_PER_STEP)
    return pl.pallas_call(
        functools.partial(_expert_kernel, d_exp=d_exp),
        grid_spec=pltpu.PrefetchScalarGridSpec(
            num_scalar_prefetch=3,
            grid=(n_rows // step_rows,),
            in_specs=[
                pl.BlockSpec((n_chunks, step_rows, SC_CHUNK),
                             lambda s, be, nv, nx: (0, blk(s, be, nv, nx), 0)),
                pl.BlockSpec(memory_space=pl.ANY),
                pl.BlockSpec((n_exp, 1, 2 * d_exp), lambda s, be, nv, nx: (0, 0, 0)),
                pl.BlockSpec(memory_space=pl.ANY),
                pl.BlockSpec((n_exp, 1, d), lambda s, be, nv, nx: (0, 0, 0)),
            ],
            out_specs=pl.BlockSpec((n_chunks, step_rows, SC_CHUNK),
                                   lambda s, be, nv, nx: (0, blk(s, be, nv, nx), 0)),
            scratch_shapes=[pltpu.VMEM((d, 2 * d_exp), F32), pltpu.VMEM((d_exp, d), F32),
                            pltpu.VMEM((d, 2 * d_exp), BF16), pltpu.VMEM((d_exp, d), BF16),
                            pltpu.SemaphoreType.DMA((2,))],
        ),
        out_shape=jax.ShapeDtypeStruct((n_chunks, n_rows, SC_CHUNK), U32),
        name="experts",
        cost_estimate=pl.CostEstimate(
            flops=6 * n_rows * d * d_exp, transcendentals=n_rows * d_exp,
            bytes_accessed=4 * n_rows * d + 4 * N_EXPERTS * 3 * d * d_exp),
        compiler_params=pltpu.CompilerParams(
            dimension_semantics=("arbitrary",), vmem_limit_bytes=VMEM_LIMIT),
    )(bexp, nvalid, next_expert, xs, wgu, bgu, wd, bd)


def _combine_kernel(prev_ref, x1_ref, gate_ref, yg_ref, out_ref):
    del prev_ref
    n_chunks = yg_ref.shape[0]
    half = n_chunks * SC_CHUNK
    for c in range(n_chunks):
        sl_lo = slice(c * SC_CHUNK, (c + 1) * SC_CHUNK)
        sl_hi = slice(half + c * SC_CHUNK, half + (c + 1) * SC_CHUNK)
        acc_lo = x1_ref[:, sl_lo]
        acc_hi = x1_ref[:, sl_hi]
        for k in range(TOP_K):
            lo, hi = _unpack_bf16_pairs(yg_ref[c, k])
            g = gate_ref[:, k:k + 1]
            acc_lo = acc_lo + g * lo
            acc_hi = acc_hi + g * hi
        out_ref[:, sl_lo] = acc_lo
        out_ref[:, sl_hi] = acc_hi


def _combine(x1, gates, yg, out_prev, tcb, row0, t_total):
    t, d = x1.shape
    n_chunks = yg.shape[0]
    blk0 = row0 // tcb
    in_specs = [
        pl.BlockSpec((tcb, d), lambda i: (i, 0)),
        pl.BlockSpec((tcb, LANES), lambda i: (i, 0)),
        pl.BlockSpec((n_chunks, TOP_K, tcb, SC_CHUNK), lambda i: (0, 0, i, 0)),
    ]
    args = [x1, gates, yg]
    aliases = {}
    body = functools.partial(_combine_kernel, None)
    if out_prev is not None:
        in_specs.append(pl.BlockSpec(memory_space=pl.ANY))
        args.append(out_prev)
        aliases = {3: 0}
        body = lambda a, b, c, prev, o: _combine_kernel(prev, a, b, c, o)
    return pl.pallas_call(
        body,
        grid=(t // tcb,),
        in_specs=in_specs,
        out_specs=pl.BlockSpec((tcb, d), lambda i: (blk0 + i, 0)),
        out_shape=jax.ShapeDtypeStruct((t_total, d), F32),
        input_output_aliases=aliases,
        name="combine",
        cost_estimate=pl.CostEstimate(
            flops=2 * TOP_K * t * d, transcendentals=0,
            bytes_accessed=t * (4 * d + 4 * d + 2 * TOP_K * d + 4 * LANES)),
        compiler_params=pltpu.CompilerParams(
            dimension_semantics=("arbitrary",), vmem_limit_bytes=VMEM_LIMIT),
    )(*args)


def _pick_tile(n, pref, *also):
    t = pref
    while any(v % t for v in (n,) + also):
        t //= 2
    return t


def _token_groups(n_batch):
    if n_batch % 4 == 0:
        return [3 * n_batch // 4, n_batch // 4]
    if n_batch % 2 == 0:
        return [n_batch // 2, n_batch // 2]
    return [n_batch]


def kernel(x, mem, positions, attn_norm_g, mem_norm_g, w_in, b_gates, a_q_norm_g, a_k_norm_g,
           a_sinks, w_o_a, gmlp_ln_g, gmlp_ln_b, gmlp_w_s, gmlp_b_s, w_o_b, w_mem_kv,
           c_q_norm_g, c_k_norm_g, w_o_c, w_out, ffn_norm_g, router_w, router_b,
           w_gate_up, b_gate_up, w_down, b_down):
    n_batch, seq, d = x.shape
    m_len = mem.shape[1]
    depth = w_in.shape[0]
    t = n_batch * seq
    group_batches = _token_groups(n_batch)

    inv_freq = ROPE_THETA ** (-jnp.arange(0, HEAD_DIM, 2, dtype=F32) / HEAD_DIM)
    invf = inv_freq[:, None]
    sgn = jnp.tile(jnp.concatenate([-jnp.ones((HEAD_DIM // 2,), F32),
                                    jnp.ones((HEAD_DIM // 2,), F32)]), LANES // HEAD_DIM)[None, :]
    pos2d = positions.reshape(t, 1).astype(I32)
    mem2d = mem.reshape(n_batch * m_len, d)
    x2d = x.reshape(t, d)

    ts = _pick_tile(seq, 512)

    for l in range(depth):
        kc, vc = _mem_kv(mem2d, mem_norm_g[l][None, :], w_mem_kv[l].astype(BF16),
                         c_k_norm_g[l][None, :], n_batch, m_len)
        w_in_bf = w_in[l].astype(BF16)
        mix_w = (w_o_a[l].astype(BF16), w_o_b[l].astype(BF16), w_o_c[l].astype(BF16),
                 w_out[l].astype(BF16))
        gq = jnp.tile(a_q_norm_g[l], LANES // HEAD_DIM)[None, :] * (HEAD_DIM ** -0.5 * LOG2_E)
        gcq = c_q_norm_g[l][None, :] * (X_HEAD_DIM ** -0.5 * LOG2_E)
        gk = jnp.tile(a_k_norm_g[l], LANES // HEAD_DIM)[None, :]
        rw32 = jnp.pad(router_w[l], ((0, 0), (0, LANES - N_EXPERTS)))
        rw_hi = rw32.astype(BF16)
        rw = jnp.concatenate([rw_hi, (rw32 - rw_hi.astype(F32)).astype(BF16)], axis=1)
        rb = jnp.pad(router_b[l], (0, LANES - N_EXPERTS))[None, :]

        routed = []
        scatter_idx = None
        batch0 = 0
        for gb in group_batches:
            tg = gb * seq
            row0 = batch0 * seq
            n_asg = tg * TOP_K
            nb = -(-n_asg // MOE_BLOCK) + N_EXPERTS
            nb = -(-nb // EXPERT_BLOCKS_PER_STEP) * EXPERT_BLOCKS_PER_STEP
            n_rows = nb * MOE_BLOCK
            tm = _pick_tile(tg, 2 * PROJ_SUB_ROWS, row0, t)
            proj = _proj(
                x2d, pos2d, attn_norm_g[l][None, :], w_in_bf, invf, sgn, gq, gk,
                gcq, gmlp_ln_g[l][None, :], gmlp_ln_b[l][None, :],
                b_gates[l].reshape(1, 3 * d), tm, row0, tg, scatter_idx)
            x1, h2, idx, rank, gates, counts = _mix(
                a_sinks[l], proj, x2d, kc, vc, gmlp_w_s[l], gmlp_b_s[l].T, *mix_w,
                ffn_norm_g[l][None, :], rw, rb, gb, seq, m_len, ts, batch0)

            cnt = counts[:, 0].astype(I32)
            padded = (cnt + MOE_BLOCK - 1) // MOE_BLOCK * MOE_BLOCK
            pend = jnp.cumsum(padded)
            pstart = (pend - padded).astype(I32)
            nvalid = (pend[-1:] // MOE_BLOCK).astype(I32)
            blk_row = jnp.arange(nb, dtype=I32) * MOE_BLOCK
            bexp = jnp.minimum(jnp.sum((pend[None, :] <= blk_row[:, None]).astype(I32), axis=1),
                               N_EXPERTS - 1).astype(I32)
            eid = jnp.arange(N_EXPERTS, dtype=I32)
            later = jnp.logical_and(eid[None, :] > eid[:, None], (cnt > 0)[None, :])
            next_expert = jnp.min(jnp.where(later, eid[None, :], N_EXPERTS), axis=1)
            next_expert = jnp.where(next_expert == N_EXPERTS, -1, next_expert).astype(I32)
            j = jnp.arange(MOE_BLOCK, dtype=I32)[None, :]
            fill = jnp.where(j < (padded - cnt)[:, None], pend[:, None] - 1 - j,
                             n_rows - 1 - j).astype(I32)

            pos = _pos(pstart, idx, rank, _pick_tile(tg, 8192))
            pos_km = pos[:TOP_K].reshape(n_asg)
            n_chunks = h2.shape[0]
            chunk_off = (jnp.arange(n_chunks, dtype=I32) * n_rows)[:, None]
            scatter_idx = jnp.concatenate([pos_km, fill.reshape(-1)])[None, :] + chunk_off
            xs = _sc_scatter_rows(h2, scatter_idx, n_rows)
            routed.append((row0, x1, gates, xs, bexp, nvalid, next_expert, pos_km, chunk_off))
            batch0 += gb

        gathered = []
        for row0, x1, gates, xs, bexp, nvalid, next_expert, pos_km, chunk_off in routed:
            ys = _experts(bexp, nvalid, next_expert, xs, w_gate_up[l], b_gate_up[l][:, None, :],
                          w_down[l], b_down[l][:, None, :])
            yg = _sc_gather_rows(ys, pos_km[None, :] + chunk_off)
            gathered.append(
                (row0, x1, gates, yg.reshape(yg.shape[0], TOP_K, x1.shape[0], SC_CHUNK)))

        out = None
        for row0, x1, gates, yg in gathered:
            out = _combine(x1, gates, yg, out, _pick_tile(x1.shape[0], 1024, row0), row0, t)
        x2d = out
    return x2d.reshape(n_batch, seq, d)
```

```python
import functools

import numpy as np
import jax
import jax.numpy as jnp
from jax import lax
from jax.experimental import pallas as pl
from jax.experimental.pallas import tpu as pltpu
from jax.experimental.pallas import tpu_sc as plsc

F32 = jnp.float32
BF16 = jnp.bfloat16
I32 = jnp.int32
U32 = jnp.uint32
HI16 = np.uint32(0xFFFF0000)
LOG2_E = float(np.log2(np.e))

EPS = 1e-6
LANES = 128
HEAD_DIM = 64
N_Q_HEADS = 16
N_KV_HEADS = 2
ATT_BLOCK = 128
ROPE_THETA = 10000.0
GMLP_WIDTH = 512
GMLP_GROUPS = 4
GMLP_CHUNK = 128
X_HEADS = 4
X_HEAD_DIM = 128
N_EXPERTS = 32
TOP_K = 4
SWIGLU_LIMIT = 7.0
SWIGLU_ALPHA = 1.702
MOE_BLOCK = 512
SC_ROWS = 128
SC_CHUNK = 256
PROJ_SUB_ROWS = 512
EXPERT_BLOCKS_PER_STEP = 4
IDX_ROWS = 8

A_Q = N_Q_HEADS * HEAD_DIM
A_KV = N_KV_HEADS * HEAD_DIM
C_Q = X_HEADS * X_HEAD_DIM

OFF_Q = 0
OFF_K4 = OFF_Q + A_Q
OFF_V4 = OFF_K4 + 4 * LANES
OFF_U = OFF_V4 + 4 * LANES
OFF_VN = OFF_U + GMLP_WIDTH
OFF_QC = OFF_VN + GMLP_WIDTH
OFF_GA = OFF_QC + C_Q
PROJ_W_BASE = OFF_GA

VMEM_LIMIT = 56 * 1024 * 1024


def _lane_iota(shape):
    return lax.broadcasted_iota(I32, shape, len(shape) - 1)


def _rms(x, g):
    return x * lax.rsqrt(jnp.mean(x * x, axis=-1, keepdims=True) + EPS) * g


def _pack_bf16_pairs(x):
    n = x.shape[1] // 2
    bits = pltpu.bitcast(x.astype(BF16).astype(F32), U32)
    return (bits[:, :n] >> 16) | (bits[:, n:] & HI16)


def _unpack_bf16_pairs(w):
    return pltpu.bitcast(w << 16, F32), pltpu.bitcast(w & HI16, F32)


def _gelu(x):
    return 0.5 * x * (1.0 + lax.erf(x * np.float32(np.sqrt(0.5))))


def _memkv_kernel(mem_ref, g_ref, w_ref, gk_ref, kc_ref, vc_ref):
    h = _rms(mem_ref[...], g_ref[...]).astype(BF16)
    kv = jnp.dot(h, w_ref[...], preferred_element_type=F32)
    for hh in range(X_HEADS):
        sl = slice(hh * X_HEAD_DIM, (hh + 1) * X_HEAD_DIM)
        kc_ref[:, sl] = _rms(kv[:, sl], gk_ref[...]).astype(BF16)
    vc_ref[...] = kv[:, C_Q:].astype(BF16)


def _mem_kv(mem2d, g, w_bf, gk, n_batch, m_len):
    d = mem2d.shape[1]
    rows = _pick_tile(n_batch * m_len, 1024)
    return pl.pallas_call(
        _memkv_kernel,
        grid=(n_batch * m_len // rows,),
        in_specs=[
            pl.BlockSpec((rows, d), lambda b: (b, 0)),
            pl.BlockSpec((1, d), lambda b: (0, 0)),
            pl.BlockSpec((d, 2 * C_Q), lambda b: (0, 0)),
            pl.BlockSpec((1, X_HEAD_DIM), lambda b: (0, 0)),
        ],
        out_specs=[
            pl.BlockSpec((rows, C_Q), lambda b: (b, 0)),
            pl.BlockSpec((rows, C_Q), lambda b: (b, 0)),
        ],
        out_shape=[jax.ShapeDtypeStruct((n_batch * m_len, C_Q), BF16)] * 2,
        name="mem_kv",
        compiler_params=pltpu.CompilerParams(dimension_semantics=("arbitrary",)),
    )(mem2d, g, w_bf, gk)


def _proj_kernel(x_ref, pos_ref, g_ref, w_ref, invf_ref, sgn_ref, gq_ref, gk_ref, gcq_ref,
                 lng_ref, lnb_ref, bg_ref, out_ref, *, d_model):
    sub = min(PROJ_SUB_ROWS, x_ref.shape[0])
    for r in range(x_ref.shape[0] // sub):
        rows = pl.ds(r * sub, sub)
        _proj_rows(x_ref.at[rows], pos_ref.at[:, rows], g_ref, w_ref, invf_ref, sgn_ref, gq_ref,
                   gk_ref, gcq_ref, lng_ref, lnb_ref, bg_ref, out_ref.at[rows], d_model=d_model)


def _proj_rows(x_ref, pos_ref, g_ref, w_ref, invf_ref, sgn_ref, gq_ref, gk_ref, gcq_ref,
               lng_ref, lnb_ref, bg_ref, out_ref, *, d_model):
    tm = x_ref.shape[0]
    h = _rms(x_ref[...], g_ref[...]).astype(BF16)

    ang_t = invf_ref[...] * pos_ref[...].astype(F32)
    reps = LANES // (HEAD_DIM // 2)
    cosv = jnp.transpose(jnp.concatenate([jnp.cos(ang_t)] * reps, axis=0))
    sinv = jnp.transpose(jnp.concatenate([jnp.sin(ang_t)] * reps, axis=0)) * sgn_ref[...]
    lane = _lane_iota((tm, LANES))
    first_head = lane < HEAD_DIM
    lo_half = (lane % HEAD_DIM) < (HEAD_DIM // 2)

    def head_norm_rope(blk, g):
        y = blk * blk
        s_lo = jnp.sum(jnp.where(first_head, y, 0.0), axis=-1, keepdims=True)
        s_hi = jnp.sum(jnp.where(first_head, 0.0, y), axis=-1, keepdims=True)
        ss = jnp.where(first_head, s_lo, s_hi)
        n = blk * lax.rsqrt(ss * (1.0 / HEAD_DIM) + EPS) * g
        rot = jnp.where(lo_half, pltpu.roll(n, LANES - HEAD_DIM // 2, 1),
                        pltpu.roll(n, HEAD_DIM // 2, 1))
        return n * cosv + rot * sinv

    def proj(a, b):
        return jnp.dot(h, w_ref[:, a:b], preferred_element_type=F32)

    pq = proj(0, A_Q)
    for c in range(A_Q // LANES):
        sl = slice(c * LANES, (c + 1) * LANES)
        out_ref[:, OFF_Q + c * LANES:OFF_Q + (c + 1) * LANES] = (
            head_norm_rope(pq[:, sl], gq_ref[...])).astype(BF16)

    pkv = proj(A_Q, A_Q + 2 * A_KV)
    kn = head_norm_rope(pkv[:, :LANES], gk_ref[...])
    vv = pkv[:, LANES:]
    for off, t in ((OFF_K4, kn), (OFF_V4, vv)):
        tr = pltpu.roll(t, HEAD_DIM, 1)
        parts = (jnp.where(first_head, t, 0.0), jnp.where(first_head, 0.0, tr),
                 jnp.where(first_head, tr, 0.0), jnp.where(first_head, 0.0, t))
        for j, p in enumerate(parts):
            out_ref[:, off + j * LANES:off + (j + 1) * LANES] = p.astype(BF16)

    o0 = A_Q + 2 * A_KV
    out_ref[:, OFF_U:OFF_U + GMLP_WIDTH] = _gelu(proj(o0, o0 + GMLP_WIDTH)).astype(BF16)
    gv = _gelu(proj(o0 + GMLP_WIDTH, o0 + 2 * GMLP_WIDTH))
    mu = jnp.mean(gv, axis=-1, keepdims=True)
    var = jnp.mean(jnp.square(gv - mu), axis=-1, keepdims=True)
    out_ref[:, OFF_VN:OFF_VN + GMLP_WIDTH] = (
        (gv - mu) * lax.rsqrt(var + EPS) * lng_ref[...] + lnb_ref[...]).astype(BF16)

    o1 = o0 + 2 * GMLP_WIDTH
    pc = proj(o1, o1 + C_Q)
    for hh in range(X_HEADS):
        sl = slice(hh * X_HEAD_DIM, (hh + 1) * X_HEAD_DIM)
        out_ref[:, OFF_QC + hh * X_HEAD_DIM:OFF_QC + (hh + 1) * X_HEAD_DIM] = (
            _rms(pc[:, sl], gcq_ref[...])).astype(BF16)

    o2 = o1 + C_Q
    for j in range(3):
        sl = slice(j * d_model, (j + 1) * d_model)
        z = proj(o2 + j * d_model, o2 + (j + 1) * d_model) + bg_ref[:, sl]
        out_ref[:, OFF_GA + j * d_model:OFF_GA + (j + 1) * d_model] = (
            0.5 * jnp.tanh(0.5 * z) + 0.5).astype(BF16)


def _proj_after_kernel(after_ref, *refs, d_model):
    del after_ref
    _proj_kernel(*refs, d_model=d_model)


def _proj(x2d, pos2d, g, w_bf, invf, sgn, gq, gk, gcq, lng, lnb, bg, tm, row0, t, after):
    d = x2d.shape[1]
    d_in = w_bf.shape[1]
    pw = PROJ_W_BASE + 3 * d
    blk0 = row0 // tm
    full = lambda shape: pl.BlockSpec(shape, lambda i: (0,) * len(shape))
    body = functools.partial(_proj_kernel, d_model=d)
    lead_specs, lead_args = [], []
    if after is not None:
        body = functools.partial(_proj_after_kernel, d_model=d)
        lead_specs, lead_args = [pl.BlockSpec(memory_space=pl.ANY)], [after]
    return pl.pallas_call(
        body,
        grid=(t // tm,),
        in_specs=lead_specs + [
            pl.BlockSpec((tm, d), lambda i: (blk0 + i, 0)),
            pl.BlockSpec((None, 1, tm), lambda i: (blk0 + i, 0, 0)),
            full((1, d)),
            pl.BlockSpec((d, d_in), lambda i: (0, 0), pipeline_mode=pl.Buffered(1)),
            full((HEAD_DIM // 2, 1)), full((1, LANES)),
            full((1, LANES)), full((1, LANES)), full((1, X_HEAD_DIM)),
            full((1, GMLP_WIDTH)), full((1, GMLP_WIDTH)), full((1, 3 * d)),
        ],
        out_specs=pl.BlockSpec((tm, pw), lambda i: (i, 0)),
        out_shape=jax.ShapeDtypeStruct((t, pw), BF16),
        name="proj",
        cost_estimate=pl.CostEstimate(
            flops=2 * t * d * d_in, transcendentals=t * (3 * d + 2 * GMLP_WIDTH + 2 * LANES),
            bytes_accessed=4 * t * d + 2 * t * pw + 2 * d * d_in),
        compiler_params=pltpu.CompilerParams(
            dimension_semantics=("arbitrary",), vmem_limit_bytes=VMEM_LIMIT),
    )(*lead_args, x2d, pos2d.reshape(-1, 1, tm), g, w_bf, invf, sgn, gq, gk, gcq, lng, lnb, bg)


def _mix_kernel(sinks_ref, proj_ref, kprev_ref, vprev_ref, x_ref, kc_ref, vc_ref, ws_ref, bst_ref,
                woa_ref, wob_ref, woc_ref, wout_ref, gffn_ref, rw_ref, rb_ref,
                x1_ref, h2_ref, idx_ref, rank_ref, gate_ref, cnt_ref,
                run_ref, oa_ref, ob_ref, oc_ref, *, d_model):
    ts = x_ref.shape[0]
    s_idx = pl.program_id(1)
    first_step = jnp.logical_and(pl.program_id(0) == 0, s_idx == 0)
    neg_inf = float("-inf")

    cpk = A_Q // LANES // N_KV_HEADS
    stack = cpk * ATT_BLOCK
    srow = lax.broadcasted_iota(I32, (stack, 2 * ATT_BLOCK), 0)
    qi = srow % ATT_BLOCK
    kj = lax.broadcasted_iota(I32, (stack, 2 * ATT_BLOCK), 1)
    band = jnp.logical_and(kj <= ATT_BLOCK + qi, kj > qi)
    chunk_of_row = lax.broadcasted_iota(I32, (stack, 1), 0) // ATT_BLOCK
    for qb in range(ts // ATT_BLOCK):
        r0 = qb * ATT_BLOCK
        rows = slice(r0, r0 + ATT_BLOCK)
        if qb == 0:
            kp, vp = kprev_ref[...], vprev_ref[...]
            mask = jnp.logical_and(band, jnp.logical_or(kj >= ATT_BLOCK, s_idx > 0))
        else:
            prow = slice(r0 - ATT_BLOCK, r0)
            kp = proj_ref[prow, OFF_K4:OFF_K4 + 4 * LANES]
            vp = proj_ref[prow, OFF_V4:OFF_V4 + 4 * LANES]
            mask = band
        k4 = jnp.concatenate([kp, proj_ref[rows, OFF_K4:OFF_K4 + 4 * LANES]], axis=0)
        v4 = jnp.concatenate([vp, proj_ref[rows, OFF_V4:OFF_V4 + 4 * LANES]], axis=0)
        for kvh in range(N_KV_HEADS):
            c0 = kvh * cpk
            q4 = jnp.concatenate(
                [proj_ref[rows, OFF_Q + (c0 + c) * LANES:OFF_Q + (c0 + c + 1) * LANES]
                 for c in range(cpk)], axis=0)
            o = jnp.zeros((stack, LANES), F32)
            for half in range(2):
                col = slice((2 * kvh + half) * LANES, (2 * kvh + half + 1) * LANES)
                s = lax.dot_general(q4, k4[:, col], (((1,), (1,)), ((), ())),
                                    preferred_element_type=F32)
                s = jnp.where(mask, s, neg_inf)
                sink = jnp.zeros((stack, 1), F32)
                for c in range(cpk):
                    sink = jnp.where(chunk_of_row == c,
                                     sinks_ref[2 * (c0 + c) + half] * LOG2_E, sink)
                m = jnp.maximum(jnp.max(s, axis=-1, keepdims=True), sink)
                p = jnp.exp2(s - m)
                den = jnp.sum(p, axis=-1, keepdims=True) + jnp.exp2(sink - m)
                o = o + jnp.dot(p.astype(BF16), v4[:, col], preferred_element_type=F32) / den
            for c in range(cpk):
                oa_ref[rows, (c0 + c) * LANES:(c0 + c + 1) * LANES] = (
                    o[c * ATT_BLOCK:(c + 1) * ATT_BLOCK].astype(BF16))

    ti = lax.broadcasted_iota(I32, (GMLP_CHUNK, GMLP_CHUNK), 0)
    si = lax.broadcasted_iota(I32, (GMLP_CHUNK, GMLP_CHUNK), 1)
    for g in range(GMLP_GROUPS):
        wt = jnp.where(si <= ti, ws_ref[g], 0.0).astype(BF16)
        bcol = bst_ref[:, g:g + 1]
        for ch in range(ts // GMLP_CHUNK):
            rows = slice(ch * GMLP_CHUNK, (ch + 1) * GMLP_CHUNK)
            vn = proj_ref[rows, OFF_VN + g * LANES:OFF_VN + (g + 1) * LANES]
            u = proj_ref[rows, OFF_U + g * LANES:OFF_U + (g + 1) * LANES].astype(F32)
            mixed = jnp.dot(wt, vn, preferred_element_type=F32) + bcol
            ob_ref[rows, g * LANES:(g + 1) * LANES] = (u * mixed).astype(BF16)

    for hh in range(X_HEADS):
        sl = slice(hh * X_HEAD_DIM, (hh + 1) * X_HEAD_DIM)
        qc = proj_ref[:, OFF_QC + hh * X_HEAD_DIM:OFF_QC + (hh + 1) * X_HEAD_DIM]
        s = lax.dot_general(qc, kc_ref[:, sl], (((1,), (1,)), ((), ())),
                            preferred_element_type=F32)
        p = jnp.exp2(s - jnp.max(s, axis=-1, keepdims=True))
        den = jnp.sum(p, axis=-1, keepdims=True)
        oc_ref[:, sl] = (jnp.dot(p.astype(BF16), vc_ref[:, sl],
                                 preferred_element_type=F32) / den).astype(BF16)

    def gate(j):
        return proj_ref[:, OFF_GA + j * d_model:OFF_GA + (j + 1) * d_model].astype(F32)

    merged = gate(0) * jnp.dot(oa_ref[...], woa_ref[...], preferred_element_type=F32)
    merged = merged + gate(1) * jnp.dot(ob_ref[...], wob_ref[...], preferred_element_type=F32)
    merged = merged + gate(2) * jnp.dot(oc_ref[...], woc_ref[...], preferred_element_type=F32)
    x1 = x_ref[...] + jnp.dot(merged.astype(BF16), wout_ref[...], preferred_element_type=F32)
    x1_ref[...] = x1

    h2 = _rms(x1, gffn_ref[...])
    h2_words = _pack_bf16_pairs(h2)
    for c in range(h2_ref.shape[0]):
        h2_ref[c] = h2_words[:, c * SC_CHUNK:(c + 1) * SC_CHUNK]
    h2_hi = h2.astype(BF16)
    h2_lo = (h2 - h2_hi.astype(F32)).astype(BF16)
    part = jnp.dot(h2_hi, rw_ref[...], preferred_element_type=F32)
    logits = (part[:, :LANES] + part[:, LANES:]
              + jnp.dot(h2_lo, rw_ref[:, :LANES], preferred_element_type=F32) + rb_ref[...])
    lt = jnp.transpose(logits)[:N_EXPERTS]
    erow = lax.broadcasted_iota(I32, (N_EXPERTS, ts), 0)
    vals, idxs = [], []
    for _ in range(TOP_K):
        m = jnp.max(lt, axis=0, keepdims=True)
        i = jnp.min(jnp.where(lt == m, erow, N_EXPERTS), axis=0, keepdims=True)
        vals.append(m)
        idxs.append(i)
        lt = jnp.where(erow == i, neg_inf, lt)
    es = [jnp.exp(v - vals[0]) for v in vals]
    den = es[0] + es[1] + es[2] + es[3]

    @pl.when(first_step)
    def _():
        run_ref[...] = jnp.zeros_like(run_ref)

    hot = [erow == i for i in idxs]
    multihot = jnp.where(jnp.logical_or(jnp.logical_or(hot[0], hot[1]),
                                        jnp.logical_or(hot[2], hot[3])), 1.0, 0.0)
    tr = lax.broadcasted_iota(I32, (ts, ts), 0)
    tc = lax.broadcasted_iota(I32, (ts, ts), 1)
    earlier = jnp.where(tr < tc, 1.0, 0.0).astype(BF16)
    before = jnp.dot(multihot.astype(BF16), earlier, preferred_element_type=F32) + run_ref[...]
    krow = lax.broadcasted_iota(I32, (IDX_ROWS, ts), 0)
    idx_out = jnp.zeros((IDX_ROWS, ts), I32)
    rank_out = jnp.zeros((IDX_ROWS, ts), I32)
    gate_rows = jnp.zeros((IDX_ROWS, ts), F32)
    for k in range(TOP_K):
        rk = jnp.sum(jnp.where(hot[k], before, 0.0), axis=0, keepdims=True)
        idx_out = jnp.where(krow == k, idxs[k], idx_out)
        rank_out = jnp.where(krow == k, rk.astype(I32), rank_out)
        gate_rows = jnp.where(krow == k, es[k] / den, gate_rows)
    idx_ref[...] = idx_out
    rank_ref[...] = rank_out
    gate_ref[...] = jnp.transpose(jnp.concatenate(
        [gate_rows, jnp.zeros((LANES - IDX_ROWS, ts), F32)], axis=0))
    run_ref[...] = run_ref[...] + jnp.sum(multihot, axis=1, keepdims=True)
    cnt_ref[...] = run_ref[...]


def _mix(sinks, proj, x2d, kc, vc, w_s, bst, woa, wob, woc, wout, gffn, rw, rb,
         n_batch, seq, m_len, ts, batch0):
    d = x2d.shape[1]
    t = n_batch * seq
    pw = proj.shape[1]
    ns = seq // ts
    nblk = seq // ATT_BLOCK
    per = ts // ATT_BLOCK
    n_chunks = d // (2 * SC_CHUNK)
    full = lambda shape: pl.BlockSpec(shape, lambda b, s: (0,) * len(shape))
    row = lambda width: pl.BlockSpec((ts, width), lambda b, s: (b * ns + s, 0))
    prev = lambda colblk: pl.BlockSpec(
        (ATT_BLOCK, 4 * LANES), lambda b, s: (b * nblk + jnp.maximum(s * per - 1, 0), colblk))
    return pl.pallas_call(
        functools.partial(_mix_kernel, d_model=d),
        grid=(n_batch, ns),
        in_specs=[
            pl.BlockSpec(memory_space=pltpu.SMEM),
            row(pw), prev(OFF_K4 // (4 * LANES)), prev(OFF_V4 // (4 * LANES)),
            pl.BlockSpec((ts, d), lambda b, s: ((batch0 + b) * ns + s, 0)),
            pl.BlockSpec((m_len, C_Q), lambda b, s: (batch0 + b, 0)),
            pl.BlockSpec((m_len, C_Q), lambda b, s: (batch0 + b, 0)),
            full((GMLP_GROUPS, GMLP_CHUNK, GMLP_CHUNK)), full((GMLP_CHUNK, GMLP_GROUPS)),
            full((A_Q, d)), full((GMLP_WIDTH, d)), full((C_Q, d)), full((d, d)),
            full((1, d)), full((d, 2 * LANES)), full((1, LANES)),
        ],
        out_specs=[row(d),
                   pl.BlockSpec((n_chunks, ts, SC_CHUNK), lambda b, s: (0, b * ns + s, 0)),
                   pl.BlockSpec((IDX_ROWS, ts), lambda b, s: (0, b * ns + s)),
                   pl.BlockSpec((IDX_ROWS, ts), lambda b, s: (0, b * ns + s)),
                   row(LANES), full((N_EXPERTS, 1))],
        out_shape=[
            jax.ShapeDtypeStruct((t, d), F32),
            jax.ShapeDtypeStruct((n_chunks, t, SC_CHUNK), U32),
            jax.ShapeDtypeStruct((IDX_ROWS, t), I32), jax.ShapeDtypeStruct((IDX_ROWS, t), I32),
            jax.ShapeDtypeStruct((t, LANES), F32), jax.ShapeDtypeStruct((N_EXPERTS, 1), F32),
        ],
        scratch_shapes=[
            pltpu.VMEM((N_EXPERTS, 1), F32),
            pltpu.VMEM((ts, A_Q), BF16), pltpu.VMEM((ts, GMLP_WIDTH), BF16),
            pltpu.VMEM((ts, C_Q), BF16),
        ],
        name="mix",
        cost_estimate=pl.CostEstimate(
            flops=2 * t * (d * (A_Q + GMLP_WIDTH + C_Q + d) + 4 * ATT_BLOCK * A_Q
                           + GMLP_CHUNK * GMLP_WIDTH + 2 * m_len * C_Q + 3 * d * LANES),
            transcendentals=t * (2 * ATT_BLOCK * N_Q_HEADS + m_len * X_HEADS),
            bytes_accessed=t * (2 * pw + 4 * d + 4 * d + 2 * d + 12 * LANES)),
        compiler_params=pltpu.CompilerParams(
            dimension_semantics=("arbitrary", "arbitrary"), vmem_limit_bytes=VMEM_LIMIT),
    )(sinks, proj, proj, proj, x2d, kc, vc, w_s, bst, woa, wob, woc, wout, gffn, rw, rb)


def _pos_kernel(pstart_ref, idx_ref, rank_ref, pos_ref):
    idx = idx_ref[...]
    pos = rank_ref[...]
    for e in range(N_EXPERTS):
        pos = pos + jnp.where(idx == e, pstart_ref[e], 0)
    pos_ref[...] = pos


def _pos(pstart, idx, rank, tp):
    t = idx.shape[1]
    return pl.pallas_call(
        _pos_kernel,
        grid_spec=pltpu.PrefetchScalarGridSpec(
            num_scalar_prefetch=1,
            grid=(t // tp,),
            in_specs=[pl.BlockSpec((IDX_ROWS, tp), lambda i, ps: (0, i)),
                      pl.BlockSpec((IDX_ROWS, tp), lambda i, ps: (0, i))],
            out_specs=pl.BlockSpec((IDX_ROWS, tp), lambda i, ps: (0, i)),
        ),
        out_shape=jax.ShapeDtypeStruct((IDX_ROWS, t), I32),
        name="slot_pos",
        compiler_params=pltpu.CompilerParams(dimension_semantics=("arbitrary",)),
    )(pstart, idx, rank)


def _sc_mesh():
    return plsc.VectorSubcoreMesh(core_axis_name="core", subcore_axis_name="subcore")


def _sc_scatter_rows(src, idx, n_rows):
    n_chunks, t, w = src.shape
    n_idx = idx.shape[1]
    src_blocks = t // SC_ROWS
    idx_blocks = n_idx // SC_ROWS

    @pl.kernel(out_type=jax.ShapeDtypeStruct((n_chunks * n_rows, w), src.dtype), mesh=_sc_mesh(),
               scratch_types=[], name="sc_dispatch",
               cost_estimate=pl.CostEstimate(
                   flops=0, transcendentals=0,
                   bytes_accessed=n_chunks * n_idx * (8 * w + 4)))
    def scatter(src_hbm, idx_hbm, out_hbm):
        def body(src_vmem, idx_vmem):
            pltpu.sync_copy(src_vmem, out_hbm.at[idx_vmem.at[0]])

        pltpu.emit_pipeline(
            body,
            grid=(n_chunks, idx_blocks),
            in_specs=[pl.BlockSpec((SC_ROWS, w), lambda c, i: (c * src_blocks + i % src_blocks, 0)),
                      pl.BlockSpec((1, SC_ROWS), lambda c, i: (0, c * idx_blocks + i))],
            out_specs=[],
            core_axis_name=("core", "subcore"),
            dimension_semantics=(pltpu.PARALLEL, pltpu.PARALLEL),
        )(src_hbm, idx_hbm)

    return scatter(src.reshape(n_chunks * t, w), idx.reshape(1, n_chunks * n_idx)).reshape(
        n_chunks, n_rows, w)


def _sc_gather_rows(table, idx):
    n_chunks, p, w = table.shape
    n_idx = idx.shape[1]
    idx_blocks = n_idx // SC_ROWS

    @pl.kernel(out_type=jax.ShapeDtypeStruct((n_chunks * n_idx, w), table.dtype), mesh=_sc_mesh(),
               scratch_types=[], name="sc_gather",
               cost_estimate=pl.CostEstimate(
                   flops=0, transcendentals=0,
                   bytes_accessed=n_chunks * n_idx * (8 * w + 4)))
    def gather(table_hbm, idx_hbm, out_hbm):
        def body(idx_vmem, out_vmem):
            pltpu.sync_copy(table_hbm.at[idx_vmem.at[0]], out_vmem)

        pltpu.emit_pipeline(
            body,
            grid=(n_chunks, idx_blocks),
            in_specs=[pl.BlockSpec((1, SC_ROWS), lambda c, i: (0, c * idx_blocks + i))],
            out_specs=[pl.BlockSpec((SC_ROWS, w), lambda c, i: (c * idx_blocks + i, 0))],
            core_axis_name=("core", "subcore"),
            dimension_semantics=(pltpu.PARALLEL, pltpu.PARALLEL),
        )(idx_hbm, out_hbm)

    return gather(table.reshape(n_chunks * p, w), idx.reshape(1, n_chunks * n_idx)).reshape(
        n_chunks, n_idx, w)


def _expert_kernel(bexp_ref, nvalid_ref, next_ref, xs_ref, wgu_hbm, bgu_ref, wd_hbm, bd_ref, ys_ref,
                   wgu_stage, wd_stage, wgu_bf_ref, wd_bf_ref, sems, *, d_exp):
    n_chunks = xs_ref.shape[0]

    def weight_copies(e):
        return (pltpu.make_async_copy(wgu_hbm.at[e], wgu_stage, sems.at[0]),
                pltpu.make_async_copy(wd_hbm.at[e], wd_stage, sems.at[1]))

    def switch_weights(b):
        e = bexp_ref[b]

        @pl.when(jnp.logical_or(b == 0, e != bexp_ref[jnp.maximum(b - 1, 0)]))
        def _():
            @pl.when(b == 0)
            def _():
                for cp in weight_copies(e):
                    cp.start()

            for cp in weight_copies(e):
                cp.wait()
            wgu_bf_ref[...] = wgu_stage[...].astype(BF16)
            wd_bf_ref[...] = wd_stage[...].astype(BF16)
            e_next = next_ref[e]

            @pl.when(e_next >= 0)
            def _():
                for cp in weight_copies(e_next):
                    cp.start()

    def compute(e, rows):
        lo, hi = _unpack_bf16_pairs(
            jnp.concatenate([xs_ref[c, rows, :] for c in range(n_chunks)], axis=1))
        xb = jnp.concatenate([lo, hi], axis=1).astype(BF16)
        gu = jnp.dot(xb, wgu_bf_ref[...], preferred_element_type=F32) + bgu_ref[e]
        gate = jnp.minimum(gu[:, :d_exp], SWIGLU_LIMIT)
        up = jnp.clip(gu[:, d_exp:], -SWIGLU_LIMIT, SWIGLU_LIMIT)
        half_gate = 0.5 * gate
        glu = half_gate + half_gate * jnp.tanh(gate * (0.5 * SWIGLU_ALPHA))
        act = ((up + 1.0) * glu).astype(BF16)
        y = jnp.dot(act, wd_bf_ref[...], preferred_element_type=F32) + bd_ref[e]
        y_words = _pack_bf16_pairs(y)
        for c in range(n_chunks):
            ys_ref[c, rows, :] = y_words[:, c * SC_CHUNK:(c + 1) * SC_CHUNK]

    b0 = pl.program_id(0) * EXPERT_BLOCKS_PER_STEP
    b_last = b0 + EXPERT_BLOCKS_PER_STEP - 1
    block_rows = [slice(h * MOE_BLOCK, (h + 1) * MOE_BLOCK) for h in range(EXPERT_BLOCKS_PER_STEP)]
    one_expert = jnp.logical_and(b_last < nvalid_ref[0], bexp_ref[b_last] == bexp_ref[b0])

    @pl.when(b0 < nvalid_ref[0])
    def _():
        switch_weights(b0)

        @pl.when(one_expert)
        def _():
            for rows in block_rows:
                compute(bexp_ref[b0], rows)

        @pl.when(jnp.logical_not(one_expert))
        def _():
            compute(bexp_ref[b0], block_rows[0])
            for h in range(1, EXPERT_BLOCKS_PER_STEP):
                @pl.when(b0 + h < nvalid_ref[0])
                def _():
                    switch_weights(b0 + h)
                    compute(bexp_ref[b0 + h], block_rows[h])


def _experts(bexp, nvalid, next_expert, xs, wgu, bgu, wd, bd):
    n_chunks, n_rows, _ = xs.shape
    n_exp, d, d_exp2 = wgu.shape
    d_exp = d_exp2 // 2
    step_rows = EXPERT_BLOCKS_PER_STEP * MOE_BLOCK
    blk = lambda s, be, nv, nx: jnp.minimum(s, (nv[0] - 1) // EXPERT_BLOCKS_PER_STEP)
    return pl.pallas_call(
        functools.partial(_expert_kernel, d_exp=d_exp),
        grid_spec=pltpu.PrefetchScalarGridSpec(
            num_scalar_prefetch=3,
            grid=(n_rows // step_rows,),
            in_specs=[
                pl.BlockSpec((n_chunks, step_rows, SC_CHUNK),
                             lambda s, be, nv, nx: (0, blk(s, be, nv, nx), 0)),
                pl.BlockSpec(memory_space=pl.ANY),
                pl.BlockSpec((n_exp, 1, 2 * d_exp), lambda s, be, nv, nx: (0, 0, 0)),
                pl.BlockSpec(memory_space=pl.ANY),
                pl.BlockSpec((n_exp, 1, d), lambda s, be, nv, nx: (0, 0, 0)),
            ],
            out_specs=pl.BlockSpec((n_chunks, step_rows, SC_CHUNK),
                                   lambda s, be, nv, nx: (0, blk(s, be, nv, nx), 0)),
            scratch_shapes=[pltpu.VMEM((d, 2 * d_exp), F32), pltpu.VMEM((d_exp, d), F32),
                            pltpu.VMEM((d, 2 * d_exp), BF16), pltpu.VMEM((d_exp, d), BF16),
                            pltpu.SemaphoreType.DMA((2,))],
        ),
        out_shape=jax.ShapeDtypeStruct((n_chunks, n_rows, SC_CHUNK), U32),
        name="experts",
        cost_estimate=pl.CostEstimate(
            flops=6 * n_rows * d * d_exp, transcendentals=n_rows * d_exp,
            bytes_accessed=4 * n_rows * d + 4 * N_EXPERTS * 3 * d * d_exp),
        compiler_params=pltpu.CompilerParams(
            dimension_semantics=("arbitrary",), vmem_limit_bytes=VMEM_LIMIT),
    )(bexp, nvalid, next_expert, xs, wgu, bgu, wd, bd)


def _combine_kernel(prev_ref, x1_ref, gate_ref, yg_ref, out_ref):
    del prev_ref
    n_chunks = yg_ref.shape[0]
    half = n_chunks * SC_CHUNK
    for c in range(n_chunks):
        sl_lo = slice(c * SC_CHUNK, (c + 1) * SC_CHUNK)
        sl_hi = slice(half + c * SC_CHUNK, half + (c + 1) * SC_CHUNK)
        acc_lo = x1_ref[:, sl_lo]
        acc_hi = x1_ref[:, sl_hi]
        for k in range(TOP_K):
            lo, hi = _unpack_bf16_pairs(yg_ref[c, k])
            g = gate_ref[:, k:k + 1]
            acc_lo = acc_lo + g * lo
            acc_hi = acc_hi + g * hi
        out_ref[:, sl_lo] = acc_lo
        out_ref[:, sl_hi] = acc_hi


def _combine(x1, gates, yg, out_prev, tcb, row0, t_total):
    t, d = x1.shape
    n_chunks = yg.shape[0]
    blk0 = row0 // tcb
    in_specs = [
        pl.BlockSpec((tcb, d), lambda i: (i, 0)),
        pl.BlockSpec((tcb, LANES), lambda i: (i, 0)),
        pl.BlockSpec((n_chunks, TOP_K, tcb, SC_CHUNK), lambda i: (0, 0, i, 0)),
    ]
    args = [x1, gates, yg]
    aliases = {}
    body = functools.partial(_combine_kernel, None)
    if out_prev is not None:
        in_specs.append(pl.BlockSpec(memory_space=pl.ANY))
        args.append(out_prev)
        aliases = {3: 0}
        body = lambda a, b, c, prev, o: _combine_kernel(prev, a, b, c, o)
    return pl.pallas_call(
        body,
        grid=(t // tcb,),
        in_specs=in_specs,
        out_specs=pl.BlockSpec((tcb, d), lambda i: (blk0 + i, 0)),
        out_shape=jax.ShapeDtypeStruct((t_total, d), F32),
        input_output_aliases=aliases,
        name="combine",
        cost_estimate=pl.CostEstimate(
            flops=2 * TOP_K * t * d, transcendentals=0,
            bytes_accessed=t * (4 * d + 4 * d + 2 * TOP_K * d + 4 * LANES)),
        compiler_params=pltpu.CompilerParams(
            dimension_semantics=("arbitrary",), vmem_limit_bytes=VMEM_LIMIT),
    )(*args)


def _pick_tile(n, pref, *also):
    t = pref
    while any(v % t for v in (n,) + also):
        t //= 2
    return t


def _token_groups(n_batch):
    if n_batch % 8 == 0:
        return [5 * n_batch // 8, 3 * n_batch // 8]
    if n_batch % 4 == 0:
        return [3 * n_batch // 4, n_batch // 4]
    if n_batch % 2 == 0:
        return [n_batch // 2, n_batch // 2]
    return [n_batch]


def kernel(x, mem, positions, attn_norm_g, mem_norm_g, w_in, b_gates, a_q_norm_g, a_k_norm_g,
           a_sinks, w_o_a, gmlp_ln_g, gmlp_ln_b, gmlp_w_s, gmlp_b_s, w_o_b, w_mem_kv,
           c_q_norm_g, c_k_norm_g, w_o_c, w_out, ffn_norm_g, router_w, router_b,
           w_gate_up, b_gate_up, w_down, b_down):
    n_batch, seq, d = x.shape
    m_len = mem.shape[1]
    depth = w_in.shape[0]
    t = n_batch * seq
    group_batches = _token_groups(n_batch)

    inv_freq = ROPE_THETA ** (-jnp.arange(0, HEAD_DIM, 2, dtype=F32) / HEAD_DIM)
    invf = inv_freq[:, None]
    sgn = jnp.tile(jnp.concatenate([-jnp.ones((HEAD_DIM // 2,), F32),
                                    jnp.ones((HEAD_DIM // 2,), F32)]), LANES // HEAD_DIM)[None, :]
    pos2d = positions.reshape(t, 1).astype(I32)
    mem2d = mem.reshape(n_batch * m_len, d)
    x2d = x.reshape(t, d)

    ts = _pick_tile(seq, 512)

    for l in range(depth):
        kc, vc = _mem_kv(mem2d, mem_norm_g[l][None, :], w_mem_kv[l].astype(BF16),
                         c_k_norm_g[l][None, :], n_batch, m_len)
        w_in_bf = w_in[l].astype(BF16)
        mix_w = (w_o_a[l].astype(BF16), w_o_b[l].astype(BF16), w_o_c[l].astype(BF16),
                 w_out[l].astype(BF16))
        gq = jnp.tile(a_q_norm_g[l], LANES // HEAD_DIM)[None, :] * (HEAD_DIM ** -0.5 * LOG2_E)
        gcq = c_q_norm_g[l][None, :] * (X_HEAD_DIM ** -0.5 * LOG2_E)
        gk = jnp.tile(a_k_norm_g[l], LANES // HEAD_DIM)[None, :]
        rw32 = jnp.pad(router_w[l], ((0, 0), (0, LANES - N_EXPERTS)))
        rw_hi = rw32.astype(BF16)
        rw = jnp.concatenate([rw_hi, (rw32 - rw_hi.astype(F32)).astype(BF16)], axis=1)
        rb = jnp.pad(router_b[l], (0, LANES - N_EXPERTS))[None, :]

        routed = []
        scatter_idx = None
        batch0 = 0
        for gb in group_batches:
            tg = gb * seq
            row0 = batch0 * seq
            n_asg = tg * TOP_K
            nb = -(-n_asg // MOE_BLOCK) + N_EXPERTS
            nb = -(-nb // EXPERT_BLOCKS_PER_STEP) * EXPERT_BLOCKS_PER_STEP
            n_rows = nb * MOE_BLOCK
            tm = _pick_tile(tg, 2 * PROJ_SUB_ROWS, row0, t)
            proj = _proj(
                x2d, pos2d, attn_norm_g[l][None, :], w_in_bf, invf, sgn, gq, gk,
                gcq, gmlp_ln_g[l][None, :], gmlp_ln_b[l][None, :],
                b_gates[l].reshape(1, 3 * d), tm, row0, tg, scatter_idx)
            x1, h2, idx, rank, gates, counts = _mix(
                a_sinks[l], proj, x2d, kc, vc, gmlp_w_s[l], gmlp_b_s[l].T, *mix_w,
                ffn_norm_g[l][None, :], rw, rb, gb, seq, m_len, ts, batch0)

            cnt = counts[:, 0].astype(I32)
            padded = (cnt + MOE_BLOCK - 1) // MOE_BLOCK * MOE_BLOCK
            pend = jnp.cumsum(padded)
            pstart = (pend - padded).astype(I32)
            nvalid = (pend[-1:] // MOE_BLOCK).astype(I32)
            blk_row = jnp.arange(nb, dtype=I32) * MOE_BLOCK
            bexp = jnp.minimum(jnp.sum((pend[None, :] <= blk_row[:, None]).astype(I32), axis=1),
                               N_EXPERTS - 1).astype(I32)
            eid = jnp.arange(N_EXPERTS, dtype=I32)
            later = jnp.logical_and(eid[None, :] > eid[:, None], (cnt > 0)[None, :])
            next_expert = jnp.min(jnp.where(later, eid[None, :], N_EXPERTS), axis=1)
            next_expert = jnp.where(next_expert == N_EXPERTS, -1, next_expert).astype(I32)
            j = jnp.arange(MOE_BLOCK, dtype=I32)[None, :]
            fill = jnp.where(j < (padded - cnt)[:, None], pend[:, None] - 1 - j,
                             n_rows - 1 - j).astype(I32)

            pos = _pos(pstart, idx, rank, _pick_tile(tg, 8192))
            pos_km = pos[:TOP_K].reshape(n_asg)
            n_chunks = h2.shape[0]
            chunk_off = (jnp.arange(n_chunks, dtype=I32) * n_rows)[:, None]
            scatter_idx = jnp.concatenate([pos_km, fill.reshape(-1)])[None, :] + chunk_off
            xs = _sc_scatter_rows(h2, scatter_idx, n_rows)
            routed.append((row0, x1, gates, xs, bexp, nvalid, next_expert, pos_km, chunk_off))
            batch0 += gb

        gathered = []
        for row0, x1, gates, xs, bexp, nvalid, next_expert, pos_km, chunk_off in routed:
            ys = _experts(bexp, nvalid, next_expert, xs, w_gate_up[l], b_gate_up[l][:, None, :],
                          w_down[l], b_down[l][:, None, :])
            yg = _sc_gather_rows(ys, pos_km[None, :] + chunk_off)
            gathered.append(
                (row0, x1, gates, yg.reshape(yg.shape[0], TOP_K, x1.shape[0], SC_CHUNK)))

        out = None
        for row0, x1, gates, yg in gathered:
            out = _combine(x1, gates, yg, out, _pick_tile(x1.shape[0], 1024, row0), row0, t)
        x2d = out
    return x2d.reshape(n_batch, seq, d)
```

```python
import functools

import numpy as np
import jax
import jax.numpy as jnp
from jax import lax
from jax.experimental import pallas as pl
from jax.experimental.pallas import tpu as pltpu
from jax.experimental.pallas import tpu_sc as plsc

F32 = jnp.float32
BF16 = jnp.bfloat16
I32 = jnp.int32
U32 = jnp.uint32
HI16 = np.uint32(0xFFFF0000)
LOG2_E = float(np.log2(np.e))

EPS = 1e-6
LANES = 128
HEAD_DIM = 64
N_Q_HEADS = 16
N_KV_HEADS = 2
ATT_BLOCK = 128
ROPE_THETA = 10000.0
GMLP_WIDTH = 512
GMLP_GROUPS = 4
GMLP_CHUNK = 128
X_HEADS = 4
X_HEAD_DIM = 128
N_EXPERTS = 32
TOP_K = 4
SWIGLU_LIMIT = 7.0
SWIGLU_ALPHA = 1.702
MOE_BLOCK = 512
SC_ROWS = 128
SC_CHUNK = 256
PROJ_SUB_ROWS = 512
EXPERT_BLOCKS_PER_STEP = 4
IDX_ROWS = 8

A_Q = N_Q_HEADS * HEAD_DIM
A_KV = N_KV_HEADS * HEAD_DIM
C_Q = X_HEADS * X_HEAD_DIM

OFF_Q = 0
OFF_K4 = OFF_Q + A_Q
OFF_V4 = OFF_K4 + 4 * LANES
OFF_U = OFF_V4 + 4 * LANES
OFF_VN = OFF_U + GMLP_WIDTH
OFF_QC = OFF_VN + GMLP_WIDTH
OFF_GA = OFF_QC + C_Q
PROJ_W_BASE = OFF_GA

VMEM_LIMIT = 56 * 1024 * 1024


def _lane_iota(shape):
    return lax.broadcasted_iota(I32, shape, len(shape) - 1)


def _rms(x, g):
    return x * lax.rsqrt(jnp.mean(x * x, axis=-1, keepdims=True) + EPS) * g


def _pack_bf16_pairs(x):
    n = x.shape[1] // 2
    bits = pltpu.bitcast(x.astype(BF16).astype(F32), U32)
    return (bits[:, :n] >> 16) | (bits[:, n:] & HI16)


def _unpack_bf16_pairs(w):
    return pltpu.bitcast(w << 16, F32), pltpu.bitcast(w & HI16, F32)


def _gelu(x):
    return 0.5 * x * (1.0 + lax.erf(x * np.float32(np.sqrt(0.5))))


def _memkv_kernel(mem_ref, g_ref, w_ref, gk_ref, kc_ref, vc_ref):
    h = _rms(mem_ref[...], g_ref[...]).astype(BF16)
    kv = jnp.dot(h, w_ref[...], preferred_element_type=F32)
    for hh in range(X_HEADS):
        sl = slice(hh * X_HEAD_DIM, (hh + 1) * X_HEAD_DIM)
        kc_ref[:, sl] = _rms(kv[:, sl], gk_ref[...]).astype(BF16)
    vc_ref[...] = kv[:, C_Q:].astype(BF16)


def _mem_kv(mem2d, g, w_bf, gk, n_batch, m_len):
    d = mem2d.shape[1]
    rows = _pick_tile(n_batch * m_len, 1024)
    return pl.pallas_call(
        _memkv_kernel,
        grid=(n_batch * m_len // rows,),
        in_specs=[
            pl.BlockSpec((rows, d), lambda b: (b, 0)),
            pl.BlockSpec((1, d), lambda b: (0, 0)),
            pl.BlockSpec((d, 2 * C_Q), lambda b: (0, 0)),
            pl.BlockSpec((1, X_HEAD_DIM), lambda b: (0, 0)),
        ],
        out_specs=[
            pl.BlockSpec((rows, C_Q), lambda b: (b, 0)),
            pl.BlockSpec((rows, C_Q), lambda b: (b, 0)),
        ],
        out_shape=[jax.ShapeDtypeStruct((n_batch * m_len, C_Q), BF16)] * 2,
        name="mem_kv",
        compiler_params=pltpu.CompilerParams(dimension_semantics=("arbitrary",)),
    )(mem2d, g, w_bf, gk)


def _proj_kernel(x_ref, pos_ref, g_ref, w_ref, invf_ref, sgn_ref, gq_ref, gk_ref, gcq_ref,
                 lng_ref, lnb_ref, bg_ref, out_ref, *, d_model):
    sub = min(PROJ_SUB_ROWS, x_ref.shape[0])
    for r in range(x_ref.shape[0] // sub):
        rows = pl.ds(r * sub, sub)
        _proj_rows(x_ref.at[rows], pos_ref.at[:, rows], g_ref, w_ref, invf_ref, sgn_ref, gq_ref,
                   gk_ref, gcq_ref, lng_ref, lnb_ref, bg_ref, out_ref.at[rows], d_model=d_model)


def _proj_rows(x_ref, pos_ref, g_ref, w_ref, invf_ref, sgn_ref, gq_ref, gk_ref, gcq_ref,
               lng_ref, lnb_ref, bg_ref, out_ref, *, d_model):
    tm = x_ref.shape[0]
    h = _rms(x_ref[...], g_ref[...]).astype(BF16)

    ang_t = invf_ref[...] * pos_ref[...].astype(F32)
    reps = LANES // (HEAD_DIM // 2)
    cosv = jnp.transpose(jnp.concatenate([jnp.cos(ang_t)] * reps, axis=0))
    sinv = jnp.transpose(jnp.concatenate([jnp.sin(ang_t)] * reps, axis=0)) * sgn_ref[...]
    lane = _lane_iota((tm, LANES))
    first_head = lane < HEAD_DIM
    lo_half = (lane % HEAD_DIM) < (HEAD_DIM // 2)

    def head_norm_rope(blk, g):
        y = blk * blk
        s_lo = jnp.sum(jnp.where(first_head, y, 0.0), axis=-1, keepdims=True)
        s_hi = jnp.sum(jnp.where(first_head, 0.0, y), axis=-1, keepdims=True)
        ss = jnp.where(first_head, s_lo, s_hi)
        n = blk * lax.rsqrt(ss * (1.0 / HEAD_DIM) + EPS) * g
        rot = jnp.where(lo_half, pltpu.roll(n, LANES - HEAD_DIM // 2, 1),
                        pltpu.roll(n, HEAD_DIM // 2, 1))
        return n * cosv + rot * sinv

    def proj(a, b):
        return jnp.dot(h, w_ref[:, a:b], preferred_element_type=F32)

    pq = proj(0, A_Q)
    for c in range(A_Q // LANES):
        sl = slice(c * LANES, (c + 1) * LANES)
        out_ref[:, OFF_Q + c * LANES:OFF_Q + (c + 1) * LANES] = (
            head_norm_rope(pq[:, sl], gq_ref[...])).astype(BF16)

    pkv = proj(A_Q, A_Q + 2 * A_KV)
    kn = head_norm_rope(pkv[:, :LANES], gk_ref[...])
    vv = pkv[:, LANES:]
    for off, t in ((OFF_K4, kn), (OFF_V4, vv)):
        tr = pltpu.roll(t, HEAD_DIM, 1)
        parts = (jnp.where(first_head, t, 0.0), jnp.where(first_head, 0.0, tr),
                 jnp.where(first_head, tr, 0.0), jnp.where(first_head, 0.0, t))
        for j, p in enumerate(parts):
            out_ref[:, off + j * LANES:off + (j + 1) * LANES] = p.astype(BF16)

    o0 = A_Q + 2 * A_KV
    out_ref[:, OFF_U:OFF_U + GMLP_WIDTH] = _gelu(proj(o0, o0 + GMLP_WIDTH)).astype(BF16)
    gv = _gelu(proj(o0 + GMLP_WIDTH, o0 + 2 * GMLP_WIDTH))
    mu = jnp.mean(gv, axis=-1, keepdims=True)
    var = jnp.mean(jnp.square(gv - mu), axis=-1, keepdims=True)
    out_ref[:, OFF_VN:OFF_VN + GMLP_WIDTH] = (
        (gv - mu) * lax.rsqrt(var + EPS) * lng_ref[...] + lnb_ref[...]).astype(BF16)

    o1 = o0 + 2 * GMLP_WIDTH
    pc = proj(o1, o1 + C_Q)
    for hh in range(X_HEADS):
        sl = slice(hh * X_HEAD_DIM, (hh + 1) * X_HEAD_DIM)
        out_ref[:, OFF_QC + hh * X_HEAD_DIM:OFF_QC + (hh + 1) * X_HEAD_DIM] = (
            _rms(pc[:, sl], gcq_ref[...])).astype(BF16)

    o2 = o1 + C_Q
    for j in range(3):
        sl = slice(j * d_model, (j + 1) * d_model)
        z = proj(o2 + j * d_model, o2 + (j + 1) * d_model) + bg_ref[:, sl]
        out_ref[:, OFF_GA + j * d_model:OFF_GA + (j + 1) * d_model] = (
            0.5 * jnp.tanh(0.5 * z) + 0.5).astype(BF16)


def _proj_after_kernel(after_ref, *refs, d_model):
    del after_ref
    _proj_kernel(*refs, d_model=d_model)


def _proj(x2d, pos2d, g, w_bf, invf, sgn, gq, gk, gcq, lng, lnb, bg, tm, row0, t, after):
    d = x2d.shape[1]
    d_in = w_bf.shape[1]
    pw = PROJ_W_BASE + 3 * d
    blk0 = row0 // tm
    full = lambda shape: pl.BlockSpec(shape, lambda i: (0,) * len(shape))
    body = functools.partial(_proj_kernel, d_model=d)
    lead_specs, lead_args = [], []
    if after is not None:
        body = functools.partial(_proj_after_kernel, d_model=d)
        lead_specs, lead_args = [pl.BlockSpec(memory_space=pl.ANY)], [after]
    return pl.pallas_call(
        body,
        grid=(t // tm,),
        in_specs=lead_specs + [
            pl.BlockSpec((tm, d), lambda i: (blk0 + i, 0)),
            pl.BlockSpec((None, 1, tm), lambda i: (blk0 + i, 0, 0)),
            full((1, d)),
            pl.BlockSpec((d, d_in), lambda i: (0, 0), pipeline_mode=pl.Buffered(1)),
            full((HEAD_DIM // 2, 1)), full((1, LANES)),
            full((1, LANES)), full((1, LANES)), full((1, X_HEAD_DIM)),
            full((1, GMLP_WIDTH)), full((1, GMLP_WIDTH)), full((1, 3 * d)),
        ],
        out_specs=pl.BlockSpec((tm, pw), lambda i: (i, 0)),
        out_shape=jax.ShapeDtypeStruct((t, pw), BF16),
        name="proj",
        cost_estimate=pl.CostEstimate(
            flops=2 * t * d * d_in, transcendentals=t * (3 * d + 2 * GMLP_WIDTH + 2 * LANES),
            bytes_accessed=4 * t * d + 2 * t * pw + 2 * d * d_in),
        compiler_params=pltpu.CompilerParams(
            dimension_semantics=("arbitrary",), vmem_limit_bytes=VMEM_LIMIT),
    )(*lead_args, x2d, pos2d.reshape(-1, 1, tm), g, w_bf, invf, sgn, gq, gk, gcq, lng, lnb, bg)


def _mix_kernel(sinks_ref, proj_ref, kprev_ref, vprev_ref, x_ref, kc_ref, vc_ref, ws_ref, bst_ref,
                woa_ref, wob_ref, woc_ref, wout_ref, gffn_ref, rw_ref, rb_ref,
                x1_ref, h2_ref, idx_ref, rank_ref, gate_ref, cnt_ref,
                run_ref, oa_ref, ob_ref, oc_ref, *, d_model):
    ts = x_ref.shape[0]
    s_idx = pl.program_id(1)
    first_step = jnp.logical_and(pl.program_id(0) == 0, s_idx == 0)
    neg_inf = float("-inf")

    cpk = A_Q // LANES // N_KV_HEADS
    stack = cpk * ATT_BLOCK
    srow = lax.broadcasted_iota(I32, (stack, 2 * ATT_BLOCK), 0)
    qi = srow % ATT_BLOCK
    kj = lax.broadcasted_iota(I32, (stack, 2 * ATT_BLOCK), 1)
    band = jnp.logical_and(kj <= ATT_BLOCK + qi, kj > qi)
    chunk_of_row = lax.broadcasted_iota(I32, (stack, 1), 0) // ATT_BLOCK
    for qb in range(ts // ATT_BLOCK):
        r0 = qb * ATT_BLOCK
        rows = slice(r0, r0 + ATT_BLOCK)
        if qb == 0:
            kp, vp = kprev_ref[...], vprev_ref[...]
            mask = jnp.logical_and(band, jnp.logical_or(kj >= ATT_BLOCK, s_idx > 0))
        else:
            prow = slice(r0 - ATT_BLOCK, r0)
            kp = proj_ref[prow, OFF_K4:OFF_K4 + 4 * LANES]
            vp = proj_ref[prow, OFF_V4:OFF_V4 + 4 * LANES]
            mask = band
        k4 = jnp.concatenate([kp, proj_ref[rows, OFF_K4:OFF_K4 + 4 * LANES]], axis=0)
        v4 = jnp.concatenate([vp, proj_ref[rows, OFF_V4:OFF_V4 + 4 * LANES]], axis=0)
        for kvh in range(N_KV_HEADS):
            c0 = kvh * cpk
            q4 = jnp.concatenate(
                [proj_ref[rows, OFF_Q + (c0 + c) * LANES:OFF_Q + (c0 + c + 1) * LANES]
                 for c in range(cpk)], axis=0)
            o = jnp.zeros((stack, LANES), F32)
            for half in range(2):
                col = slice((2 * kvh + half) * LANES, (2 * kvh + half + 1) * LANES)
                s = lax.dot_general(q4, k4[:, col], (((1,), (1,)), ((), ())),
                                    preferred_element_type=F32)
                s = jnp.where(mask, s, neg_inf)
                sink = jnp.zeros((stack, 1), F32)
                for c in range(cpk):
                    sink = jnp.where(chunk_of_row == c,
                                     sinks_ref[2 * (c0 + c) + half] * LOG2_E, sink)
                m = jnp.maximum(jnp.max(s, axis=-1, keepdims=True), sink)
                p = jnp.exp2(s - m)
                den = jnp.sum(p, axis=-1, keepdims=True) + jnp.exp2(sink - m)
                o = o + jnp.dot(p.astype(BF16), v4[:, col], preferred_element_type=F32) / den
            for c in range(cpk):
                oa_ref[rows, (c0 + c) * LANES:(c0 + c + 1) * LANES] = (
                    o[c * ATT_BLOCK:(c + 1) * ATT_BLOCK].astype(BF16))

    ti = lax.broadcasted_iota(I32, (GMLP_CHUNK, GMLP_CHUNK), 0)
    si = lax.broadcasted_iota(I32, (GMLP_CHUNK, GMLP_CHUNK), 1)
    for g in range(GMLP_GROUPS):
        wt = jnp.where(si <= ti, ws_ref[g], 0.0).astype(BF16)
        bcol = bst_ref[:, g:g + 1]
        for ch in range(ts // GMLP_CHUNK):
            rows = slice(ch * GMLP_CHUNK, (ch + 1) * GMLP_CHUNK)
            vn = proj_ref[rows, OFF_VN + g * LANES:OFF_VN + (g + 1) * LANES]
            u = proj_ref[rows, OFF_U + g * LANES:OFF_U + (g + 1) * LANES].astype(F32)
            mixed = jnp.dot(wt, vn, preferred_element_type=F32) + bcol
            ob_ref[rows, g * LANES:(g + 1) * LANES] = (u * mixed).astype(BF16)

    for hh in range(X_HEADS):
        sl = slice(hh * X_HEAD_DIM, (hh + 1) * X_HEAD_DIM)
        qc = proj_ref[:, OFF_QC + hh * X_HEAD_DIM:OFF_QC + (hh + 1) * X_HEAD_DIM]
        s = lax.dot_general(qc, kc_ref[:, sl], (((1,), (1,)), ((), ())),
                            preferred_element_type=F32)
        p = jnp.exp2(s - jnp.max(s, axis=-1, keepdims=True))
        den = jnp.sum(p, axis=-1, keepdims=True)
        oc_ref[:, sl] = (jnp.dot(p.astype(BF16), vc_ref[:, sl],
                                 preferred_element_type=F32) / den).astype(BF16)

    def gate(j):
        return proj_ref[:, OFF_GA + j * d_model:OFF_GA + (j + 1) * d_model].astype(F32)

    merged = gate(0) * jnp.dot(oa_ref[...], woa_ref[...], preferred_element_type=F32)
    merged = merged + gate(1) * jnp.dot(ob_ref[...], wob_ref[...], preferred_element_type=F32)
    merged = merged + gate(2) * jnp.dot(oc_ref[...], woc_ref[...], preferred_element_type=F32)
    x1 = x_ref[...] + jnp.dot(merged.astype(BF16), wout_ref[...], preferred_element_type=F32)
    x1_ref[...] = x1

    h2 = _rms(x1, gffn_ref[...])
    h2_words = _pack_bf16_pairs(h2)
    for c in range(h2_ref.shape[0]):
        h2_ref[c] = h2_words[:, c * SC_CHUNK:(c + 1) * SC_CHUNK]
    h2_hi = h2.astype(BF16)
    h2_lo = (h2 - h2_hi.astype(F32)).astype(BF16)
    part = jnp.dot(h2_hi, rw_ref[...], preferred_element_type=F32)
    logits = (part[:, :LANES] + part[:, LANES:]
              + jnp.dot(h2_lo, rw_ref[:, :LANES], preferred_element_type=F32) + rb_ref[...])
    lt = jnp.transpose(logits)[:N_EXPERTS]
    erow = lax.broadcasted_iota(I32, (N_EXPERTS, ts), 0)
    vals, idxs = [], []
    for _ in range(TOP_K):
        m = jnp.max(lt, axis=0, keepdims=True)
        i = jnp.min(jnp.where(lt == m, erow, N_EXPERTS), axis=0, keepdims=True)
        vals.append(m)
        idxs.append(i)
        lt = jnp.where(erow == i, neg_inf, lt)
    es = [jnp.exp(v - vals[0]) for v in vals]
    den = es[0] + es[1] + es[2] + es[3]

    @pl.when(first_step)
    def _():
        run_ref[...] = jnp.zeros_like(run_ref)

    hot = [erow == i for i in idxs]
    multihot = jnp.where(jnp.logical_or(jnp.logical_or(hot[0], hot[1]),
                                        jnp.logical_or(hot[2], hot[3])), 1.0, 0.0)
    tr = lax.broadcasted_iota(I32, (ts, ts), 0)
    tc = lax.broadcasted_iota(I32, (ts, ts), 1)
    earlier = jnp.where(tr < tc, 1.0, 0.0).astype(BF16)
    before = jnp.dot(multihot.astype(BF16), earlier, preferred_element_type=F32) + run_ref[...]
    krow = lax.broadcasted_iota(I32, (IDX_ROWS, ts), 0)
    idx_out = jnp.zeros((IDX_ROWS, ts), I32)
    rank_out = jnp.zeros((IDX_ROWS, ts), I32)
    gate_rows = jnp.zeros((IDX_ROWS, ts), F32)
    for k in range(TOP_K):
        rk = jnp.sum(jnp.where(hot[k], before, 0.0), axis=0, keepdims=True)
        idx_out = jnp.where(krow == k, idxs[k], idx_out)
        rank_out = jnp.where(krow == k, rk.astype(I32), rank_out)
        gate_rows = jnp.where(krow == k, es[k] / den, gate_rows)
    idx_ref[...] = idx_out
    rank_ref[...] = rank_out
    gate_ref[...] = jnp.transpose(jnp.concatenate(
        [gate_rows, jnp.zeros((LANES - IDX_ROWS, ts), F32)], axis=0))
    run_ref[...] = run_ref[...] + jnp.sum(multihot, axis=1, keepdims=True)
    cnt_ref[...] = run_ref[...]


def _mix(sinks, proj, x2d, kc, vc, w_s, bst, woa, wob, woc, wout, gffn, rw, rb,
         n_batch, seq, m_len, ts, batch0):
    d = x2d.shape[1]
    t = n_batch * seq
    pw = proj.shape[1]
    ns = seq // ts
    nblk = seq // ATT_BLOCK
    per = ts // ATT_BLOCK
    n_chunks = d // (2 * SC_CHUNK)
    full = lambda shape: pl.BlockSpec(shape, lambda b, s: (0,) * len(shape))
    row = lambda width: pl.BlockSpec((ts, width), lambda b, s: (b * ns + s, 0))
    prev = lambda colblk: pl.BlockSpec(
        (ATT_BLOCK, 4 * LANES), lambda b, s: (b * nblk + jnp.maximum(s * per - 1, 0), colblk))
    return pl.pallas_call(
        functools.partial(_mix_kernel, d_model=d),
        grid=(n_batch, ns),
        in_specs=[
            pl.BlockSpec(memory_space=pltpu.SMEM),
            row(pw), prev(OFF_K4 // (4 * LANES)), prev(OFF_V4 // (4 * LANES)),
            pl.BlockSpec((ts, d), lambda b, s: ((batch0 + b) * ns + s, 0)),
            pl.BlockSpec((m_len, C_Q), lambda b, s: (batch0 + b, 0)),
            pl.BlockSpec((m_len, C_Q), lambda b, s: (batch0 + b, 0)),
            full((GMLP_GROUPS, GMLP_CHUNK, GMLP_CHUNK)), full((GMLP_CHUNK, GMLP_GROUPS)),
            full((A_Q, d)), full((GMLP_WIDTH, d)), full((C_Q, d)), full((d, d)),
            full((1, d)), full((d, 2 * LANES)), full((1, LANES)),
        ],
        out_specs=[row(d),
                   pl.BlockSpec((n_chunks, ts, SC_CHUNK), lambda b, s: (0, b * ns + s, 0)),
                   pl.BlockSpec((IDX_ROWS, ts), lambda b, s: (0, b * ns + s)),
                   pl.BlockSpec((IDX_ROWS, ts), lambda b, s: (0, b * ns + s)),
                   row(LANES), full((N_EXPERTS, 1))],
        out_shape=[
            jax.ShapeDtypeStruct((t, d), F32),
            jax.ShapeDtypeStruct((n_chunks, t, SC_CHUNK), U32),
            jax.ShapeDtypeStruct((IDX_ROWS, t), I32), jax.ShapeDtypeStruct((IDX_ROWS, t), I32),
            jax.ShapeDtypeStruct((t, LANES), F32), jax.ShapeDtypeStruct((N_EXPERTS, 1), F32),
        ],
        scratch_shapes=[
            pltpu.VMEM((N_EXPERTS, 1), F32),
            pltpu.VMEM((ts, A_Q), BF16), pltpu.VMEM((ts, GMLP_WIDTH), BF16),
            pltpu.VMEM((ts, C_Q), BF16),
        ],
        name="mix",
        cost_estimate=pl.CostEstimate(
            flops=2 * t * (d * (A_Q + GMLP_WIDTH + C_Q + d) + 4 * ATT_BLOCK * A_Q
                           + GMLP_CHUNK * GMLP_WIDTH + 2 * m_len * C_Q + 3 * d * LANES),
            transcendentals=t * (2 * ATT_BLOCK * N_Q_HEADS + m_len * X_HEADS),
            bytes_accessed=t * (2 * pw + 4 * d + 4 * d + 2 * d + 12 * LANES)),
        compiler_params=pltpu.CompilerParams(
            dimension_semantics=("arbitrary", "arbitrary"), vmem_limit_bytes=VMEM_LIMIT),
    )(sinks, proj, proj, proj, x2d, kc, vc, w_s, bst, woa, wob, woc, wout, gffn, rw, rb)


def _pos_kernel(pstart_ref, idx_ref, rank_ref, pos_ref):
    idx = idx_ref[...]
    pos = rank_ref[...]
    for e in range(N_EXPERTS):
        pos = pos + jnp.where(idx == e, pstart_ref[e], 0)
    pos_ref[...] = pos


def _pos(pstart, idx, rank, tp):
    t = idx.shape[1]
    return pl.pallas_call(
        _pos_kernel,
        grid_spec=pltpu.PrefetchScalarGridSpec(
            num_scalar_prefetch=1,
            grid=(t // tp,),
            in_specs=[pl.BlockSpec((IDX_ROWS, tp), lambda i, ps: (0, i)),
                      pl.BlockSpec((IDX_ROWS, tp), lambda i, ps: (0, i))],
            out_specs=pl.BlockSpec((IDX_ROWS, tp), lambda i, ps: (0, i)),
        ),
        out_shape=jax.ShapeDtypeStruct((IDX_ROWS, t), I32),
        name="slot_pos",
        compiler_params=pltpu.CompilerParams(dimension_semantics=("arbitrary",)),
    )(pstart, idx, rank)


def _sc_mesh():
    return plsc.VectorSubcoreMesh(core_axis_name="core", subcore_axis_name="subcore")


def _sc_scatter_rows(src, idx, n_rows):
    n_chunks, t, w = src.shape
    n_idx = idx.shape[1]
    src_blocks = t // SC_ROWS
    idx_blocks = n_idx // SC_ROWS

    @pl.kernel(out_type=jax.ShapeDtypeStruct((n_chunks * n_rows, w), src.dtype), mesh=_sc_mesh(),
               scratch_types=[], name="sc_dispatch",
               cost_estimate=pl.CostEstimate(
                   flops=0, transcendentals=0,
                   bytes_accessed=n_chunks * n_idx * (8 * w + 4)))
    def scatter(src_hbm, idx_hbm, out_hbm):
        def body(src_vmem, idx_vmem):
            pltpu.sync_copy(src_vmem, out_hbm.at[idx_vmem.at[0]])

        pltpu.emit_pipeline(
            body,
            grid=(n_chunks, idx_blocks),
            in_specs=[pl.BlockSpec((SC_ROWS, w), lambda c, i: (c * src_blocks + i % src_blocks, 0)),
                      pl.BlockSpec((1, SC_ROWS), lambda c, i: (0, c * idx_blocks + i))],
            out_specs=[],
            core_axis_name=("core", "subcore"),
            dimension_semantics=(pltpu.PARALLEL, pltpu.PARALLEL),
        )(src_hbm, idx_hbm)

    return scatter(src.reshape(n_chunks * t, w), idx.reshape(1, n_chunks * n_idx)).reshape(
        n_chunks, n_rows, w)


def _sc_gather_rows(table, idx):
    n_chunks, p, w = table.shape
    n_idx = idx.shape[1]
    idx_blocks = n_idx // SC_ROWS

    @pl.kernel(out_type=jax.ShapeDtypeStruct((n_chunks * n_idx, w), table.dtype), mesh=_sc_mesh(),
               scratch_types=[], name="sc_gather",
               cost_estimate=pl.CostEstimate(
                   flops=0, transcendentals=0,
                   bytes_accessed=n_chunks * n_idx * (8 * w + 4)))
    def gather(table_hbm, idx_hbm, out_hbm):
        def body(idx_vmem, out_vmem):
            pltpu.sync_copy(table_hbm.at[idx_vmem.at[0]], out_vmem)

        pltpu.emit_pipeline(
            body,
            grid=(n_chunks, idx_blocks),
            in_specs=[pl.BlockSpec((1, SC_ROWS), lambda c, i: (0, c * idx_blocks + i))],
            out_specs=[pl.BlockSpec((SC_ROWS, w), lambda c, i: (c * idx_blocks + i, 0))],
            core_axis_name=("core", "subcore"),
            dimension_semantics=(pltpu.PARALLEL, pltpu.PARALLEL),
        )(idx_hbm, out_hbm)

    return gather(table.reshape(n_chunks * p, w), idx.reshape(1, n_chunks * n_idx)).reshape(
        n_chunks, n_idx, w)


def _expert_kernel(bexp_ref, nvalid_ref, next_ref, xs_ref, wgu_hbm, bgu_ref, wd_hbm, bd_ref, ys_ref,
                   wgu_stage, wd_stage, wgu_bf_ref, wd_bf_ref, sems, *, d_exp):
    n_chunks = xs_ref.shape[0]

    def weight_copies(e):
        return (pltpu.make_async_copy(wgu_hbm.at[e], wgu_stage, sems.at[0]),
                pltpu.make_async_copy(wd_hbm.at[e], wd_stage, sems.at[1]))

    def switch_weights(b):
        e = bexp_ref[b]

        @pl.when(jnp.logical_or(b == 0, e != bexp_ref[jnp.maximum(b - 1, 0)]))
        def _():
            @pl.when(b == 0)
            def _():
                for cp in weight_copies(e):
                    cp.start()

            for cp in weight_copies(e):
                cp.wait()
            wgu_bf_ref[...] = wgu_stage[...].astype(BF16)
            wd_bf_ref[...] = wd_stage[...].astype(BF16)
            e_next = next_ref[e]

            @pl.when(e_next >= 0)
            def _():
                for cp in weight_copies(e_next):
                    cp.start()

    def compute(e, rows):
        lo, hi = _unpack_bf16_pairs(
            jnp.concatenate([xs_ref[c, rows, :] for c in range(n_chunks)], axis=1))
        xb = jnp.concatenate([lo, hi], axis=1).astype(BF16)
        gu = jnp.dot(xb, wgu_bf_ref[...], preferred_element_type=F32) + bgu_ref[e]
        gate = jnp.minimum(gu[:, :d_exp], SWIGLU_LIMIT)
        up = jnp.clip(gu[:, d_exp:], -SWIGLU_LIMIT, SWIGLU_LIMIT)
        half_gate = 0.5 * gate
        glu = half_gate + half_gate * jnp.tanh(gate * (0.5 * SWIGLU_ALPHA))
        act = ((up + 1.0) * glu).astype(BF16)
        y = jnp.dot(act, wd_bf_ref[...], preferred_element_type=F32) + bd_ref[e]
        y_words = _pack_bf16_pairs(y)
        for c in range(n_chunks):
            ys_ref[c, rows, :] = y_words[:, c * SC_CHUNK:(c + 1) * SC_CHUNK]

    b0 = pl.program_id(0) * EXPERT_BLOCKS_PER_STEP
    b_last = b0 + EXPERT_BLOCKS_PER_STEP - 1
    block_rows = [slice(h * MOE_BLOCK, (h + 1) * MOE_BLOCK) for h in range(EXPERT_BLOCKS_PER_STEP)]
    one_expert = jnp.logical_and(b_last < nvalid_ref[0], bexp_ref[b_last] == bexp_ref[b0])

    @pl.when(b0 < nvalid_ref[0])
    def _():
        switch_weights(b0)

        @pl.when(one_expert)
        def _():
            for rows in block_rows:
                compute(bexp_ref[b0], rows)

        @pl.when(jnp.logical_not(one_expert))
        def _():
            compute(bexp_ref[b0], block_rows[0])
            for h in range(1, EXPERT_BLOCKS_PER_STEP):
                @pl.when(b0 + h < nvalid_ref[0])
                def _():
                    switch_weights(b0 + h)
                    compute(bexp_ref[b0 + h], block_rows[h])


def _experts(bexp, nvalid, next_expert, xs, wgu, bgu, wd, bd):
    n_chunks, n_rows, _ = xs.shape
    n_exp, d, d_exp2 = wgu.shape
    d_exp = d_exp2 // 2
    step_rows = EXPERT_BLOCKS_PER_STEP * MOE_BLOCK
    blk = lambda s, be, nv, nx: jnp.minimum(s, (nv[0] - 1) // EXPERT_BLOCKS_PER_STEP)
    return pl.pallas_call(
        functools.partial(_expert_kernel, d_exp=d_exp),
        grid_spec=pltpu.PrefetchScalarGridSpec(
            num_scalar_prefetch=3,
            grid=(n_rows // step_rows,),
            in_specs=[
                pl.BlockSpec((n_chunks, step_rows, SC_CHUNK),
                             lambda s, be, nv, nx: (0, blk(s, be, nv, nx), 0)),
                pl.BlockSpec(memory_space=pl.ANY),
                pl.BlockSpec((n_exp, 1, 2 * d_exp), lambda s, be, nv, nx: (0, 0, 0)),
                pl.BlockSpec(memory_space=pl.ANY),
                pl.BlockSpec((n_exp, 1, d), lambda s, be, nv, nx: (0, 0, 0)),
            ],
            out_specs=pl.BlockSpec((n_chunks, step_rows, SC_CHUNK),
                                   lambda s, be, nv, nx: (0, blk(s, be, nv, nx), 0)),
            scratch_shapes=[pltpu.VMEM((d, 2 * d_exp), F32), pltpu.VMEM((d_exp, d), F32),
                            pltpu.VMEM((d, 2 * d_exp), BF16), pltpu.VMEM((d_exp, d), BF16),
                            pltpu.SemaphoreType.DMA((2,))],
        ),
        out_shape=jax.ShapeDtypeStruct((n_chunks, n_rows, SC_CHUNK), U32),
        name="experts",
        cost_estimate=pl.CostEstimate(
            flops=6 * n_rows * d * d_exp, transcendentals=n_rows * d_exp,
            bytes_accessed=4 * n_rows * d + 4 * N_EXPERTS * 3 * d * d_exp),
        compiler_params=pltpu.CompilerParams(
            dimension_semantics=("arbitrary",), vmem_limit_bytes=VMEM_LIMIT),
    )(bexp, nvalid, next_expert, xs, wgu, bgu, wd, bd)


def _combine_kernel(prev_ref, x1_ref, gate_ref, yg_ref, out_ref):
    del prev_ref
    n_chunks = yg_ref.shape[0]
    half = n_chunks * SC_CHUNK
    for c in range(n_chunks):
        sl_lo = slice(c * SC_CHUNK, (c + 1) * SC_CHUNK)
        sl_hi = slice(half + c * SC_CHUNK, half + (c + 1) * SC_CHUNK)
        acc_lo = x1_ref[:, sl_lo]
        acc_hi = x1_ref[:, sl_hi]
        for k in range(TOP_K):
            lo, hi = _unpack_bf16_pairs(yg_ref[c, k])
            g = gate_ref[:, k:k + 1]
            acc_lo = acc_lo + g * lo
            acc_hi = acc_hi + g * hi
        out_ref[:, sl_lo] = acc_lo
        out_ref[:, sl_hi] = acc_hi


def _combine(x1, gates, yg, out_prev, tcb, row0, t_total):
    t, d = x1.shape
    n_chunks = yg.shape[0]
    blk0 = row0 // tcb
    in_specs = [
        pl.BlockSpec((tcb, d), lambda i: (i, 0)),
        pl.BlockSpec((tcb, LANES), lambda i: (i, 0)),
        pl.BlockSpec((n_chunks, TOP_K, tcb, SC_CHUNK), lambda i: (0, 0, i, 0)),
    ]
    args = [x1, gates, yg]
    aliases = {}
    body = functools.partial(_combine_kernel, None)
    if out_prev is not None:
        in_specs.append(pl.BlockSpec(memory_space=pl.ANY))
        args.append(out_prev)
        aliases = {3: 0}
        body = lambda a, b, c, prev, o: _combine_kernel(prev, a, b, c, o)
    return pl.pallas_call(
        body,
        grid=(t // tcb,),
        in_specs=in_specs,
        out_specs=pl.BlockSpec((tcb, d), lambda i: (blk0 + i, 0)),
        out_shape=jax.ShapeDtypeStruct((t_total, d), F32),
        input_output_aliases=aliases,
        name="combine",
        cost_estimate=pl.CostEstimate(
            flops=2 * TOP_K * t * d, transcendentals=0,
            bytes_accessed=t * (4 * d + 4 * d + 2 * TOP_K * d + 4 * LANES)),
        compiler_params=pltpu.CompilerParams(
            dimension_semantics=("arbitrary",), vmem_limit_bytes=VMEM_LIMIT),
    )(*args)


def _pick_tile(n, pref, *also):
    t = pref
    while any(v % t for v in (n,) + also):
        t //= 2
    return t


def _token_groups(n_batch):
    if n_batch % 8 == 0:
        return [5 * n_batch // 8, 3 * n_batch // 8]
    if n_batch % 4 == 0:
        return [3 * n_batch // 4, n_batch // 4]
    if n_batch % 2 == 0:
        return [n_batch // 2, n_batch // 2]
    return [n_batch]


def kernel(x, mem, positions, attn_norm_g, mem_norm_g, w_in, b_gates, a_q_norm_g, a_k_norm_g,
           a_sinks, w_o_a, gmlp_ln_g, gmlp_ln_b, gmlp_w_s, gmlp_b_s, w_o_b, w_mem_kv,
           c_q_norm_g, c_k_norm_g, w_o_c, w_out, ffn_norm_g, router_w, router_b,
           w_gate_up, b_gate_up, w_down, b_down):
    n_batch, seq, d = x.shape
    m_len = mem.shape[1]
    depth = w_in.shape[0]
    t = n_batch * seq
    assert seq % ATT_BLOCK == 0 and seq % GMLP_CHUNK == 0, "sequence must be whole 128-token blocks"
    assert d % (2 * SC_CHUNK) == 0, "model width must be a multiple of the packed row chunk"
    assert m_len % 8 == 0 and w_in.shape[2] == A_Q + 2 * A_KV + 2 * GMLP_WIDTH + C_Q + 3 * d
    assert router_w.shape[2] == N_EXPERTS and w_gate_up.shape[1:] == (N_EXPERTS, d, 2 * w_down.shape[2])
    group_batches = _token_groups(n_batch)

    inv_freq = ROPE_THETA ** (-jnp.arange(0, HEAD_DIM, 2, dtype=F32) / HEAD_DIM)
    invf = inv_freq[:, None]
    sgn = jnp.tile(jnp.concatenate([-jnp.ones((HEAD_DIM // 2,), F32),
                                    jnp.ones((HEAD_DIM // 2,), F32)]), LANES // HEAD_DIM)[None, :]
    pos2d = positions.reshape(t, 1).astype(I32)
    mem2d = mem.reshape(n_batch * m_len, d)
    x2d = x.reshape(t, d)

    ts = _pick_tile(seq, 512)

    for l in range(depth):
        kc, vc = _mem_kv(mem2d, mem_norm_g[l][None, :], w_mem_kv[l].astype(BF16),
                         c_k_norm_g[l][None, :], n_batch, m_len)
        w_in_bf = w_in[l].astype(BF16)
        mix_w = (w_o_a[l].astype(BF16), w_o_b[l].astype(BF16), w_o_c[l].astype(BF16),
                 w_out[l].astype(BF16))
        gq = jnp.tile(a_q_norm_g[l], LANES // HEAD_DIM)[None, :] * (HEAD_DIM ** -0.5 * LOG2_E)
        gcq = c_q_norm_g[l][None, :] * (X_HEAD_DIM ** -0.5 * LOG2_E)
        gk = jnp.tile(a_k_norm_g[l], LANES // HEAD_DIM)[None, :]
        rw32 = jnp.pad(router_w[l], ((0, 0), (0, LANES - N_EXPERTS)))
        rw_hi = rw32.astype(BF16)
        rw = jnp.concatenate([rw_hi, (rw32 - rw_hi.astype(F32)).astype(BF16)], axis=1)
        rb = jnp.pad(router_b[l], (0, LANES - N_EXPERTS))[None, :]

        routed = []
        scatter_idx = None
        batch0 = 0
        for gb in group_batches:
            tg = gb * seq
            row0 = batch0 * seq
            n_asg = tg * TOP_K
            nb = -(-n_asg // MOE_BLOCK) + N_EXPERTS
            nb = -(-nb // EXPERT_BLOCKS_PER_STEP) * EXPERT_BLOCKS_PER_STEP
            n_rows = nb * MOE_BLOCK
            tm = _pick_tile(tg, 2 * PROJ_SUB_ROWS, row0, t)
            proj = _proj(
                x2d, pos2d, attn_norm_g[l][None, :], w_in_bf, invf, sgn, gq, gk,
                gcq, gmlp_ln_g[l][None, :], gmlp_ln_b[l][None, :],
                b_gates[l].reshape(1, 3 * d), tm, row0, tg, scatter_idx)
            x1, h2, idx, rank, gates, counts = _mix(
                a_sinks[l], proj, x2d, kc, vc, gmlp_w_s[l], gmlp_b_s[l].T, *mix_w,
                ffn_norm_g[l][None, :], rw, rb, gb, seq, m_len, ts, batch0)

            cnt = counts[:, 0].astype(I32)
            padded = (cnt + MOE_BLOCK - 1) // MOE_BLOCK * MOE_BLOCK
            pend = jnp.cumsum(padded)
            pstart = (pend - padded).astype(I32)
            nvalid = (pend[-1:] // MOE_BLOCK).astype(I32)
            blk_row = jnp.arange(nb, dtype=I32) * MOE_BLOCK
            bexp = jnp.minimum(jnp.sum((pend[None, :] <= blk_row[:, None]).astype(I32), axis=1),
                               N_EXPERTS - 1).astype(I32)
            eid = jnp.arange(N_EXPERTS, dtype=I32)
            later = jnp.logical_and(eid[None, :] > eid[:, None], (cnt > 0)[None, :])
            next_expert = jnp.min(jnp.where(later, eid[None, :], N_EXPERTS), axis=1)
            next_expert = jnp.where(next_expert == N_EXPERTS, -1, next_expert).astype(I32)
            j = jnp.arange(MOE_BLOCK, dtype=I32)[None, :]
            fill = jnp.where(j < (padded - cnt)[:, None], pend[:, None] - 1 - j,
                             n_rows - 1 - j).astype(I32)

            pos = _pos(pstart, idx, rank, _pick_tile(tg, 8192))
            pos_km = pos[:TOP_K].reshape(n_asg)
            n_chunks = h2.shape[0]
            chunk_off = (jnp.arange(n_chunks, dtype=I32) * n_rows)[:, None]
            scatter_idx = jnp.concatenate([pos_km, fill.reshape(-1)])[None, :] + chunk_off
            xs = _sc_scatter_rows(h2, scatter_idx, n_rows)
            routed.append((row0, x1, gates, xs, bexp, nvalid, next_expert, pos_km, chunk_off))
            batch0 += gb

        gathered = []
        for row0, x1, gates, xs, bexp, nvalid, next_expert, pos_km, chunk_off in routed:
            ys = _experts(bexp, nvalid, next_expert, xs, w_gate_up[l], b_gate_up[l][:, None, :],
                          w_down[l], b_down[l][:, None, :])
            yg = _sc_gather_rows(ys, pos_km[None, :] + chunk_off)
            gathered.append(
                (row0, x1, gates, yg.reshape(yg.shape[0], TOP_K, x1.shape[0], SC_CHUNK)))

        out = None
        for row0, x1, gates, yg in gathered:
            out = _combine(x1, gates, yg, out, _pick_tile(x1.shape[0], 1024, row0), row0, t)
        x2d = out
    return x2d.reshape(n_batch, seq, d)
```

```python
import functools

import numpy as np
import jax
import jax.numpy as jnp
from jax import lax
from jax.experimental import pallas as pl
from jax.experimental.pallas import tpu as pltpu
from jax.experimental.pallas import tpu_sc as plsc

F32 = jnp.float32
BF16 = jnp.bfloat16
I32 = jnp.int32
U32 = jnp.uint32
HI16 = np.uint32(0xFFFF0000)
LOG2_E = float(np.log2(np.e))

EPS = 1e-6
LANES = 128
HEAD_DIM = 64
N_Q_HEADS = 16
N_KV_HEADS = 2
ATT_BLOCK = 128
ROPE_THETA = 10000.0
GMLP_WIDTH = 512
GMLP_GROUPS = 4
GMLP_CHUNK = 128
X_HEADS = 4
X_HEAD_DIM = 128
N_EXPERTS = 32
TOP_K = 4
SWIGLU_LIMIT = 7.0
SWIGLU_ALPHA = 1.702
MOE_BLOCK = 512
SC_ROWS = 128
SC_CHUNK = 256
PROJ_SUB_ROWS = 512
EXPERT_BLOCKS_PER_STEP = 4
IDX_ROWS = 8

A_Q = N_Q_HEADS * HEAD_DIM
A_KV = N_KV_HEADS * HEAD_DIM
C_Q = X_HEADS * X_HEAD_DIM

OFF_Q = 0
OFF_K4 = OFF_Q + A_Q
OFF_V4 = OFF_K4 + 4 * LANES
OFF_U = OFF_V4 + 4 * LANES
OFF_VN = OFF_U + GMLP_WIDTH
OFF_QC = OFF_VN + GMLP_WIDTH
OFF_GA = OFF_QC + C_Q
PROJ_W_BASE = OFF_GA

VMEM_LIMIT = 56 * 1024 * 1024


def _lane_iota(shape):
    return lax.broadcasted_iota(I32, shape, len(shape) - 1)


def _rms(x, g):
    return x * lax.rsqrt(jnp.mean(x * x, axis=-1, keepdims=True) + EPS) * g


def _pack_bf16_pairs(x):
    n = x.shape[1] // 2
    bits = pltpu.bitcast(x.astype(BF16).astype(F32), U32)
    return (bits[:, :n] >> 16) | (bits[:, n:] & HI16)


def _unpack_bf16_pairs(w):
    return pltpu.bitcast(w << 16, F32), pltpu.bitcast(w & HI16, F32)


def _gelu(x):
    return 0.5 * x * (1.0 + lax.erf(x * np.float32(np.sqrt(0.5))))


def _memkv_kernel(mem_ref, g_ref, w_ref, gk_ref, kc_ref, vc_ref):
    h = _rms(mem_ref[...], g_ref[...]).astype(BF16)
    kv = jnp.dot(h, w_ref[...], preferred_element_type=F32)
    for hh in range(X_HEADS):
        sl = slice(hh * X_HEAD_DIM, (hh + 1) * X_HEAD_DIM)
        kc_ref[:, sl] = _rms(kv[:, sl], gk_ref[...]).astype(BF16)
    vc_ref[...] = kv[:, C_Q:].astype(BF16)


def _mem_kv(mem2d, g, w_bf, gk, n_batch, m_len):
    d = mem2d.shape[1]
    rows = _pick_tile(n_batch * m_len, 1024)
    return pl.pallas_call(
        _memkv_kernel,
        grid=(n_batch * m_len // rows,),
        in_specs=[
            pl.BlockSpec((rows, d), lambda b: (b, 0)),
            pl.BlockSpec((1, d), lambda b: (0, 0)),
            pl.BlockSpec((d, 2 * C_Q), lambda b: (0, 0)),
            pl.BlockSpec((1, X_HEAD_DIM), lambda b: (0, 0)),
        ],
        out_specs=[
            pl.BlockSpec((rows, C_Q), lambda b: (b, 0)),
            pl.BlockSpec((rows, C_Q), lambda b: (b, 0)),
        ],
        out_shape=[jax.ShapeDtypeStruct((n_batch * m_len, C_Q), BF16)] * 2,
        name="mem_kv",
        compiler_params=pltpu.CompilerParams(dimension_semantics=("arbitrary",)),
    )(mem2d, g, w_bf, gk)


def _proj_kernel(x_ref, pos_ref, g_ref, w_ref, invf_ref, sgn_ref, gq_ref, gk_ref, gcq_ref,
                 lng_ref, lnb_ref, bg_ref, out_ref, *, d_model):
    sub = min(PROJ_SUB_ROWS, x_ref.shape[0])
    for r in range(x_ref.shape[0] // sub):
        rows = pl.ds(r * sub, sub)
        _proj_rows(x_ref.at[rows], pos_ref.at[:, rows], g_ref, w_ref, invf_ref, sgn_ref, gq_ref,
                   gk_ref, gcq_ref, lng_ref, lnb_ref, bg_ref, out_ref.at[rows], d_model=d_model)


def _proj_rows(x_ref, pos_ref, g_ref, w_ref, invf_ref, sgn_ref, gq_ref, gk_ref, gcq_ref,
               lng_ref, lnb_ref, bg_ref, out_ref, *, d_model):
    tm = x_ref.shape[0]
    h = _rms(x_ref[...], g_ref[...]).astype(BF16)

    ang_t = invf_ref[...] * pos_ref[...].astype(F32)
    reps = LANES // (HEAD_DIM // 2)
    cosv = jnp.transpose(jnp.concatenate([jnp.cos(ang_t)] * reps, axis=0))
    sinv = jnp.transpose(jnp.concatenate([jnp.sin(ang_t)] * reps, axis=0)) * sgn_ref[...]
    lane = _lane_iota((tm, LANES))
    first_head = lane < HEAD_DIM
    lo_half = (lane % HEAD_DIM) < (HEAD_DIM // 2)

    def head_norm_rope(blk, g):
        y = blk * blk
        s_lo = jnp.sum(jnp.where(first_head, y, 0.0), axis=-1, keepdims=True)
        s_hi = jnp.sum(jnp.where(first_head, 0.0, y), axis=-1, keepdims=True)
        ss = jnp.where(first_head, s_lo, s_hi)
        n = blk * lax.rsqrt(ss * (1.0 / HEAD_DIM) + EPS) * g
        rot = jnp.where(lo_half, pltpu.roll(n, LANES - HEAD_DIM // 2, 1),
                        pltpu.roll(n, HEAD_DIM // 2, 1))
        return n * cosv + rot * sinv

    def proj(a, b):
        return jnp.dot(h, w_ref[:, a:b], preferred_element_type=F32)

    pq = proj(0, A_Q)
    for c in range(A_Q // LANES):
        sl = slice(c * LANES, (c + 1) * LANES)
        out_ref[:, OFF_Q + c * LANES:OFF_Q + (c + 1) * LANES] = (
            head_norm_rope(pq[:, sl], gq_ref[...])).astype(BF16)

    pkv = proj(A_Q, A_Q + 2 * A_KV)
    kn = head_norm_rope(pkv[:, :LANES], gk_ref[...])
    vv = pkv[:, LANES:]
    for off, t in ((OFF_K4, kn), (OFF_V4, vv)):
        tr = pltpu.roll(t, HEAD_DIM, 1)
        parts = (jnp.where(first_head, t, 0.0), jnp.where(first_head, 0.0, tr),
                 jnp.where(first_head, tr, 0.0), jnp.where(first_head, 0.0, t))
        for j, p in enumerate(parts):
            out_ref[:, off + j * LANES:off + (j + 1) * LANES] = p.astype(BF16)

    o0 = A_Q + 2 * A_KV
    out_ref[:, OFF_U:OFF_U + GMLP_WIDTH] = _gelu(proj(o0, o0 + GMLP_WIDTH)).astype(BF16)
    gv = _gelu(proj(o0 + GMLP_WIDTH, o0 + 2 * GMLP_WIDTH))
    mu = jnp.mean(gv, axis=-1, keepdims=True)
    var = jnp.mean(jnp.square(gv - mu), axis=-1, keepdims=True)
    out_ref[:, OFF_VN:OFF_VN + GMLP_WIDTH] = (
        (gv - mu) * lax.rsqrt(var + EPS) * lng_ref[...] + lnb_ref[...]).astype(BF16)

    o1 = o0 + 2 * GMLP_WIDTH
    pc = proj(o1, o1 + C_Q)
    for hh in range(X_HEADS):
        sl = slice(hh * X_HEAD_DIM, (hh + 1) * X_HEAD_DIM)
        out_ref[:, OFF_QC + hh * X_HEAD_DIM:OFF_QC + (hh + 1) * X_HEAD_DIM] = (
            _rms(pc[:, sl], gcq_ref[...])).astype(BF16)

    o2 = o1 + C_Q
    for j in range(3):
        sl = slice(j * d_model, (j + 1) * d_model)
        z = proj(o2 + j * d_model, o2 + (j + 1) * d_model) + bg_ref[:, sl]
        out_ref[:, OFF_GA + j * d_model:OFF_GA + (j + 1) * d_model] = (
            0.5 * jnp.tanh(0.5 * z) + 0.5).astype(BF16)


def _proj_after_kernel(after_ref, *refs, d_model):
    del after_ref
    _proj_kernel(*refs, d_model=d_model)


def _proj(x2d, pos2d, g, w_bf, invf, sgn, gq, gk, gcq, lng, lnb, bg, tm, row0, t, after):
    d = x2d.shape[1]
    d_in = w_bf.shape[1]
    pw = PROJ_W_BASE + 3 * d
    blk0 = row0 // tm
    full = lambda shape: pl.BlockSpec(shape, lambda i: (0,) * len(shape))
    body = functools.partial(_proj_kernel, d_model=d)
    lead_specs, lead_args = [], []
    if after is not None:
        body = functools.partial(_proj_after_kernel, d_model=d)
        lead_specs, lead_args = [pl.BlockSpec(memory_space=pl.ANY)], [after]
    return pl.pallas_call(
        body,
        grid=(t // tm,),
        in_specs=lead_specs + [
            pl.BlockSpec((tm, d), lambda i: (blk0 + i, 0)),
            pl.BlockSpec((None, 1, tm), lambda i: (blk0 + i, 0, 0)),
            full((1, d)),
            pl.BlockSpec((d, d_in), lambda i: (0, 0), pipeline_mode=pl.Buffered(1)),
            full((HEAD_DIM // 2, 1)), full((1, LANES)),
            full((1, LANES)), full((1, LANES)), full((1, X_HEAD_DIM)),
            full((1, GMLP_WIDTH)), full((1, GMLP_WIDTH)), full((1, 3 * d)),
        ],
        out_specs=pl.BlockSpec((tm, pw), lambda i: (i, 0)),
        out_shape=jax.ShapeDtypeStruct((t, pw), BF16),
        name="proj",
        cost_estimate=pl.CostEstimate(
            flops=2 * t * d * d_in, transcendentals=t * (3 * d + 2 * GMLP_WIDTH + 2 * LANES),
            bytes_accessed=4 * t * d + 2 * t * pw + 2 * d * d_in),
        compiler_params=pltpu.CompilerParams(
            dimension_semantics=("arbitrary",), vmem_limit_bytes=VMEM_LIMIT),
    )(*lead_args, x2d, pos2d.reshape(-1, 1, tm), g, w_bf, invf, sgn, gq, gk, gcq, lng, lnb, bg)


def _mix_kernel(sinks_ref, proj_ref, kprev_ref, vprev_ref, x_ref, kc_ref, vc_ref, ws_ref, bst_ref,
                woa_ref, wob_ref, woc_ref, wout_ref, gffn_ref, rw_ref, rb_ref,
                x1_ref, h2_ref, idx_ref, rank_ref, gate_ref, cnt_ref,
                run_ref, oa_ref, ob_ref, oc_ref, *, d_model):
    ts = x_ref.shape[0]
    s_idx = pl.program_id(1)
    first_step = jnp.logical_and(pl.program_id(0) == 0, s_idx == 0)
    neg_inf = float("-inf")

    cpk = A_Q // LANES // N_KV_HEADS
    stack = cpk * ATT_BLOCK
    srow = lax.broadcasted_iota(I32, (stack, 2 * ATT_BLOCK), 0)
    qi = srow % ATT_BLOCK
    kj = lax.broadcasted_iota(I32, (stack, 2 * ATT_BLOCK), 1)
    band = jnp.logical_and(kj <= ATT_BLOCK + qi, kj > qi)
    chunk_of_row = lax.broadcasted_iota(I32, (stack, 1), 0) // ATT_BLOCK
    for qb in range(ts // ATT_BLOCK):
        r0 = qb * ATT_BLOCK
        rows = slice(r0, r0 + ATT_BLOCK)
        if qb == 0:
            kp, vp = kprev_ref[...], vprev_ref[...]
            mask = jnp.logical_and(band, jnp.logical_or(kj >= ATT_BLOCK, s_idx > 0))
        else:
            prow = slice(r0 - ATT_BLOCK, r0)
            kp = proj_ref[prow, OFF_K4:OFF_K4 + 4 * LANES]
            vp = proj_ref[prow, OFF_V4:OFF_V4 + 4 * LANES]
            mask = band
        k4 = jnp.concatenate([kp, proj_ref[rows, OFF_K4:OFF_K4 + 4 * LANES]], axis=0)
        v4 = jnp.concatenate([vp, proj_ref[rows, OFF_V4:OFF_V4 + 4 * LANES]], axis=0)
        for kvh in range(N_KV_HEADS):
            c0 = kvh * cpk
            q4 = jnp.concatenate(
                [proj_ref[rows, OFF_Q + (c0 + c) * LANES:OFF_Q + (c0 + c + 1) * LANES]
                 for c in range(cpk)], axis=0)
            o = jnp.zeros((stack, LANES), F32)
            for half in range(2):
                col = slice((2 * kvh + half) * LANES, (2 * kvh + half + 1) * LANES)
                s = lax.dot_general(q4, k4[:, col], (((1,), (1,)), ((), ())),
                                    preferred_element_type=F32)
                s = jnp.where(mask, s, neg_inf)
                sink = jnp.zeros((stack, 1), F32)
                for c in range(cpk):
                    sink = jnp.where(chunk_of_row == c,
                                     sinks_ref[2 * (c0 + c) + half] * LOG2_E, sink)
                m = jnp.maximum(jnp.max(s, axis=-1, keepdims=True), sink)
                p = jnp.exp2(s - m)
                den = jnp.sum(p, axis=-1, keepdims=True) + jnp.exp2(sink - m)
                o = o + jnp.dot(p.astype(BF16), v4[:, col], preferred_element_type=F32) / den
            for c in range(cpk):
                oa_ref[rows, (c0 + c) * LANES:(c0 + c + 1) * LANES] = (
                    o[c * ATT_BLOCK:(c + 1) * ATT_BLOCK].astype(BF16))

    ti = lax.broadcasted_iota(I32, (GMLP_CHUNK, GMLP_CHUNK), 0)
    si = lax.broadcasted_iota(I32, (GMLP_CHUNK, GMLP_CHUNK), 1)
    for g in range(GMLP_GROUPS):
        wt = jnp.where(si <= ti, ws_ref[g], 0.0).astype(BF16)
        bcol = bst_ref[:, g:g + 1]
        for ch in range(ts // GMLP_CHUNK):
            rows = slice(ch * GMLP_CHUNK, (ch + 1) * GMLP_CHUNK)
            vn = proj_ref[rows, OFF_VN + g * LANES:OFF_VN + (g + 1) * LANES]
            u = proj_ref[rows, OFF_U + g * LANES:OFF_U + (g + 1) * LANES].astype(F32)
            mixed = jnp.dot(wt, vn, preferred_element_type=F32) + bcol
            ob_ref[rows, g * LANES:(g + 1) * LANES] = (u * mixed).astype(BF16)

    for hh in range(X_HEADS):
        sl = slice(hh * X_HEAD_DIM, (hh + 1) * X_HEAD_DIM)
        qc = proj_ref[:, OFF_QC + hh * X_HEAD_DIM:OFF_QC + (hh + 1) * X_HEAD_DIM]
        s = lax.dot_general(qc, kc_ref[:, sl], (((1,), (1,)), ((), ())),
                            preferred_element_type=F32)
        p = jnp.exp2(s - jnp.max(s, axis=-1, keepdims=True))
        den = jnp.sum(p, axis=-1, keepdims=True)
        oc_ref[:, sl] = (jnp.dot(p.astype(BF16), vc_ref[:, sl],
                                 preferred_element_type=F32) / den).astype(BF16)

    def gate(j):
        return proj_ref[:, OFF_GA + j * d_model:OFF_GA + (j + 1) * d_model].astype(F32)

    merged = gate(0) * jnp.dot(oa_ref[...], woa_ref[...], preferred_element_type=F32)
    merged = merged + gate(1) * jnp.dot(ob_ref[...], wob_ref[...], preferred_element_type=F32)
    merged = merged + gate(2) * jnp.dot(oc_ref[...], woc_ref[...], preferred_element_type=F32)
    x1 = x_ref[...] + jnp.dot(merged.astype(BF16), wout_ref[...], preferred_element_type=F32)
    x1_ref[...] = x1

    h2 = _rms(x1, gffn_ref[...])
    h2_words = _pack_bf16_pairs(h2)
    for c in range(h2_ref.shape[0]):
        h2_ref[c] = h2_words[:, c * SC_CHUNK:(c + 1) * SC_CHUNK]
    h2_hi = h2.astype(BF16)
    h2_lo = (h2 - h2_hi.astype(F32)).astype(BF16)
    part = jnp.dot(h2_hi, rw_ref[...], preferred_element_type=F32)
    logits = (part[:, :LANES] + part[:, LANES:]
              + jnp.dot(h2_lo, rw_ref[:, :LANES], preferred_element_type=F32) + rb_ref[...])
    lt = jnp.transpose(logits)[:N_EXPERTS]
    erow = lax.broadcasted_iota(I32, (N_EXPERTS, ts), 0)
    vals, idxs = [], []
    for _ in range(TOP_K):
        m = jnp.max(lt, axis=0, keepdims=True)
        i = jnp.min(jnp.where(lt == m, erow, N_EXPERTS), axis=0, keepdims=True)
        vals.append(m)
        idxs.append(i)
        lt = jnp.where(erow == i, neg_inf, lt)
    es = [jnp.exp(v - vals[0]) for v in vals]
    den = es[0] + es[1] + es[2] + es[3]

    @pl.when(first_step)
    def _():
        run_ref[...] = jnp.zeros_like(run_ref)

    hot = [erow == i for i in idxs]
    multihot = jnp.where(jnp.logical_or(jnp.logical_or(hot[0], hot[1]),
                                        jnp.logical_or(hot[2], hot[3])), 1.0, 0.0)
    tr = lax.broadcasted_iota(I32, (ts, ts), 0)
    tc = lax.broadcasted_iota(I32, (ts, ts), 1)
    earlier = jnp.where(tr < tc, 1.0, 0.0).astype(BF16)
    before = jnp.dot(multihot.astype(BF16), earlier, preferred_element_type=F32) + run_ref[...]
    krow = lax.broadcasted_iota(I32, (IDX_ROWS, ts), 0)
    idx_out = jnp.zeros((IDX_ROWS, ts), I32)
    rank_out = jnp.zeros((IDX_ROWS, ts), I32)
    gate_rows = jnp.zeros((IDX_ROWS, ts), F32)
    for k in range(TOP_K):
        rk = jnp.sum(jnp.where(hot[k], before, 0.0), axis=0, keepdims=True)
        idx_out = jnp.where(krow == k, idxs[k], idx_out)
        rank_out = jnp.where(krow == k, rk.astype(I32), rank_out)
        gate_rows = jnp.where(krow == k, es[k] / den, gate_rows)
    idx_ref[...] = idx_out
    rank_ref[...] = rank_out
    gate_ref[...] = jnp.transpose(jnp.concatenate(
        [gate_rows, jnp.zeros((LANES - IDX_ROWS, ts), F32)], axis=0))
    run_ref[...] = run_ref[...] + jnp.sum(multihot, axis=1, keepdims=True)
    cnt_ref[...] = run_ref[...]


def _mix(sinks, proj, x2d, kc, vc, w_s, bst, woa, wob, woc, wout, gffn, rw, rb,
         n_batch, seq, m_len, ts, batch0):
    d = x2d.shape[1]
    t = n_batch * seq
    pw = proj.shape[1]
    ns = seq // ts
    nblk = seq // ATT_BLOCK
    per = ts // ATT_BLOCK
    n_chunks = d // (2 * SC_CHUNK)
    full = lambda shape: pl.BlockSpec(shape, lambda b, s: (0,) * len(shape))
    row = lambda width: pl.BlockSpec((ts, width), lambda b, s: (b * ns + s, 0))
    prev = lambda colblk: pl.BlockSpec(
        (ATT_BLOCK, 4 * LANES), lambda b, s: (b * nblk + jnp.maximum(s * per - 1, 0), colblk))
    return pl.pallas_call(
        functools.partial(_mix_kernel, d_model=d),
        grid=(n_batch, ns),
        in_specs=[
            pl.BlockSpec(memory_space=pltpu.SMEM),
            row(pw), prev(OFF_K4 // (4 * LANES)), prev(OFF_V4 // (4 * LANES)),
            pl.BlockSpec((ts, d), lambda b, s: ((batch0 + b) * ns + s, 0)),
            pl.BlockSpec((m_len, C_Q), lambda b, s: (batch0 + b, 0)),
            pl.BlockSpec((m_len, C_Q), lambda b, s: (batch0 + b, 0)),
            full((GMLP_GROUPS, GMLP_CHUNK, GMLP_CHUNK)), full((GMLP_CHUNK, GMLP_GROUPS)),
            full((A_Q, d)), full((GMLP_WIDTH, d)), full((C_Q, d)), full((d, d)),
            full((1, d)), full((d, 2 * LANES)), full((1, LANES)),
        ],
        out_specs=[row(d),
                   pl.BlockSpec((n_chunks, ts, SC_CHUNK), lambda b, s: (0, b * ns + s, 0)),
                   pl.BlockSpec((IDX_ROWS, ts), lambda b, s: (0, b * ns + s)),
                   pl.BlockSpec((IDX_ROWS, ts), lambda b, s: (0, b * ns + s)),
                   row(LANES), full((N_EXPERTS, 1))],
        out_shape=[
            jax.ShapeDtypeStruct((t, d), F32),
            jax.ShapeDtypeStruct((n_chunks, t, SC_CHUNK), U32),
            jax.ShapeDtypeStruct((IDX_ROWS, t), I32), jax.ShapeDtypeStruct((IDX_ROWS, t), I32),
            jax.ShapeDtypeStruct((t, LANES), F32), jax.ShapeDtypeStruct((N_EXPERTS, 1), F32),
        ],
        scratch_shapes=[
            pltpu.VMEM((N_EXPERTS, 1), F32),
            pltpu.VMEM((ts, A_Q), BF16), pltpu.VMEM((ts, GMLP_WIDTH), BF16),
            pltpu.VMEM((ts, C_Q), BF16),
        ],
        name="mix",
        cost_estimate=pl.CostEstimate(
            flops=2 * t * (d * (A_Q + GMLP_WIDTH + C_Q + d) + 4 * ATT_BLOCK * A_Q
                           + GMLP_CHUNK * GMLP_WIDTH + 2 * m_len * C_Q + 3 * d * LANES),
            transcendentals=t * (2 * ATT_BLOCK * N_Q_HEADS + m_len * X_HEADS),
            bytes_accessed=t * (2 * pw + 4 * d + 4 * d + 2 * d + 12 * LANES)),
        compiler_params=pltpu.CompilerParams(
            dimension_semantics=("arbitrary", "arbitrary"), vmem_limit_bytes=VMEM_LIMIT),
    )(sinks, proj, proj, proj, x2d, kc, vc, w_s, bst, woa, wob, woc, wout, gffn, rw, rb)


def _pos_kernel(pstart_ref, idx_ref, rank_ref, pos_ref):
    idx = idx_ref[...]
    pos = rank_ref[...]
    for e in range(N_EXPERTS):
        pos = pos + jnp.where(idx == e, pstart_ref[e], 0)
    pos_ref[...] = pos


def _pos(pstart, idx, rank, tp):
    t = idx.shape[1]
    return pl.pallas_call(
        _pos_kernel,
        grid_spec=pltpu.PrefetchScalarGridSpec(
            num_scalar_prefetch=1,
            grid=(t // tp,),
            in_specs=[pl.BlockSpec((IDX_ROWS, tp), lambda i, ps: (0, i)),
                      pl.BlockSpec((IDX_ROWS, tp), lambda i, ps: (0, i))],
            out_specs=pl.BlockSpec((IDX_ROWS, tp), lambda i, ps: (0, i)),
        ),
        out_shape=jax.ShapeDtypeStruct((IDX_ROWS, t), I32),
        name="slot_pos",
        compiler_params=pltpu.CompilerParams(dimension_semantics=("arbitrary",)),
    )(pstart, idx, rank)


def _sc_mesh():
    return plsc.VectorSubcoreMesh(core_axis_name="core", subcore_axis_name="subcore")


def _sc_scatter_rows(src, idx, n_rows):
    n_chunks, t, w = src.shape
    n_idx = idx.shape[1]
    src_blocks = t // SC_ROWS
    idx_blocks = n_idx // SC_ROWS

    @pl.kernel(out_type=jax.ShapeDtypeStruct((n_chunks * n_rows, w), src.dtype), mesh=_sc_mesh(),
               scratch_types=[], name="sc_dispatch",
               cost_estimate=pl.CostEstimate(
                   flops=0, transcendentals=0,
                   bytes_accessed=n_chunks * n_idx * (8 * w + 4)))
    def scatter(src_hbm, idx_hbm, out_hbm):
        def body(src_vmem, idx_vmem):
            pltpu.sync_copy(src_vmem, out_hbm.at[idx_vmem.at[0]])

        pltpu.emit_pipeline(
            body,
            grid=(n_chunks, idx_blocks),
            in_specs=[pl.BlockSpec((SC_ROWS, w), lambda c, i: (c * src_blocks + i % src_blocks, 0)),
                      pl.BlockSpec((1, SC_ROWS), lambda c, i: (0, c * idx_blocks + i))],
            out_specs=[],
            core_axis_name=("core", "subcore"),
            dimension_semantics=(pltpu.PARALLEL, pltpu.PARALLEL),
        )(src_hbm, idx_hbm)

    return scatter(src.reshape(n_chunks * t, w), idx.reshape(1, n_chunks * n_idx)).reshape(
        n_chunks, n_rows, w)


def _sc_gather_rows(table, idx):
    n_chunks, p, w = table.shape
    n_idx = idx.shape[1]
    idx_blocks = n_idx // SC_ROWS

    @pl.kernel(out_type=jax.ShapeDtypeStruct((n_chunks * n_idx, w), table.dtype), mesh=_sc_mesh(),
               scratch_types=[], name="sc_gather",
               cost_estimate=pl.CostEstimate(
                   flops=0, transcendentals=0,
                   bytes_accessed=n_chunks * n_idx * (8 * w + 4)))
    def gather(table_hbm, idx_hbm, out_hbm):
        def body(idx_vmem, out_vmem):
            pltpu.sync_copy(table_hbm.at[idx_vmem.at[0]], out_vmem)

        pltpu.emit_pipeline(
            body,
            grid=(n_chunks, idx_blocks),
            in_specs=[pl.BlockSpec((1, SC_ROWS), lambda c, i: (0, c * idx_blocks + i))],
            out_specs=[pl.BlockSpec((SC_ROWS, w), lambda c, i: (c * idx_blocks + i, 0))],
            core_axis_name=("core", "subcore"),
            dimension_semantics=(pltpu.PARALLEL, pltpu.PARALLEL),
        )(idx_hbm, out_hbm)

    return gather(table.reshape(n_chunks * p, w), idx.reshape(1, n_chunks * n_idx)).reshape(
        n_chunks, n_idx, w)


def _expert_kernel(bexp_ref, nvalid_ref, next_ref, xs_ref, wgu_hbm, bgu_ref, wd_hbm, bd_ref, ys_ref,
                   wgu_stage, wd_stage, wgu_bf_ref, wd_bf_ref, sems, *, d_exp):
    n_chunks = xs_ref.shape[0]

    def weight_copies(e):
        return (pltpu.make_async_copy(wgu_hbm.at[e], wgu_stage, sems.at[0]),
                pltpu.make_async_copy(wd_hbm.at[e], wd_stage, sems.at[1]))

    def switch_weights(b):
        e = bexp_ref[b]

        @pl.when(jnp.logical_or(b == 0, e != bexp_ref[jnp.maximum(b - 1, 0)]))
        def _():
            @pl.when(b == 0)
            def _():
                for cp in weight_copies(e):
                    cp.start()

            for cp in weight_copies(e):
                cp.wait()
            wgu_bf_ref[...] = wgu_stage[...].astype(BF16)
            wd_bf_ref[...] = wd_stage[...].astype(BF16)
            e_next = next_ref[e]

            @pl.when(e_next >= 0)
            def _():
                for cp in weight_copies(e_next):
                    cp.start()

    def compute(e, rows):
        lo, hi = _unpack_bf16_pairs(
            jnp.concatenate([xs_ref[c, rows, :] for c in range(n_chunks)], axis=1))
        xb = jnp.concatenate([lo, hi], axis=1).astype(BF16)
        gu = jnp.dot(xb, wgu_bf_ref[...], preferred_element_type=F32) + bgu_ref[e]
        gate = jnp.minimum(gu[:, :d_exp], SWIGLU_LIMIT)
        up = jnp.clip(gu[:, d_exp:], -SWIGLU_LIMIT, SWIGLU_LIMIT)
        half_gate = 0.5 * gate
        glu = half_gate + half_gate * jnp.tanh(gate * (0.5 * SWIGLU_ALPHA))
        act = ((up + 1.0) * glu).astype(BF16)
        y = jnp.dot(act, wd_bf_ref[...], preferred_element_type=F32) + bd_ref[e]
        y_words = _pack_bf16_pairs(y)
        for c in range(n_chunks):
            ys_ref[c, rows, :] = y_words[:, c * SC_CHUNK:(c + 1) * SC_CHUNK]

    b0 = pl.program_id(0) * EXPERT_BLOCKS_PER_STEP
    b_last = b0 + EXPERT_BLOCKS_PER_STEP - 1
    block_rows = [slice(h * MOE_BLOCK, (h + 1) * MOE_BLOCK) for h in range(EXPERT_BLOCKS_PER_STEP)]
    one_expert = jnp.logical_and(b_last < nvalid_ref[0], bexp_ref[b_last] == bexp_ref[b0])

    @pl.when(b0 < nvalid_ref[0])
    def _():
        switch_weights(b0)

        @pl.when(one_expert)
        def _():
            for rows in block_rows:
                compute(bexp_ref[b0], rows)

        @pl.when(jnp.logical_not(one_expert))
        def _():
            compute(bexp_ref[b0], block_rows[0])
            for h in range(1, EXPERT_BLOCKS_PER_STEP):
                @pl.when(b0 + h < nvalid_ref[0])
                def _():
                    switch_weights(b0 + h)
                    compute(bexp_ref[b0 + h], block_rows[h])


def _experts(bexp, nvalid, next_expert, xs, wgu, bgu, wd, bd):
    n_chunks, n_rows, _ = xs.shape
    n_exp, d, d_exp2 = wgu.shape
    d_exp = d_exp2 // 2
    step_rows = EXPERT_BLOCKS_PER_STEP * MOE_BLOCK
    blk = lambda s, be, nv, nx: jnp.minimum(s, (nv[0] - 1) // EXPERT_BLOCKS_PER_STEP)
    return pl.pallas_call(
        functools.partial(_expert_kernel, d_exp=d_exp),
        grid_spec=pltpu.PrefetchScalarGridSpec(
            num_scalar_prefetch=3,
            grid=(n_rows // step_rows,),
            in_specs=[
                pl.BlockSpec((n_chunks, step_rows, SC_CHUNK),
                             lambda s, be, nv, nx: (0, blk(s, be, nv, nx), 0)),
                pl.BlockSpec(memory_space=pl.ANY),
                pl.BlockSpec((n_exp, 1, 2 * d_exp), lambda s, be, nv, nx: (0, 0, 0)),
                pl.BlockSpec(memory_space=pl.ANY),
                pl.BlockSpec((n_exp, 1, d), lambda s, be, nv, nx: (0, 0, 0)),
            ],
            out_specs=pl.BlockSpec((n_chunks, step_rows, SC_CHUNK),
                                   lambda s, be, nv, nx: (0, blk(s, be, nv, nx), 0)),
            scratch_shapes=[pltpu.VMEM((d, 2 * d_exp), F32), pltpu.VMEM((d_exp, d), F32),
                            pltpu.VMEM((d, 2 * d_exp), BF16), pltpu.VMEM((d_exp, d), BF16),
                            pltpu.SemaphoreType.DMA((2,))],
        ),
        out_shape=jax.ShapeDtypeStruct((n_chunks, n_rows, SC_CHUNK), U32),
        name="experts",
        cost_estimate=pl.CostEstimate(
            flops=6 * n_rows * d * d_exp, transcendentals=n_rows * d_exp,
            bytes_accessed=4 * n_rows * d + 4 * N_EXPERTS * 3 * d * d_exp),
        compiler_params=pltpu.CompilerParams(
            dimension_semantics=("arbitrary",), vmem_limit_bytes=VMEM_LIMIT),
    )(bexp, nvalid, next_expert, xs, wgu, bgu, wd, bd)


def _combine_kernel(prev_ref, x1_ref, gate_ref, yg_ref, out_ref):
    del prev_ref
    n_chunks = yg_ref.shape[0]
    half = n_chunks * SC_CHUNK
    for c in range(n_chunks):
        sl_lo = slice(c * SC_CHUNK, (c + 1) * SC_CHUNK)
        sl_hi = slice(half + c * SC_CHUNK, half + (c + 1) * SC_CHUNK)
        acc_lo = x1_ref[:, sl_lo]
        acc_hi = x1_ref[:, sl_hi]
        for k in range(TOP_K):
            lo, hi = _unpack_bf16_pairs(yg_ref[c, k])
            g = gate_ref[:, k:k + 1]
            acc_lo = acc_lo + g * lo
            acc_hi = acc_hi + g * hi
        out_ref[:, sl_lo] = acc_lo
        out_ref[:, sl_hi] = acc_hi


def _combine(x1, gates, yg, out_prev, tcb, row0, t_total):
    t, d = x1.shape
    n_chunks = yg.shape[0]
    blk0 = row0 // tcb
    in_specs = [
        pl.BlockSpec((tcb, d), lambda i: (i, 0)),
        pl.BlockSpec((tcb, LANES), lambda i: (i, 0)),
        pl.BlockSpec((n_chunks, TOP_K, tcb, SC_CHUNK), lambda i: (0, 0, i, 0)),
    ]
    args = [x1, gates, yg]
    aliases = {}
    body = functools.partial(_combine_kernel, None)
    if out_prev is not None:
        in_specs.append(pl.BlockSpec(memory_space=pl.ANY))
        args.append(out_prev)
        aliases = {3: 0}
        body = lambda a, b, c, prev, o: _combine_kernel(prev, a, b, c, o)
    return pl.pallas_call(
        body,
        grid=(t // tcb,),
        in_specs=in_specs,
        out_specs=pl.BlockSpec((tcb, d), lambda i: (blk0 + i, 0)),
        out_shape=jax.ShapeDtypeStruct((t_total, d), F32),
        input_output_aliases=aliases,
        name="combine",
        cost_estimate=pl.CostEstimate(
            flops=2 * TOP_K * t * d, transcendentals=0,
            bytes_accessed=t * (4 * d + 4 * d + 2 * TOP_K * d + 4 * LANES)),
        compiler_params=pltpu.CompilerParams(
            dimension_semantics=("arbitrary",), vmem_limit_bytes=VMEM_LIMIT),
    )(*args)


def _pick_tile(n, pref, *also):
    t = pref
    while any(v % t for v in (n,) + also):
        t //= 2
    return t


def _token_groups(n_batch):
    if n_batch % 16 == 0:
        return [11 * n_batch // 16, 5 * n_batch // 16]
    if n_batch % 8 == 0:
        return [5 * n_batch // 8, 3 * n_batch // 8]
    if n_batch % 4 == 0:
        return [3 * n_batch // 4, n_batch // 4]
    if n_batch % 2 == 0:
        return [n_batch // 2, n_batch // 2]
    return [n_batch]


def kernel(x, mem, positions, attn_norm_g, mem_norm_g, w_in, b_gates, a_q_norm_g, a_k_norm_g,
           a_sinks, w_o_a, gmlp_ln_g, gmlp_ln_b, gmlp_w_s, gmlp_b_s, w_o_b, w_mem_kv,
           c_q_norm_g, c_k_norm_g, w_o_c, w_out, ffn_norm_g, router_w, router_b,
           w_gate_up, b_gate_up, w_down, b_down):
    n_batch, seq, d = x.shape
    m_len = mem.shape[1]
    depth = w_in.shape[0]
    t = n_batch * seq
    assert seq % ATT_BLOCK == 0 and seq % GMLP_CHUNK == 0, "sequence must be whole 128-token blocks"
    assert d % (2 * SC_CHUNK) == 0, "model width must be a multiple of the packed row chunk"
    assert m_len % 8 == 0 and w_in.shape[2] == A_Q + 2 * A_KV + 2 * GMLP_WIDTH + C_Q + 3 * d
    assert router_w.shape[2] == N_EXPERTS and w_gate_up.shape[1:] == (N_EXPERTS, d, 2 * w_down.shape[2])
    group_batches = _token_groups(n_batch)

    inv_freq = ROPE_THETA ** (-jnp.arange(0, HEAD_DIM, 2, dtype=F32) / HEAD_DIM)
    invf = inv_freq[:, None]
    sgn = jnp.tile(jnp.concatenate([-jnp.ones((HEAD_DIM // 2,), F32),
                                    jnp.ones((HEAD_DIM // 2,), F32)]), LANES // HEAD_DIM)[None, :]
    pos2d = positions.reshape(t, 1).astype(I32)
    mem2d = mem.reshape(n_batch * m_len, d)
    x2d = x.reshape(t, d)

    ts = _pick_tile(seq, 512)

    for l in range(depth):
        kc, vc = _mem_kv(mem2d, mem_norm_g[l][None, :], w_mem_kv[l].astype(BF16),
                         c_k_norm_g[l][None, :], n_batch, m_len)
        w_in_bf = w_in[l].astype(BF16)
        mix_w = (w_o_a[l].astype(BF16), w_o_b[l].astype(BF16), w_o_c[l].astype(BF16),
                 w_out[l].astype(BF16))
        gq = jnp.tile(a_q_norm_g[l], LANES // HEAD_DIM)[None, :] * (HEAD_DIM ** -0.5 * LOG2_E)
        gcq = c_q_norm_g[l][None, :] * (X_HEAD_DIM ** -0.5 * LOG2_E)
        gk = jnp.tile(a_k_norm_g[l], LANES // HEAD_DIM)[None, :]
        rw32 = jnp.pad(router_w[l], ((0, 0), (0, LANES - N_EXPERTS)))
        rw_hi = rw32.astype(BF16)
        rw = jnp.concatenate([rw_hi, (rw32 - rw_hi.astype(F32)).astype(BF16)], axis=1)
        rb = jnp.pad(router_b[l], (0, LANES - N_EXPERTS))[None, :]

        routed = []
        scatter_idx = None
        batch0 = 0
        for gb in group_batches:
            tg = gb * seq
            row0 = batch0 * seq
            n_asg = tg * TOP_K
            nb = -(-n_asg // MOE_BLOCK) + N_EXPERTS
            nb = -(-nb // EXPERT_BLOCKS_PER_STEP) * EXPERT_BLOCKS_PER_STEP
            n_rows = nb * MOE_BLOCK
            tm = _pick_tile(tg, 2 * PROJ_SUB_ROWS, row0, t)
            proj = _proj(
                x2d, pos2d, attn_norm_g[l][None, :], w_in_bf, invf, sgn, gq, gk,
                gcq, gmlp_ln_g[l][None, :], gmlp_ln_b[l][None, :],
                b_gates[l].reshape(1, 3 * d), tm, row0, tg, scatter_idx)
            x1, h2, idx, rank, gates, counts = _mix(
                a_sinks[l], proj, x2d, kc, vc, gmlp_w_s[l], gmlp_b_s[l].T, *mix_w,
                ffn_norm_g[l][None, :], rw, rb, gb, seq, m_len, ts, batch0)

            cnt = counts[:, 0].astype(I32)
            padded = (cnt + MOE_BLOCK - 1) // MOE_BLOCK * MOE_BLOCK
            pend = jnp.cumsum(padded)
            pstart = (pend - padded).astype(I32)
            nvalid = (pend[-1:] // MOE_BLOCK).astype(I32)
            blk_row = jnp.arange(nb, dtype=I32) * MOE_BLOCK
            bexp = jnp.minimum(jnp.sum((pend[None, :] <= blk_row[:, None]).astype(I32), axis=1),
                               N_EXPERTS - 1).astype(I32)
            eid = jnp.arange(N_EXPERTS, dtype=I32)
            later = jnp.logical_and(eid[None, :] > eid[:, None], (cnt > 0)[None, :])
            next_expert = jnp.min(jnp.where(later, eid[None, :], N_EXPERTS), axis=1)
            next_expert = jnp.where(next_expert == N_EXPERTS, -1, next_expert).astype(I32)
            j = jnp.arange(MOE_BLOCK, dtype=I32)[None, :]
            fill = jnp.where(j < (padded - cnt)[:, None], pend[:, None] - 1 - j,
                             n_rows - 1 - j).astype(I32)

            pos = _pos(pstart, idx, rank, _pick_tile(tg, 8192))
            pos_km = pos[:TOP_K].reshape(n_asg)
            n_chunks = h2.shape[0]
            chunk_off = (jnp.arange(n_chunks, dtype=I32) * n_rows)[:, None]
            scatter_idx = jnp.concatenate([pos_km, fill.reshape(-1)])[None, :] + chunk_off
            xs = _sc_scatter_rows(h2, scatter_idx, n_rows)
            routed.append((row0, x1, gates, xs, bexp, nvalid, next_expert, pos_km, chunk_off))
            batch0 += gb

        gathered = []
        for row0, x1, gates, xs, bexp, nvalid, next_expert, pos_km, chunk_off in routed:
            ys = _experts(bexp, nvalid, next_expert, xs, w_gate_up[l], b_gate_up[l][:, None, :],
                          w_down[l], b_down[l][:, None, :])
            yg = _sc_gather_rows(ys, pos_km[None, :] + chunk_off)
            gathered.append(
                (row0, x1, gates, yg.reshape(yg.shape[0], TOP_K, x1.shape[0], SC_CHUNK)))

        out = None
        for row0, x1, gates, yg in gathered:
            out = _combine(x1, gates, yg, out, _pick_tile(x1.shape[0], 1024, row0), row0, t)
        x2d = out
    return x2d.reshape(n_batch, seq, d)
```

```python
import functools

import numpy as np
import jax
import jax.numpy as jnp
from jax import lax
from jax.experimental import pallas as pl
from jax.experimental.pallas import tpu as pltpu
from jax.experimental.pallas import tpu_sc as plsc

F32 = jnp.float32
BF16 = jnp.bfloat16
I32 = jnp.int32
U32 = jnp.uint32
HI16 = np.uint32(0xFFFF0000)
LOG2_E = float(np.log2(np.e))

EPS = 1e-6
LANES = 128
HEAD_DIM = 64
N_Q_HEADS = 16
N_KV_HEADS = 2
ATT_BLOCK = 128
ROPE_THETA = 10000.0
GMLP_WIDTH = 512
GMLP_GROUPS = 4
GMLP_CHUNK = 128
X_HEADS = 4
X_HEAD_DIM = 128
N_EXPERTS = 32
TOP_K = 4
SWIGLU_LIMIT = 7.0
SWIGLU_ALPHA = 1.702
MOE_BLOCK = 512
SC_ROWS = 128
SC_CHUNK = 256
PROJ_SUB_ROWS = 512
EXPERT_BLOCKS_PER_STEP = 4
IDX_ROWS = 8

A_Q = N_Q_HEADS * HEAD_DIM
A_KV = N_KV_HEADS * HEAD_DIM
C_Q = X_HEADS * X_HEAD_DIM

OFF_Q = 0
OFF_K4 = OFF_Q + A_Q
OFF_V4 = OFF_K4 + 4 * LANES
OFF_U = OFF_V4 + 4 * LANES
OFF_VN = OFF_U + GMLP_WIDTH
OFF_QC = OFF_VN + GMLP_WIDTH
OFF_GA = OFF_QC + C_Q
PROJ_W_BASE = OFF_GA

VMEM_LIMIT = 56 * 1024 * 1024


def _lane_iota(shape):
    return lax.broadcasted_iota(I32, shape, len(shape) - 1)


def _rms(x, g):
    return x * lax.rsqrt(jnp.mean(x * x, axis=-1, keepdims=True) + EPS) * g


def _pack_bf16_pairs(x):
    n = x.shape[1] // 2
    bits = pltpu.bitcast(x.astype(BF16).astype(F32), U32)
    return (bits[:, :n] >> 16) | (bits[:, n:] & HI16)


def _unpack_bf16_pairs(w):
    return pltpu.bitcast(w << 16, F32), pltpu.bitcast(w & HI16, F32)


def _gelu(x):
    return 0.5 * x * (1.0 + lax.erf(x * np.float32(np.sqrt(0.5))))


def _memkv_kernel(mem_ref, g_ref, w_ref, gk_ref, kc_ref, vc_ref):
    h = _rms(mem_ref[...], g_ref[...]).astype(BF16)
    kv = jnp.dot(h, w_ref[...], preferred_element_type=F32)
    for hh in range(X_HEADS):
        sl = slice(hh * X_HEAD_DIM, (hh + 1) * X_HEAD_DIM)
        kc_ref[:, sl] = _rms(kv[:, sl], gk_ref[...]).astype(BF16)
    vc_ref[...] = kv[:, C_Q:].astype(BF16)


def _mem_kv(mem2d, g, w_bf, gk, n_batch, m_len):
    d = mem2d.shape[1]
    rows = _pick_tile(n_batch * m_len, 1024)
    return pl.pallas_call(
        _memkv_kernel,
        grid=(n_batch * m_len // rows,),
        in_specs=[
            pl.BlockSpec((rows, d), lambda b: (b, 0)),
            pl.BlockSpec((1, d), lambda b: (0, 0)),
            pl.BlockSpec((d, 2 * C_Q), lambda b: (0, 0)),
            pl.BlockSpec((1, X_HEAD_DIM), lambda b: (0, 0)),
        ],
        out_specs=[
            pl.BlockSpec((rows, C_Q), lambda b: (b, 0)),
            pl.BlockSpec((rows, C_Q), lambda b: (b, 0)),
        ],
        out_shape=[jax.ShapeDtypeStruct((n_batch * m_len, C_Q), BF16)] * 2,
        name="mem_kv",
        compiler_params=pltpu.CompilerParams(dimension_semantics=("arbitrary",)),
    )(mem2d, g, w_bf, gk)


def _proj_kernel(x_ref, pos_ref, g_ref, w_ref, invf_ref, sgn_ref, gq_ref, gk_ref, gcq_ref,
                 lng_ref, lnb_ref, bg_ref, out_ref, *, d_model):
    sub = min(PROJ_SUB_ROWS, x_ref.shape[0])
    for r in range(x_ref.shape[0] // sub):
        rows = pl.ds(r * sub, sub)
        _proj_rows(x_ref.at[rows], pos_ref.at[:, rows], g_ref, w_ref, invf_ref, sgn_ref, gq_ref,
                   gk_ref, gcq_ref, lng_ref, lnb_ref, bg_ref, out_ref.at[rows], d_model=d_model)


def _proj_rows(x_ref, pos_ref, g_ref, w_ref, invf_ref, sgn_ref, gq_ref, gk_ref, gcq_ref,
               lng_ref, lnb_ref, bg_ref, out_ref, *, d_model):
    tm = x_ref.shape[0]
    h = _rms(x_ref[...], g_ref[...]).astype(BF16)

    ang_t = invf_ref[...] * pos_ref[...].astype(F32)
    reps = LANES // (HEAD_DIM // 2)
    cosv = jnp.transpose(jnp.concatenate([jnp.cos(ang_t)] * reps, axis=0))
    sinv = jnp.transpose(jnp.concatenate([jnp.sin(ang_t)] * reps, axis=0)) * sgn_ref[...]
    lane = _lane_iota((tm, LANES))
    first_head = lane < HEAD_DIM
    lo_half = (lane % HEAD_DIM) < (HEAD_DIM // 2)

    def head_norm_rope(blk, g):
        y = blk * blk
        s_lo = jnp.sum(jnp.where(first_head, y, 0.0), axis=-1, keepdims=True)
        s_hi = jnp.sum(jnp.where(first_head, 0.0, y), axis=-1, keepdims=True)
        ss = jnp.where(first_head, s_lo, s_hi)
        n = blk * lax.rsqrt(ss * (1.0 / HEAD_DIM) + EPS) * g
        rot = jnp.where(lo_half, pltpu.roll(n, LANES - HEAD_DIM // 2, 1),
                        pltpu.roll(n, HEAD_DIM // 2, 1))
        return n * cosv + rot * sinv

    def proj(a, b):
        return jnp.dot(h, w_ref[:, a:b], preferred_element_type=F32)

    pq = proj(0, A_Q)
    for c in range(A_Q // LANES):
        sl = slice(c * LANES, (c + 1) * LANES)
        out_ref[:, OFF_Q + c * LANES:OFF_Q + (c + 1) * LANES] = (
            head_norm_rope(pq[:, sl], gq_ref[...])).astype(BF16)

    pkv = proj(A_Q, A_Q + 2 * A_KV)
    kn = head_norm_rope(pkv[:, :LANES], gk_ref[...])
    vv = pkv[:, LANES:]
    for off, t in ((OFF_K4, kn), (OFF_V4, vv)):
        tr = pltpu.roll(t, HEAD_DIM, 1)
        parts = (jnp.where(first_head, t, 0.0), jnp.where(first_head, 0.0, tr),
                 jnp.where(first_head, tr, 0.0), jnp.where(first_head, 0.0, t))
        for j, p in enumerate(parts):
            out_ref[:, off + j * LANES:off + (j + 1) * LANES] = p.astype(BF16)

    o0 = A_Q + 2 * A_KV
    out_ref[:, OFF_U:OFF_U + GMLP_WIDTH] = _gelu(proj(o0, o0 + GMLP_WIDTH)).astype(BF16)
    gv = _gelu(proj(o0 + GMLP_WIDTH, o0 + 2 * GMLP_WIDTH))
    mu = jnp.mean(gv, axis=-1, keepdims=True)
    var = jnp.mean(jnp.square(gv - mu), axis=-1, keepdims=True)
    out_ref[:, OFF_VN:OFF_VN + GMLP_WIDTH] = (
        (gv - mu) * lax.rsqrt(var + EPS) * lng_ref[...] + lnb_ref[...]).astype(BF16)

    o1 = o0 + 2 * GMLP_WIDTH
    pc = proj(o1, o1 + C_Q)
    for hh in range(X_HEADS):
        sl = slice(hh * X_HEAD_DIM, (hh + 1) * X_HEAD_DIM)
        out_ref[:, OFF_QC + hh * X_HEAD_DIM:OFF_QC + (hh + 1) * X_HEAD_DIM] = (
            _rms(pc[:, sl], gcq_ref[...])).astype(BF16)

    o2 = o1 + C_Q
    for j in range(3):
        sl = slice(j * d_model, (j + 1) * d_model)
        z = proj(o2 + j * d_model, o2 + (j + 1) * d_model) + bg_ref[:, sl]
        out_ref[:, OFF_GA + j * d_model:OFF_GA + (j + 1) * d_model] = (
            0.5 * jnp.tanh(0.5 * z) + 0.5).astype(BF16)


def _proj_after_kernel(after_ref, *refs, d_model):
    del after_ref
    _proj_kernel(*refs, d_model=d_model)


def _proj(x2d, pos2d, g, w_bf, invf, sgn, gq, gk, gcq, lng, lnb, bg, tm, row0, t, after):
    d = x2d.shape[1]
    d_in = w_bf.shape[1]
    pw = PROJ_W_BASE + 3 * d
    blk0 = row0 // tm
    full = lambda shape: pl.BlockSpec(shape, lambda i: (0,) * len(shape))
    body = functools.partial(_proj_kernel, d_model=d)
    lead_specs, lead_args = [], []
    if after is not None:
        body = functools.partial(_proj_after_kernel, d_model=d)
        lead_specs, lead_args = [pl.BlockSpec(memory_space=pl.ANY)], [after]
    return pl.pallas_call(
        body,
        grid=(t // tm,),
        in_specs=lead_specs + [
            pl.BlockSpec((tm, d), lambda i: (blk0 + i, 0)),
            pl.BlockSpec((None, 1, tm), lambda i: (blk0 + i, 0, 0)),
            full((1, d)),
            pl.BlockSpec((d, d_in), lambda i: (0, 0), pipeline_mode=pl.Buffered(1)),
            full((HEAD_DIM // 2, 1)), full((1, LANES)),
            full((1, LANES)), full((1, LANES)), full((1, X_HEAD_DIM)),
            full((1, GMLP_WIDTH)), full((1, GMLP_WIDTH)), full((1, 3 * d)),
        ],
        out_specs=pl.BlockSpec((tm, pw), lambda i: (i, 0)),
        out_shape=jax.ShapeDtypeStruct((t, pw), BF16),
        name="proj",
        cost_estimate=pl.CostEstimate(
            flops=2 * t * d * d_in, transcendentals=t * (3 * d + 2 * GMLP_WIDTH + 2 * LANES),
            bytes_accessed=4 * t * d + 2 * t * pw + 2 * d * d_in),
        compiler_params=pltpu.CompilerParams(
            dimension_semantics=("arbitrary",), vmem_limit_bytes=VMEM_LIMIT),
    )(*lead_args, x2d, pos2d.reshape(-1, 1, tm), g, w_bf, invf, sgn, gq, gk, gcq, lng, lnb, bg)


def _mix_kernel(sinks_ref, proj_ref, kvprev_ref, x_ref, kc_ref, vc_ref, ws_ref, bst_ref,
                woa_ref, wob_ref, woc_ref, wout_ref, gffn_ref, rw_ref, rb_ref,
                x1_ref, h2_ref, route_ref, gate_ref, cnt_ref,
                run_ref, oa_ref, ob_ref, oc_ref, *, d_model):
    ts = x_ref.shape[0]
    s_idx = pl.program_id(1)
    first_step = jnp.logical_and(pl.program_id(0) == 0, s_idx == 0)
    neg_inf = float("-inf")

    cpk = A_Q // LANES // N_KV_HEADS
    stack = cpk * ATT_BLOCK
    srow = lax.broadcasted_iota(I32, (stack, 2 * ATT_BLOCK), 0)
    qi = srow % ATT_BLOCK
    kj = lax.broadcasted_iota(I32, (stack, 2 * ATT_BLOCK), 1)
    band = jnp.logical_and(kj <= ATT_BLOCK + qi, kj > qi)
    chunk_of_row = lax.broadcasted_iota(I32, (stack, 1), 0) // ATT_BLOCK
    for qb in range(ts // ATT_BLOCK):
        r0 = qb * ATT_BLOCK
        rows = slice(r0, r0 + ATT_BLOCK)
        if qb == 0:
            kp, vp = kvprev_ref[:, :4 * LANES], kvprev_ref[:, 4 * LANES:]
            mask = jnp.logical_and(band, jnp.logical_or(kj >= ATT_BLOCK, s_idx > 0))
        else:
            prow = slice(r0 - ATT_BLOCK, r0)
            kp = proj_ref[prow, OFF_K4:OFF_K4 + 4 * LANES]
            vp = proj_ref[prow, OFF_V4:OFF_V4 + 4 * LANES]
            mask = band
        k4 = jnp.concatenate([kp, proj_ref[rows, OFF_K4:OFF_K4 + 4 * LANES]], axis=0)
        v4 = jnp.concatenate([vp, proj_ref[rows, OFF_V4:OFF_V4 + 4 * LANES]], axis=0)
        for kvh in range(N_KV_HEADS):
            c0 = kvh * cpk
            q4 = jnp.concatenate(
                [proj_ref[rows, OFF_Q + (c0 + c) * LANES:OFF_Q + (c0 + c + 1) * LANES]
                 for c in range(cpk)], axis=0)
            o = jnp.zeros((stack, LANES), F32)
            for half in range(2):
                col = slice((2 * kvh + half) * LANES, (2 * kvh + half + 1) * LANES)
                s = lax.dot_general(q4, k4[:, col], (((1,), (1,)), ((), ())),
                                    preferred_element_type=F32)
                s = jnp.where(mask, s, neg_inf)
                sink = jnp.zeros((stack, 1), F32)
                for c in range(cpk):
                    sink = jnp.where(chunk_of_row == c,
                                     sinks_ref[2 * (c0 + c) + half] * LOG2_E, sink)
                m = jnp.maximum(jnp.max(s, axis=-1, keepdims=True), sink)
                p = jnp.exp2(s - m)
                den = jnp.sum(p, axis=-1, keepdims=True) + jnp.exp2(sink - m)
                o = o + jnp.dot(p.astype(BF16), v4[:, col], preferred_element_type=F32) / den
            for c in range(cpk):
                oa_ref[rows, (c0 + c) * LANES:(c0 + c + 1) * LANES] = (
                    o[c * ATT_BLOCK:(c + 1) * ATT_BLOCK].astype(BF16))

    ti = lax.broadcasted_iota(I32, (GMLP_CHUNK, GMLP_CHUNK), 0)
    si = lax.broadcasted_iota(I32, (GMLP_CHUNK, GMLP_CHUNK), 1)
    for g in range(GMLP_GROUPS):
        wt = jnp.where(si <= ti, ws_ref[g], 0.0).astype(BF16)
        bcol = bst_ref[:, g:g + 1]
        for ch in range(ts // GMLP_CHUNK):
            rows = slice(ch * GMLP_CHUNK, (ch + 1) * GMLP_CHUNK)
            vn = proj_ref[rows, OFF_VN + g * LANES:OFF_VN + (g + 1) * LANES]
            u = proj_ref[rows, OFF_U + g * LANES:OFF_U + (g + 1) * LANES].astype(F32)
            mixed = jnp.dot(wt, vn, preferred_element_type=F32) + bcol
            ob_ref[rows, g * LANES:(g + 1) * LANES] = (u * mixed).astype(BF16)

    for hh in range(X_HEADS):
        sl = slice(hh * X_HEAD_DIM, (hh + 1) * X_HEAD_DIM)
        qc = proj_ref[:, OFF_QC + hh * X_HEAD_DIM:OFF_QC + (hh + 1) * X_HEAD_DIM]
        s = lax.dot_general(qc, kc_ref[:, sl], (((1,), (1,)), ((), ())),
                            preferred_element_type=F32)
        p = jnp.exp2(s - jnp.max(s, axis=-1, keepdims=True))
        den = jnp.sum(p, axis=-1, keepdims=True)
        oc_ref[:, sl] = (jnp.dot(p.astype(BF16), vc_ref[:, sl],
                                 preferred_element_type=F32) / den).astype(BF16)

    def gate(j):
        return proj_ref[:, OFF_GA + j * d_model:OFF_GA + (j + 1) * d_model].astype(F32)

    merged = gate(0) * jnp.dot(oa_ref[...], woa_ref[...], preferred_element_type=F32)
    merged = merged + gate(1) * jnp.dot(ob_ref[...], wob_ref[...], preferred_element_type=F32)
    merged = merged + gate(2) * jnp.dot(oc_ref[...], woc_ref[...], preferred_element_type=F32)
    x1 = x_ref[...] + jnp.dot(merged.astype(BF16), wout_ref[...], preferred_element_type=F32)
    x1_ref[...] = x1

    h2 = _rms(x1, gffn_ref[...])
    h2_words = _pack_bf16_pairs(h2)
    for c in range(h2_ref.shape[0]):
        h2_ref[c] = h2_words[:, c * SC_CHUNK:(c + 1) * SC_CHUNK]
    h2_hi = h2.astype(BF16)
    h2_lo = (h2 - h2_hi.astype(F32)).astype(BF16)
    part = jnp.dot(h2_hi, rw_ref[...], preferred_element_type=F32)
    logits = (part[:, :LANES] + part[:, LANES:]
              + jnp.dot(h2_lo, rw_ref[:, :LANES], preferred_element_type=F32) + rb_ref[...])
    lt = jnp.transpose(logits)[:N_EXPERTS]
    erow = lax.broadcasted_iota(I32, (N_EXPERTS, ts), 0)
    vals, idxs = [], []
    for _ in range(TOP_K):
        m = jnp.max(lt, axis=0, keepdims=True)
        i = jnp.min(jnp.where(lt == m, erow, N_EXPERTS), axis=0, keepdims=True)
        vals.append(m)
        idxs.append(i)
        lt = jnp.where(erow == i, neg_inf, lt)
    es = [jnp.exp(v - vals[0]) for v in vals]
    den = es[0] + es[1] + es[2] + es[3]

    @pl.when(first_step)
    def _():
        run_ref[...] = jnp.zeros_like(run_ref)

    hot = [erow == i for i in idxs]
    multihot = jnp.where(jnp.logical_or(jnp.logical_or(hot[0], hot[1]),
                                        jnp.logical_or(hot[2], hot[3])), 1.0, 0.0)
    tr = lax.broadcasted_iota(I32, (ts, ts), 0)
    tc = lax.broadcasted_iota(I32, (ts, ts), 1)
    earlier = jnp.where(tr < tc, 1.0, 0.0).astype(BF16)
    before = jnp.dot(multihot.astype(BF16), earlier, preferred_element_type=F32) + run_ref[...]
    krow = lax.broadcasted_iota(I32, (IDX_ROWS, ts), 0)
    idx_out = jnp.zeros((IDX_ROWS, ts), I32)
    rank_out = jnp.zeros((IDX_ROWS, ts), I32)
    gate_rows = jnp.zeros((IDX_ROWS, ts), F32)
    for k in range(TOP_K):
        rk = jnp.sum(jnp.where(hot[k], before, 0.0), axis=0, keepdims=True)
        idx_out = jnp.where(krow == k, idxs[k], idx_out)
        rank_out = jnp.where(krow == k, rk.astype(I32), rank_out)
        gate_rows = jnp.where(krow == k, es[k] / den, gate_rows)
    route_ref[:IDX_ROWS] = idx_out
    route_ref[IDX_ROWS:] = rank_out
    gate_ref[...] = jnp.transpose(jnp.concatenate(
        [gate_rows, jnp.zeros((LANES - IDX_ROWS, ts), F32)], axis=0))
    run_ref[...] = run_ref[...] + jnp.sum(multihot, axis=1, keepdims=True)
    cnt_ref[...] = run_ref[...]


def _mix(sinks, proj, x2d, kc, vc, w_s, bst, woa, wob, woc, wout, gffn, rw, rb,
         n_batch, seq, m_len, ts, batch0):
    d = x2d.shape[1]
    t = n_batch * seq
    pw = proj.shape[1]
    ns = seq // ts
    nblk = seq // ATT_BLOCK
    per = ts // ATT_BLOCK
    n_chunks = d // (2 * SC_CHUNK)
    full = lambda shape: pl.BlockSpec(shape, lambda b, s: (0,) * len(shape))
    row = lambda width: pl.BlockSpec((ts, width), lambda b, s: (b * ns + s, 0))
    assert OFF_K4 % (8 * LANES) == 0 and OFF_V4 == OFF_K4 + 4 * LANES
    kv_prev = pl.BlockSpec(
        (ATT_BLOCK, 8 * LANES),
        lambda b, s: (b * nblk + jnp.maximum(s * per - 1, 0), OFF_K4 // (8 * LANES)))
    return pl.pallas_call(
        functools.partial(_mix_kernel, d_model=d),
        grid=(n_batch, ns),
        in_specs=[
            pl.BlockSpec(memory_space=pltpu.SMEM),
            row(pw), kv_prev,
            pl.BlockSpec((ts, d), lambda b, s: ((batch0 + b) * ns + s, 0)),
            pl.BlockSpec((m_len, C_Q), lambda b, s: (batch0 + b, 0)),
            pl.BlockSpec((m_len, C_Q), lambda b, s: (batch0 + b, 0)),
            full((GMLP_GROUPS, GMLP_CHUNK, GMLP_CHUNK)), full((GMLP_CHUNK, GMLP_GROUPS)),
            full((A_Q, d)), full((GMLP_WIDTH, d)), full((C_Q, d)), full((d, d)),
            full((1, d)), full((d, 2 * LANES)), full((1, LANES)),
        ],
        out_specs=[row(d),
                   pl.BlockSpec((n_chunks, ts, SC_CHUNK), lambda b, s: (0, b * ns + s, 0)),
                   pl.BlockSpec((2 * IDX_ROWS, ts), lambda b, s: (0, b * ns + s)),
                   row(LANES), full((N_EXPERTS, 1))],
        out_shape=[
            jax.ShapeDtypeStruct((t, d), F32),
            jax.ShapeDtypeStruct((n_chunks, t, SC_CHUNK), U32),
            jax.ShapeDtypeStruct((2 * IDX_ROWS, t), I32),
            jax.ShapeDtypeStruct((t, LANES), F32), jax.ShapeDtypeStruct((N_EXPERTS, 1), F32),
        ],
        scratch_shapes=[
            pltpu.VMEM((N_EXPERTS, 1), F32),
            pltpu.VMEM((ts, A_Q), BF16), pltpu.VMEM((ts, GMLP_WIDTH), BF16),
            pltpu.VMEM((ts, C_Q), BF16),
        ],
        name="mix",
        cost_estimate=pl.CostEstimate(
            flops=2 * t * (d * (A_Q + GMLP_WIDTH + C_Q + d) + 4 * ATT_BLOCK * A_Q
                           + GMLP_CHUNK * GMLP_WIDTH + 2 * m_len * C_Q + 3 * d * LANES),
            transcendentals=t * (2 * ATT_BLOCK * N_Q_HEADS + m_len * X_HEADS),
            bytes_accessed=t * (2 * pw + 4 * d + 4 * d + 2 * d + 12 * LANES)),
        compiler_params=pltpu.CompilerParams(
            dimension_semantics=("arbitrary", "arbitrary"), vmem_limit_bytes=VMEM_LIMIT),
    )(sinks, proj, proj, x2d, kc, vc, w_s, bst, woa, wob, woc, wout, gffn, rw, rb)


def _pos_kernel(pstart_ref, route_ref, pos_ref):
    idx = route_ref[:IDX_ROWS]
    pos = route_ref[IDX_ROWS:]
    for e in range(N_EXPERTS):
        pos = pos + jnp.where(idx == e, pstart_ref[e], 0)
    pos_ref[...] = pos


def _pos(pstart, route, tp):
    t = route.shape[1]
    return pl.pallas_call(
        _pos_kernel,
        grid_spec=pltpu.PrefetchScalarGridSpec(
            num_scalar_prefetch=1,
            grid=(t // tp,),
            in_specs=[pl.BlockSpec((2 * IDX_ROWS, tp), lambda i, ps: (0, i))],
            out_specs=pl.BlockSpec((IDX_ROWS, tp), lambda i, ps: (0, i)),
        ),
        out_shape=jax.ShapeDtypeStruct((IDX_ROWS, t), I32),
        name="slot_pos",
        compiler_params=pltpu.CompilerParams(dimension_semantics=("arbitrary",)),
    )(pstart, route)


def _sc_mesh():
    return plsc.VectorSubcoreMesh(core_axis_name="core", subcore_axis_name="subcore")


def _sc_scatter_rows(src, idx, n_rows):
    n_chunks, t, w = src.shape
    n_idx = idx.shape[1]
    src_blocks = t // SC_ROWS
    idx_blocks = n_idx // SC_ROWS

    @pl.kernel(out_type=jax.ShapeDtypeStruct((n_chunks * n_rows, w), src.dtype), mesh=_sc_mesh(),
               scratch_types=[], name="sc_dispatch",
               cost_estimate=pl.CostEstimate(
                   flops=0, transcendentals=0,
                   bytes_accessed=n_chunks * n_idx * (8 * w + 4)))
    def scatter(src_hbm, idx_hbm, out_hbm):
        def body(src_vmem, idx_vmem):
            pltpu.sync_copy(src_vmem, out_hbm.at[idx_vmem.at[0]])

        pltpu.emit_pipeline(
            body,
            grid=(n_chunks, idx_blocks),
            in_specs=[pl.BlockSpec((SC_ROWS, w), lambda c, i: (c * src_blocks + i % src_blocks, 0)),
                      pl.BlockSpec((1, SC_ROWS), lambda c, i: (0, c * idx_blocks + i))],
            out_specs=[],
            core_axis_name=("core", "subcore"),
            dimension_semantics=(pltpu.PARALLEL, pltpu.PARALLEL),
        )(src_hbm, idx_hbm)

    return scatter(src.reshape(n_chunks * t, w), idx.reshape(1, n_chunks * n_idx)).reshape(
        n_chunks, n_rows, w)


def _sc_gather_rows(table, idx):
    n_chunks, p, w = table.shape
    n_idx = idx.shape[1]
    idx_blocks = n_idx // SC_ROWS

    @pl.kernel(out_type=jax.ShapeDtypeStruct((n_chunks * n_idx, w), table.dtype), mesh=_sc_mesh(),
               scratch_types=[], name="sc_gather",
               cost_estimate=pl.CostEstimate(
                   flops=0, transcendentals=0,
                   bytes_accessed=n_chunks * n_idx * (8 * w + 4)))
    def gather(table_hbm, idx_hbm, out_hbm):
        def body(idx_vmem, out_vmem):
            pltpu.sync_copy(table_hbm.at[idx_vmem.at[0]], out_vmem)

        pltpu.emit_pipeline(
            body,
            grid=(n_chunks, idx_blocks),
            in_specs=[pl.BlockSpec((1, SC_ROWS), lambda c, i: (0, c * idx_blocks + i))],
            out_specs=[pl.BlockSpec((SC_ROWS, w), lambda c, i: (c * idx_blocks + i, 0))],
            core_axis_name=("core", "subcore"),
            dimension_semantics=(pltpu.PARALLEL, pltpu.PARALLEL),
        )(idx_hbm, out_hbm)

    return gather(table.reshape(n_chunks * p, w), idx.reshape(1, n_chunks * n_idx)).reshape(
        n_chunks, n_idx, w)


def _expert_kernel(bexp_ref, nvalid_ref, next_ref, xs_ref, wgu_hbm, bgu_ref, wd_hbm, bd_ref, ys_ref,
                   wgu_stage, wd_stage, wgu_bf_ref, wd_bf_ref, sems, *, d_exp):
    n_chunks = xs_ref.shape[0]

    def weight_copies(e):
        return (pltpu.make_async_copy(wgu_hbm.at[e], wgu_stage, sems.at[0]),
                pltpu.make_async_copy(wd_hbm.at[e], wd_stage, sems.at[1]))

    def switch_weights(b):
        e = bexp_ref[b]

        @pl.when(jnp.logical_or(b == 0, e != bexp_ref[jnp.maximum(b - 1, 0)]))
        def _():
            @pl.when(b == 0)
            def _():
                for cp in weight_copies(e):
                    cp.start()

            for cp in weight_copies(e):
                cp.wait()
            wgu_bf_ref[...] = wgu_stage[...].astype(BF16)
            wd_bf_ref[...] = wd_stage[...].astype(BF16)
            e_next = next_ref[e]

            @pl.when(e_next >= 0)
            def _():
                for cp in weight_copies(e_next):
                    cp.start()

    def compute(e, rows):
        lo, hi = _unpack_bf16_pairs(
            jnp.concatenate([xs_ref[c, rows, :] for c in range(n_chunks)], axis=1))
        xb = jnp.concatenate([lo, hi], axis=1).astype(BF16)
        gu = jnp.dot(xb, wgu_bf_ref[...], preferred_element_type=F32) + bgu_ref[e]
        gate = jnp.minimum(gu[:, :d_exp], SWIGLU_LIMIT)
        up = jnp.clip(gu[:, d_exp:], -SWIGLU_LIMIT, SWIGLU_LIMIT)
        half_gate = 0.5 * gate
        glu = half_gate + half_gate * jnp.tanh(gate * (0.5 * SWIGLU_ALPHA))
        act = ((up + 1.0) * glu).astype(BF16)
        y = jnp.dot(act, wd_bf_ref[...], preferred_element_type=F32) + bd_ref[e]
        y_words = _pack_bf16_pairs(y)
        for c in range(n_chunks):
            ys_ref[c, rows, :] = y_words[:, c * SC_CHUNK:(c + 1) * SC_CHUNK]

    b0 = pl.program_id(0) * EXPERT_BLOCKS_PER_STEP
    b_last = b0 + EXPERT_BLOCKS_PER_STEP - 1
    block_rows = [slice(h * MOE_BLOCK, (h + 1) * MOE_BLOCK) for h in range(EXPERT_BLOCKS_PER_STEP)]
    one_expert = jnp.logical_and(b_last < nvalid_ref[0], bexp_ref[b_last] == bexp_ref[b0])

    @pl.when(b0 < nvalid_ref[0])
    def _():
        switch_weights(b0)

        @pl.when(one_expert)
        def _():
            for rows in block_rows:
                compute(bexp_ref[b0], rows)

        @pl.when(jnp.logical_not(one_expert))
        def _():
            compute(bexp_ref[b0], block_rows[0])
            for h in range(1, EXPERT_BLOCKS_PER_STEP):
                @pl.when(b0 + h < nvalid_ref[0])
                def _():
                    switch_weights(b0 + h)
                    compute(bexp_ref[b0 + h], block_rows[h])


def _experts(bexp, nvalid, next_expert, xs, wgu, bgu, wd, bd):
    n_chunks, n_rows, _ = xs.shape
    n_exp, d, d_exp2 = wgu.shape
    d_exp = d_exp2 // 2
    step_rows = EXPERT_BLOCKS_PER_STEP * MOE_BLOCK
    blk = lambda s, be, nv, nx: jnp.minimum(s, (nv[0] - 1) // EXPERT_BLOCKS_PER_STEP)
    return pl.pallas_call(
        functools.partial(_expert_kernel, d_exp=d_exp),
        grid_spec=pltpu.PrefetchScalarGridSpec(
            num_scalar_prefetch=3,
            grid=(n_rows // step_rows,),
            in_specs=[
                pl.BlockSpec((n_chunks, step_rows, SC_CHUNK),
                             lambda s, be, nv, nx: (0, blk(s, be, nv, nx), 0)),
                pl.BlockSpec(memory_space=pl.ANY),
                pl.BlockSpec((n_exp, 1, 2 * d_exp), lambda s, be, nv, nx: (0, 0, 0)),
                pl.BlockSpec(memory_space=pl.ANY),
                pl.BlockSpec((n_exp, 1, d), lambda s, be, nv, nx: (0, 0, 0)),
            ],
            out_specs=pl.BlockSpec((n_chunks, step_rows, SC_CHUNK),
                                   lambda s, be, nv, nx: (0, blk(s, be, nv, nx), 0)),
            scratch_shapes=[pltpu.VMEM((d, 2 * d_exp), F32), pltpu.VMEM((d_exp, d), F32),
                            pltpu.VMEM((d, 2 * d_exp), BF16), pltpu.VMEM((d_exp, d), BF16),
                            pltpu.SemaphoreType.DMA((2,))],
        ),
        out_shape=jax.ShapeDtypeStruct((n_chunks, n_rows, SC_CHUNK), U32),
        name="experts",
        cost_estimate=pl.CostEstimate(
            flops=6 * n_rows * d * d_exp, transcendentals=n_rows * d_exp,
            bytes_accessed=4 * n_rows * d + 4 * N_EXPERTS * 3 * d * d_exp),
        compiler_params=pltpu.CompilerParams(
            dimension_semantics=("arbitrary",), vmem_limit_bytes=VMEM_LIMIT),
    )(bexp, nvalid, next_expert, xs, wgu, bgu, wd, bd)


def _combine_kernel(prev_ref, x1_ref, gate_ref, yg_ref, out_ref):
    del prev_ref
    n_chunks = yg_ref.shape[0]
    half = n_chunks * SC_CHUNK
    for c in range(n_chunks):
        sl_lo = slice(c * SC_CHUNK, (c + 1) * SC_CHUNK)
        sl_hi = slice(half + c * SC_CHUNK, half + (c + 1) * SC_CHUNK)
        acc_lo = x1_ref[:, sl_lo]
        acc_hi = x1_ref[:, sl_hi]
        for k in range(TOP_K):
            lo, hi = _unpack_bf16_pairs(yg_ref[c, k])
            g = gate_ref[:, k:k + 1]
            acc_lo = acc_lo + g * lo
            acc_hi = acc_hi + g * hi
        out_ref[:, sl_lo] = acc_lo
        out_ref[:, sl_hi] = acc_hi


def _combine(x1, gates, yg, out_prev, tcb, row0, t_total):
    t, d = x1.shape
    n_chunks = yg.shape[0]
    blk0 = row0 // tcb
    in_specs = [
        pl.BlockSpec((tcb, d), lambda i: (i, 0)),
        pl.BlockSpec((tcb, LANES), lambda i: (i, 0)),
        pl.BlockSpec((n_chunks, TOP_K, tcb, SC_CHUNK), lambda i: (0, 0, i, 0)),
    ]
    args = [x1, gates, yg]
    aliases = {}
    body = functools.partial(_combine_kernel, None)
    if out_prev is not None:
        in_specs.append(pl.BlockSpec(memory_space=pl.ANY))
        args.append(out_prev)
        aliases = {3: 0}
        body = lambda a, b, c, prev, o: _combine_kernel(prev, a, b, c, o)
    return pl.pallas_call(
        body,
        grid=(t // tcb,),
        in_specs=in_specs,
        out_specs=pl.BlockSpec((tcb, d), lambda i: (blk0 + i, 0)),
        out_shape=jax.ShapeDtypeStruct((t_total, d), F32),
        input_output_aliases=aliases,
        name="combine",
        cost_estimate=pl.CostEstimate(
            flops=2 * TOP_K * t * d, transcendentals=0,
            bytes_accessed=t * (4 * d + 4 * d + 2 * TOP_K * d + 4 * LANES)),
        compiler_params=pltpu.CompilerParams(
            dimension_semantics=("arbitrary",), vmem_limit_bytes=VMEM_LIMIT),
    )(*args)


def _pick_tile(n, pref, *also):
    t = pref
    while any(v % t for v in (n,) + also):
        t //= 2
    return t


def _token_groups(n_batch):
    if n_batch % 8 == 0:
        return [5 * n_batch // 8, 3 * n_batch // 8]
    if n_batch % 4 == 0:
        return [3 * n_batch // 4, n_batch // 4]
    if n_batch % 2 == 0:
        return [n_batch // 2, n_batch // 2]
    return [n_batch]


def kernel(x, mem, positions, attn_norm_g, mem_norm_g, w_in, b_gates, a_q_norm_g, a_k_norm_g,
           a_sinks, w_o_a, gmlp_ln_g, gmlp_ln_b, gmlp_w_s, gmlp_b_s, w_o_b, w_mem_kv,
           c_q_norm_g, c_k_norm_g, w_o_c, w_out, ffn_norm_g, router_w, router_b,
           w_gate_up, b_gate_up, w_down, b_down):
    n_batch, seq, d = x.shape
    m_len = mem.shape[1]
    depth = w_in.shape[0]
    t = n_batch * seq
    assert seq % ATT_BLOCK == 0 and seq % GMLP_CHUNK == 0, "sequence must be whole 128-token blocks"
    assert d % (2 * SC_CHUNK) == 0, "model width must be a multiple of the packed row chunk"
    assert m_len % 8 == 0 and w_in.shape[2] == A_Q + 2 * A_KV + 2 * GMLP_WIDTH + C_Q + 3 * d
    assert router_w.shape[2] == N_EXPERTS and w_gate_up.shape[1:] == (N_EXPERTS, d, 2 * w_down.shape[2])
    group_batches = _token_groups(n_batch)

    inv_freq = ROPE_THETA ** (-jnp.arange(0, HEAD_DIM, 2, dtype=F32) / HEAD_DIM)
    invf = inv_freq[:, None]
    sgn = jnp.tile(jnp.concatenate([-jnp.ones((HEAD_DIM // 2,), F32),
                                    jnp.ones((HEAD_DIM // 2,), F32)]), LANES // HEAD_DIM)[None, :]
    pos2d = positions.reshape(t, 1).astype(I32)
    mem2d = mem.reshape(n_batch * m_len, d)
    x2d = x.reshape(t, d)

    ts = _pick_tile(seq, 512)

    for l in range(depth):
        kc, vc = _mem_kv(mem2d, mem_norm_g[l][None, :], w_mem_kv[l].astype(BF16),
                         c_k_norm_g[l][None, :], n_batch, m_len)
        w_in_bf = w_in[l].astype(BF16)
        mix_w = (w_o_a[l].astype(BF16), w_o_b[l].astype(BF16), w_o_c[l].astype(BF16),
                 w_out[l].astype(BF16))
        gq = jnp.tile(a_q_norm_g[l], LANES // HEAD_DIM)[None, :] * (HEAD_DIM ** -0.5 * LOG2_E)
        gcq = c_q_norm_g[l][None, :] * (X_HEAD_DIM ** -0.5 * LOG2_E)
        gk = jnp.tile(a_k_norm_g[l], LANES // HEAD_DIM)[None, :]
        rw32 = jnp.pad(router_w[l], ((0, 0), (0, LANES - N_EXPERTS)))
        rw_hi = rw32.astype(BF16)
        rw = jnp.concatenate([rw_hi, (rw32 - rw_hi.astype(F32)).astype(BF16)], axis=1)
        rb = jnp.pad(router_b[l], (0, LANES - N_EXPERTS))[None, :]

        routed = []
        scatter_idx = None
        batch0 = 0
        for gb in group_batches:
            tg = gb * seq
            row0 = batch0 * seq
            n_asg = tg * TOP_K
            nb = -(-n_asg // MOE_BLOCK) + N_EXPERTS
            nb = -(-nb // EXPERT_BLOCKS_PER_STEP) * EXPERT_BLOCKS_PER_STEP
            n_rows = nb * MOE_BLOCK
            tm = _pick_tile(tg, 2 * PROJ_SUB_ROWS, row0, t)
            proj = _proj(
                x2d, pos2d, attn_norm_g[l][None, :], w_in_bf, invf, sgn, gq, gk,
                gcq, gmlp_ln_g[l][None, :], gmlp_ln_b[l][None, :],
                b_gates[l].reshape(1, 3 * d), tm, row0, tg, scatter_idx)
            x1, h2, route, gates, counts = _mix(
                a_sinks[l], proj, x2d, kc, vc, gmlp_w_s[l], gmlp_b_s[l].T, *mix_w,
                ffn_norm_g[l][None, :], rw, rb, gb, seq, m_len, ts, batch0)

            cnt = counts[:, 0].astype(I32)
            padded = (cnt + MOE_BLOCK - 1) // MOE_BLOCK * MOE_BLOCK
            pend = jnp.cumsum(padded)
            pstart = (pend - padded).astype(I32)
            nvalid = (pend[-1:] // MOE_BLOCK).astype(I32)
            blk_row = jnp.arange(nb, dtype=I32) * MOE_BLOCK
            bexp = jnp.minimum(jnp.sum((pend[None, :] <= blk_row[:, None]).astype(I32), axis=1),
                               N_EXPERTS - 1).astype(I32)
            eid = jnp.arange(N_EXPERTS, dtype=I32)
            later = jnp.logical_and(eid[None, :] > eid[:, None], (cnt > 0)[None, :])
            next_expert = jnp.min(jnp.where(later, eid[None, :], N_EXPERTS), axis=1)
            next_expert = jnp.where(next_expert == N_EXPERTS, -1, next_expert).astype(I32)
            j = jnp.arange(MOE_BLOCK, dtype=I32)[None, :]
            fill = jnp.where(j < (padded - cnt)[:, None], pend[:, None] - 1 - j,
                             n_rows - 1 - j).astype(I32)

            pos = _pos(pstart, route, _pick_tile(tg, 8192))
            pos_km = pos[:TOP_K].reshape(n_asg)
            n_chunks = h2.shape[0]
            chunk_off = (jnp.arange(n_chunks, dtype=I32) * n_rows)[:, None]
            scatter_idx = jnp.concatenate([pos_km, fill.reshape(-1)])[None, :] + chunk_off
            xs = _sc_scatter_rows(h2, scatter_idx, n_rows)
            routed.append((row0, x1, gates, xs, bexp, nvalid, next_expert, pos_km, chunk_off))
            batch0 += gb

        gathered = []
        for row0, x1, gates, xs, bexp, nvalid, next_expert, pos_km, chunk_off in routed:
            ys = _experts(bexp, nvalid, next_expert, xs, w_gate_up[l], b_gate_up[l][:, None, :],
                          w_down[l], b_down[l][:, None, :])
            yg = _sc_gather_rows(ys, pos_km[None, :] + chunk_off)
            gathered.append(
                (row0, x1, gates, yg.reshape(yg.shape[0], TOP_K, x1.shape[0], SC_CHUNK)))

        out = None
        for row0, x1, gates, yg in gathered:
            out = _combine(x1, gates, yg, out, _pick_tile(x1.shape[0], 1024, row0), row0, t)
        x2d = out
    return x2d.reshape(n_batch, seq, d)
```
